```python
import jax, jax.numpy as jnp
from jax import lax
import numpy as np


D_MODEL = 1024
BATCH = 8
SEQ = 4096
DEPTH = 2
DEC_BATCH = 128
DEC_SEQ = 8
PAST_LEN = 16384
PAGE_SIZE = 128

HEAD_DIM = 64
A_HEADS = 8
A_KV_HEADS = 2
A_GROUP = A_HEADS // A_KV_HEADS
A_WINDOW = 128
A_Q = A_HEADS * HEAD_DIM
A_KV = A_KV_HEADS * HEAD_DIM
B_WIDTH = D_MODEL // 2
CONV_WIDTH = 3
C_WIDTH = D_MODEL // 2
POOL_WINDOWS = (2, 4, 8, 16)
C_GROUPS = len(POOL_WINDOWS)
C_GROUP_WIDTH = C_WIDTH // C_GROUPS
POOL_MAX = max(POOL_WINDOWS)
D_HEADS = 8
D_QKV = D_HEADS * HEAD_DIM
D_BRANCHES = ((128, 1), (512, 4), (2048, 16))
D_WINDOW_MAX = max(w for w, _ in D_BRANCHES)
EVEN_IN = A_Q + 2 * A_KV + 3 * B_WIDTH
EVEN_MIX = A_Q + B_WIDTH
ODD_IN = C_WIDTH + 3 * D_QKV
ODD_MIX = C_WIDTH + D_QKV
D_MLP = 4 * D_MODEL
N_EVEN = (DEPTH + 1) // 2
N_ODD = DEPTH // 2
DEEPNORM_ALPHA = (2 * DEPTH) ** 0.25
DEEPNORM_BETA = (8 * DEPTH) ** -0.25
LN_EPS = 1e-5

kernel_name = "hybrid_swa_conv_pool_dilated_step"


def alibi_slopes(n_heads):
    return jnp.asarray(2.0 ** (-8.0 * np.arange(1, n_heads + 1) / n_heads), dtype=jnp.float32)


def layer_norm(x, g, b):
    xf = x.astype(jnp.float32)
    mu = jnp.mean(xf, axis=-1, keepdims=True)
    var = jnp.mean(jnp.square(xf - mu), axis=-1, keepdims=True)
    return ((xf - mu) * lax.rsqrt(var + LN_EPS) * g.astype(jnp.float32) + b.astype(jnp.float32)).astype(x.dtype)


def squared_relu_mlp(x, w1, w2):
    return jnp.square(jax.nn.relu(x @ w1)) @ w2


def combine_branches(parts, sink=None):
    o = jnp.stack([p[0] for p in parts])
    m = jnp.stack([p[1] for p in parts])
    s = jnp.stack([p[2] for p in parts])
    m_ref = jnp.max(m, axis=0)
    if sink is not None:
        m_ref = jnp.maximum(m_ref, sink)
    w = s * jnp.exp(m - m_ref)
    den = jnp.sum(w, axis=0)
    if sink is not None:
        den = den + jnp.exp(sink - m_ref)
    return jnp.sum(w[..., None] * o, axis=0) / den[..., None]


def banded_window_attn(q, k, v, dilation, n_w, slopes):
    B, S = q.shape[0], q.shape[1]
    n = S // dilation
    blk = n_w
    nb = -(-n // blk)
    pad = nb * blk - n

    def strided(x):
        rest = x.shape[2:]
        x = jnp.moveaxis(x.reshape((B, n, dilation) + rest), 2, 1)
        return x.reshape((B * dilation, n) + rest)

    def blocks(x):
        x = jnp.pad(x, [(0, 0), (0, pad)] + [(0, 0)] * (x.ndim - 2))
        return x.reshape((x.shape[0], nb, blk) + x.shape[2:])

    def with_prev(xb):
        prev = jnp.pad(xb[:, :-1], [(0, 0), (1, 0)] + [(0, 0)] * (xb.ndim - 2))
        return jnp.concatenate([prev, xb], axis=2)

    def unstrided(x):
        rest = x.shape[3:]
        x = x.reshape((B * dilation, nb * blk) + rest)[:, :n]
        x = jnp.moveaxis(x.reshape((B, dilation, n) + rest), 1, 2)
        return x.reshape((B, S) + rest)

    qb = blocks(strided(q))
    kk = with_prev(blocks(strided(k)))
    vv = with_prev(blocks(strided(v)))
    scores = jnp.einsum('bnqhgd,bnkhd->bnhgqk', qb, kk, preferred_element_type=jnp.float32)
    qi = jnp.arange(blk)[:, None]
    kj = jnp.arange(2 * blk)[None, :]
    dist = qi + blk - kj
    key_idx = jnp.arange(nb)[:, None, None] * blk - blk + kj[None]
    valid = (dist >= 0)[None] & (dist <= n_w)[None] & (key_idx >= 0)
    bias = -slopes[:, :, None, None] * (dist * dilation).astype(jnp.float32)[None, None]
    scores = jnp.where(valid[None, :, None, None], scores + bias[None, None], -jnp.inf)
    m = jnp.max(scores, axis=-1)
    p = jnp.exp(scores - m[..., None])
    s = jnp.sum(p, axis=-1)
    o = jnp.einsum('bnhgqk,bnkhd->bnqhgd', p, vv.astype(jnp.float32))
    m = jnp.moveaxis(m, -1, 2)
    s = jnp.moveaxis(s, -1, 2)
    o = o / s[..., None]
    return unstrided(o), unstrided(m), unstrided(s)


def gathered_window_attn(q, k_all, v_all, n_hist, dilation, n_w, slopes):
    T = q.shape[1]
    j = jnp.arange(n_w + 1)
    idx = n_hist + jnp.arange(T)[:, None] - j[None, :] * dilation
    valid = idx >= 0
    idx = jnp.maximum(idx, 0)
    kg = k_all[:, idx]
    vg = v_all[:, idx]
    scores = jnp.einsum('bthgd,btjhd->bthgj', q, kg, preferred_element_type=jnp.float32)
    bias = -slopes[:, :, None] * (j * dilation).astype(jnp.float32)[None, None]
    scores = jnp.where(valid[None, :, None, None, :], scores + bias, -jnp.inf)
    m = jnp.max(scores, axis=-1)
    p = jnp.exp(scores - m[..., None])
    s = jnp.sum(p, axis=-1)
    o = jnp.einsum('bthgj,btjhd->bthgd', p, vg.astype(jnp.float32)) / s[..., None]
    return o, m, s


def multi_pool_minus_identity(u_all, n_new, pos):
    H = u_all.shape[1] - n_new
    uf = u_all.astype(jnp.float32)
    cs = jnp.pad(jnp.cumsum(uf, axis=1), [(0, 0), (1, 0), (0, 0)])
    end = cs[:, H + 1:H + 1 + n_new]
    outs = []
    for g, w in enumerate(POOL_WINDOWS):
        lo, hi = g * C_GROUP_WIDTH, (g + 1) * C_GROUP_WIDTH
        start = cs[:, H + 1 - w:H + 1 - w + n_new, lo:hi]
        cnt = jnp.minimum(pos + 1, w).astype(jnp.float32)[:, None]
        outs.append((end[..., lo:hi] - start) / cnt - uf[:, H:, lo:hi])
    return jnp.concatenate(outs, axis=-1)


def even_mixer(x, hist, w_in, b_in, sinks, conv_w, w_out, b_out):
    Bn, T, _ = x.shape
    proj = x @ w_in + b_in
    o1 = A_Q
    o2 = o1 + A_KV
    o3 = o2 + A_KV
    o4 = o3 + B_WIDTH
    o5 = o4 + B_WIDTH
    q, k, v, h, gb, gc = jnp.split(proj, [o1, o2, o3, o4, o5], axis=-1)
    q = q.reshape(Bn, T, A_KV_HEADS, A_GROUP, HEAD_DIM) * HEAD_DIM ** -0.5
    k = k.reshape(Bn, T, A_KV_HEADS, HEAD_DIM)
    v = v.reshape(Bn, T, A_KV_HEADS, HEAD_DIM)
    c = gc * h
    slopes = alibi_slopes(A_HEADS).reshape(A_KV_HEADS, A_GROUP)
    if hist is None:
        part = banded_window_attn(q, k, v, 1, A_WINDOW, slopes)
        c_hist = jnp.zeros((Bn, CONV_WIDTH - 1, B_WIDTH), c.dtype)
        n_keep = min(A_WINDOW, T)
        new_k, new_v = k[:, -n_keep:], v[:, -n_keep:]
    else:
        k_cache, v_cache, c_hist = hist
        n_hist = k_cache.shape[1]
        k_all = jnp.concatenate([k_cache, k], axis=1)
        v_all = jnp.concatenate([v_cache, v], axis=1)
        part = gathered_window_attn(q, k_all, v_all, n_hist, 1, A_WINDOW, slopes)
        new_k, new_v = k_all[:, -n_hist:], v_all[:, -n_hist:]
    sink = sinks.astype(jnp.float32).reshape(A_KV_HEADS, A_GROUP)
    y_a = combine_branches([part], sink=sink).reshape(Bn, T, A_Q).astype(x.dtype)
    c_all = jnp.concatenate([c_hist, c], axis=1)
    conv = sum(c_all[:, i:i + T] * conv_w[i] for i in range(CONV_WIDTH))
    y_b = gb * conv
    mix = jnp.concatenate([y_a, y_b], axis=-1) @ w_out + b_out
    return mix, (new_k, new_v, c_all[:, -(CONV_WIDTH - 1):])


def odd_mixer(x, pos, hist, w_in, b_in, w_group, scale, w_out, b_out):
    Bn, T, _ = x.shape
    proj = x @ w_in + b_in
    u, q, k, v = jnp.split(proj, [C_WIDTH, C_WIDTH + D_QKV, C_WIDTH + 2 * D_QKV], axis=-1)
    q = q.reshape(Bn, T, D_HEADS, 1, HEAD_DIM) * HEAD_DIM ** -0.5
    k = k.reshape(Bn, T, D_HEADS, HEAD_DIM)
    v = v.reshape(Bn, T, D_HEADS, HEAD_DIM)
    slopes = alibi_slopes(D_HEADS).reshape(D_HEADS, 1)
    if hist is None:
        parts = [banded_window_attn(q, k, v, d, w // d, slopes) for w, d in D_BRANCHES]
        u_hist = jnp.zeros((Bn, POOL_MAX - 1, C_WIDTH), u.dtype)
        n_keep = min(D_WINDOW_MAX, T)
        new_k, new_v = k[:, -n_keep:], v[:, -n_keep:]
    else:
        u_hist, k_cache, v_cache = hist
        n_hist = k_cache.shape[1]
        k_all = jnp.concatenate([k_cache, k], axis=1)
        v_all = jnp.concatenate([v_cache, v], axis=1)
        parts = [gathered_window_attn(q, k_all, v_all, n_hist, d, w // d, slopes) for w, d in D_BRANCHES]
        new_k, new_v = k_all[:, -n_hist:], v_all[:, -n_hist:]
    y_d = combine_branches(parts).reshape(Bn, T, D_QKV).astype(x.dtype)
    u_all = jnp.concatenate([u_hist, u], axis=1)
    pooled = multi_pool_minus_identity(u_all, T, pos).astype(x.dtype)
    y_c = jnp.einsum('btgc,gcd->btgd', pooled.reshape(Bn, T, C_GROUPS, C_GROUP_WIDTH), w_group)
    y_c = y_c.reshape(Bn, T, C_WIDTH) * scale
    mix = jnp.concatenate([y_c, y_d], axis=-1) @ w_out + b_out
    return mix, (u_all[:, -(POOL_MAX - 1):], new_k, new_v)


def trunk(x, pos, caches, even_w_in, even_b_in, a_sinks, b_conv_w, even_w_out, even_b_out,
          odd_w_in, odd_b_in, c_w_group, c_scale, odd_w_out, odd_b_out,
          mlp_w1, mlp_w2, ln1_g, ln1_b, ln2_g, ln2_b):
    even_states, odd_states = [], []
    for layer in range(DEPTH):
        i = layer // 2
        if layer % 2 == 0:
            hist = None if caches is None else (caches[0][i], caches[1][i], caches[2][i])
            mix, st = even_mixer(x, hist, even_w_in[i], even_b_in[i], a_sinks[i], b_conv_w[i],
                                 even_w_out[i], even_b_out[i])
            even_states.append(st)
        else:
            hist = None if caches is None else (caches[3][i], caches[4][i], caches[5][i])
            mix, st = odd_mixer(x, pos, hist, odd_w_in[i], odd_b_in[i], c_w_group[i], c_scale[i],
                                odd_w_out[i], odd_b_out[i])
            odd_states.append(st)
        x = layer_norm(DEEPNORM_ALPHA * x + mix, ln1_g[layer], ln1_b[layer])
        x = layer_norm(DEEPNORM_ALPHA * x + squared_relu_mlp(x, mlp_w1[layer], mlp_w2[layer]),
                       ln2_g[layer], ln2_b[layer])
    a_k = jnp.stack([s[0] for s in even_states])
    a_v = jnp.stack([s[1] for s in even_states])
    b_conv = jnp.stack([s[2] for s in even_states])
    c_pool = jnp.stack([s[0] for s in odd_states])
    d_k = jnp.stack([s[1] for s in odd_states])
    d_v = jnp.stack([s[2] for s in odd_states])
    return x, a_k, a_v, b_conv, c_pool, d_k, d_v


def setup_inputs(seed: int = 0) -> dict:
    key = jax.random.key(seed)
    ks = jax.random.split(key, 26)

    def nrm(k, shape, scale):
        return jax.random.normal(k, shape, jnp.float32) * scale

    n_a = min(A_WINDOW, PAST_LEN)
    n_d = min(D_WINDOW_MAX, PAST_LEN)
    return {
        "x_prompt": nrm(ks[0], (BATCH, SEQ, D_MODEL), 1.0),
        "x_sample": nrm(ks[1], (DEC_BATCH, DEC_SEQ, D_MODEL), 1.0),
        "cache_a_k": nrm(ks[2], (N_EVEN, DEC_BATCH, n_a, A_KV_HEADS, HEAD_DIM), 1.0),
        "cache_a_v": nrm(ks[3], (N_EVEN, DEC_BATCH, n_a, A_KV_HEADS, HEAD_DIM), 1.0),
        "state_b_conv": nrm(ks[4], (N_EVEN, DEC_BATCH, CONV_WIDTH - 1, B_WIDTH), 1.0),
        "state_c_pool": nrm(ks[5], (N_ODD, DEC_BATCH, POOL_MAX - 1, C_WIDTH), 1.0),
        "cache_d_k": nrm(ks[6], (N_ODD, DEC_BATCH, n_d, D_HEADS, HEAD_DIM), 1.0),
        "cache_d_v": nrm(ks[7], (N_ODD, DEC_BATCH, n_d, D_HEADS, HEAD_DIM), 1.0),
        "even_w_in": nrm(ks[8], (N_EVEN, D_MODEL, EVEN_IN), D_MODEL ** -0.5),
        "even_b_in": nrm(ks[9], (N_EVEN, EVEN_IN), 0.02),
        "a_sinks": nrm(ks[10], (N_EVEN, A_HEADS), 1.0),
        "b_conv_w": nrm(ks[11], (N_EVEN, CONV_WIDTH, B_WIDTH), CONV_WIDTH ** -0.5),
        "even_w_out": nrm(ks[12], (N_EVEN, EVEN_MIX, D_MODEL), EVEN_MIX ** -0.5 * DEEPNORM_BETA),
        "even_b_out": nrm(ks[13], (N_EVEN, D_MODEL), 0.02),
        "odd_w_in": nrm(ks[14], (N_ODD, D_MODEL, ODD_IN), D_MODEL ** -0.5),
        "odd_b_in": nrm(ks[15], (N_ODD, ODD_IN), 0.02),
        "c_w_group": nrm(ks[16], (N_ODD, C_GROUPS, C_GROUP_WIDTH, C_GROUP_WIDTH), C_GROUP_WIDTH ** -0.5),
        "c_scale": 1.0 + nrm(ks[17], (N_ODD, C_WIDTH), 0.1),
        "odd_w_out": nrm(ks[18], (N_ODD, ODD_MIX, D_MODEL), ODD_MIX ** -0.5 * DEEPNORM_BETA),
        "odd_b_out": nrm(ks[19], (N_ODD, D_MODEL), 0.02),
        "mlp_w1": nrm(ks[20], (DEPTH, D_MODEL, D_MLP), D_MODEL ** -0.5),
        "mlp_w2": nrm(ks[21], (DEPTH, D_MLP, D_MODEL), D_MLP ** -0.5 * DEEPNORM_BETA),
        "ln1_g": 1.0 + nrm(ks[22], (DEPTH, D_MODEL), 0.05),
        "ln1_b": nrm(ks[23], (DEPTH, D_MODEL), 0.02),
        "ln2_g": 1.0 + nrm(ks[24], (DEPTH, D_MODEL), 0.05),
        "ln2_b": nrm(ks[25], (DEPTH, D_MODEL), 0.02),
    }


def reference(x_prompt, x_sample, cache_a_k, cache_a_v, state_b_conv, state_c_pool, cache_d_k, cache_d_v,
              even_w_in, even_b_in, a_sinks, b_conv_w, even_w_out, even_b_out,
              odd_w_in, odd_b_in, c_w_group, c_scale, odd_w_out, odd_b_out,
              mlp_w1, mlp_w2, ln1_g, ln1_b, ln2_g, ln2_b):
    weights = (even_w_in, even_b_in, a_sinks, b_conv_w, even_w_out, even_b_out,
               odd_w_in, odd_b_in, c_w_group, c_scale, odd_w_out, odd_b_out,
               mlp_w1, mlp_w2, ln1_g, ln1_b, ln2_g, ln2_b)
    pos_prompt = jnp.arange(x_prompt.shape[1])
    pos_sample = PAST_LEN + jnp.arange(x_sample.shape[1])
    y_prompt, ak_p, av_p, bc_p, cp_p, dk_p, dv_p = trunk(x_prompt, pos_prompt, None, *weights)
    sample_caches = (cache_a_k, cache_a_v, state_b_conv, state_c_pool, cache_d_k, cache_d_v)
    y_sample, ak_s, av_s, bc_s, cp_s, dk_s, dv_s = trunk(x_sample, pos_sample, sample_caches, *weights)
    return (y_prompt, y_sample, ak_p, av_p, bc_p, cp_p, dk_p, dv_p, ak_s, av_s, bc_s, cp_s, dk_s, dv_s)
```

```python
import functools

import numpy as np
import jax
import jax.numpy as jnp
from jax import lax
from jax.experimental import pallas as pl
from jax.experimental.pallas import tpu as pltpu

HEAD_DIM = 64
N_HEADS = 8
A_KV_HEADS = 2
A_WINDOW = 128
D_BRANCHES = ((128, 1), (512, 4), (2048, 16))
CONV_WIDTH = 3
POOL_WINDOWS = (2, 4, 8, 16)
POOL_MAX = 16
DEPTH = 2
PAST_LEN = 16384
DEEPNORM_ALPHA = (2 * DEPTH) ** 0.25
LN_EPS = 1e-5

MIX_W = N_HEADS * HEAD_DIM
LANES = 128
N_PAIRS = MIX_W // LANES
BAND = 128
HALO = 16
EXT0 = 24
VMEM_LIMIT = 56 * 1024 * 1024

A_HEAD_ORDER = (0, 4, 1, 5, 2, 6, 3, 7)
A_PAIR_HEADS = tuple((p, p + 4) for p in range(N_PAIRS))
D_PAIR_HEADS = tuple((2 * p, 2 * p + 1) for p in range(N_PAIRS))

_BF = jnp.bfloat16
_F32 = jnp.float32
_NEG_INF = float("-inf")


def _alibi_slopes(n_heads):
    return 2.0 ** (-8.0 * np.arange(1, n_heads + 1) / n_heads)


def _cparams(n_axes):
    return pltpu.CompilerParams(dimension_semantics=("arbitrary",) * n_axes, vmem_limit_bytes=VMEM_LIMIT)


def _const_spec(shape):
    nd = len(shape)
    return pl.BlockSpec(shape, lambda *_: (0,) * nd)


def _layer_norm(y, g, b):
    mu = jnp.mean(y, axis=-1, keepdims=True)
    yc = y - mu
    var = jnp.mean(yc * yc, axis=-1, keepdims=True)
    return yc * lax.rsqrt(var + LN_EPS) * g + b


def _proj_kernel(x_ref, w_ref, b_ref, o_ref, *, tn):
    x = x_ref[...].astype(_BF)
    for j in range(o_ref.shape[1] // tn):
        cols = slice(j * tn, (j + 1) * tn)
        acc = jnp.dot(x, w_ref[:, cols], preferred_element_type=_F32)
        o_ref[:, cols] = (acc + b_ref[:, cols]).astype(o_ref.dtype)


def _proj(x, w, b, out_dtype, tm):
    n, k = x.shape
    m = w.shape[1]
    return pl.pallas_call(
        functools.partial(_proj_kernel, tn=256),
        grid=(n // tm,),
        in_specs=[pl.BlockSpec((tm, k), lambda i: (i, 0)), _const_spec((k, m)), _const_spec((1, m))],
        out_specs=pl.BlockSpec((tm, m), lambda i: (i, 0)),
        out_shape=jax.ShapeDtypeStruct((n, m), out_dtype),
        compiler_params=_cparams(1),
        name="proj",
    )(x, w, b)


def _mlp_ln_kernel(x_ref, w1_ref, w2_ref, g_ref, b_ref, o_ref, *, th):
    x = x_ref[...]
    xb = x.astype(_BF)
    acc = jnp.zeros(x.shape, _F32)
    for c in range(w1_ref.shape[1] // th):
        h = jnp.dot(xb, w1_ref[:, c * th:(c + 1) * th], preferred_element_type=_F32)
        h = jnp.square(jnp.maximum(h, 0.0)).astype(_BF)
        acc = acc + jnp.dot(h, w2_ref[c * th:(c + 1) * th, :], preferred_element_type=_F32)
    o_ref[...] = _layer_norm(DEEPNORM_ALPHA * x + acc, g_ref[...], b_ref[...])


def _mlp_ln(x, w1, w2, g, b, tm):
    n, dm = x.shape
    dh = w1.shape[1]
    return pl.pallas_call(
        functools.partial(_mlp_ln_kernel, th=512),
        grid=(n // tm,),
        in_specs=[pl.BlockSpec((tm, dm), lambda i: (i, 0)), _const_spec((dm, dh)), _const_spec((dh, dm)),
                  _const_spec((1, dm)), _const_spec((1, dm))],
        out_specs=pl.BlockSpec((tm, dm), lambda i: (i, 0)),
        out_shape=jax.ShapeDtypeStruct((n, dm), _F32),
        compiler_params=_cparams(1),
        name="mlp_ln",
    )(x, w1, w2, g, b)


def _lane_is_left():
    return lax.broadcasted_iota(jnp.int32, (1, LANES), 1) < HEAD_DIM


def _finish_mix(x_ref, left, right, wo_ref, bo_ref, g_ref, b_ref, o_ref):
    mix = jnp.dot(left, wo_ref[:MIX_W, :], preferred_element_type=_F32)
    mix = mix + jnp.dot(right, wo_ref[MIX_W:, :], preferred_element_type=_F32) + bo_ref[...]
    o_ref[...] = _layer_norm(DEEPNORM_ALPHA * x_ref[...] + mix, g_ref[...], b_ref[...])


def _even_out_kernel(x_ref, ya_ref, yb_ref, wo_ref, bo_ref, g_ref, b_ref, o_ref):
    _finish_mix(x_ref, ya_ref[...], yb_ref[...], wo_ref, bo_ref, g_ref, b_ref, o_ref)


def _odd_out_kernel(x_ref, pooled_ref, wg_ref, scale_ref, o1_ref, o2_ref, o3_ref, s1_ref, s2_ref, s3_ref,
                    wo_ref, bo_ref, g_ref, b_ref, o_ref):
    yc = jnp.dot(pooled_ref[...], wg_ref[...], preferred_element_type=_F32) * scale_ref[...]
    lses = (s1_ref[...], s2_ref[...], s3_ref[...])
    top = jnp.maximum(jnp.maximum(lses[0], lses[1]), lses[2])
    es = [jnp.exp(s - top) for s in lses]
    den = es[0] + es[1] + es[2]
    cs = [e / den for e in es]
    outs = (o1_ref, o2_ref, o3_ref)
    tm = x_ref.shape[0]
    left = _lane_is_left()
    tiles = []
    for p, (ha, hb) in enumerate(D_PAIR_HEADS):
        acc = None
        for c, o_ref_b in zip(cs, outs):
            wa = jnp.broadcast_to(c[:, ha:ha + 1], (tm, LANES))
            wb = jnp.broadcast_to(c[:, hb:hb + 1], (tm, LANES))
            term = jnp.where(left, wa, wb) * o_ref_b[:, p * LANES:(p + 1) * LANES].astype(_F32)
            acc = term if acc is None else acc + term
        tiles.append(acc)
    yd = jnp.concatenate(tiles, axis=1)
    _finish_mix(x_ref, yc.astype(_BF), yd.astype(_BF), wo_ref, bo_ref, g_ref, b_ref, o_ref)


def _row_spec(tm, width):
    return pl.BlockSpec((tm, width), lambda i: (i, 0))


def _even_out(x, ya, yb, wo, bo, g, b, tm):
    n, dm = x.shape
    return pl.pallas_call(
        _even_out_kernel,
        grid=(n // tm,),
        in_specs=[_row_spec(tm, dm), _row_spec(tm, MIX_W), _row_spec(tm, MIX_W), _const_spec(wo.shape),
                  _const_spec((1, dm)), _const_spec((1, dm)), _const_spec((1, dm))],
        out_specs=_row_spec(tm, dm),
        out_shape=jax.ShapeDtypeStruct((n, dm), _F32),
        compiler_params=_cparams(1),
        name="even_out",
    )(x, ya, yb, wo, bo, g, b)


def _odd_out(x, pooled, wg, scale, outs, stats, wo, bo, g, b, tm):
    n, dm = x.shape
    return pl.pallas_call(
        _odd_out_kernel,
        grid=(n // tm,),
        in_specs=[_row_spec(tm, dm), _row_spec(tm, MIX_W), _const_spec(wg.shape), _const_spec((1, MIX_W))]
        + [_row_spec(tm, MIX_W)] * 3 + [_row_spec(tm, LANES)] * 3
        + [_const_spec(wo.shape), _const_spec((1, dm)), _const_spec((1, dm)), _const_spec((1, dm))],
        out_specs=_row_spec(tm, dm),
        out_shape=jax.ShapeDtypeStruct((n, dm), _F32),
        compiler_params=_cparams(1),
        name="odd_out",
    )(x, pooled, wg, scale, *outs, *stats, wo, bo, g, b)


def _fill_ext(ext_ref, hist, cur, t):
    nb = ext_ref.shape[0]
    ext_ref[:, 0:8, :] = jnp.zeros((nb, 8, MIX_W), _F32)
    ext_ref[:, 8:EXT0, :] = hist
    ext_ref[:, EXT0:EXT0 + t, :] = cur


def _conv_body(h, gb, gc, c_hist, w_ref, ext_ref, yb_ref, ctail_ref):
    t = h.shape[1]
    c = gc * h
    _fill_ext(ext_ref, c_hist, c, t)
    conv = ext_ref[:, EXT0 - 2:EXT0 - 2 + t, :] * w_ref[0:1, :]
    conv = conv + ext_ref[:, EXT0 - 1:EXT0 - 1 + t, :] * w_ref[1:2, :]
    conv = conv + c * w_ref[2:3, :]
    yb_ref[...] = (gb * conv).astype(yb_ref.dtype)
    ctail_ref[...] = ext_ref[:, EXT0 + t - 8:EXT0 + t, :]


def _prompt_conv_kernel(h_ref, gb_ref, gc_ref, hp_ref, gcp_ref, w_ref, yb_ref, ctail_ref, ext_ref):
    c_hist = jnp.where(pl.program_id(1) > 0, hp_ref[...].astype(_F32) * gcp_ref[...].astype(_F32), 0.0)
    _conv_body(h_ref[...].astype(_F32), gb_ref[...].astype(_F32), gc_ref[...].astype(_F32), c_hist,
               w_ref, ext_ref, yb_ref, ctail_ref)


def _sample_conv_kernel(h_ref, gb_ref, gc_ref, hist_ref, w_ref, yb_ref, ctail_ref, ext_ref):
    _conv_body(h_ref[...], gb_ref[...], gc_ref[...], hist_ref[...], w_ref, ext_ref, yb_ref, ctail_ref)


def _pool_body(u, hist, pos0, ext_ref, s2_ref, s4_ref, s8_ref, out_ref):
    nb, t, _ = u.shape
    _fill_ext(ext_ref, hist, u, t)
    hi = EXT0 + t
    zeros8 = jnp.zeros((nb, 8, MIX_W), _F32)
    s2_ref[:, 0:8, :] = zeros8
    s4_ref[:, 0:8, :] = zeros8
    s8_ref[:, 0:8, :] = zeros8
    s2_ref[:, 8:hi, :] = ext_ref[:, 8:hi, :] + ext_ref[:, 7:hi - 1, :]
    s4_ref[:, 8:hi, :] = s2_ref[:, 8:hi, :] + s2_ref[:, 6:hi - 2, :]
    s8_ref[:, 8:hi, :] = s4_ref[:, 8:hi, :] + s4_ref[:, 4:hi - 4, :]
    sums = (
        s2_ref[:, EXT0:hi, 0:LANES],
        s4_ref[:, EXT0:hi, LANES:2 * LANES],
        s8_ref[:, EXT0:hi, 2 * LANES:3 * LANES],
        s8_ref[:, EXT0:hi, 3 * LANES:] + s8_ref[:, EXT0 - 8:hi - 8, 3 * LANES:],
    )
    pos = (pos0 + lax.broadcasted_iota(jnp.int32, (1, t, LANES), 1) + 1).astype(_F32)
    tiles = []
    for g, (w, s) in enumerate(zip(POOL_WINDOWS, sums)):
        cnt = jnp.minimum(pos, float(w))
        tiles.append(s / cnt - u[:, :, g * LANES:(g + 1) * LANES])
    out_ref[...] = jnp.concatenate(tiles, axis=2).astype(out_ref.dtype)


def _prompt_pool_kernel(u_ref, up_ref, out_ref, ext_ref, s2_ref, s4_ref, s8_ref, *, tm):
    i = pl.program_id(1)
    hist = jnp.where(i > 0, up_ref[...].astype(_F32), 0.0)
    _pool_body(u_ref[...].astype(_F32), hist, i * tm, ext_ref, s2_ref, s4_ref, s8_ref, out_ref)


def _sample_pool_kernel(u_ref, hist_ref, out_ref, ext_ref, s2_ref, s4_ref, s8_ref):
    _pool_body(u_ref[...], hist_ref[...], PAST_LEN, ext_ref, s2_ref, s4_ref, s8_ref, out_ref)


def _prompt_conv(p3, conv_w, tm):
    bsz, s, _ = p3.shape
    col = lambda c: pl.BlockSpec((1, tm, MIX_W), lambda b, i: (b, i, c))
    prev = lambda c: pl.BlockSpec((1, HALO, MIX_W), lambda b, i: (b, jnp.maximum(i * (tm // HALO) - 1, 0), c))
    return pl.pallas_call(
        _prompt_conv_kernel,
        grid=(bsz, s // tm),
        in_specs=[col(1), col(2), col(3), prev(1), prev(3), _const_spec(conv_w.shape)],
        out_specs=[pl.BlockSpec((1, tm, MIX_W), lambda b, i: (b, i, 0)),
                   pl.BlockSpec((1, 8, MIX_W), lambda b, i: (b, i, 0))],
        out_shape=[jax.ShapeDtypeStruct((bsz, s, MIX_W), _BF),
                   jax.ShapeDtypeStruct((bsz, (s // tm) * 8, MIX_W), _F32)],
        scratch_shapes=[pltpu.VMEM((1, EXT0 + tm, MIX_W), _F32)],
        compiler_params=_cparams(2),
        name="prompt_conv",
    )(p3, p3, p3, p3, p3, conv_w)


def _sample_conv(p3, hist, conv_w, bt):
    bsz, t, _ = p3.shape
    col = lambda c: pl.BlockSpec((bt, t, MIX_W), lambda b: (b, 0, c))
    return pl.pallas_call(
        _sample_conv_kernel,
        grid=(bsz // bt,),
        in_specs=[col(1), col(2), col(3), pl.BlockSpec((bt, HALO, MIX_W), lambda b: (b, 0, 0)),
                  _const_spec(conv_w.shape)],
        out_specs=[pl.BlockSpec((bt, t, MIX_W), lambda b: (b, 0, 0)),
                   pl.BlockSpec((bt, 8, MIX_W), lambda b: (b, 0, 0))],
        out_shape=[jax.ShapeDtypeStruct((bsz, t, MIX_W), _BF), jax.ShapeDtypeStruct((bsz, 8, MIX_W), _F32)],
        scratch_shapes=[pltpu.VMEM((bt, EXT0 + t, MIX_W), _F32)],
        compiler_params=_cparams(1),
        name="sample_conv",
    )(p3, p3, p3, hist, conv_w)


def _prompt_pool(p3, tm):
    bsz, s, _ = p3.shape
    scratch = pltpu.VMEM((1, EXT0 + tm, MIX_W), _F32)
    return pl.pallas_call(
        functools.partial(_prompt_pool_kernel, tm=tm),
        grid=(bsz, s // tm),
        in_specs=[pl.BlockSpec((1, tm, MIX_W), lambda b, i: (b, i, 0)),
                  pl.BlockSpec((1, HALO, MIX_W), lambda b, i: (b, jnp.maximum(i * (tm // HALO) - 1, 0), 0))],
        out_specs=pl.BlockSpec((1, tm, MIX_W), lambda b, i: (b, i, 0)),
        out_shape=jax.ShapeDtypeStruct((bsz, s, MIX_W), _BF),
        scratch_shapes=[scratch] * 4,
        compiler_params=_cparams(2),
        name="prompt_pool",
    )(p3, p3)


def _sample_pool(p3, hist, bt):
    bsz, t, _ = p3.shape
    scratch = pltpu.VMEM((bt, EXT0 + t, MIX_W), _F32)
    return pl.pallas_call(
        _sample_pool_kernel,
        grid=(bsz // bt,),
        in_specs=[pl.BlockSpec((bt, t, MIX_W), lambda b: (b, 0, 0)),
                  pl.BlockSpec((bt, HALO, MIX_W), lambda b: (b, 0, 0))],
        out_specs=pl.BlockSpec((bt, t, MIX_W), lambda b: (b, 0, 0)),
        out_shape=jax.ShapeDtypeStruct((bsz, t, MIX_W), _BF),
        scratch_shapes=[scratch] * 4,
        compiler_params=_cparams(1),
        name="sample_pool",
    )(p3, hist)


def _split_heads(q_pair):
    left = _lane_is_left()
    zero = jnp.zeros_like(q_pair)
    return jnp.concatenate([jnp.where(left, q_pair, zero), jnp.where(left, zero, q_pair)], axis=0)


def _band_attn_kernel(*refs, tq, kv_lane, has_sink, want_stat):
    refs = list(refs)
    q_ref, kc_ref, kp_ref, vc_ref, vp_ref, bias_ref = refs[:6]
    rest = refs[6:]
    sink_ref = rest.pop(0) if has_sink else None
    o_ref = rest.pop(0)
    st_ref = rest.pop(0) if want_stat else None

    first = pl.program_id(2) == 0
    left = _lane_is_left()
    lane = lax.broadcasted_iota(jnp.int32, (1, LANES), 1)
    prev_cols = lax.broadcasted_iota(jnp.int32, (1, 2 * BAND), 1) < BAND
    top_rows = lax.broadcasted_iota(jnp.int32, (2 * BAND, 1), 0) < BAND
    ones = jnp.ones((2 * BAND, LANES), _BF)

    for j in range(tq // BAND):
        rows = slice(j * BAND, (j + 1) * BAND)
        stat = jnp.zeros((BAND, LANES), _F32)
        for p in range(N_PAIRS):
            kl = slice(kv_lane[p], kv_lane[p] + LANES)
            if j == 0:
                k_prev, v_prev = kp_ref[0, :, kl], vp_ref[0, :, kl]
            else:
                k_prev, v_prev = kc_ref[0, (j - 1) * BAND:j * BAND, kl], vc_ref[0, (j - 1) * BAND:j * BAND, kl]
            k2 = jnp.concatenate([k_prev, kc_ref[0, rows, kl]], axis=0)
            v2 = jnp.concatenate([v_prev, vc_ref[0, rows, kl]], axis=0)
            q2 = _split_heads(q_ref[0, rows, p * LANES:(p + 1) * LANES])
            s = lax.dot_general(q2, k2, (((1,), (1,)), ((), ())), preferred_element_type=_F32)
            s = s + bias_ref[p]
            if j == 0:
                s = jnp.where(jnp.logical_and(first, prev_cols), _NEG_INF, s)
            m = jnp.max(s, axis=1, keepdims=True)
            prob = jnp.exp(s - m).astype(_BF)
            r = jnp.dot(prob, jnp.concatenate([v2, ones], axis=1), preferred_element_type=_F32)
            pv, l = r[:, :LANES], r[:, LANES:]
            if has_sink:
                ha, hb = A_PAIR_HEADS[p]
                sink = jnp.where(top_rows, sink_ref[ha], sink_ref[hb])
                m2 = jnp.maximum(m, sink)
                a = jnp.exp(m - m2)
                o = pv * a / (l * a + jnp.exp(sink - m2))
            else:
                o = pv / l
            o_ref[0, rows, p * LANES:(p + 1) * LANES] = jnp.where(left, o[:BAND], o[BAND:]).astype(o_ref.dtype)
            if want_stat:
                lse = m + jnp.log(l)
                ha, hb = D_PAIR_HEADS[p]
                stat = jnp.where(lane == ha, lse[:BAND], stat)
                stat = jnp.where(lane == hb, lse[BAND:], stat)
        if want_stat:
            st_ref[0, rows, :] = stat


def _band_bias(pair_heads, dilation):
    slopes = _alibi_slopes(N_HEADS)
    qi = np.arange(BAND)[:, None]
    kj = np.arange(2 * BAND)[None, :]
    dist = qi + BAND - kj
    valid = (dist >= 0) & (dist <= BAND)
    out = np.empty((len(pair_heads), 2 * BAND, 2 * BAND), np.float32)
    for p, heads in enumerate(pair_heads):
        for half, h in enumerate(heads):
            bias = -np.float32(slopes[h]) * (dist * dilation).astype(np.float32)
            out[p, half * BAND:(half + 1) * BAND] = np.where(valid, bias, -np.inf)
    return jnp.asarray(out)


def _band_attn(arr, bsz, seq, dilation, q_col, k_col, v_col, kv_width, kv_lane, pair_heads, sinks, want_stat):
    width = arr.shape[1]
    n = seq // dilation
    view = arr.reshape(bsz, n, dilation * width)
    tq = min(512, n)
    sub = tq // BAND
    assert q_col % MIX_W == 0 and k_col % kv_width == 0 and v_col % kv_width == 0
    assert dilation == 1 or (width % MIX_W == 0 and width % kv_width == 0)

    def cur(col, w):
        return pl.BlockSpec((1, tq, w), lambda b, r, i: (b, i, (r * width + col) // w))

    def prev(col, w):
        return pl.BlockSpec((1, BAND, w), lambda b, r, i: (b, jnp.maximum(i * sub - 1, 0), (r * width + col) // w))

    in_specs = [cur(q_col, MIX_W), cur(k_col, kv_width), prev(k_col, kv_width), cur(v_col, kv_width),
                prev(v_col, kv_width), _const_spec((N_PAIRS, 2 * BAND, 2 * BAND))]
    args = [view, view, view, view, view, _band_bias(pair_heads, dilation)]
    if sinks is not None:
        in_specs.append(pl.BlockSpec(memory_space=pltpu.SMEM))
        args.append(sinks)
    out_specs = [pl.BlockSpec((1, tq, MIX_W), lambda b, r, i: (b, i, r))]
    out_shape = [jax.ShapeDtypeStruct((bsz, n, dilation * MIX_W), _BF)]
    if want_stat:
        out_specs.append(pl.BlockSpec((1, tq, LANES), lambda b, r, i: (b, i, r)))
        out_shape.append(jax.ShapeDtypeStruct((bsz, n, dilation * LANES), _F32))
    res = pl.pallas_call(
        functools.partial(_band_attn_kernel, tq=tq, kv_lane=kv_lane, has_sink=sinks is not None,
                          want_stat=want_stat),
        grid=(bsz, dilation, n // tq),
        in_specs=in_specs,
        out_specs=out_specs,
        out_shape=out_shape,
        compiler_params=_cparams(3),
        name=f"band_attn_d{dilation}",
    )(*args)
    o = res[0].reshape(bsz * seq, MIX_W)
    if want_stat:
        return o, res[1].reshape(bsz * seq, LANES)
    return o


def _sample_attn_kernel(*refs, n_hist, t_new, kv_lane, pair_heads, has_sink):
    refs = list(refs)
    q_ref, kn_ref, vn_ref, kc_ref, vc_ref, bias_ref, mult_ref = refs[:7]
    rest = refs[7:]
    sink_ref = rest.pop(0) if has_sink else None
    y_ref, ko_ref, vo_ref, kb_ref, vb_ref = rest

    kw = kc_ref.shape[2]
    pad = jnp.zeros((kb_ref.shape[0] - n_hist - t_new, kw), _BF)
    for src_c, src_n, dst in ((kc_ref, kn_ref, kb_ref), (vc_ref, vn_ref, vb_ref)):
        dst[0:n_hist, :] = src_c[0].astype(_BF)
        dst[n_hist:, :] = jnp.concatenate([src_n[0].astype(_BF), pad], axis=0)
    for src_c, src_n, dst in ((kc_ref, kn_ref, ko_ref), (vc_ref, vn_ref, vo_ref)):
        dst[0, 0:n_hist - t_new, :] = src_c[0, t_new:n_hist, :]
        dst[0, n_hist - t_new:n_hist, :] = src_n[0]

    left = _lane_is_left()
    top_rows = lax.broadcasted_iota(jnp.int32, (2 * t_new, 1), 0) < t_new
    mult = mult_ref[...]
    for p in range(N_PAIRS):
        kl = slice(kv_lane[p], kv_lane[p] + LANES)
        q2 = _split_heads(q_ref[0, :, p * LANES:(p + 1) * LANES].astype(_BF))
        s = lax.dot_general(q2, kb_ref[:, kl], (((1,), (1,)), ((), ())), preferred_element_type=_F32)
        s = s + bias_ref[p]
        m = jnp.max(s, axis=1, keepdims=True)
        prob = (mult * jnp.exp(s - m)).astype(_BF)
        l = jnp.sum(prob.astype(_F32), axis=1, keepdims=True)
        pv = jnp.dot(prob, vb_ref[:, kl], preferred_element_type=_F32)
        if has_sink:
            ha, hb = pair_heads[p]
            sink = jnp.where(top_rows, sink_ref[ha], sink_ref[hb])
            m2 = jnp.maximum(m, sink)
            a = jnp.exp(m - m2)
            o = pv * a / (l * a + jnp.exp(sink - m2))
        else:
            o = pv / l
        y_ref[0, :, p * LANES:(p + 1) * LANES] = jnp.where(left, o[:t_new], o[t_new:]).astype(y_ref.dtype)


def _sample_tables(pair_heads, branches, n_hist, t_new, n_keys):
    slopes = _alibi_slopes(N_HEADS)
    delta = (n_hist + np.arange(t_new))[:, None] - np.arange(n_keys)[None, :]
    mult = np.zeros((t_new, n_keys), np.float32)
    for window, dil in branches:
        mult += ((delta >= 0) & (delta % dil == 0) & (delta <= window)).astype(np.float32)
    bias = np.empty((len(pair_heads), 2 * t_new, n_keys), np.float32)
    for p, heads in enumerate(pair_heads):
        for half, h in enumerate(heads):
            b = -np.float32(slopes[h]) * delta.astype(np.float32)
            bias[p, half * t_new:(half + 1) * t_new] = np.where(mult > 0, b, -np.inf)
    return jnp.asarray(bias), jnp.asarray(np.concatenate([mult, mult], axis=0))


def _sample_attn(p3, q_col, k_col, v_col, kv_width, kv_lane, pair_heads, branches, k_cache, v_cache, sinks):
    bsz, t_new, _ = p3.shape
    n_hist = k_cache.shape[1]
    n_keys = n_hist + 16
    bias, mult = _sample_tables(pair_heads, branches, n_hist, t_new, n_keys)
    new = lambda col, w: pl.BlockSpec((1, t_new, w), lambda b: (b, 0, col // w))
    cache = pl.BlockSpec((1, n_hist, kv_width), lambda b: (b, 0, 0))
    in_specs = [new(q_col, MIX_W), new(k_col, kv_width), new(v_col, kv_width), cache, cache,
                _const_spec(bias.shape), _const_spec(mult.shape)]
    args = [p3, p3, p3, k_cache, v_cache, bias, mult]
    if sinks is not None:
        in_specs.append(pl.BlockSpec(memory_space=pltpu.SMEM))
        args.append(sinks)
    return pl.pallas_call(
        functools.partial(_sample_attn_kernel, n_hist=n_hist, t_new=t_new, kv_lane=kv_lane,
                          pair_heads=pair_heads, has_sink=sinks is not None),
        grid=(bsz,),
        in_specs=in_specs,
        out_specs=[pl.BlockSpec((1, t_new, MIX_W), lambda b: (b, 0, 0)), cache, cache],
        out_shape=[jax.ShapeDtypeStruct((bsz, t_new, MIX_W), _BF),
                   jax.ShapeDtypeStruct(k_cache.shape, _F32), jax.ShapeDtypeStruct(v_cache.shape, _F32)],
        scratch_shapes=[pltpu.VMEM((n_keys, kv_width), _BF)] * 2,
        compiler_params=_cparams(1),
        name=f"sample_attn_{n_hist}",
    )(*args)


A_Q = N_HEADS * HEAD_DIM
A_KV = A_KV_HEADS * HEAD_DIM
E_Q, E_H, E_GB, E_GC, E_K, E_V = 0, 512, 1024, 1536, 2048, 2176
O_U, O_Q, O_K, O_V = 0, 512, 1024, 1536


def _prep_layer_weights(even_w_in, even_b_in, even_w_out, odd_w_in, odd_b_in, c_w_group):
    q_cols = np.concatenate([h * HEAD_DIM + np.arange(HEAD_DIM) for h in A_HEAD_ORDER])
    o1, o2, o3 = A_Q, A_Q + A_KV, A_Q + 2 * A_KV
    order = np.concatenate([q_cols, np.arange(o3, o3 + 3 * MIX_W), np.arange(o1, o3)])
    scale = np.ones((order.size,), np.float32)
    scale[:A_Q] = HEAD_DIM ** -0.5
    ew = (even_w_in[:, order] * scale).astype(_BF)
    eb = (even_b_in[order] * scale)[None, :]
    ewo = jnp.concatenate([even_w_out[q_cols], even_w_out[A_Q:]], axis=0).astype(_BF)
    oscale = np.ones((odd_w_in.shape[1],), np.float32)
    oscale[O_Q:O_K] = HEAD_DIM ** -0.5
    ow = (odd_w_in * oscale).astype(_BF)
    ob = (odd_b_in * oscale)[None, :]
    groups, gw, _ = c_w_group.shape
    wg = jnp.zeros((MIX_W, MIX_W), _F32)
    for g in range(groups):
        wg = wg.at[g * gw:(g + 1) * gw, g * gw:(g + 1) * gw].set(c_w_group[g])
    return ew, eb, ewo, ow, ob, wg.astype(_BF)


def _row(v):
    return v[None, :]


def _trunk(x, caches, wts, tm):
    (even_w_in, even_b_in, a_sinks, b_conv_w, even_w_out, even_b_out, odd_w_in, odd_b_in, c_w_group, c_scale,
     odd_w_out, odd_b_out, mlp_w1, mlp_w2, ln1_g, ln1_b, ln2_g, ln2_b) = wts
    bsz, seq, dm = x.shape
    n = bsz * seq
    ew, eb, ewo, ow, ob, wg = _prep_layer_weights(even_w_in[0], even_b_in[0], even_w_out[0], odd_w_in[0],
                                                  odd_b_in[0], c_w_group[0])
    owo = odd_w_out[0].astype(_BF)
    w1 = mlp_w1.astype(_BF)
    w2 = mlp_w2.astype(_BF)
    prompt = caches is None
    x = x.reshape(n, dm)
    a_kv_lane = (0,) * N_PAIRS
    d_kv_lane = tuple(p * LANES for p in range(N_PAIRS))

    if prompt:
        pe = _proj(x, ew, eb, _BF, tm)
        ya = _band_attn(pe, bsz, seq, 1, E_Q, E_K, E_V, LANES, a_kv_lane, A_PAIR_HEADS, a_sinks[0], False)
        pe3 = pe.reshape(bsz, seq, -1)
        yb, ctail = _prompt_conv(pe3, b_conv_w[0], tm)
        n_keep = min(A_WINDOW, seq)
        a_k = pe3[:, seq - n_keep:, E_K:E_V].astype(_F32).reshape(bsz, n_keep, A_KV_HEADS, HEAD_DIM)
        a_v = pe3[:, seq - n_keep:, E_V:].astype(_F32).reshape(bsz, n_keep, A_KV_HEADS, HEAD_DIM)
        b_conv = ctail[:, -(CONV_WIDTH - 1):, :]
    else:
        cache_a_k, cache_a_v, state_b_conv, state_c_pool, cache_d_k, cache_d_v = caches
        pe3 = _proj(x, ew, eb, _F32, tm).reshape(bsz, seq, -1)
        n_hist = cache_a_k.shape[2]
        ya, a_k, a_v = _sample_attn(pe3, E_Q, E_K, E_V, LANES, a_kv_lane, A_PAIR_HEADS, ((A_WINDOW, 1),),
                                    cache_a_k[0].reshape(bsz, n_hist, A_KV), cache_a_v[0].reshape(bsz, n_hist, A_KV),
                                    a_sinks[0])
        a_k = a_k.reshape(bsz, n_hist, A_KV_HEADS, HEAD_DIM)
        a_v = a_v.reshape(bsz, n_hist, A_KV_HEADS, HEAD_DIM)
        c_hist = jnp.pad(state_b_conv[0], ((0, 0), (HALO - (CONV_WIDTH - 1), 0), (0, 0)))
        yb, ctail = _sample_conv(pe3, c_hist, b_conv_w[0], 32)
        b_conv = ctail[:, -(CONV_WIDTH - 1):, :]
    x = _even_out(x, ya.reshape(n, MIX_W), yb.reshape(n, MIX_W), ewo, _row(even_b_out[0]), _row(ln1_g[0]),
                  _row(ln1_b[0]), tm)
    x = _mlp_ln(x, w1[0], w2[0], _row(ln2_g[0]), _row(ln2_b[0]), tm)

    if prompt:
        po = _proj(x, ow, ob, _BF, tm)
        outs, stats = [], []
        for _, dil in D_BRANCHES:
            o, st = _band_attn(po, bsz, seq, dil, O_Q, O_K, O_V, MIX_W, d_kv_lane, D_PAIR_HEADS, None, True)
            outs.append(o)
            stats.append(st)
        po3 = po.reshape(bsz, seq, -1)
        pooled = _prompt_pool(po3, tm).reshape(n, MIX_W)
        n_keep = min(D_BRANCHES[-1][0], seq)
        c_pool = po3[:, seq - (POOL_MAX - 1):, O_U:O_Q].astype(_F32)
        d_k = po3[:, seq - n_keep:, O_K:O_V].astype(_F32).reshape(bsz, n_keep, N_HEADS, HEAD_DIM)
        d_v = po3[:, seq - n_keep:, O_V:].astype(_F32).reshape(bsz, n_keep, N_HEADS, HEAD_DIM)
        x = _odd_out(x, pooled, wg, _row(c_scale[0]), outs, stats, owo, _row(odd_b_out[0]), _row(ln1_g[1]),
                     _row(ln1_b[1]), tm)
    else:
        po3 = _proj(x, ow, ob, _F32, tm).reshape(bsz, seq, -1)
        n_hist = cache_d_k.shape[2]
        yd, d_k, d_v = _sample_attn(po3, O_Q, O_K, O_V, MIX_W, d_kv_lane, D_PAIR_HEADS, D_BRANCHES,
                                    cache_d_k[0].reshape(bsz, n_hist, MIX_W), cache_d_v[0].reshape(bsz, n_hist, MIX_W),
                                    None)
        d_k = d_k.reshape(bsz, n_hist, N_HEADS, HEAD_DIM)
        d_v = d_v.reshape(bsz, n_hist, N_HEADS, HEAD_DIM)
        u_hist = jnp.pad(state_c_pool[0], ((0, 0), (HALO - (POOL_MAX - 1), 0), (0, 0)))
        pooled = _sample_pool(po3, u_hist, 32).reshape(n, MIX_W)
        c_pool = jnp.concatenate([state_c_pool[0], po3[:, :, O_U:O_Q]], axis=1)[:, -(POOL_MAX - 1):]
        x = _sample_odd_out(x, pooled, wg, _row(c_scale[0]), yd.reshape(n, MIX_W), owo, _row(odd_b_out[0]),
                            _row(ln1_g[1]), _row(ln1_b[1]), tm)
    x = _mlp_ln(x, w1[1], w2[1], _row(ln2_g[1]), _row(ln2_b[1]), tm)
    return (x.reshape(bsz, seq, dm), a_k[None], a_v[None], b_conv[None], c_pool[None], d_k[None], d_v[None])


def _sample_odd_out_kernel(x_ref, pooled_ref, wg_ref, scale_ref, yd_ref, wo_ref, bo_ref, g_ref, b_ref, o_ref):
    yc = jnp.dot(pooled_ref[...], wg_ref[...], preferred_element_type=_F32) * scale_ref[...]
    _finish_mix(x_ref, yc.astype(_BF), yd_ref[...], wo_ref, bo_ref, g_ref, b_ref, o_ref)


def _sample_odd_out(x, pooled, wg, scale, yd, wo, bo, g, b, tm):
    n, dm = x.shape
    return pl.pallas_call(
        _sample_odd_out_kernel,
        grid=(n // tm,),
        in_specs=[_row_spec(tm, dm), _row_spec(tm, MIX_W), _const_spec(wg.shape), _const_spec((1, MIX_W)),
                  _row_spec(tm, MIX_W), _const_spec(wo.shape), _const_spec((1, dm)), _const_spec((1, dm)),
                  _const_spec((1, dm))],
        out_specs=_row_spec(tm, dm),
        out_shape=jax.ShapeDtypeStruct((n, dm), _F32),
        compiler_params=_cparams(1),
        name="sample_odd_out",
    )(x, pooled, wg, scale, yd, wo, bo, g, b)


def kernel(x_prompt, x_sample, cache_a_k, cache_a_v, state_b_conv, state_c_pool, cache_d_k, cache_d_v, even_w_in, even_b_in, a_sinks, b_conv_w, even_w_out, even_b_out, odd_w_in, odd_b_in, c_w_group, c_scale, odd_w_out, odd_b_out, mlp_w1, mlp_w2, ln1_g, ln1_b, ln2_g, ln2_b):
    wts = (even_w_in, even_b_in, a_sinks, b_conv_w, even_w_out, even_b_out, odd_w_in, odd_b_in, c_w_group, c_scale,
           odd_w_out, odd_b_out, mlp_w1, mlp_w2, ln1_g, ln1_b, ln2_g, ln2_b)
    caches = (cache_a_k, cache_a_v, state_b_conv, state_c_pool, cache_d_k, cache_d_v)
    y_p, ak_p, av_p, bc_p, cp_p, dk_p, dv_p = _trunk(x_prompt, None, wts, 512)
    y_s, ak_s, av_s, bc_s, cp_s, dk_s, dv_s = _trunk(x_sample, caches, wts, 512)
    return (y_p, y_s, ak_p, av_p, bc_p, cp_p, dk_p, dv_p, ak_s, av_s, bc_s, cp_s, dk_s, dv_s)
```

```python
import functools

import numpy as np
import jax
import jax.numpy as jnp
from jax import lax
from jax.experimental import pallas as pl
from jax.experimental.pallas import tpu as pltpu

HEAD_DIM = 64
N_HEADS = 8
A_KV_HEADS = 2
A_WINDOW = 128
D_BRANCHES = ((128, 1), (512, 4), (2048, 16))
CONV_WIDTH = 3
POOL_WINDOWS = (2, 4, 8, 16)
POOL_MAX = 16
DEPTH = 2
PAST_LEN = 16384
DEEPNORM_ALPHA = (2 * DEPTH) ** 0.25
LN_EPS = 1e-5

MIX_W = N_HEADS * HEAD_DIM
LANES = 128
N_PAIRS = MIX_W // LANES
BAND = 128
HALO = 16
EXT0 = 24
VMEM_LIMIT = 56 * 1024 * 1024

A_HEAD_ORDER = (0, 4, 1, 5, 2, 6, 3, 7)
A_PAIR_HEADS = tuple((p, p + 4) for p in range(N_PAIRS))
D_PAIR_HEADS = tuple((2 * p, 2 * p + 1) for p in range(N_PAIRS))

_BF = jnp.bfloat16
_F32 = jnp.float32
_NEG_INF = float("-inf")


def _alibi_slopes(n_heads):
    return 2.0 ** (-8.0 * np.arange(1, n_heads + 1) / n_heads)


def _cparams(n_axes):
    return pltpu.CompilerParams(dimension_semantics=("arbitrary",) * n_axes, vmem_limit_bytes=VMEM_LIMIT)


def _const_spec(shape):
    nd = len(shape)
    return pl.BlockSpec(shape, lambda *_: (0,) * nd)


def _layer_norm(y, g, b):
    mu = jnp.mean(y, axis=-1, keepdims=True)
    yc = y - mu
    var = jnp.mean(yc * yc, axis=-1, keepdims=True)
    return yc * lax.rsqrt(var + LN_EPS) * g + b


def _proj_kernel(x_ref, w_ref, b_ref, o_ref, *, tn):
    x = x_ref[...].astype(_BF)
    for j in range(o_ref.shape[1] // tn):
        cols = slice(j * tn, (j + 1) * tn)
        acc = jnp.dot(x, w_ref[:, cols], preferred_element_type=_F32)
        o_ref[:, cols] = (acc + b_ref[:, cols]).astype(o_ref.dtype)


def _proj(x, w, b, out_dtype, tm):
    n, k = x.shape
    m = w.shape[1]
    return pl.pallas_call(
        functools.partial(_proj_kernel, tn=256),
        grid=(n // tm,),
        in_specs=[pl.BlockSpec((tm, k), lambda i: (i, 0)), _const_spec((k, m)), _const_spec((1, m))],
        out_specs=pl.BlockSpec((tm, m), lambda i: (i, 0)),
        out_shape=jax.ShapeDtypeStruct((n, m), out_dtype),
        compiler_params=_cparams(1),
        name="proj",
    )(x, w, b)


MLP_CHUNK = 512


def _lane_is_left():
    return lax.broadcasted_iota(jnp.int32, (1, LANES), 1) < HEAD_DIM


def _layer_tail(x_ref, left, right, tail_refs):
    wo_ref, bo_ref, g1_ref, b1_ref, w1_ref, w2_ref, g2_ref, b2_ref, o_ref = tail_refs
    mix = jnp.dot(left, wo_ref[:MIX_W, :], preferred_element_type=_F32)
    mix = mix + jnp.dot(right, wo_ref[MIX_W:, :], preferred_element_type=_F32) + bo_ref[...]
    x = _layer_norm(DEEPNORM_ALPHA * x_ref[...] + mix, g1_ref[...], b1_ref[...])
    xb = x.astype(_BF)
    acc = jnp.zeros(x.shape, _F32)
    for c in range(w1_ref.shape[1] // MLP_CHUNK):
        cols = slice(c * MLP_CHUNK, (c + 1) * MLP_CHUNK)
        h = jnp.dot(xb, w1_ref[:, cols], preferred_element_type=_F32)
        h = jnp.square(jnp.maximum(h, 0.0)).astype(_BF)
        acc = acc + jnp.dot(h, w2_ref[cols, :], preferred_element_type=_F32)
    o_ref[...] = _layer_norm(DEEPNORM_ALPHA * x + acc, g2_ref[...], b2_ref[...])


def _even_tail_kernel(x_ref, ya_ref, yb_ref, *tail_refs):
    _layer_tail(x_ref, ya_ref[...], yb_ref[...], tail_refs)


def _group_c(pooled_ref, wg_ref, scale_ref):
    return (jnp.dot(pooled_ref[...], wg_ref[...], preferred_element_type=_F32) * scale_ref[...]).astype(_BF)


def _sample_odd_tail_kernel(x_ref, pooled_ref, wg_ref, scale_ref, yd_ref, *tail_refs):
    _layer_tail(x_ref, _group_c(pooled_ref, wg_ref, scale_ref), yd_ref[...], tail_refs)


def _odd_tail_kernel(x_ref, pooled_ref, wg_ref, scale_ref, o1_ref, o2_ref, o3_ref, s1_ref, s2_ref, s3_ref,
                     *tail_refs):
    yc = _group_c(pooled_ref, wg_ref, scale_ref)
    lses = (s1_ref[...], s2_ref[...], s3_ref[...])
    top = jnp.maximum(jnp.maximum(lses[0], lses[1]), lses[2])
    es = [jnp.exp(s - top) for s in lses]
    den = es[0] + es[1] + es[2]
    cs = [e / den for e in es]
    outs = (o1_ref, o2_ref, o3_ref)
    tm = x_ref.shape[0]
    left = _lane_is_left()
    tiles = []
    for p, (ha, hb) in enumerate(D_PAIR_HEADS):
        acc = None
        for c, o_ref_b in zip(cs, outs):
            wa = jnp.broadcast_to(c[:, ha:ha + 1], (tm, LANES))
            wb = jnp.broadcast_to(c[:, hb:hb + 1], (tm, LANES))
            term = jnp.where(left, wa, wb) * o_ref_b[:, p * LANES:(p + 1) * LANES].astype(_F32)
            acc = term if acc is None else acc + term
        tiles.append(acc)
    yd = jnp.concatenate(tiles, axis=1)
    _layer_tail(x_ref, yc, yd.astype(_BF), tail_refs)


def _row_spec(tm, width):
    return pl.BlockSpec((tm, width), lambda i: (i, 0))


def _resident_spec(shape):
    nd = len(shape)
    return pl.BlockSpec(shape, lambda *_: (0,) * nd, pipeline_mode=pl.Buffered(1))


def _tail_call(kernel_fn, name, x, mixer_args, mixer_specs, tail_params, tm):
    n, dm = x.shape
    return pl.pallas_call(
        kernel_fn,
        grid=(n // tm,),
        in_specs=[_row_spec(tm, dm)] + mixer_specs + [_resident_spec(p.shape) for p in tail_params],
        out_specs=_row_spec(tm, dm),
        out_shape=jax.ShapeDtypeStruct((n, dm), _F32),
        compiler_params=_cparams(1),
        name=name,
    )(x, *mixer_args, *tail_params)


def _even_tail(x, ya, yb, tail_params, tm):
    return _tail_call(_even_tail_kernel, "even_tail", x, [ya, yb], [_row_spec(tm, MIX_W)] * 2, tail_params, tm)


def _odd_tail(x, pooled, wg, scale, outs, stats, tail_params, tm):
    specs = ([_row_spec(tm, MIX_W), _resident_spec(wg.shape), _resident_spec(scale.shape)]
             + [_row_spec(tm, MIX_W)] * 3 + [_row_spec(tm, LANES)] * 3)
    return _tail_call(_odd_tail_kernel, "odd_tail", x, [pooled, wg, scale, *outs, *stats], specs, tail_params, tm)


def _sample_odd_tail(x, pooled, wg, scale, yd, tail_params, tm):
    specs = [_row_spec(tm, MIX_W), _resident_spec(wg.shape), _resident_spec(scale.shape), _row_spec(tm, MIX_W)]
    return _tail_call(_sample_odd_tail_kernel, "sample_odd_tail", x, [pooled, wg, scale, yd], specs, tail_params, tm)


def _fill_ext(ext_ref, hist, cur, t):
    nb = ext_ref.shape[0]
    ext_ref[:, 0:8, :] = jnp.zeros((nb, 8, MIX_W), _F32)
    ext_ref[:, 8:EXT0, :] = hist
    ext_ref[:, EXT0:EXT0 + t, :] = cur


def _conv_body(h, gb, gc, c_hist, w_ref, ext_ref, yb_ref, ctail_ref):
    t = h.shape[1]
    c = gc * h
    _fill_ext(ext_ref, c_hist, c, t)
    conv = ext_ref[:, EXT0 - 2:EXT0 - 2 + t, :] * w_ref[0:1, :]
    conv = conv + ext_ref[:, EXT0 - 1:EXT0 - 1 + t, :] * w_ref[1:2, :]
    conv = conv + c * w_ref[2:3, :]
    yb_ref[...] = (gb * conv).astype(yb_ref.dtype)
    ctail_ref[...] = ext_ref[:, EXT0 + t - 8:EXT0 + t, :]


def _prompt_conv_kernel(h_ref, gb_ref, gc_ref, hp_ref, gcp_ref, w_ref, yb_ref, ctail_ref, ext_ref):
    c_hist = jnp.where(pl.program_id(1) > 0, hp_ref[...].astype(_F32) * gcp_ref[...].astype(_F32), 0.0)
    _conv_body(h_ref[...].astype(_F32), gb_ref[...].astype(_F32), gc_ref[...].astype(_F32), c_hist,
               w_ref, ext_ref, yb_ref, ctail_ref)


def _sample_conv_kernel(h_ref, gb_ref, gc_ref, hist_ref, w_ref, yb_ref, ctail_ref, ext_ref):
    _conv_body(h_ref[...], gb_ref[...], gc_ref[...], hist_ref[...], w_ref, ext_ref, yb_ref, ctail_ref)


def _pool_body(u, hist, pos0, ext_ref, s2_ref, s4_ref, s8_ref, out_ref):
    nb, t, _ = u.shape
    _fill_ext(ext_ref, hist, u, t)
    hi = EXT0 + t
    zeros8 = jnp.zeros((nb, 8, MIX_W), _F32)
    s2_ref[:, 0:8, :] = zeros8
    s4_ref[:, 0:8, :] = zeros8
    s8_ref[:, 0:8, :] = zeros8
    s2_ref[:, 8:hi, :] = ext_ref[:, 8:hi, :] + ext_ref[:, 7:hi - 1, :]
    s4_ref[:, 8:hi, :] = s2_ref[:, 8:hi, :] + s2_ref[:, 6:hi - 2, :]
    s8_ref[:, 8:hi, :] = s4_ref[:, 8:hi, :] + s4_ref[:, 4:hi - 4, :]
    sums = (
        s2_ref[:, EXT0:hi, 0:LANES],
        s4_ref[:, EXT0:hi, LANES:2 * LANES],
        s8_ref[:, EXT0:hi, 2 * LANES:3 * LANES],
        s8_ref[:, EXT0:hi, 3 * LANES:] + s8_ref[:, EXT0 - 8:hi - 8, 3 * LANES:],
    )
    pos = (pos0 + lax.broadcasted_iota(jnp.int32, (1, t, LANES), 1) + 1).astype(_F32)
    tiles = []
    for g, (w, s) in enumerate(zip(POOL_WINDOWS, sums)):
        cnt = jnp.minimum(pos, float(w))
        tiles.append(s / cnt - u[:, :, g * LANES:(g + 1) * LANES])
    out_ref[...] = jnp.concatenate(tiles, axis=2).astype(out_ref.dtype)


def _prompt_pool_kernel(u_ref, up_ref, out_ref, ext_ref, s2_ref, s4_ref, s8_ref, *, tm):
    i = pl.program_id(1)
    hist = jnp.where(i > 0, up_ref[...].astype(_F32), 0.0)
    _pool_body(u_ref[...].astype(_F32), hist, i * tm, ext_ref, s2_ref, s4_ref, s8_ref, out_ref)


def _sample_pool_kernel(u_ref, hist_ref, out_ref, ext_ref, s2_ref, s4_ref, s8_ref):
    _pool_body(u_ref[...], hist_ref[...], PAST_LEN, ext_ref, s2_ref, s4_ref, s8_ref, out_ref)


def _prompt_conv(p3, conv_w, tm):
    bsz, s, _ = p3.shape
    col = lambda c: pl.BlockSpec((1, tm, MIX_W), lambda b, i: (b, i, c))
    prev = lambda c: pl.BlockSpec((1, HALO, MIX_W), lambda b, i: (b, jnp.maximum(i * (tm // HALO) - 1, 0), c))
    return pl.pallas_call(
        _prompt_conv_kernel,
        grid=(bsz, s // tm),
        in_specs=[col(1), col(2), col(3), prev(1), prev(3), _const_spec(conv_w.shape)],
        out_specs=[pl.BlockSpec((1, tm, MIX_W), lambda b, i: (b, i, 0)),
                   pl.BlockSpec((1, 8, MIX_W), lambda b, i: (b, i, 0))],
        out_shape=[jax.ShapeDtypeStruct((bsz, s, MIX_W), _BF),
                   jax.ShapeDtypeStruct((bsz, (s // tm) * 8, MIX_W), _F32)],
        scratch_shapes=[pltpu.VMEM((1, EXT0 + tm, MIX_W), _F32)],
        compiler_params=_cparams(2),
        name="prompt_conv",
    )(p3, p3, p3, p3, p3, conv_w)


def _sample_conv(p3, hist, conv_w, bt):
    bsz, t, _ = p3.shape
    col = lambda c: pl.BlockSpec((bt, t, MIX_W), lambda b: (b, 0, c))
    return pl.pallas_call(
        _sample_conv_kernel,
        grid=(bsz // bt,),
        in_specs=[col(1), col(2), col(3), pl.BlockSpec((bt, HALO, MIX_W), lambda b: (b, 0, 0)),
                  _const_spec(conv_w.shape)],
        out_specs=[pl.BlockSpec((bt, t, MIX_W), lambda b: (b, 0, 0)),
                   pl.BlockSpec((bt, 8, MIX_W), lambda b: (b, 0, 0))],
        out_shape=[jax.ShapeDtypeStruct((bsz, t, MIX_W), _BF), jax.ShapeDtypeStruct((bsz, 8, MIX_W), _F32)],
        scratch_shapes=[pltpu.VMEM((bt, EXT0 + t, MIX_W), _F32)],
        compiler_params=_cparams(1),
        name="sample_conv",
    )(p3, p3, p3, hist, conv_w)


def _prompt_pool(p3, tm):
    bsz, s, _ = p3.shape
    scratch = pltpu.VMEM((1, EXT0 + tm, MIX_W), _F32)
    return pl.pallas_call(
        functools.partial(_prompt_pool_kernel, tm=tm),
        grid=(bsz, s // tm),
        in_specs=[pl.BlockSpec((1, tm, MIX_W), lambda b, i: (b, i, 0)),
                  pl.BlockSpec((1, HALO, MIX_W), lambda b, i: (b, jnp.maximum(i * (tm // HALO) - 1, 0), 0))],
        out_specs=pl.BlockSpec((1, tm, MIX_W), lambda b, i: (b, i, 0)),
        out_shape=jax.ShapeDtypeStruct((bsz, s, MIX_W), _BF),
        scratch_shapes=[scratch] * 4,
        compiler_params=_cparams(2),
        name="prompt_pool",
    )(p3, p3)


def _sample_pool(p3, hist, bt):
    bsz, t, _ = p3.shape
    scratch = pltpu.VMEM((bt, EXT0 + t, MIX_W), _F32)
    return pl.pallas_call(
        _sample_pool_kernel,
        grid=(bsz // bt,),
        in_specs=[pl.BlockSpec((bt, t, MIX_W), lambda b: (b, 0, 0)),
                  pl.BlockSpec((bt, HALO, MIX_W), lambda b: (b, 0, 0))],
        out_specs=pl.BlockSpec((bt, t, MIX_W), lambda b: (b, 0, 0)),
        out_shape=jax.ShapeDtypeStruct((bsz, t, MIX_W), _BF),
        scratch_shapes=[scratch] * 4,
        compiler_params=_cparams(1),
        name="sample_pool",
    )(p3, hist)


def _split_heads(q_pair):
    left = _lane_is_left()
    zero = jnp.zeros_like(q_pair)
    return jnp.concatenate([jnp.where(left, q_pair, zero), jnp.where(left, zero, q_pair)], axis=0)


def _band_attn_kernel(*refs, tq, kv_lane, has_sink, want_stat):
    refs = list(refs)
    q_ref, kc_ref, kp_ref, vc_ref, vp_ref, bias_ref = refs[:6]
    rest = refs[6:]
    sink_ref = rest.pop(0) if has_sink else None
    o_ref = rest.pop(0)
    st_ref = rest.pop(0) if want_stat else None

    first = pl.program_id(2) == 0
    left = _lane_is_left()
    lane = lax.broadcasted_iota(jnp.int32, (1, LANES), 1)
    prev_cols = lax.broadcasted_iota(jnp.int32, (1, 2 * BAND), 1) < BAND
    top_rows = lax.broadcasted_iota(jnp.int32, (2 * BAND, 1), 0) < BAND
    ones = jnp.ones((2 * BAND, LANES), _BF)

    for j in range(tq // BAND):
        rows = slice(j * BAND, (j + 1) * BAND)
        stat = jnp.zeros((BAND, LANES), _F32)
        for p in range(N_PAIRS):
            kl = slice(kv_lane[p], kv_lane[p] + LANES)
            if j == 0:
                k_prev, v_prev = kp_ref[0, :, kl], vp_ref[0, :, kl]
            else:
                k_prev, v_prev = kc_ref[0, (j - 1) * BAND:j * BAND, kl], vc_ref[0, (j - 1) * BAND:j * BAND, kl]
            k2 = jnp.concatenate([k_prev, kc_ref[0, rows, kl]], axis=0)
            v2 = jnp.concatenate([v_prev, vc_ref[0, rows, kl]], axis=0)
            q2 = _split_heads(q_ref[0, rows, p * LANES:(p + 1) * LANES])
            s = lax.dot_general(q2, k2, (((1,), (1,)), ((), ())), preferred_element_type=_F32)
            s = s + bias_ref[p]
            if j == 0:
                s = jnp.where(jnp.logical_and(first, prev_cols), _NEG_INF, s)
            m = jnp.max(s, axis=1, keepdims=True)
            prob = jnp.exp(s - m).astype(_BF)
            r = jnp.dot(prob, jnp.concatenate([v2, ones], axis=1), preferred_element_type=_F32)
            pv, l = r[:, :LANES], r[:, LANES:]
            if has_sink:
                ha, hb = A_PAIR_HEADS[p]
                sink = jnp.where(top_rows, sink_ref[ha], sink_ref[hb])
                m2 = jnp.maximum(m, sink)
                a = jnp.exp(m - m2)
                o = pv * a / (l * a + jnp.exp(sink - m2))
            else:
                o = pv / l
            o_ref[0, rows, p * LANES:(p + 1) * LANES] = jnp.where(left, o[:BAND], o[BAND:]).astype(o_ref.dtype)
            if want_stat:
                lse = m + jnp.log(l)
                ha, hb = D_PAIR_HEADS[p]
                stat = jnp.where(lane == ha, lse[:BAND], stat)
                stat = jnp.where(lane == hb, lse[BAND:], stat)
        if want_stat:
            st_ref[0, rows, :] = stat


def _band_bias(pair_heads, dilation):
    slopes = _alibi_slopes(N_HEADS)
    qi = np.arange(BAND)[:, None]
    kj = np.arange(2 * BAND)[None, :]
    dist = qi + BAND - kj
    valid = (dist >= 0) & (dist <= BAND)
    out = np.empty((len(pair_heads), 2 * BAND, 2 * BAND), np.float32)
    for p, heads in enumerate(pair_heads):
        for half, h in enumerate(heads):
            bias = -np.float32(slopes[h]) * (dist * dilation).astype(np.float32)
            out[p, half * BAND:(half + 1) * BAND] = np.where(valid, bias, -np.inf)
    return jnp.asarray(out)


def _band_attn(arr, bsz, seq, dilation, q_col, k_col, v_col, kv_width, kv_lane, pair_heads, sinks, want_stat):
    width = arr.shape[1]
    n = seq // dilation
    view = arr.reshape(bsz, n, dilation * width)
    tq = min(512, n)
    sub = tq // BAND
    assert q_col % MIX_W == 0 and k_col % kv_width == 0 and v_col % kv_width == 0
    assert dilation == 1 or (width % MIX_W == 0 and width % kv_width == 0)

    def cur(col, w):
        return pl.BlockSpec((1, tq, w), lambda b, r, i: (b, i, (r * width + col) // w))

    def prev(col, w):
        return pl.BlockSpec((1, BAND, w), lambda b, r, i: (b, jnp.maximum(i * sub - 1, 0), (r * width + col) // w))

    in_specs = [cur(q_col, MIX_W), cur(k_col, kv_width), prev(k_col, kv_width), cur(v_col, kv_width),
                prev(v_col, kv_width), _const_spec((N_PAIRS, 2 * BAND, 2 * BAND))]
    args = [view, view, view, view, view, _band_bias(pair_heads, dilation)]
    if sinks is not None:
        in_specs.append(pl.BlockSpec(memory_space=pltpu.SMEM))
        args.append(sinks)
    out_specs = [pl.BlockSpec((1, tq, MIX_W), lambda b, r, i: (b, i, r))]
    out_shape = [jax.ShapeDtypeStruct((bsz, n, dilation * MIX_W), _BF)]
    if want_stat:
        out_specs.append(pl.BlockSpec((1, tq, LANES), lambda b, r, i: (b, i, r)))
        out_shape.append(jax.ShapeDtypeStruct((bsz, n, dilation * LANES), _F32))
    res = pl.pallas_call(
        functools.partial(_band_attn_kernel, tq=tq, kv_lane=kv_lane, has_sink=sinks is not None,
                          want_stat=want_stat),
        grid=(bsz, dilation, n // tq),
        in_specs=in_specs,
        out_specs=out_specs,
        out_shape=out_shape,
        compiler_params=_cparams(3),
        name=f"band_attn_d{dilation}",
    )(*args)
    o = res[0].reshape(bsz * seq, MIX_W)
    if want_stat:
        return o, res[1].reshape(bsz * seq, LANES)
    return o


def _sample_attn_kernel(*refs, n_hist, t_new, q_pairs_of_kv, pair_heads, has_sink):
    refs = list(refs)
    q_ref, kn_ref, vn_ref, kc_ref, vc_ref, bias_ref, mult_ref = refs[:7]
    rest = refs[7:]
    sink_ref = rest.pop(0) if has_sink else None
    y_ref, ko_ref, vo_ref = rest

    left = _lane_is_left()
    new_lanes = lax.broadcasted_iota(jnp.int32, (1, LANES), 1) >= LANES - t_new
    top_rows = lax.broadcasted_iota(jnp.int32, (2 * t_new, 1), 0) < t_new
    zpad = jnp.zeros((LANES - t_new, LANES), _F32)
    mult = mult_ref[...]

    def one_batch(b, carry):
        for kvp, q_pairs in enumerate(q_pairs_of_kv):
            heads = slice(2 * kvp, 2 * kvp + 2)
            lanes = slice(kvp * LANES, (kvp + 1) * LANES)
            ext = []
            for c_ref, n_ref, o_ref in ((kc_ref, kn_ref, ko_ref), (vc_ref, vn_ref, vo_ref)):
                old = c_ref[b, heads].reshape(LANES, n_hist)
                new = jnp.concatenate([zpad, n_ref[b, :, lanes]], axis=0).T
                rolled = pltpu.roll(old, n_hist - t_new, axis=1)
                tail = jnp.where(new_lanes, new, rolled[:, n_hist - LANES:])
                out = tail if n_hist == LANES else jnp.concatenate([rolled[:, :n_hist - LANES], tail], axis=1)
                o_ref[b, heads] = out.reshape(2, HEAD_DIM, n_hist)
                ext.append(jnp.concatenate([old.astype(_BF), new.astype(_BF)], axis=1))
            k_ext, v_ext = ext
            for p in q_pairs:
                q2 = _split_heads(q_ref[b, :, p * LANES:(p + 1) * LANES].astype(_BF))
                s = jnp.dot(q2, k_ext, preferred_element_type=_F32) + bias_ref[p]
                m = jnp.max(s, axis=1, keepdims=True)
                prob = (mult * jnp.exp(s - m)).astype(_BF)
                l = jnp.sum(prob.astype(_F32), axis=1, keepdims=True)
                pv = lax.dot_general(prob, v_ext, (((1,), (1,)), ((), ())), preferred_element_type=_F32)
                if has_sink:
                    ha, hb = pair_heads[p]
                    sink = jnp.where(top_rows, sink_ref[ha], sink_ref[hb])
                    m2 = jnp.maximum(m, sink)
                    a = jnp.exp(m - m2)
                    o = pv * a / (l * a + jnp.exp(sink - m2))
                else:
                    o = pv / l
                y_ref[b, :, p * LANES:(p + 1) * LANES] = jnp.where(left, o[:t_new], o[t_new:]).astype(y_ref.dtype)
        return carry

    lax.fori_loop(0, q_ref.shape[0], one_batch, 0)


def _sample_tables(pair_heads, branches, n_hist, t_new):
    slopes = _alibi_slopes(N_HEADS)
    key_pos = np.concatenate([np.arange(n_hist), n_hist + np.arange(LANES) - (LANES - t_new)])
    is_key = np.concatenate([np.ones(n_hist, bool), np.arange(LANES) >= LANES - t_new])
    delta = (n_hist + np.arange(t_new))[:, None] - key_pos[None, :]
    mult = np.zeros(delta.shape, np.float32)
    for window, dil in branches:
        mult += ((delta >= 0) & (delta % dil == 0) & (delta <= window) & is_key[None, :]).astype(np.float32)
    bias = np.empty((len(pair_heads), 2 * t_new, key_pos.size), np.float32)
    for p, heads in enumerate(pair_heads):
        for half, h in enumerate(heads):
            b = -np.float32(slopes[h]) * delta.astype(np.float32)
            bias[p, half * t_new:(half + 1) * t_new] = np.where(mult > 0, b, -np.inf)
    return jnp.asarray(bias), jnp.asarray(np.concatenate([mult, mult], axis=0))


def _sample_attn(p3, q_col, k_col, v_col, pair_heads, branches, k_cache, v_cache, sinks, bt):
    bsz, t_new, _ = p3.shape
    _, n_hist, kvh, _ = k_cache.shape
    kv_width = kvh * HEAD_DIM
    kt = jnp.transpose(k_cache, (0, 2, 3, 1))
    vt = jnp.transpose(v_cache, (0, 2, 3, 1))
    n_kv_pairs = kvh // 2
    q_pairs_of_kv = tuple(tuple(p for p in range(N_PAIRS) if p % n_kv_pairs == kvp) for kvp in range(n_kv_pairs))
    bias, mult = _sample_tables(pair_heads, branches, n_hist, t_new)
    new = lambda col, w: pl.BlockSpec((bt, t_new, w), lambda b: (b, 0, col // w))
    cache = pl.BlockSpec((bt, kvh, HEAD_DIM, n_hist), lambda b: (b, 0, 0, 0))
    in_specs = [new(q_col, MIX_W), new(k_col, kv_width), new(v_col, kv_width), cache, cache,
                _const_spec(bias.shape), _const_spec(mult.shape)]
    args = [p3, p3, p3, kt, vt, bias, mult]
    if sinks is not None:
        in_specs.append(pl.BlockSpec(memory_space=pltpu.SMEM))
        args.append(sinks)
    y, ko, vo = pl.pallas_call(
        functools.partial(_sample_attn_kernel, n_hist=n_hist, t_new=t_new, q_pairs_of_kv=q_pairs_of_kv,
                          pair_heads=pair_heads, has_sink=sinks is not None),
        grid=(bsz // bt,),
        in_specs=in_specs,
        out_specs=[pl.BlockSpec((bt, t_new, MIX_W), lambda b: (b, 0, 0)), cache, cache],
        out_shape=[jax.ShapeDtypeStruct((bsz, t_new, MIX_W), _BF),
                   jax.ShapeDtypeStruct(kt.shape, _F32), jax.ShapeDtypeStruct(vt.shape, _F32)],
        compiler_params=_cparams(1),
        name=f"sample_attn_{n_hist}",
    )(*args)
    return y, jnp.transpose(ko, (0, 3, 1, 2)), jnp.transpose(vo, (0, 3, 1, 2))


A_Q = N_HEADS * HEAD_DIM
A_KV = A_KV_HEADS * HEAD_DIM
E_Q, E_H, E_GB, E_GC, E_K, E_V = 0, 512, 1024, 1536, 2048, 2176
O_U, O_Q, O_K, O_V = 0, 512, 1024, 1536


def _prep_layer_weights(even_w_in, even_b_in, even_w_out, odd_w_in, odd_b_in, c_w_group):
    q_cols = np.concatenate([h * HEAD_DIM + np.arange(HEAD_DIM) for h in A_HEAD_ORDER])
    o1, o2, o3 = A_Q, A_Q + A_KV, A_Q + 2 * A_KV
    order = np.concatenate([q_cols, np.arange(o3, o3 + 3 * MIX_W), np.arange(o1, o3)])
    scale = np.ones((order.size,), np.float32)
    scale[:A_Q] = HEAD_DIM ** -0.5
    ew = (even_w_in[:, order] * scale).astype(_BF)
    eb = (even_b_in[order] * scale)[None, :]
    ewo = jnp.concatenate([even_w_out[q_cols], even_w_out[A_Q:]], axis=0).astype(_BF)
    oscale = np.ones((odd_w_in.shape[1],), np.float32)
    oscale[O_Q:O_K] = HEAD_DIM ** -0.5
    ow = (odd_w_in * oscale).astype(_BF)
    ob = (odd_b_in * oscale)[None, :]
    groups, gw, _ = c_w_group.shape
    wg = jnp.zeros((MIX_W, MIX_W), _F32)
    for g in range(groups):
        wg = wg.at[g * gw:(g + 1) * gw, g * gw:(g + 1) * gw].set(c_w_group[g])
    return ew, eb, ewo, ow, ob, wg.astype(_BF)


def _row(v):
    return v[None, :]


def _trunk(x, caches, wts, tm):
    (even_w_in, even_b_in, a_sinks, b_conv_w, even_w_out, even_b_out, odd_w_in, odd_b_in, c_w_group, c_scale,
     odd_w_out, odd_b_out, mlp_w1, mlp_w2, ln1_g, ln1_b, ln2_g, ln2_b) = wts
    bsz, seq, dm = x.shape
    n = bsz * seq
    ew, eb, ewo, ow, ob, wg = _prep_layer_weights(even_w_in[0], even_b_in[0], even_w_out[0], odd_w_in[0],
                                                  odd_b_in[0], c_w_group[0])
    owo = odd_w_out[0].astype(_BF)
    w1 = mlp_w1.astype(_BF)
    w2 = mlp_w2.astype(_BF)
    prompt = caches is None
    x = x.reshape(n, dm)
    a_kv_lane = (0,) * N_PAIRS
    d_kv_lane = tuple(p * LANES for p in range(N_PAIRS))

    if prompt:
        pe = _proj(x, ew, eb, _BF, tm)
        ya = _band_attn(pe, bsz, seq, 1, E_Q, E_K, E_V, LANES, a_kv_lane, A_PAIR_HEADS, a_sinks[0], False)
        pe3 = pe.reshape(bsz, seq, -1)
        yb, ctail = _prompt_conv(pe3, b_conv_w[0], tm)
        n_keep = min(A_WINDOW, seq)
        a_k = pe3[:, seq - n_keep:, E_K:E_V].astype(_F32).reshape(bsz, n_keep, A_KV_HEADS, HEAD_DIM)
        a_v = pe3[:, seq - n_keep:, E_V:].astype(_F32).reshape(bsz, n_keep, A_KV_HEADS, HEAD_DIM)
        b_conv = ctail[:, -(CONV_WIDTH - 1):, :]
    else:
        cache_a_k, cache_a_v, state_b_conv, state_c_pool, cache_d_k, cache_d_v = caches
        pe3 = _proj(x, ew, eb, _F32, tm).reshape(bsz, seq, -1)
        ya, a_k, a_v = _sample_attn(pe3, E_Q, E_K, E_V, A_PAIR_HEADS, ((A_WINDOW, 1),), cache_a_k[0], cache_a_v[0],
                                    a_sinks[0], 16)
        c_hist = jnp.pad(state_b_conv[0], ((0, 0), (HALO - (CONV_WIDTH - 1), 0), (0, 0)))
        yb, ctail = _sample_conv(pe3, c_hist, b_conv_w[0], 32)
        b_conv = ctail[:, -(CONV_WIDTH - 1):, :]
    tails = [(wo, _row(bo), _row(ln1_g[i]), _row(ln1_b[i]), w1[i], w2[i], _row(ln2_g[i]), _row(ln2_b[i]))
             for i, (wo, bo) in enumerate(((ewo, even_b_out[0]), (owo, odd_b_out[0])))]
    x = _even_tail(x, ya.reshape(n, MIX_W), yb.reshape(n, MIX_W), tails[0], tm)

    if prompt:
        po = _proj(x, ow, ob, _BF, tm)
        outs, stats = [], []
        for _, dil in D_BRANCHES:
            o, st = _band_attn(po, bsz, seq, dil, O_Q, O_K, O_V, MIX_W, d_kv_lane, D_PAIR_HEADS, None, True)
            outs.append(o)
            stats.append(st)
        po3 = po.reshape(bsz, seq, -1)
        pooled = _prompt_pool(po3, tm).reshape(n, MIX_W)
        n_keep = min(D_BRANCHES[-1][0], seq)
        c_pool = po3[:, seq - (POOL_MAX - 1):, O_U:O_Q].astype(_F32)
        d_k = po3[:, seq - n_keep:, O_K:O_V].astype(_F32).reshape(bsz, n_keep, N_HEADS, HEAD_DIM)
        d_v = po3[:, seq - n_keep:, O_V:].astype(_F32).reshape(bsz, n_keep, N_HEADS, HEAD_DIM)
        x = _odd_tail(x, pooled, wg, _row(c_scale[0]), outs, stats, tails[1], tm)
    else:
        po3 = _proj(x, ow, ob, _F32, tm).reshape(bsz, seq, -1)
        yd, d_k, d_v = _sample_attn(po3, O_Q, O_K, O_V, D_PAIR_HEADS, D_BRANCHES, cache_d_k[0], cache_d_v[0], None, 1)
        u_hist = jnp.pad(state_c_pool[0], ((0, 0), (HALO - (POOL_MAX - 1), 0), (0, 0)))
        pooled = _sample_pool(po3, u_hist, 32).reshape(n, MIX_W)
        c_pool = jnp.concatenate([state_c_pool[0], po3[:, :, O_U:O_Q]], axis=1)[:, -(POOL_MAX - 1):]
        x = _sample_odd_tail(x, pooled, wg, _row(c_scale[0]), yd.reshape(n, MIX_W), tails[1], tm)
    return (x.reshape(bsz, seq, dm), a_k[None], a_v[None], b_conv[None], c_pool[None], d_k[None], d_v[None])


def kernel(x_prompt, x_sample, cache_a_k, cache_a_v, state_b_conv, state_c_pool, cache_d_k, cache_d_v, even_w_in, even_b_in, a_sinks, b_conv_w, even_w_out, even_b_out, odd_w_in, odd_b_in, c_w_group, c_scale, odd_w_out, odd_b_out, mlp_w1, mlp_w2, ln1_g, ln1_b, ln2_g, ln2_b):
    wts = (even_w_in, even_b_in, a_sinks, b_conv_w, even_w_out, even_b_out, odd_w_in, odd_b_in, c_w_group, c_scale,
           odd_w_out, odd_b_out, mlp_w1, mlp_w2, ln1_g, ln1_b, ln2_g, ln2_b)
    caches = (cache_a_k, cache_a_v, state_b_conv, state_c_pool, cache_d_k, cache_d_v)
    y_p, ak_p, av_p, bc_p, cp_p, dk_p, dv_p = _trunk(x_prompt, None, wts, 512)
    y_s, ak_s, av_s, bc_s, cp_s, dk_s, dv_s = _trunk(x_sample, caches, wts, 512)
    return (y_p, y_s, ak_p, av_p, bc_p, cp_p, dk_p, dv_p, ak_s, av_s, bc_s, cp_s, dk_s, dv_s)
```

```python
import functools

import numpy as np
import jax
import jax.numpy as jnp
from jax import lax
from jax.experimental import pallas as pl
from jax.experimental.pallas import tpu as pltpu

HEAD_DIM = 64
N_HEADS = 8
A_KV_HEADS = 2
A_WINDOW = 128
D_BRANCHES = ((128, 1), (512, 4), (2048, 16))
CONV_WIDTH = 3
POOL_WINDOWS = (2, 4, 8, 16)
POOL_MAX = 16
DEPTH = 2
PAST_LEN = 16384
DEEPNORM_ALPHA = (2 * DEPTH) ** 0.25
LN_EPS = 1e-5

MIX_W = N_HEADS * HEAD_DIM
LANES = 128
N_PAIRS = MIX_W // LANES
BAND = 128
HALO = 16
EXT0 = 24
VMEM_LIMIT = 56 * 1024 * 1024

A_HEAD_ORDER = (0, 4, 1, 5, 2, 6, 3, 7)
A_PAIR_HEADS = tuple((p, p + 4) for p in range(N_PAIRS))
D_PAIR_HEADS = tuple((2 * p, 2 * p + 1) for p in range(N_PAIRS))

_BF = jnp.bfloat16
_F32 = jnp.float32
_NEG_INF = float("-inf")


def _alibi_slopes(n_heads):
    return 2.0 ** (-8.0 * np.arange(1, n_heads + 1) / n_heads)


def _cparams(n_axes):
    return pltpu.CompilerParams(dimension_semantics=("arbitrary",) * n_axes, vmem_limit_bytes=VMEM_LIMIT)


def _const_spec(shape):
    nd = len(shape)
    return pl.BlockSpec(shape, lambda *_: (0,) * nd)


def _layer_norm(y, g, b):
    mu = jnp.mean(y, axis=-1, keepdims=True)
    yc = y - mu
    var = jnp.mean(yc * yc, axis=-1, keepdims=True)
    return yc * lax.rsqrt(var + LN_EPS) * g + b


def _proj_kernel(x_ref, w_ref, b_ref, o_ref, *, tn):
    x = x_ref[...].astype(_BF)
    for j in range(o_ref.shape[1] // tn):
        cols = slice(j * tn, (j + 1) * tn)
        acc = jnp.dot(x, w_ref[:, cols], preferred_element_type=_F32)
        o_ref[:, cols] = (acc + b_ref[:, cols]).astype(o_ref.dtype)


def _proj(x, w, b, out_dtype, tm):
    n, k = x.shape
    m = w.shape[1]
    return pl.pallas_call(
        functools.partial(_proj_kernel, tn=256),
        grid=(n // tm,),
        in_specs=[pl.BlockSpec((tm, k), lambda i: (i, 0)), _const_spec((k, m)), _const_spec((1, m))],
        out_specs=pl.BlockSpec((tm, m), lambda i: (i, 0)),
        out_shape=jax.ShapeDtypeStruct((n, m), out_dtype),
        compiler_params=_cparams(1),
        name="proj",
    )(x, w, b)


def _proj_dilated_kernel(x_ref, w_ref, b_ref, o_ref, *rest, tn, first_col, dilations):
    dil_refs, stage_ref = rest[:-1], rest[-1]
    tm = x_ref.shape[0]
    x = x_ref[...].astype(_BF)
    for j in range(o_ref.shape[1] // tn):
        cols = slice(j * tn, (j + 1) * tn)
        acc = jnp.dot(x, w_ref[:, cols], preferred_element_type=_F32) + b_ref[:, cols]
        o_ref[:, cols] = acc.astype(o_ref.dtype)
        if j * tn >= first_col:
            for h in range(tn // LANES):
                stage_ref[(j * tn - first_col) // LANES + h] = acc[:, h * LANES:(h + 1) * LANES]
    for d_ref, d in zip(dil_refs, dilations):
        for r in range(d):
            for s in range(stage_ref.shape[0]):
                d_ref[r, :, s * LANES:(s + 1) * LANES] = stage_ref[s, pl.ds(r, tm // d, stride=d), :].astype(d_ref.dtype)


def _proj_dilated(x3, w, b, tm, first_col, dilations):
    bsz, seq, k = x3.shape
    m = w.shape[1]
    wd = m - first_col
    assert first_col % 256 == 0 and all(tm % (16 * d) == 0 for d in dilations)
    out_specs = [pl.BlockSpec((None, tm, m), lambda bi, i: (bi, i, 0))]
    out_shape = [jax.ShapeDtypeStruct((bsz, seq, m), _BF)]
    for d in dilations:
        out_specs.append(pl.BlockSpec((None, d, tm // d, wd), lambda bi, i: (bi, 0, i, 0)))
        out_shape.append(jax.ShapeDtypeStruct((bsz, d, seq // d, wd), _BF))
    return pl.pallas_call(
        functools.partial(_proj_dilated_kernel, tn=256, first_col=first_col, dilations=dilations),
        grid=(bsz, seq // tm),
        in_specs=[pl.BlockSpec((None, tm, k), lambda bi, i: (bi, i, 0)), _const_spec((k, m)), _const_spec((1, m))],
        out_specs=out_specs,
        out_shape=out_shape,
        scratch_shapes=[pltpu.VMEM((wd // LANES, tm, LANES), _F32)],
        compiler_params=_cparams(2),
        name="proj_dilated",
    )(x3, w, b)


MLP_CHUNK = 512


def _lane_is_left():
    return lax.broadcasted_iota(jnp.int32, (1, LANES), 1) < HEAD_DIM


def _layer_tail(x_ref, left, right, tail_refs):
    wo_ref, bo_ref, g1_ref, b1_ref, w1_ref, w2_ref, g2_ref, b2_ref, o_ref = tail_refs
    mix = jnp.dot(left, wo_ref[:MIX_W, :], preferred_element_type=_F32)
    mix = mix + jnp.dot(right, wo_ref[MIX_W:, :], preferred_element_type=_F32) + bo_ref[...]
    x = _layer_norm(DEEPNORM_ALPHA * x_ref[...] + mix, g1_ref[...], b1_ref[...])
    xb = x.astype(_BF)
    acc = jnp.zeros(x.shape, _F32)
    for c in range(w1_ref.shape[1] // MLP_CHUNK):
        cols = slice(c * MLP_CHUNK, (c + 1) * MLP_CHUNK)
        h = jnp.dot(xb, w1_ref[:, cols], preferred_element_type=_F32)
        h = jnp.square(jnp.maximum(h, 0.0)).astype(_BF)
        acc = acc + jnp.dot(h, w2_ref[cols, :], preferred_element_type=_F32)
    o_ref[...] = _layer_norm(DEEPNORM_ALPHA * x + acc, g2_ref[...], b2_ref[...])


def _even_tail_kernel(x_ref, ya_ref, yb_ref, *tail_refs):
    _layer_tail(x_ref, ya_ref[...], yb_ref[...], tail_refs)


def _group_c(pooled_ref, wg_ref, scale_ref):
    return (jnp.dot(pooled_ref[...], wg_ref[...], preferred_element_type=_F32) * scale_ref[...]).astype(_BF)


def _sample_odd_tail_kernel(x_ref, pooled_ref, wg_ref, scale_ref, yd_ref, *tail_refs):
    _layer_tail(x_ref, _group_c(pooled_ref, wg_ref, scale_ref), yd_ref[...], tail_refs)


def _odd_tail_kernel(x_ref, pooled_ref, wg_ref, scale_ref, o1_ref, o2_ref, o3_ref, s1_ref, s2_ref, s3_ref,
                     *tail_refs):
    yc = _group_c(pooled_ref, wg_ref, scale_ref)
    lses = (s1_ref[...], s2_ref[...], s3_ref[...])
    top = jnp.maximum(jnp.maximum(lses[0], lses[1]), lses[2])
    es = [jnp.exp(s - top) for s in lses]
    den = es[0] + es[1] + es[2]
    cs = [e / den for e in es]
    outs = (o1_ref, o2_ref, o3_ref)
    tm = x_ref.shape[0]
    left = _lane_is_left()
    tiles = []
    for p, (ha, hb) in enumerate(D_PAIR_HEADS):
        acc = None
        for c, o_ref_b in zip(cs, outs):
            wa = jnp.broadcast_to(c[:, ha:ha + 1], (tm, LANES))
            wb = jnp.broadcast_to(c[:, hb:hb + 1], (tm, LANES))
            term = jnp.where(left, wa, wb) * o_ref_b[:, p * LANES:(p + 1) * LANES].astype(_F32)
            acc = term if acc is None else acc + term
        tiles.append(acc)
    yd = jnp.concatenate(tiles, axis=1)
    _layer_tail(x_ref, yc, yd.astype(_BF), tail_refs)


def _row_spec(tm, width):
    return pl.BlockSpec((tm, width), lambda i: (i, 0))


def _resident_spec(shape):
    nd = len(shape)
    return pl.BlockSpec(shape, lambda *_: (0,) * nd, pipeline_mode=pl.Buffered(1))


def _tail_call(kernel_fn, name, x, mixer_args, mixer_specs, tail_params, tm):
    n, dm = x.shape
    return pl.pallas_call(
        kernel_fn,
        grid=(n // tm,),
        in_specs=[_row_spec(tm, dm)] + mixer_specs + [_resident_spec(p.shape) for p in tail_params],
        out_specs=_row_spec(tm, dm),
        out_shape=jax.ShapeDtypeStruct((n, dm), _F32),
        compiler_params=_cparams(1),
        name=name,
    )(x, *mixer_args, *tail_params)


def _even_tail(x, ya, yb, tail_params, tm):
    return _tail_call(_even_tail_kernel, "even_tail", x, [ya, yb], [_row_spec(tm, MIX_W)] * 2, tail_params, tm)


def _odd_tail(x, pooled, wg, scale, outs, stats, tail_params, tm):
    specs = ([_row_spec(tm, MIX_W), _resident_spec(wg.shape), _resident_spec(scale.shape)]
             + [_row_spec(tm, MIX_W)] * 3 + [_row_spec(tm, LANES)] * 3)
    return _tail_call(_odd_tail_kernel, "odd_tail", x, [pooled, wg, scale, *outs, *stats], specs, tail_params, tm)


def _sample_odd_tail(x, pooled, wg, scale, yd, tail_params, tm):
    specs = [_row_spec(tm, MIX_W), _resident_spec(wg.shape), _resident_spec(scale.shape), _row_spec(tm, MIX_W)]
    return _tail_call(_sample_odd_tail_kernel, "sample_odd_tail", x, [pooled, wg, scale, yd], specs, tail_params, tm)


def _fill_ext(ext_ref, hist, cur, t):
    nb = ext_ref.shape[0]
    ext_ref[:, 0:8, :] = jnp.zeros((nb, 8, MIX_W), _F32)
    ext_ref[:, 8:EXT0, :] = hist
    ext_ref[:, EXT0:EXT0 + t, :] = cur


def _conv_body(h, gb, gc, c_hist, w_ref, ext_ref, yb_ref, ctail_ref):
    t = h.shape[1]
    c = gc * h
    _fill_ext(ext_ref, c_hist, c, t)
    conv = ext_ref[:, EXT0 - 2:EXT0 - 2 + t, :] * w_ref[0:1, :]
    conv = conv + ext_ref[:, EXT0 - 1:EXT0 - 1 + t, :] * w_ref[1:2, :]
    conv = conv + c * w_ref[2:3, :]
    yb_ref[...] = (gb * conv).astype(yb_ref.dtype)
    ctail_ref[...] = ext_ref[:, EXT0 + t - 8:EXT0 + t, :]


def _prompt_conv_kernel(h_ref, gb_ref, gc_ref, hp_ref, gcp_ref, w_ref, yb_ref, ctail_ref, ext_ref):
    c_hist = jnp.where(pl.program_id(1) > 0, hp_ref[...].astype(_F32) * gcp_ref[...].astype(_F32), 0.0)
    _conv_body(h_ref[...].astype(_F32), gb_ref[...].astype(_F32), gc_ref[...].astype(_F32), c_hist,
               w_ref, ext_ref, yb_ref, ctail_ref)


def _sample_conv_kernel(h_ref, gb_ref, gc_ref, hist_ref, w_ref, yb_ref, ctail_ref, ext_ref):
    _conv_body(h_ref[...], gb_ref[...], gc_ref[...], hist_ref[...], w_ref, ext_ref, yb_ref, ctail_ref)


def _pool_body(u, hist, pos0, ext_ref, s2_ref, s4_ref, s8_ref, out_ref):
    nb, t, _ = u.shape
    _fill_ext(ext_ref, hist, u, t)
    hi = EXT0 + t
    zeros8 = jnp.zeros((nb, 8, MIX_W), _F32)
    s2_ref[:, 0:8, :] = zeros8
    s4_ref[:, 0:8, :] = zeros8
    s8_ref[:, 0:8, :] = zeros8
    s2_ref[:, 8:hi, :] = ext_ref[:, 8:hi, :] + ext_ref[:, 7:hi - 1, :]
    s4_ref[:, 8:hi, :] = s2_ref[:, 8:hi, :] + s2_ref[:, 6:hi - 2, :]
    s8_ref[:, 8:hi, :] = s4_ref[:, 8:hi, :] + s4_ref[:, 4:hi - 4, :]
    sums = (
        s2_ref[:, EXT0:hi, 0:LANES],
        s4_ref[:, EXT0:hi, LANES:2 * LANES],
        s8_ref[:, EXT0:hi, 2 * LANES:3 * LANES],
        s8_ref[:, EXT0:hi, 3 * LANES:] + s8_ref[:, EXT0 - 8:hi - 8, 3 * LANES:],
    )
    pos = (pos0 + lax.broadcasted_iota(jnp.int32, (1, t, LANES), 1) + 1).astype(_F32)
    tiles = []
    for g, (w, s) in enumerate(zip(POOL_WINDOWS, sums)):
        cnt = jnp.minimum(pos, float(w))
        tiles.append(s / cnt - u[:, :, g * LANES:(g + 1) * LANES])
    out_ref[...] = jnp.concatenate(tiles, axis=2).astype(out_ref.dtype)


def _prompt_pool_kernel(u_ref, up_ref, out_ref, ext_ref, s2_ref, s4_ref, s8_ref, *, tm):
    i = pl.program_id(1)
    hist = jnp.where(i > 0, up_ref[...].astype(_F32), 0.0)
    _pool_body(u_ref[...].astype(_F32), hist, i * tm, ext_ref, s2_ref, s4_ref, s8_ref, out_ref)


def _sample_pool_kernel(u_ref, hist_ref, out_ref, ext_ref, s2_ref, s4_ref, s8_ref):
    _pool_body(u_ref[...], hist_ref[...], PAST_LEN, ext_ref, s2_ref, s4_ref, s8_ref, out_ref)


def _prompt_conv(p3, conv_w, tm):
    bsz, s, _ = p3.shape
    col = lambda c: pl.BlockSpec((1, tm, MIX_W), lambda b, i: (b, i, c))
    prev = lambda c: pl.BlockSpec((1, HALO, MIX_W), lambda b, i: (b, jnp.maximum(i * (tm // HALO) - 1, 0), c))
    return pl.pallas_call(
        _prompt_conv_kernel,
        grid=(bsz, s // tm),
        in_specs=[col(1), col(2), col(3), prev(1), prev(3), _const_spec(conv_w.shape)],
        out_specs=[pl.BlockSpec((1, tm, MIX_W), lambda b, i: (b, i, 0)),
                   pl.BlockSpec((1, 8, MIX_W), lambda b, i: (b, i, 0))],
        out_shape=[jax.ShapeDtypeStruct((bsz, s, MIX_W), _BF),
                   jax.ShapeDtypeStruct((bsz, (s // tm) * 8, MIX_W), _F32)],
        scratch_shapes=[pltpu.VMEM((1, EXT0 + tm, MIX_W), _F32)],
        compiler_params=_cparams(2),
        name="prompt_conv",
    )(p3, p3, p3, p3, p3, conv_w)


def _sample_conv(p3, hist, conv_w, bt):
    bsz, t, _ = p3.shape
    col = lambda c: pl.BlockSpec((bt, t, MIX_W), lambda b: (b, 0, c))
    return pl.pallas_call(
        _sample_conv_kernel,
        grid=(bsz // bt,),
        in_specs=[col(1), col(2), col(3), pl.BlockSpec((bt, HALO, MIX_W), lambda b: (b, 0, 0)),
                  _const_spec(conv_w.shape)],
        out_specs=[pl.BlockSpec((bt, t, MIX_W), lambda b: (b, 0, 0)),
                   pl.BlockSpec((bt, 8, MIX_W), lambda b: (b, 0, 0))],
        out_shape=[jax.ShapeDtypeStruct((bsz, t, MIX_W), _BF), jax.ShapeDtypeStruct((bsz, 8, MIX_W), _F32)],
        scratch_shapes=[pltpu.VMEM((bt, EXT0 + t, MIX_W), _F32)],
        compiler_params=_cparams(1),
        name="sample_conv",
    )(p3, p3, p3, hist, conv_w)


def _prompt_pool(p3, tm):
    bsz, s, _ = p3.shape
    scratch = pltpu.VMEM((1, EXT0 + tm, MIX_W), _F32)
    return pl.pallas_call(
        functools.partial(_prompt_pool_kernel, tm=tm),
        grid=(bsz, s // tm),
        in_specs=[pl.BlockSpec((1, tm, MIX_W), lambda b, i: (b, i, 0)),
                  pl.BlockSpec((1, HALO, MIX_W), lambda b, i: (b, jnp.maximum(i * (tm // HALO) - 1, 0), 0))],
        out_specs=pl.BlockSpec((1, tm, MIX_W), lambda b, i: (b, i, 0)),
        out_shape=jax.ShapeDtypeStruct((bsz, s, MIX_W), _BF),
        scratch_shapes=[scratch] * 4,
        compiler_params=_cparams(2),
        name="prompt_pool",
    )(p3, p3)


def _sample_pool(p3, hist, bt):
    bsz, t, _ = p3.shape
    scratch = pltpu.VMEM((bt, EXT0 + t, MIX_W), _F32)
    return pl.pallas_call(
        _sample_pool_kernel,
        grid=(bsz // bt,),
        in_specs=[pl.BlockSpec((bt, t, MIX_W), lambda b: (b, 0, 0)),
                  pl.BlockSpec((bt, HALO, MIX_W), lambda b: (b, 0, 0))],
        out_specs=pl.BlockSpec((bt, t, MIX_W), lambda b: (b, 0, 0)),
        out_shape=jax.ShapeDtypeStruct((bsz, t, MIX_W), _BF),
        scratch_shapes=[scratch] * 4,
        compiler_params=_cparams(1),
        name="sample_pool",
    )(p3, hist)


def _split_heads(q_pair):
    left = _lane_is_left()
    zero = jnp.zeros_like(q_pair)
    return jnp.concatenate([jnp.where(left, q_pair, zero), jnp.where(left, zero, q_pair)], axis=0)


def _band_attn_kernel(*refs, tq, dilation, kv_lane, has_sink, want_stat):
    refs = list(refs)
    q_ref, kc_ref, kp_ref, vc_ref, vp_ref, bias_ref = refs[:6]
    rest = refs[6:]
    sink_ref = rest.pop(0) if has_sink else None
    o_ref = rest.pop(0)
    st_ref = rest.pop(0) if want_stat else None
    stage_ref = rest.pop(0) if dilation > 1 else None

    first = pl.program_id(1) == 0
    res = pl.program_id(2)
    left = _lane_is_left()
    lane = lax.broadcasted_iota(jnp.int32, (1, LANES), 1)
    prev_cols = lax.broadcasted_iota(jnp.int32, (1, 2 * BAND), 1) < BAND
    top_rows = lax.broadcasted_iota(jnp.int32, (2 * BAND, 1), 0) < BAND
    ones = jnp.ones((2 * BAND, LANES), _BF)

    for j in range(tq // BAND):
        rows = slice(j * BAND, (j + 1) * BAND)
        out_rows = rows if dilation == 1 else pl.ds(j * BAND * dilation + res, BAND, stride=dilation)
        stat = jnp.zeros((BAND, LANES), _F32)
        for p in range(N_PAIRS):
            kl = slice(kv_lane[p], kv_lane[p] + LANES)
            if j == 0:
                k_prev, v_prev = kp_ref[:, kl], vp_ref[:, kl]
            else:
                k_prev, v_prev = kc_ref[(j - 1) * BAND:j * BAND, kl], vc_ref[(j - 1) * BAND:j * BAND, kl]
            k2 = jnp.concatenate([k_prev, kc_ref[rows, kl]], axis=0)
            v2 = jnp.concatenate([v_prev, vc_ref[rows, kl]], axis=0)
            q2 = _split_heads(q_ref[rows, p * LANES:(p + 1) * LANES])
            s = lax.dot_general(q2, k2, (((1,), (1,)), ((), ())), preferred_element_type=_F32)
            s = s + bias_ref[p]
            if j == 0:
                s = jnp.where(jnp.logical_and(first, prev_cols), _NEG_INF, s)
            m = jnp.max(s, axis=1, keepdims=True)
            prob = jnp.exp(s - m).astype(_BF)
            r = jnp.dot(prob, jnp.concatenate([v2, ones], axis=1), preferred_element_type=_F32)
            pv, l = r[:, :LANES], r[:, LANES:]
            if has_sink:
                ha, hb = A_PAIR_HEADS[p]
                sink = jnp.where(top_rows, sink_ref[ha], sink_ref[hb])
                m2 = jnp.maximum(m, sink)
                a = jnp.exp(m - m2)
                o = pv * a / (l * a + jnp.exp(sink - m2))
            else:
                o = pv / l
            o_pair = jnp.where(left, o[:BAND], o[BAND:])
            if dilation == 1:
                o_ref[rows, p * LANES:(p + 1) * LANES] = o_pair.astype(o_ref.dtype)
            else:
                stage_ref[p, out_rows, :] = o_pair
            if want_stat:
                lse = m + jnp.log(l)
                ha, hb = D_PAIR_HEADS[p]
                stat = jnp.where(lane == ha, lse[:BAND], stat)
                stat = jnp.where(lane == hb, lse[BAND:], stat)
        if want_stat:
            st_ref[out_rows, :] = stat

    if dilation > 1:
        @pl.when(res == dilation - 1)
        def _():
            for p in range(N_PAIRS):
                o_ref[:, p * LANES:(p + 1) * LANES] = stage_ref[p].astype(o_ref.dtype)


def _band_bias(pair_heads, dilation):
    slopes = _alibi_slopes(N_HEADS)
    qi = np.arange(BAND)[:, None]
    kj = np.arange(2 * BAND)[None, :]
    dist = qi + BAND - kj
    valid = (dist >= 0) & (dist <= BAND)
    out = np.empty((len(pair_heads), 2 * BAND, 2 * BAND), np.float32)
    for p, heads in enumerate(pair_heads):
        for half, h in enumerate(heads):
            bias = -np.float32(slopes[h]) * (dist * dilation).astype(np.float32)
            out[p, half * BAND:(half + 1) * BAND] = np.where(valid, bias, -np.inf)
    return jnp.asarray(out)


BAND_TOKENS = 2048


def _band_attn(arr, q_col, k_col, v_col, kv_width, kv_lane, pair_heads, sinks, want_stat):
    bsz, dilation, n, _ = arr.shape
    seq = n * dilation
    tq = min(512, n, BAND_TOKENS // dilation)
    sub = tq // BAND
    assert q_col % MIX_W == 0 and k_col % kv_width == 0 and v_col % kv_width == 0

    def cur(col, w):
        return pl.BlockSpec((None, None, tq, w), lambda b, i, r: (b, r, i, col // w))

    def prev(col, w):
        return pl.BlockSpec((None, None, BAND, w), lambda b, i, r: (b, r, jnp.maximum(i * sub - 1, 0), col // w))

    in_specs = [cur(q_col, MIX_W), cur(k_col, kv_width), prev(k_col, kv_width), cur(v_col, kv_width),
                prev(v_col, kv_width), _const_spec((N_PAIRS, 2 * BAND, 2 * BAND))]
    args = [arr, arr, arr, arr, arr, _band_bias(pair_heads, dilation)]
    if sinks is not None:
        in_specs.append(pl.BlockSpec(memory_space=pltpu.SMEM))
        args.append(sinks)
    out_specs = [pl.BlockSpec((None, tq * dilation, MIX_W), lambda b, i, r: (b, i, 0))]
    out_shape = [jax.ShapeDtypeStruct((bsz, seq, MIX_W), _BF)]
    if want_stat:
        out_specs.append(pl.BlockSpec((None, tq * dilation, LANES), lambda b, i, r: (b, i, 0)))
        out_shape.append(jax.ShapeDtypeStruct((bsz, seq, LANES), _F32))
    scratch = [pltpu.VMEM((N_PAIRS, tq * dilation, LANES), _F32)] if dilation > 1 else []
    res = pl.pallas_call(
        functools.partial(_band_attn_kernel, tq=tq, dilation=dilation, kv_lane=kv_lane, has_sink=sinks is not None,
                          want_stat=want_stat),
        grid=(bsz, n // tq, dilation),
        in_specs=in_specs,
        out_specs=out_specs,
        out_shape=out_shape,
        scratch_shapes=scratch,
        compiler_params=_cparams(3),
        name=f"band_attn_d{dilation}",
    )(*args)
    o = res[0].reshape(bsz * seq, MIX_W)
    if want_stat:
        return o, res[1].reshape(bsz * seq, LANES)
    return o


def _sample_attn_kernel(*refs, n_hist, t_new, q_pairs_of_kv, pair_heads, has_sink):
    refs = list(refs)
    q_ref, kn_ref, vn_ref, kc_ref, vc_ref, bias_ref, mult_ref = refs[:7]
    rest = refs[7:]
    sink_ref = rest.pop(0) if has_sink else None
    y_ref, ko_ref, vo_ref = rest

    left = _lane_is_left()
    new_lanes = lax.broadcasted_iota(jnp.int32, (1, LANES), 1) >= LANES - t_new
    top_rows = lax.broadcasted_iota(jnp.int32, (2 * t_new, 1), 0) < t_new
    zpad = jnp.zeros((LANES - t_new, LANES), _F32)
    mult = mult_ref[...]

    def one_batch(b, carry):
        for kvp, q_pairs in enumerate(q_pairs_of_kv):
            heads = slice(2 * kvp, 2 * kvp + 2)
            lanes = slice(kvp * LANES, (kvp + 1) * LANES)
            ext = []
            for c_ref, n_ref, o_ref in ((kc_ref, kn_ref, ko_ref), (vc_ref, vn_ref, vo_ref)):
                old = c_ref[b, heads].reshape(LANES, n_hist)
                new = jnp.concatenate([zpad, n_ref[b, :, lanes]], axis=0).T
                rolled = pltpu.roll(old, n_hist - t_new, axis=1)
                tail = jnp.where(new_lanes, new, rolled[:, n_hist - LANES:])
                out = tail if n_hist == LANES else jnp.concatenate([rolled[:, :n_hist - LANES], tail], axis=1)
                o_ref[b, heads] = out.reshape(2, HEAD_DIM, n_hist)
                ext.append(jnp.concatenate([old.astype(_BF), new.astype(_BF)], axis=1))
            k_ext, v_ext = ext
            for p in q_pairs:
                q2 = _split_heads(q_ref[b, :, p * LANES:(p + 1) * LANES].astype(_BF))
                s = jnp.dot(q2, k_ext, preferred_element_type=_F32) + bias_ref[p]
                m = jnp.max(s, axis=1, keepdims=True)
                prob = (mult * jnp.exp(s - m)).astype(_BF)
                l = jnp.sum(prob.astype(_F32), axis=1, keepdims=True)
                pv = lax.dot_general(prob, v_ext, (((1,), (1,)), ((), ())), preferred_element_type=_F32)
                if has_sink:
                    ha, hb = pair_heads[p]
                    sink = jnp.where(top_rows, sink_ref[ha], sink_ref[hb])
                    m2 = jnp.maximum(m, sink)
                    a = jnp.exp(m - m2)
                    o = pv * a / (l * a + jnp.exp(sink - m2))
                else:
                    o = pv / l
                y_ref[b, :, p * LANES:(p + 1) * LANES] = jnp.where(left, o[:t_new], o[t_new:]).astype(y_ref.dtype)
        return carry

    lax.fori_loop(0, q_ref.shape[0], one_batch, 0)


def _sample_tables(pair_heads, branches, n_hist, t_new):
    slopes = _alibi_slopes(N_HEADS)
    key_pos = np.concatenate([np.arange(n_hist), n_hist + np.arange(LANES) - (LANES - t_new)])
    is_key = np.concatenate([np.ones(n_hist, bool), np.arange(LANES) >= LANES - t_new])
    delta = (n_hist + np.arange(t_new))[:, None] - key_pos[None, :]
    mult = np.zeros(delta.shape, np.float32)
    for window, dil in branches:
        mult += ((delta >= 0) & (delta % dil == 0) & (delta <= window) & is_key[None, :]).astype(np.float32)
    bias = np.empty((len(pair_heads), 2 * t_new, key_pos.size), np.float32)
    for p, heads in enumerate(pair_heads):
        for half, h in enumerate(heads):
            b = -np.float32(slopes[h]) * delta.astype(np.float32)
            bias[p, half * t_new:(half + 1) * t_new] = np.where(mult > 0, b, -np.inf)
    return jnp.asarray(bias), jnp.asarray(np.concatenate([mult, mult], axis=0))


def _sample_attn(p3, q_col, k_col, v_col, pair_heads, branches, k_cache, v_cache, sinks, bt):
    bsz, t_new, _ = p3.shape
    _, n_hist, kvh, _ = k_cache.shape
    kv_width = kvh * HEAD_DIM
    kt = jnp.transpose(k_cache, (0, 2, 3, 1))
    vt = jnp.transpose(v_cache, (0, 2, 3, 1))
    n_kv_pairs = kvh // 2
    q_pairs_of_kv = tuple(tuple(p for p in range(N_PAIRS) if p % n_kv_pairs == kvp) for kvp in range(n_kv_pairs))
    bias, mult = _sample_tables(pair_heads, branches, n_hist, t_new)
    new = lambda col, w: pl.BlockSpec((bt, t_new, w), lambda b: (b, 0, col // w))
    cache = pl.BlockSpec((bt, kvh, HEAD_DIM, n_hist), lambda b: (b, 0, 0, 0))
    in_specs = [new(q_col, MIX_W), new(k_col, kv_width), new(v_col, kv_width), cache, cache,
                _const_spec(bias.shape), _const_spec(mult.shape)]
    args = [p3, p3, p3, kt, vt, bias, mult]
    if sinks is not None:
        in_specs.append(pl.BlockSpec(memory_space=pltpu.SMEM))
        args.append(sinks)
    y, ko, vo = pl.pallas_call(
        functools.partial(_sample_attn_kernel, n_hist=n_hist, t_new=t_new, q_pairs_of_kv=q_pairs_of_kv,
                          pair_heads=pair_heads, has_sink=sinks is not None),
        grid=(bsz // bt,),
        in_specs=in_specs,
        out_specs=[pl.BlockSpec((bt, t_new, MIX_W), lambda b: (b, 0, 0)), cache, cache],
        out_shape=[jax.ShapeDtypeStruct((bsz, t_new, MIX_W), _BF),
                   jax.ShapeDtypeStruct(kt.shape, _F32), jax.ShapeDtypeStruct(vt.shape, _F32)],
        compiler_params=_cparams(1),
        name=f"sample_attn_{n_hist}",
    )(*args)
    return y, jnp.transpose(ko, (0, 3, 1, 2)), jnp.transpose(vo, (0, 3, 1, 2))


A_Q = N_HEADS * HEAD_DIM
A_KV = A_KV_HEADS * HEAD_DIM
E_Q, E_H, E_GB, E_GC, E_K, E_V = 0, 512, 1024, 1536, 2048, 2176
O_U, O_Q, O_K, O_V = 0, 512, 1024, 1536


def _prep_layer_weights(even_w_in, even_b_in, even_w_out, odd_w_in, odd_b_in, c_w_group):
    q_cols = np.concatenate([h * HEAD_DIM + np.arange(HEAD_DIM) for h in A_HEAD_ORDER])
    o1, o2, o3 = A_Q, A_Q + A_KV, A_Q + 2 * A_KV
    order = np.concatenate([q_cols, np.arange(o3, o3 + 3 * MIX_W), np.arange(o1, o3)])
    scale = np.ones((order.size,), np.float32)
    scale[:A_Q] = HEAD_DIM ** -0.5
    ew = (even_w_in[:, order] * scale).astype(_BF)
    eb = (even_b_in[order] * scale)[None, :]
    ewo = jnp.concatenate([even_w_out[q_cols], even_w_out[A_Q:]], axis=0).astype(_BF)
    oscale = np.ones((odd_w_in.shape[1],), np.float32)
    oscale[O_Q:O_K] = HEAD_DIM ** -0.5
    ow = (odd_w_in * oscale).astype(_BF)
    ob = (odd_b_in * oscale)[None, :]
    groups, gw, _ = c_w_group.shape
    wg = jnp.zeros((MIX_W, MIX_W), _F32)
    for g in range(groups):
        wg = wg.at[g * gw:(g + 1) * gw, g * gw:(g + 1) * gw].set(c_w_group[g])
    return ew, eb, ewo, ow, ob, wg.astype(_BF)


def _row(v):
    return v[None, :]


def _trunk(x, caches, wts, tm):
    (even_w_in, even_b_in, a_sinks, b_conv_w, even_w_out, even_b_out, odd_w_in, odd_b_in, c_w_group, c_scale,
     odd_w_out, odd_b_out, mlp_w1, mlp_w2, ln1_g, ln1_b, ln2_g, ln2_b) = wts
    bsz, seq, dm = x.shape
    n = bsz * seq
    ew, eb, ewo, ow, ob, wg = _prep_layer_weights(even_w_in[0], even_b_in[0], even_w_out[0], odd_w_in[0],
                                                  odd_b_in[0], c_w_group[0])
    owo = odd_w_out[0].astype(_BF)
    w1 = mlp_w1.astype(_BF)
    w2 = mlp_w2.astype(_BF)
    prompt = caches is None
    x = x.reshape(n, dm)
    a_kv_lane = (0,) * N_PAIRS
    d_kv_lane = tuple(p * LANES for p in range(N_PAIRS))

    if prompt:
        pe = _proj(x, ew, eb, _BF, tm)
        pe3 = pe.reshape(bsz, seq, -1)
        ya = _band_attn(pe3[:, None], E_Q, E_K, E_V, LANES, a_kv_lane, A_PAIR_HEADS, a_sinks[0], False)
        yb, ctail = _prompt_conv(pe3, b_conv_w[0], tm)
        n_keep = min(A_WINDOW, seq)
        a_k = pe3[:, seq - n_keep:, E_K:E_V].astype(_F32).reshape(bsz, n_keep, A_KV_HEADS, HEAD_DIM)
        a_v = pe3[:, seq - n_keep:, E_V:].astype(_F32).reshape(bsz, n_keep, A_KV_HEADS, HEAD_DIM)
        b_conv = ctail[:, -(CONV_WIDTH - 1):, :]
    else:
        cache_a_k, cache_a_v, state_b_conv, state_c_pool, cache_d_k, cache_d_v = caches
        pe3 = _proj(x, ew, eb, _F32, tm).reshape(bsz, seq, -1)
        ya, a_k, a_v = _sample_attn(pe3, E_Q, E_K, E_V, A_PAIR_HEADS, ((A_WINDOW, 1),), cache_a_k[0], cache_a_v[0],
                                    a_sinks[0], 16)
        c_hist = jnp.pad(state_b_conv[0], ((0, 0), (HALO - (CONV_WIDTH - 1), 0), (0, 0)))
        yb, ctail = _sample_conv(pe3, c_hist, b_conv_w[0], 32)
        b_conv = ctail[:, -(CONV_WIDTH - 1):, :]
    tails = [(wo, _row(bo), _row(ln1_g[i]), _row(ln1_b[i]), w1[i], w2[i], _row(ln2_g[i]), _row(ln2_b[i]))
             for i, (wo, bo) in enumerate(((ewo, even_b_out[0]), (owo, odd_b_out[0])))]
    x = _even_tail(x, ya.reshape(n, MIX_W), yb.reshape(n, MIX_W), tails[0], tm)

    if prompt:
        dils = tuple(d for _, d in D_BRANCHES if d > 1)
        po3, *regrouped = _proj_dilated(x.reshape(bsz, seq, dm), ow, ob, tm, O_Q, dils)
        outs, stats = [], []
        for _, dil in D_BRANCHES:
            if dil == 1:
                o, st = _band_attn(po3[:, None], O_Q, O_K, O_V, MIX_W, d_kv_lane, D_PAIR_HEADS, None, True)
            else:
                o, st = _band_attn(regrouped[dils.index(dil)], 0, O_K - O_Q, O_V - O_Q, MIX_W, d_kv_lane,
                                   D_PAIR_HEADS, None, True)
            outs.append(o)
            stats.append(st)
        pooled = _prompt_pool(po3, tm).reshape(n, MIX_W)
        n_keep = min(D_BRANCHES[-1][0], seq)
        c_pool = po3[:, seq - (POOL_MAX - 1):, O_U:O_Q].astype(_F32)
        d_k = po3[:, seq - n_keep:, O_K:O_V].astype(_F32).reshape(bsz, n_keep, N_HEADS, HEAD_DIM)
        d_v = po3[:, seq - n_keep:, O_V:].astype(_F32).reshape(bsz, n_keep, N_HEADS, HEAD_DIM)
        x = _odd_tail(x, pooled, wg, _row(c_scale[0]), outs, stats, tails[1], tm)
    else:
        po3 = _proj(x, ow, ob, _F32, tm).reshape(bsz, seq, -1)
        yd, d_k, d_v = _sample_attn(po3, O_Q, O_K, O_V, D_PAIR_HEADS, D_BRANCHES, cache_d_k[0], cache_d_v[0], None, 1)
        u_hist = jnp.pad(state_c_pool[0], ((0, 0), (HALO - (POOL_MAX - 1), 0), (0, 0)))
        pooled = _sample_pool(po3, u_hist, 32).reshape(n, MIX_W)
        c_pool = jnp.concatenate([state_c_pool[0], po3[:, :, O_U:O_Q]], axis=1)[:, -(POOL_MAX - 1):]
        x = _sample_odd_tail(x, pooled, wg, _row(c_scale[0]), yd.reshape(n, MIX_W), tails[1], tm)
    return (x.reshape(bsz, seq, dm), a_k[None], a_v[None], b_conv[None], c_pool[None], d_k[None], d_v[None])


def kernel(x_prompt, x_sample, cache_a_k, cache_a_v, state_b_conv, state_c_pool, cache_d_k, cache_d_v, even_w_in, even_b_in, a_sinks, b_conv_w, even_w_out, even_b_out, odd_w_in, odd_b_in, c_w_group, c_scale, odd_w_out, odd_b_out, mlp_w1, mlp_w2, ln1_g, ln1_b, ln2_g, ln2_b):
    wts = (even_w_in, even_b_in, a_sinks, b_conv_w, even_w_out, even_b_out, odd_w_in, odd_b_in, c_w_group, c_scale,
           odd_w_out, odd_b_out, mlp_w1, mlp_w2, ln1_g, ln1_b, ln2_g, ln2_b)
    caches = (cache_a_k, cache_a_v, state_b_conv, state_c_pool, cache_d_k, cache_d_v)
    y_p, ak_p, av_p, bc_p, cp_p, dk_p, dv_p = _trunk(x_prompt, None, wts, 512)
    y_s, ak_s, av_s, bc_s, cp_s, dk_s, dv_s = _trunk(x_sample, caches, wts, 512)
    return (y_p, y_s, ak_p, av_p, bc_p, cp_p, dk_p, dv_p, ak_s, av_s, bc_s, cp_s, dk_s, dv_s)
```

```python
import functools

import numpy as np
import jax
import jax.numpy as jnp
from jax import lax
from jax.experimental import pallas as pl
from jax.experimental.pallas import tpu as pltpu

HEAD_DIM = 64
N_HEADS = 8
A_KV_HEADS = 2
A_WINDOW = 128
D_BRANCHES = ((128, 1), (512, 4), (2048, 16))
CONV_WIDTH = 3
POOL_WINDOWS = (2, 4, 8, 16)
POOL_MAX = 16
DEPTH = 2
PAST_LEN = 16384
DEEPNORM_ALPHA = (2 * DEPTH) ** 0.25
LN_EPS = 1e-5

MIX_W = N_HEADS * HEAD_DIM
LANES = 128
N_PAIRS = MIX_W // LANES
BAND = 128
HALO = 16
EXT0 = 24
VMEM_LIMIT = 56 * 1024 * 1024
FUSED_VMEM_LIMIT = 62 * 1024 * 1024

A_HEAD_ORDER = (0, 4, 1, 5, 2, 6, 3, 7)
A_PAIR_HEADS = tuple((p, p + 4) for p in range(N_PAIRS))
D_PAIR_HEADS = tuple((2 * p, 2 * p + 1) for p in range(N_PAIRS))

_BF = jnp.bfloat16
_F32 = jnp.float32
_NEG_INF = float("-inf")


def _alibi_slopes(n_heads):
    return 2.0 ** (-8.0 * np.arange(1, n_heads + 1) / n_heads)


def _cparams(n_axes):
    return pltpu.CompilerParams(dimension_semantics=("arbitrary",) * n_axes, vmem_limit_bytes=VMEM_LIMIT)


def _const_spec(shape):
    nd = len(shape)
    return pl.BlockSpec(shape, lambda *_: (0,) * nd)


def _layer_norm(y, g, b):
    mu = jnp.mean(y, axis=-1, keepdims=True)
    yc = y - mu
    var = jnp.mean(yc * yc, axis=-1, keepdims=True)
    return yc * lax.rsqrt(var + LN_EPS) * g + b


def _proj_kernel(x_ref, w_ref, b_ref, o_ref, *, tn):
    x = x_ref[...].astype(_BF)
    for j in range(o_ref.shape[1] // tn):
        cols = slice(j * tn, (j + 1) * tn)
        acc = jnp.dot(x, w_ref[:, cols], preferred_element_type=_F32)
        o_ref[:, cols] = (acc + b_ref[:, cols]).astype(o_ref.dtype)


def _proj(x, w, b, out_dtype, tm):
    n, k = x.shape
    m = w.shape[1]
    return pl.pallas_call(
        functools.partial(_proj_kernel, tn=256),
        grid=(n // tm,),
        in_specs=[pl.BlockSpec((tm, k), lambda i: (i, 0)), _const_spec((k, m)), _const_spec((1, m))],
        out_specs=pl.BlockSpec((tm, m), lambda i: (i, 0)),
        out_shape=jax.ShapeDtypeStruct((n, m), out_dtype),
        compiler_params=_cparams(1),
        name="proj",
    )(x, w, b)


def _proj_dilated_kernel(x_ref, w_ref, b_ref, o_ref, *rest, tn, first_col, dilations):
    dil_refs, stage_ref = rest[:-1], rest[-1]
    tm = x_ref.shape[0]
    x = x_ref[...].astype(_BF)
    for j in range(o_ref.shape[1] // tn):
        cols = slice(j * tn, (j + 1) * tn)
        acc = jnp.dot(x, w_ref[:, cols], preferred_element_type=_F32) + b_ref[:, cols]
        o_ref[:, cols] = acc.astype(o_ref.dtype)
        if j * tn >= first_col:
            for h in range(tn // LANES):
                stage_ref[(j * tn - first_col) // LANES + h] = acc[:, h * LANES:(h + 1) * LANES]
    for d_ref, d in zip(dil_refs, dilations):
        for r in range(d):
            for s in range(stage_ref.shape[0]):
                d_ref[r, :, s * LANES:(s + 1) * LANES] = stage_ref[s, pl.ds(r, tm // d, stride=d), :].astype(d_ref.dtype)


def _proj_dilated(x3, w, b, tm, first_col, dilations):
    bsz, seq, k = x3.shape
    m = w.shape[1]
    wd = m - first_col
    assert first_col % 256 == 0 and all(tm % (16 * d) == 0 for d in dilations)
    out_specs = [pl.BlockSpec((None, tm, m), lambda bi, i: (bi, i, 0))]
    out_shape = [jax.ShapeDtypeStruct((bsz, seq, m), _BF)]
    for d in dilations:
        out_specs.append(pl.BlockSpec((None, d, tm // d, wd), lambda bi, i: (bi, 0, i, 0)))
        out_shape.append(jax.ShapeDtypeStruct((bsz, d, seq // d, wd), _BF))
    return pl.pallas_call(
        functools.partial(_proj_dilated_kernel, tn=256, first_col=first_col, dilations=dilations),
        grid=(bsz, seq // tm),
        in_specs=[pl.BlockSpec((None, tm, k), lambda bi, i: (bi, i, 0)), _const_spec((k, m)), _const_spec((1, m))],
        out_specs=out_specs,
        out_shape=out_shape,
        scratch_shapes=[pltpu.VMEM((wd // LANES, tm, LANES), _F32)],
        compiler_params=_cparams(2),
        name="proj_dilated",
    )(x3, w, b)


MLP_CHUNK = 512


def _lane_is_left():
    return lax.broadcasted_iota(jnp.int32, (1, LANES), 1) < HEAD_DIM


def _layer_tail(x_ref, left, right, tail_refs):
    wo_ref, bo_ref, g1_ref, b1_ref, w1_ref, w2_ref, g2_ref, b2_ref, o_ref = tail_refs
    mix = jnp.dot(left, wo_ref[:MIX_W, :], preferred_element_type=_F32)
    mix = mix + jnp.dot(right, wo_ref[MIX_W:, :], preferred_element_type=_F32) + bo_ref[...]
    x = _layer_norm(DEEPNORM_ALPHA * x_ref[...] + mix, g1_ref[...], b1_ref[...])
    xb = x.astype(_BF)
    acc = jnp.zeros(x.shape, _F32)
    for c in range(w1_ref.shape[1] // MLP_CHUNK):
        cols = slice(c * MLP_CHUNK, (c + 1) * MLP_CHUNK)
        h = jnp.dot(xb, w1_ref[:, cols], preferred_element_type=_F32)
        h = jnp.square(jnp.maximum(h, 0.0)).astype(_BF)
        acc = acc + jnp.dot(h, w2_ref[cols, :], preferred_element_type=_F32)
    o_ref[...] = _layer_norm(DEEPNORM_ALPHA * x + acc, g2_ref[...], b2_ref[...])


def _even_tail_kernel(x_ref, ya_ref, yb_ref, *tail_refs):
    _layer_tail(x_ref, ya_ref[...], yb_ref[...], tail_refs)


def _group_c(pooled_ref, wg_ref, scale_ref):
    return (jnp.dot(pooled_ref[...], wg_ref[...], preferred_element_type=_F32) * scale_ref[...]).astype(_BF)


def _sample_odd_tail_kernel(x_ref, pooled_ref, wg_ref, scale_ref, yd_ref, *tail_refs):
    _layer_tail(x_ref, _group_c(pooled_ref, wg_ref, scale_ref), yd_ref[...], tail_refs)


def _odd_tail_kernel(x_ref, pooled_ref, wg_ref, scale_ref, o1_ref, o2_ref, o3_ref, s1_ref, s2_ref, s3_ref,
                     *tail_refs):
    yc = _group_c(pooled_ref, wg_ref, scale_ref)
    lses = (s1_ref[...], s2_ref[...], s3_ref[...])
    top = jnp.maximum(jnp.maximum(lses[0], lses[1]), lses[2])
    es = [jnp.exp(s - top) for s in lses]
    den = es[0] + es[1] + es[2]
    cs = [e / den for e in es]
    outs = (o1_ref, o2_ref, o3_ref)
    tm = x_ref.shape[0]
    left = _lane_is_left()
    tiles = []
    for p, (ha, hb) in enumerate(D_PAIR_HEADS):
        acc = None
        for c, o_ref_b in zip(cs, outs):
            wa = jnp.broadcast_to(c[:, ha:ha + 1], (tm, LANES))
            wb = jnp.broadcast_to(c[:, hb:hb + 1], (tm, LANES))
            term = jnp.where(left, wa, wb) * o_ref_b[:, p * LANES:(p + 1) * LANES].astype(_F32)
            acc = term if acc is None else acc + term
        tiles.append(acc)
    yd = jnp.concatenate(tiles, axis=1)
    _layer_tail(x_ref, yc, yd.astype(_BF), tail_refs)


def _row_spec(tm, width):
    return pl.BlockSpec((tm, width), lambda i: (i, 0))


def _resident_spec(shape):
    nd = len(shape)
    return pl.BlockSpec(shape, lambda *_: (0,) * nd, pipeline_mode=pl.Buffered(1))


def _tail_call(kernel_fn, name, x, mixer_args, mixer_specs, tail_params, tm):
    n, dm = x.shape
    return pl.pallas_call(
        kernel_fn,
        grid=(n // tm,),
        in_specs=[_row_spec(tm, dm)] + mixer_specs + [_resident_spec(p.shape) for p in tail_params],
        out_specs=_row_spec(tm, dm),
        out_shape=jax.ShapeDtypeStruct((n, dm), _F32),
        compiler_params=_cparams(1),
        name=name,
    )(x, *mixer_args, *tail_params)


def _even_tail(x, ya, yb, tail_params, tm):
    return _tail_call(_even_tail_kernel, "even_tail", x, [ya, yb], [_row_spec(tm, MIX_W)] * 2, tail_params, tm)


def _odd_tail(x, pooled, wg, scale, outs, stats, tail_params, tm):
    specs = ([_row_spec(tm, MIX_W), _resident_spec(wg.shape), _resident_spec(scale.shape)]
             + [_row_spec(tm, MIX_W)] * 3 + [_row_spec(tm, LANES)] * 3)
    return _tail_call(_odd_tail_kernel, "odd_tail", x, [pooled, wg, scale, *outs, *stats], specs, tail_params, tm)


def _sample_odd_tail(x, pooled, wg, scale, yd, tail_params, tm):
    specs = [_row_spec(tm, MIX_W), _resident_spec(wg.shape), _resident_spec(scale.shape), _row_spec(tm, MIX_W)]
    return _tail_call(_sample_odd_tail_kernel, "sample_odd_tail", x, [pooled, wg, scale, yd], specs, tail_params, tm)


def _fill_ext(ext_ref, hist, cur, t):
    nb = ext_ref.shape[0]
    ext_ref[:, 0:8, :] = jnp.zeros((nb, 8, MIX_W), _F32)
    ext_ref[:, 8:EXT0, :] = hist
    ext_ref[:, EXT0:EXT0 + t, :] = cur


def _conv_body(h, gb, gc, c_hist, w_ref, ext_ref, yb_ref, ctail_ref):
    t = h.shape[1]
    c = gc * h
    _fill_ext(ext_ref, c_hist, c, t)
    conv = ext_ref[:, EXT0 - 2:EXT0 - 2 + t, :] * w_ref[0:1, :]
    conv = conv + ext_ref[:, EXT0 - 1:EXT0 - 1 + t, :] * w_ref[1:2, :]
    conv = conv + c * w_ref[2:3, :]
    yb_ref[...] = (gb * conv).astype(yb_ref.dtype)
    ctail_ref[...] = ext_ref[:, EXT0 + t - 8:EXT0 + t, :]


def _prompt_conv_kernel(h_ref, gb_ref, gc_ref, hp_ref, gcp_ref, w_ref, yb_ref, ctail_ref, ext_ref):
    c_hist = jnp.where(pl.program_id(1) > 0, hp_ref[...].astype(_F32) * gcp_ref[...].astype(_F32), 0.0)
    _conv_body(h_ref[...].astype(_F32), gb_ref[...].astype(_F32), gc_ref[...].astype(_F32), c_hist,
               w_ref, ext_ref, yb_ref, ctail_ref)


def _sample_conv_kernel(h_ref, gb_ref, gc_ref, hist_ref, w_ref, yb_ref, ctail_ref, ext_ref):
    _conv_body(h_ref[...], gb_ref[...], gc_ref[...], hist_ref[...], w_ref, ext_ref, yb_ref, ctail_ref)


def _pool_body(u, hist, pos0, ext_ref, s2_ref, s4_ref, s8_ref, out_ref):
    nb, t, _ = u.shape
    _fill_ext(ext_ref, hist, u, t)
    hi = EXT0 + t
    zeros8 = jnp.zeros((nb, 8, MIX_W), _F32)
    s2_ref[:, 0:8, :] = zeros8
    s4_ref[:, 0:8, :] = zeros8
    s8_ref[:, 0:8, :] = zeros8
    s2_ref[:, 8:hi, :] = ext_ref[:, 8:hi, :] + ext_ref[:, 7:hi - 1, :]
    s4_ref[:, 8:hi, :] = s2_ref[:, 8:hi, :] + s2_ref[:, 6:hi - 2, :]
    s8_ref[:, 8:hi, :] = s4_ref[:, 8:hi, :] + s4_ref[:, 4:hi - 4, :]
    sums = (
        s2_ref[:, EXT0:hi, 0:LANES],
        s4_ref[:, EXT0:hi, LANES:2 * LANES],
        s8_ref[:, EXT0:hi, 2 * LANES:3 * LANES],
        s8_ref[:, EXT0:hi, 3 * LANES:] + s8_ref[:, EXT0 - 8:hi - 8, 3 * LANES:],
    )
    pos = (pos0 + lax.broadcasted_iota(jnp.int32, (1, t, LANES), 1) + 1).astype(_F32)
    tiles = []
    for g, (w, s) in enumerate(zip(POOL_WINDOWS, sums)):
        cnt = jnp.minimum(pos, float(w))
        tiles.append(s / cnt - u[:, :, g * LANES:(g + 1) * LANES])
    out_ref[...] = jnp.concatenate(tiles, axis=2).astype(out_ref.dtype)


def _prompt_pool_kernel(u_ref, up_ref, out_ref, ext_ref, s2_ref, s4_ref, s8_ref, *, tm):
    i = pl.program_id(1)
    hist = jnp.where(i > 0, up_ref[...].astype(_F32), 0.0)
    _pool_body(u_ref[...].astype(_F32), hist, i * tm, ext_ref, s2_ref, s4_ref, s8_ref, out_ref)


def _sample_pool_kernel(u_ref, hist_ref, out_ref, ext_ref, s2_ref, s4_ref, s8_ref):
    _pool_body(u_ref[...], hist_ref[...], PAST_LEN, ext_ref, s2_ref, s4_ref, s8_ref, out_ref)


def _prompt_conv(p3, conv_w, tm):
    bsz, s, _ = p3.shape
    col = lambda c: pl.BlockSpec((1, tm, MIX_W), lambda b, i: (b, i, c))
    prev = lambda c: pl.BlockSpec((1, HALO, MIX_W), lambda b, i: (b, jnp.maximum(i * (tm // HALO) - 1, 0), c))
    return pl.pallas_call(
        _prompt_conv_kernel,
        grid=(bsz, s // tm),
        in_specs=[col(1), col(2), col(3), prev(1), prev(3), _const_spec(conv_w.shape)],
        out_specs=[pl.BlockSpec((1, tm, MIX_W), lambda b, i: (b, i, 0)),
                   pl.BlockSpec((1, 8, MIX_W), lambda b, i: (b, i, 0))],
        out_shape=[jax.ShapeDtypeStruct((bsz, s, MIX_W), _BF),
                   jax.ShapeDtypeStruct((bsz, (s // tm) * 8, MIX_W), _F32)],
        scratch_shapes=[pltpu.VMEM((1, EXT0 + tm, MIX_W), _F32)],
        compiler_params=_cparams(2),
        name="prompt_conv",
    )(p3, p3, p3, p3, p3, conv_w)


def _sample_conv(p3, hist, conv_w, bt):
    bsz, t, _ = p3.shape
    col = lambda c: pl.BlockSpec((bt, t, MIX_W), lambda b: (b, 0, c))
    return pl.pallas_call(
        _sample_conv_kernel,
        grid=(bsz // bt,),
        in_specs=[col(1), col(2), col(3), pl.BlockSpec((bt, HALO, MIX_W), lambda b: (b, 0, 0)),
                  _const_spec(conv_w.shape)],
        out_specs=[pl.BlockSpec((bt, t, MIX_W), lambda b: (b, 0, 0)),
                   pl.BlockSpec((bt, 8, MIX_W), lambda b: (b, 0, 0))],
        out_shape=[jax.ShapeDtypeStruct((bsz, t, MIX_W), _BF), jax.ShapeDtypeStruct((bsz, 8, MIX_W), _F32)],
        scratch_shapes=[pltpu.VMEM((bt, EXT0 + t, MIX_W), _F32)],
        compiler_params=_cparams(1),
        name="sample_conv",
    )(p3, p3, p3, hist, conv_w)


def _prompt_pool(p3, tm):
    bsz, s, _ = p3.shape
    scratch = pltpu.VMEM((1, EXT0 + tm, MIX_W), _F32)
    return pl.pallas_call(
        functools.partial(_prompt_pool_kernel, tm=tm),
        grid=(bsz, s // tm),
        in_specs=[pl.BlockSpec((1, tm, MIX_W), lambda b, i: (b, i, 0)),
                  pl.BlockSpec((1, HALO, MIX_W), lambda b, i: (b, jnp.maximum(i * (tm // HALO) - 1, 0), 0))],
        out_specs=pl.BlockSpec((1, tm, MIX_W), lambda b, i: (b, i, 0)),
        out_shape=jax.ShapeDtypeStruct((bsz, s, MIX_W), _BF),
        scratch_shapes=[scratch] * 4,
        compiler_params=_cparams(2),
        name="prompt_pool",
    )(p3, p3)


def _sample_pool(p3, hist, bt):
    bsz, t, _ = p3.shape
    scratch = pltpu.VMEM((bt, EXT0 + t, MIX_W), _F32)
    return pl.pallas_call(
        _sample_pool_kernel,
        grid=(bsz // bt,),
        in_specs=[pl.BlockSpec((bt, t, MIX_W), lambda b: (b, 0, 0)),
                  pl.BlockSpec((bt, HALO, MIX_W), lambda b: (b, 0, 0))],
        out_specs=pl.BlockSpec((bt, t, MIX_W), lambda b: (b, 0, 0)),
        out_shape=jax.ShapeDtypeStruct((bsz, t, MIX_W), _BF),
        scratch_shapes=[scratch] * 4,
        compiler_params=_cparams(1),
        name="sample_pool",
    )(p3, hist)


def _split_heads(q_pair):
    left = _lane_is_left()
    zero = jnp.zeros_like(q_pair)
    return jnp.concatenate([jnp.where(left, q_pair, zero), jnp.where(left, zero, q_pair)], axis=0)


def _band_attn_kernel(*refs, tq, dilation, kv_lane, has_sink, want_stat):
    refs = list(refs)
    q_ref, kc_ref, kp_ref, vc_ref, vp_ref, bias_ref = refs[:6]
    rest = refs[6:]
    sink_ref = rest.pop(0) if has_sink else None
    o_ref = rest.pop(0)
    st_ref = rest.pop(0) if want_stat else None
    stage_ref = rest.pop(0) if dilation > 1 else None

    first = pl.program_id(1) == 0
    res = pl.program_id(2)
    left = _lane_is_left()
    lane = lax.broadcasted_iota(jnp.int32, (1, LANES), 1)
    prev_cols = lax.broadcasted_iota(jnp.int32, (1, 2 * BAND), 1) < BAND
    top_rows = lax.broadcasted_iota(jnp.int32, (2 * BAND, 1), 0) < BAND
    ones = jnp.ones((2 * BAND, LANES), _BF)

    for j in range(tq // BAND):
        rows = slice(j * BAND, (j + 1) * BAND)
        out_rows = rows if dilation == 1 else pl.ds(j * BAND * dilation + res, BAND, stride=dilation)
        stat = jnp.zeros((BAND, LANES), _F32)
        for p in range(N_PAIRS):
            kl = slice(kv_lane[p], kv_lane[p] + LANES)
            if j == 0:
                k_prev, v_prev = kp_ref[:, kl], vp_ref[:, kl]
            else:
                k_prev, v_prev = kc_ref[(j - 1) * BAND:j * BAND, kl], vc_ref[(j - 1) * BAND:j * BAND, kl]
            k2 = jnp.concatenate([k_prev, kc_ref[rows, kl]], axis=0)
            v2 = jnp.concatenate([v_prev, vc_ref[rows, kl]], axis=0)
            q2 = _split_heads(q_ref[rows, p * LANES:(p + 1) * LANES])
            s = lax.dot_general(q2, k2, (((1,), (1,)), ((), ())), preferred_element_type=_F32)
            s = s + bias_ref[p]
            if j == 0:
                s = jnp.where(jnp.logical_and(first, prev_cols), _NEG_INF, s)
            m = jnp.max(s, axis=1, keepdims=True)
            prob = jnp.exp(s - m).astype(_BF)
            r = jnp.dot(prob, jnp.concatenate([v2, ones], axis=1), preferred_element_type=_F32)
            pv, l = r[:, :LANES], r[:, LANES:]
            if has_sink:
                ha, hb = A_PAIR_HEADS[p]
                sink = jnp.where(top_rows, sink_ref[ha], sink_ref[hb])
                m2 = jnp.maximum(m, sink)
                a = jnp.exp(m - m2)
                o = pv * a / (l * a + jnp.exp(sink - m2))
            else:
                o = pv / l
            o_pair = jnp.where(left, o[:BAND], o[BAND:])
            if dilation == 1:
                o_ref[rows, p * LANES:(p + 1) * LANES] = o_pair.astype(o_ref.dtype)
            else:
                stage_ref[p, out_rows, :] = o_pair
            if want_stat:
                lse = m + jnp.log(l)
                ha, hb = D_PAIR_HEADS[p]
                stat = jnp.where(lane == ha, lse[:BAND], stat)
                stat = jnp.where(lane == hb, lse[BAND:], stat)
        if want_stat:
            st_ref[out_rows, :] = stat

    if dilation > 1:
        @pl.when(res == dilation - 1)
        def _():
            for p in range(N_PAIRS):
                o_ref[:, p * LANES:(p + 1) * LANES] = stage_ref[p].astype(o_ref.dtype)


def _band_bias(pair_heads, dilation):
    slopes = _alibi_slopes(N_HEADS)
    qi = np.arange(BAND)[:, None]
    kj = np.arange(2 * BAND)[None, :]
    dist = qi + BAND - kj
    valid = (dist >= 0) & (dist <= BAND)
    out = np.empty((len(pair_heads), 2 * BAND, 2 * BAND), np.float32)
    for p, heads in enumerate(pair_heads):
        for half, h in enumerate(heads):
            bias = -np.float32(slopes[h]) * (dist * dilation).astype(np.float32)
            out[p, half * BAND:(half + 1) * BAND] = np.where(valid, bias, -np.inf)
    return jnp.asarray(out)


BAND_TOKENS = 4096


def _band_attn(arr, q_col, k_col, v_col, kv_width, kv_lane, pair_heads, sinks, want_stat):
    bsz, dilation, n, _ = arr.shape
    seq = n * dilation
    tq = min(512, n, BAND_TOKENS // dilation)
    sub = tq // BAND
    assert q_col % MIX_W == 0 and k_col % kv_width == 0 and v_col % kv_width == 0

    def cur(col, w):
        return pl.BlockSpec((None, None, tq, w), lambda b, i, r: (b, r, i, col // w))

    def prev(col, w):
        return pl.BlockSpec((None, None, BAND, w), lambda b, i, r: (b, r, jnp.maximum(i * sub - 1, 0), col // w))

    in_specs = [cur(q_col, MIX_W), cur(k_col, kv_width), prev(k_col, kv_width), cur(v_col, kv_width),
                prev(v_col, kv_width), _const_spec((N_PAIRS, 2 * BAND, 2 * BAND))]
    args = [arr, arr, arr, arr, arr, _band_bias(pair_heads, dilation)]
    if sinks is not None:
        in_specs.append(pl.BlockSpec(memory_space=pltpu.SMEM))
        args.append(sinks)
    out_specs = [pl.BlockSpec((None, tq * dilation, MIX_W), lambda b, i, r: (b, i, 0))]
    out_shape = [jax.ShapeDtypeStruct((bsz, seq, MIX_W), _BF)]
    if want_stat:
        out_specs.append(pl.BlockSpec((None, tq * dilation, LANES), lambda b, i, r: (b, i, 0)))
        out_shape.append(jax.ShapeDtypeStruct((bsz, seq, LANES), _F32))
    scratch = [pltpu.VMEM((N_PAIRS, tq * dilation, LANES), _F32)] if dilation > 1 else []
    res = pl.pallas_call(
        functools.partial(_band_attn_kernel, tq=tq, dilation=dilation, kv_lane=kv_lane, has_sink=sinks is not None,
                          want_stat=want_stat),
        grid=(bsz, n // tq, dilation),
        in_specs=in_specs,
        out_specs=out_specs,
        out_shape=out_shape,
        scratch_shapes=scratch,
        compiler_params=_cparams(3),
        name=f"band_attn_d{dilation}",
    )(*args)
    o = res[0].reshape(bsz * seq, MIX_W)
    if want_stat:
        return o, res[1].reshape(bsz * seq, LANES)
    return o


def _sample_attn_unit(b, pair0, refs, sink_ref, *, n_hist, t_new, q_pairs_of_kv, pair_heads):
    q_ref, kn_ref, vn_ref, kc_ref, vc_ref, bias_ref, mult_ref, y_ref, ko_ref, vo_ref = refs
    left = _lane_is_left()
    new_lanes = lax.broadcasted_iota(jnp.int32, (1, LANES), 1) >= LANES - t_new
    top_rows = lax.broadcasted_iota(jnp.int32, (2 * t_new, 1), 0) < t_new
    zpad = jnp.zeros((LANES - t_new, LANES), _F32)
    mult = mult_ref[...]
    for kvp, q_pairs in enumerate(q_pairs_of_kv):
        heads = slice(2 * kvp, 2 * kvp + 2)
        lanes = slice(kvp * LANES, (kvp + 1) * LANES)
        ext = []
        for c_ref, n_ref, o_ref in ((kc_ref, kn_ref, ko_ref), (vc_ref, vn_ref, vo_ref)):
            old = c_ref[b, heads].reshape(LANES, n_hist)
            new = jnp.concatenate([zpad, n_ref[b, :, lanes]], axis=0).T
            rolled = pltpu.roll(old, n_hist - t_new, axis=1)
            tail = jnp.where(new_lanes, new, rolled[:, n_hist - LANES:])
            out = tail if n_hist == LANES else jnp.concatenate([rolled[:, :n_hist - LANES], tail], axis=1)
            o_ref[b, heads] = out.reshape(2, HEAD_DIM, n_hist)
            ext.append(jnp.concatenate([old.astype(_BF), new.astype(_BF)], axis=1))
        k_ext, v_ext = ext
        for p in q_pairs:
            q2 = _split_heads(q_ref[b, :, p * LANES:(p + 1) * LANES].astype(_BF))
            s = jnp.dot(q2, k_ext, preferred_element_type=_F32) + bias_ref[pair0 + p]
            m = jnp.max(s, axis=1, keepdims=True)
            prob = (mult * jnp.exp(s - m)).astype(_BF)
            l = jnp.sum(prob.astype(_F32), axis=1, keepdims=True)
            pv = lax.dot_general(prob, v_ext, (((1,), (1,)), ((), ())), preferred_element_type=_F32)
            if sink_ref is not None:
                ha, hb = pair_heads[p]
                sink = jnp.where(top_rows, sink_ref[ha], sink_ref[hb])
                m2 = jnp.maximum(m, sink)
                a = jnp.exp(m - m2)
                o = pv * a / (l * a + jnp.exp(sink - m2))
            else:
                o = pv / l
            y_ref[b, :, p * LANES:(p + 1) * LANES] = jnp.where(left, o[:t_new], o[t_new:]).astype(y_ref.dtype)


def _sample_attn_kernel(*refs, has_sink, **statics):
    refs = list(refs)
    sink_ref = refs.pop(7) if has_sink else None

    def one_batch(b, carry):
        _sample_attn_unit(b, 0, refs, sink_ref, **statics)
        return carry

    lax.fori_loop(0, refs[0].shape[0], one_batch, 0)


def _sample_tables(pair_heads, branches, n_hist, t_new):
    slopes = _alibi_slopes(N_HEADS)
    key_pos = np.concatenate([np.arange(n_hist), n_hist + np.arange(LANES) - (LANES - t_new)])
    is_key = np.concatenate([np.ones(n_hist, bool), np.arange(LANES) >= LANES - t_new])
    delta = (n_hist + np.arange(t_new))[:, None] - key_pos[None, :]
    mult = np.zeros(delta.shape, np.float32)
    for window, dil in branches:
        mult += ((delta >= 0) & (delta % dil == 0) & (delta <= window) & is_key[None, :]).astype(np.float32)
    bias = np.empty((len(pair_heads), 2 * t_new, key_pos.size), np.float32)
    for p, heads in enumerate(pair_heads):
        for half, h in enumerate(heads):
            b = -np.float32(slopes[h]) * delta.astype(np.float32)
            bias[p, half * t_new:(half + 1) * t_new] = np.where(mult > 0, b, -np.inf)
    return jnp.asarray(bias), jnp.asarray(np.concatenate([mult, mult], axis=0))


UNIT_HEADS = 4


def _tail_cache_kernel(x_ref, ya_ref, yb_ref, wo_ref, bo_ref, g1_ref, b1_ref, w1_ref, w2_ref, g2_ref, b2_ref,
                       q_ref, kn_ref, vn_ref, kc_ref, vc_ref, bias_ref, mult_ref,
                       o_ref, y_ref, ko_ref, vo_ref, xb_ref, acc_ref, *, units_per_row, **statics):
    c = pl.program_id(1)
    n_chunks = pl.num_programs(1)
    unit = pl.program_id(0) * n_chunks + c

    @pl.when(c == 0)
    def _():
        mix = jnp.dot(ya_ref[...], wo_ref[:MIX_W, :], preferred_element_type=_F32)
        mix = mix + jnp.dot(yb_ref[...], wo_ref[MIX_W:, :], preferred_element_type=_F32) + bo_ref[...]
        x1 = _layer_norm(DEEPNORM_ALPHA * x_ref[...] + mix, g1_ref[...], b1_ref[...])
        xb_ref[...] = x1.astype(_BF)
        acc_ref[...] = DEEPNORM_ALPHA * x1

    pair0 = (unit % units_per_row) * (UNIT_HEADS // 2)
    _sample_attn_unit(0, pair0, (q_ref, kn_ref, vn_ref, kc_ref, vc_ref, bias_ref, mult_ref, y_ref, ko_ref, vo_ref),
                      None, **statics)
    h = jnp.dot(xb_ref[...], w1_ref[c], preferred_element_type=_F32)
    h = jnp.square(jnp.maximum(h, 0.0)).astype(_BF)
    acc_ref[...] += jnp.dot(h, w2_ref[c], preferred_element_type=_F32)

    @pl.when(c == n_chunks - 1)
    def _():
        o_ref[...] = _layer_norm(acc_ref[...], g2_ref[...], b2_ref[...])


def _even_tail_with_cache(x, ya, yb, tail_params, p3, q_col, k_col, v_col, pair_heads, branches, k_cache, v_cache,
                          tm, n_chunks):
    n, dm = x.shape
    bsz, t_new, _ = p3.shape
    _, n_hist, kvh, _ = k_cache.shape
    units_per_row = kvh // UNIT_HEADS
    unit_w = UNIT_HEADS * HEAD_DIM
    assert (n // tm) * n_chunks == bsz * units_per_row and kvh == N_HEADS
    wo, bo, g1, b1, w1, w2, g2, b2 = tail_params
    dh = w1.shape[1]
    w1c = jnp.transpose(w1.reshape(dm, n_chunks, dh // n_chunks), (1, 0, 2))
    w2c = w2.reshape(n_chunks, dh // n_chunks, dm)
    kt = jnp.transpose(k_cache, (0, 2, 3, 1))
    vt = jnp.transpose(v_cache, (0, 2, 3, 1))
    bias, mult = _sample_tables(pair_heads, branches, n_hist, t_new)

    row = lambda w: pl.BlockSpec((tm, w), lambda t, c: (t, 0))
    unit_of = lambda t, c: t * n_chunks + c
    new = lambda col: pl.BlockSpec(
        (1, t_new, unit_w),
        lambda t, c: (unit_of(t, c) // units_per_row, 0, col // unit_w + unit_of(t, c) % units_per_row))
    cache = pl.BlockSpec((1, UNIT_HEADS, HEAD_DIM, n_hist),
                         lambda t, c: (unit_of(t, c) // units_per_row, unit_of(t, c) % units_per_row, 0, 0))
    params = (wo, bo, g1, b1, w1c, w2c, g2, b2)
    out, y, ko, vo = pl.pallas_call(
        functools.partial(_tail_cache_kernel, units_per_row=units_per_row, n_hist=n_hist, t_new=t_new,
                          q_pairs_of_kv=tuple((p,) for p in range(UNIT_HEADS // 2)), pair_heads=pair_heads),
        grid=(n // tm, n_chunks),
        in_specs=[row(dm), row(MIX_W), row(MIX_W)] + [_resident_spec(p.shape) for p in params]
        + [new(q_col), new(k_col), new(v_col), cache, cache, _resident_spec(bias.shape), _resident_spec(mult.shape)],
        out_specs=[row(dm), new(0), cache, cache],
        out_shape=[jax.ShapeDtypeStruct((n, dm), _F32), jax.ShapeDtypeStruct((bsz, t_new, MIX_W), _BF),
                   jax.ShapeDtypeStruct(kt.shape, _F32), jax.ShapeDtypeStruct(vt.shape, _F32)],
        scratch_shapes=[pltpu.VMEM((tm, dm), _BF), pltpu.VMEM((tm, dm), _F32)],
        compiler_params=pltpu.CompilerParams(dimension_semantics=("arbitrary", "arbitrary"),
                                             vmem_limit_bytes=FUSED_VMEM_LIMIT),
        name="even_tail_with_cache",
    )(x, ya, yb, *params, p3, p3, p3, kt, vt, bias, mult)
    return out, y, jnp.transpose(ko, (0, 3, 1, 2)), jnp.transpose(vo, (0, 3, 1, 2))


def _sample_attn(p3, q_col, k_col, v_col, pair_heads, branches, k_cache, v_cache, sinks, bt):
    bsz, t_new, _ = p3.shape
    _, n_hist, kvh, _ = k_cache.shape
    kv_width = kvh * HEAD_DIM
    kt = jnp.transpose(k_cache, (0, 2, 3, 1))
    vt = jnp.transpose(v_cache, (0, 2, 3, 1))
    n_kv_pairs = kvh // 2
    q_pairs_of_kv = tuple(tuple(p for p in range(N_PAIRS) if p % n_kv_pairs == kvp) for kvp in range(n_kv_pairs))
    bias, mult = _sample_tables(pair_heads, branches, n_hist, t_new)
    new = lambda col, w: pl.BlockSpec((bt, t_new, w), lambda b: (b, 0, col // w))
    cache = pl.BlockSpec((bt, kvh, HEAD_DIM, n_hist), lambda b: (b, 0, 0, 0))
    in_specs = [new(q_col, MIX_W), new(k_col, kv_width), new(v_col, kv_width), cache, cache,
                _const_spec(bias.shape), _const_spec(mult.shape)]
    args = [p3, p3, p3, kt, vt, bias, mult]
    if sinks is not None:
        in_specs.append(pl.BlockSpec(memory_space=pltpu.SMEM))
        args.append(sinks)
    y, ko, vo = pl.pallas_call(
        functools.partial(_sample_attn_kernel, n_hist=n_hist, t_new=t_new, q_pairs_of_kv=q_pairs_of_kv,
                          pair_heads=pair_heads, has_sink=sinks is not None),
        grid=(bsz // bt,),
        in_specs=in_specs,
        out_specs=[pl.BlockSpec((bt, t_new, MIX_W), lambda b: (b, 0, 0)), cache, cache],
        out_shape=[jax.ShapeDtypeStruct((bsz, t_new, MIX_W), _BF),
                   jax.ShapeDtypeStruct(kt.shape, _F32), jax.ShapeDtypeStruct(vt.shape, _F32)],
        compiler_params=_cparams(1),
        name=f"sample_attn_{n_hist}",
    )(*args)
    return y, jnp.transpose(ko, (0, 3, 1, 2)), jnp.transpose(vo, (0, 3, 1, 2))


A_Q = N_HEADS * HEAD_DIM
A_KV = A_KV_HEADS * HEAD_DIM
E_Q, E_H, E_GB, E_GC, E_K, E_V = 0, 512, 1024, 1536, 2048, 2176
O_U, O_Q, O_K, O_V = 0, 512, 1024, 1536


def _prep_layer_weights(even_w_in, even_b_in, even_w_out, odd_w_in, odd_b_in, c_w_group):
    q_cols = np.concatenate([h * HEAD_DIM + np.arange(HEAD_DIM) for h in A_HEAD_ORDER])
    o1, o2, o3 = A_Q, A_Q + A_KV, A_Q + 2 * A_KV
    order = np.concatenate([q_cols, np.arange(o3, o3 + 3 * MIX_W), np.arange(o1, o3)])
    scale = np.ones((order.size,), np.float32)
    scale[:A_Q] = HEAD_DIM ** -0.5
    ew = (even_w_in[:, order] * scale).astype(_BF)
    eb = (even_b_in[order] * scale)[None, :]
    ewo = jnp.concatenate([even_w_out[q_cols], even_w_out[A_Q:]], axis=0).astype(_BF)
    oscale = np.ones((odd_w_in.shape[1],), np.float32)
    oscale[O_Q:O_K] = HEAD_DIM ** -0.5
    ow = (odd_w_in * oscale).astype(_BF)
    ob = (odd_b_in * oscale)[None, :]
    groups, gw, _ = c_w_group.shape
    wg = jnp.zeros((MIX_W, MIX_W), _F32)
    for g in range(groups):
        wg = wg.at[g * gw:(g + 1) * gw, g * gw:(g + 1) * gw].set(c_w_group[g])
    return ew, eb, ewo, ow, ob, wg.astype(_BF)


def _row(v):
    return v[None, :]


def _forward(xp, xs, caches, wts, tm, bt_attn, bt_shift, n_chunks):
    (even_w_in, even_b_in, a_sinks, b_conv_w, even_w_out, even_b_out, odd_w_in, odd_b_in, c_w_group, c_scale,
     odd_w_out, odd_b_out, mlp_w1, mlp_w2, ln1_g, ln1_b, ln2_g, ln2_b) = wts
    cache_a_k, cache_a_v, state_b_conv, state_c_pool, cache_d_k, cache_d_v = caches
    bsz, seq, dm = xp.shape
    bs, ts, _ = xs.shape
    n, ns = bsz * seq, bs * ts
    tms = min(tm, ns)
    ew, eb, ewo, ow, ob, wg = _prep_layer_weights(even_w_in[0], even_b_in[0], even_w_out[0], odd_w_in[0],
                                                  odd_b_in[0], c_w_group[0])
    owo = odd_w_out[0].astype(_BF)
    w1 = mlp_w1.astype(_BF)
    w2 = mlp_w2.astype(_BF)
    tails = [(wo, _row(bo), _row(ln1_g[i]), _row(ln1_b[i]), w1[i], w2[i], _row(ln2_g[i]), _row(ln2_b[i]))
             for i, (wo, bo) in enumerate(((ewo, even_b_out[0]), (owo, odd_b_out[0])))]
    a_kv_lane = (0,) * N_PAIRS
    d_kv_lane = tuple(p * LANES for p in range(N_PAIRS))

    xs = xs.reshape(ns, dm)
    pe3 = _proj(xs, ew, eb, _F32, tms).reshape(bs, ts, -1)
    ya, ak_s, av_s = _sample_attn(pe3, E_Q, E_K, E_V, A_PAIR_HEADS, ((A_WINDOW, 1),), cache_a_k[0], cache_a_v[0],
                                  a_sinks[0], bt_attn)
    c_hist = jnp.pad(state_b_conv[0], ((0, 0), (HALO - (CONV_WIDTH - 1), 0), (0, 0)))
    yb, ctail = _sample_conv(pe3, c_hist, b_conv_w[0], bt_shift)
    bc_s = ctail[:, -(CONV_WIDTH - 1):, :]
    xs = _even_tail(xs, ya.reshape(ns, MIX_W), yb.reshape(ns, MIX_W), tails[0], tms)
    po3s = _proj(xs, ow, ob, _F32, tms).reshape(bs, ts, -1)
    u_hist = jnp.pad(state_c_pool[0], ((0, 0), (HALO - (POOL_MAX - 1), 0), (0, 0)))
    pooled_s = _sample_pool(po3s, u_hist, bt_shift).reshape(ns, MIX_W)
    cp_s = jnp.concatenate([state_c_pool[0], po3s[:, :, O_U:O_Q]], axis=1)[:, -(POOL_MAX - 1):]

    x = xp.reshape(n, dm)
    pe3 = _proj(x, ew, eb, _BF, tm).reshape(bsz, seq, -1)
    ya = _band_attn(pe3[:, None], E_Q, E_K, E_V, LANES, a_kv_lane, A_PAIR_HEADS, a_sinks[0], False)
    yb, ctail = _prompt_conv(pe3, b_conv_w[0], tm)
    n_keep = min(A_WINDOW, seq)
    a_k = pe3[:, seq - n_keep:, E_K:E_V].astype(_F32).reshape(bsz, n_keep, A_KV_HEADS, HEAD_DIM)
    a_v = pe3[:, seq - n_keep:, E_V:].astype(_F32).reshape(bsz, n_keep, A_KV_HEADS, HEAD_DIM)
    b_conv = ctail[:, -(CONV_WIDTH - 1):, :]
    x, yd, dk_s, dv_s = _even_tail_with_cache(x, ya, yb.reshape(n, MIX_W), tails[0], po3s, O_Q, O_K, O_V,
                                              D_PAIR_HEADS, D_BRANCHES, cache_d_k[0], cache_d_v[0], tm, n_chunks)
    xs = _sample_odd_tail(xs, pooled_s, wg, _row(c_scale[0]), yd.reshape(ns, MIX_W), tails[1], tms)

    dils = tuple(d for _, d in D_BRANCHES if d > 1)
    po3, *regrouped = _proj_dilated(x.reshape(bsz, seq, dm), ow, ob, tm, O_Q, dils)
    outs, stats = [], []
    for _, dil in D_BRANCHES:
        if dil == 1:
            o, st = _band_attn(po3[:, None], O_Q, O_K, O_V, MIX_W, d_kv_lane, D_PAIR_HEADS, None, True)
        else:
            o, st = _band_attn(regrouped[dils.index(dil)], 0, O_K - O_Q, O_V - O_Q, MIX_W, d_kv_lane,
                               D_PAIR_HEADS, None, True)
        outs.append(o)
        stats.append(st)
    pooled = _prompt_pool(po3, tm).reshape(n, MIX_W)
    n_keep = min(D_BRANCHES[-1][0], seq)
    c_pool = po3[:, seq - (POOL_MAX - 1):, O_U:O_Q].astype(_F32)
    d_k = po3[:, seq - n_keep:, O_K:O_V].astype(_F32).reshape(bsz, n_keep, N_HEADS, HEAD_DIM)
    d_v = po3[:, seq - n_keep:, O_V:].astype(_F32).reshape(bsz, n_keep, N_HEADS, HEAD_DIM)
    x = _odd_tail(x, pooled, wg, _row(c_scale[0]), outs, stats, tails[1], tm)
    return (x.reshape(bsz, seq, dm), xs.reshape(bs, ts, dm), a_k[None], a_v[None], b_conv[None], c_pool[None],
            d_k[None], d_v[None], ak_s[None], av_s[None], bc_s[None], cp_s[None], dk_s[None], dv_s[None])


def kernel(x_prompt, x_sample, cache_a_k, cache_a_v, state_b_conv, state_c_pool, cache_d_k, cache_d_v, even_w_in, even_b_in, a_sinks, b_conv_w, even_w_out, even_b_out, odd_w_in, odd_b_in, c_w_group, c_scale, odd_w_out, odd_b_out, mlp_w1, mlp_w2, ln1_g, ln1_b, ln2_g, ln2_b):
    wts = (even_w_in, even_b_in, a_sinks, b_conv_w, even_w_out, even_b_out, odd_w_in, odd_b_in, c_w_group, c_scale,
           odd_w_out, odd_b_out, mlp_w1, mlp_w2, ln1_g, ln1_b, ln2_g, ln2_b)
    caches = (cache_a_k, cache_a_v, state_b_conv, state_c_pool, cache_d_k, cache_d_v)
    return _forward(x_prompt, x_sample, caches, wts, tm=512, bt_attn=16, bt_shift=32, n_chunks=4)
```

```python
import functools

import numpy as np
import jax
import jax.numpy as jnp
from jax import lax
from jax.experimental import pallas as pl
from jax.experimental.pallas import tpu as pltpu

HEAD_DIM = 64
N_HEADS = 8
A_KV_HEADS = 2
A_WINDOW = 128
D_BRANCHES = ((128, 1), (512, 4), (2048, 16))
CONV_WIDTH = 3
POOL_WINDOWS = (2, 4, 8, 16)
POOL_MAX = 16
DEPTH = 2
PAST_LEN = 16384
DEEPNORM_ALPHA = (2 * DEPTH) ** 0.25
LN_EPS = 1e-5

MIX_W = N_HEADS * HEAD_DIM
LANES = 128
N_PAIRS = MIX_W // LANES
BAND = 128
HALO = 16
EXT0 = 24
VMEM_LIMIT = 56 * 1024 * 1024
FUSED_VMEM_LIMIT = 62 * 1024 * 1024

A_HEAD_ORDER = (0, 4, 1, 5, 2, 6, 3, 7)
A_PAIR_HEADS = tuple((p, p + 4) for p in range(N_PAIRS))
D_PAIR_HEADS = tuple((2 * p, 2 * p + 1) for p in range(N_PAIRS))

_BF = jnp.bfloat16
_F32 = jnp.float32
_NEG_INF = float("-inf")


def _alibi_slopes(n_heads):
    return 2.0 ** (-8.0 * np.arange(1, n_heads + 1) / n_heads)


def _cparams(n_axes):
    return pltpu.CompilerParams(dimension_semantics=("arbitrary",) * n_axes, vmem_limit_bytes=VMEM_LIMIT)


def _const_spec(shape):
    nd = len(shape)
    return pl.BlockSpec(shape, lambda *_: (0,) * nd)


def _layer_norm(y, g, b):
    mu = jnp.mean(y, axis=-1, keepdims=True)
    yc = y - mu
    var = jnp.mean(yc * yc, axis=-1, keepdims=True)
    return yc * lax.rsqrt(var + LN_EPS) * g + b


def _proj_kernel(x_ref, w_ref, b_ref, o_ref, *, tn):
    x = x_ref[...].astype(_BF)
    for j in range(o_ref.shape[1] // tn):
        cols = slice(j * tn, (j + 1) * tn)
        acc = jnp.dot(x, w_ref[:, cols], preferred_element_type=_F32)
        o_ref[:, cols] = (acc + b_ref[:, cols]).astype(o_ref.dtype)


def _proj(x, w, b, out_dtype, tm):
    n, k = x.shape
    m = w.shape[1]
    return pl.pallas_call(
        functools.partial(_proj_kernel, tn=256),
        grid=(n // tm,),
        in_specs=[pl.BlockSpec((tm, k), lambda i: (i, 0)), _const_spec((k, m)), _const_spec((1, m))],
        out_specs=pl.BlockSpec((tm, m), lambda i: (i, 0)),
        out_shape=jax.ShapeDtypeStruct((n, m), out_dtype),
        compiler_params=_cparams(1),
        name="proj",
    )(x, w, b)


def _proj_dilated_kernel(x_ref, w_ref, b_ref, o_ref, *rest, tn, first_col, dilations):
    dil_refs, (stage_ref, stage4_ref) = rest[:-2], rest[-2:]
    d4_ref, d16_ref = dil_refs
    tm = x_ref.shape[0]
    x = x_ref[...].astype(_BF)
    for j in range(o_ref.shape[1] // tn):
        cols = slice(j * tn, (j + 1) * tn)
        acc = jnp.dot(x, w_ref[:, cols], preferred_element_type=_F32) + b_ref[:, cols]
        o_ref[:, cols] = acc.astype(o_ref.dtype)
        if j * tn >= first_col:
            for h in range(tn // LANES):
                stage_ref[(j * tn - first_col) // LANES + h] = acc[:, h * LANES:(h + 1) * LANES]
    q4, q16 = tm // 4, tm // 16
    for s in range(stage_ref.shape[0]):
        lanes = slice(s * LANES, (s + 1) * LANES)
        for r in range(4):
            rows = stage_ref[s, pl.ds(r, q4, stride=4), :]
            stage4_ref[s, r * q4:(r + 1) * q4, :] = rows
            d4_ref[r, :, lanes] = rows.astype(d4_ref.dtype)
        for r in range(4):
            for k in range(4):
                d16_ref[r + 4 * k, :, lanes] = stage4_ref[s, pl.ds(r * q4 + k, q16, stride=4), :].astype(d16_ref.dtype)


def _proj_dilated(x3, w, b, tm, first_col, dilations):
    bsz, seq, k = x3.shape
    m = w.shape[1]
    wd = m - first_col
    assert first_col % 256 == 0 and tuple(dilations) == (4, 16) and tm % 256 == 0
    out_specs = [pl.BlockSpec((None, tm, m), lambda bi, i: (bi, i, 0))]
    out_shape = [jax.ShapeDtypeStruct((bsz, seq, m), _BF)]
    for d in dilations:
        out_specs.append(pl.BlockSpec((None, d, tm // d, wd), lambda bi, i: (bi, 0, i, 0)))
        out_shape.append(jax.ShapeDtypeStruct((bsz, d, seq // d, wd), _BF))
    return pl.pallas_call(
        functools.partial(_proj_dilated_kernel, tn=256, first_col=first_col, dilations=dilations),
        grid=(bsz, seq // tm),
        in_specs=[pl.BlockSpec((None, tm, k), lambda bi, i: (bi, i, 0)), _const_spec((k, m)), _const_spec((1, m))],
        out_specs=out_specs,
        out_shape=out_shape,
        scratch_shapes=[pltpu.VMEM((wd // LANES, tm, LANES), _F32)] * 2,
        compiler_params=_cparams(2),
        name="proj_dilated",
    )(x3, w, b)


MLP_CHUNK = 512


def _lane_is_left():
    return lax.broadcasted_iota(jnp.int32, (1, LANES), 1) < HEAD_DIM


def _layer_tail(x_ref, left, right, tail_refs):
    wo_ref, bo_ref, g1_ref, b1_ref, w1_ref, w2_ref, g2_ref, b2_ref, o_ref = tail_refs
    mix = jnp.dot(left, wo_ref[:MIX_W, :], preferred_element_type=_F32)
    mix = mix + jnp.dot(right, wo_ref[MIX_W:, :], preferred_element_type=_F32) + bo_ref[...]
    x = _layer_norm(DEEPNORM_ALPHA * x_ref[...] + mix, g1_ref[...], b1_ref[...])
    xb = x.astype(_BF)
    acc = jnp.zeros(x.shape, _F32)
    for c in range(w1_ref.shape[1] // MLP_CHUNK):
        cols = slice(c * MLP_CHUNK, (c + 1) * MLP_CHUNK)
        h = jnp.dot(xb, w1_ref[:, cols], preferred_element_type=_F32)
        h = jnp.square(jnp.maximum(h, 0.0)).astype(_BF)
        acc = acc + jnp.dot(h, w2_ref[cols, :], preferred_element_type=_F32)
    o_ref[...] = _layer_norm(DEEPNORM_ALPHA * x + acc, g2_ref[...], b2_ref[...])


def _even_tail_kernel(x_ref, ya_ref, yb_ref, *tail_refs):
    _layer_tail(x_ref, ya_ref[...], yb_ref[...], tail_refs)


def _group_c(pooled_ref, wg_ref, scale_ref):
    return (jnp.dot(pooled_ref[...], wg_ref[...], preferred_element_type=_F32) * scale_ref[...]).astype(_BF)


def _sample_odd_tail_kernel(x_ref, pooled_ref, wg_ref, scale_ref, yd_ref, *tail_refs):
    _layer_tail(x_ref, _group_c(pooled_ref, wg_ref, scale_ref), yd_ref[...], tail_refs)


def _odd_tail_kernel(x_ref, pooled_ref, wg_ref, scale_ref, o1_ref, o2_ref, o3_ref, s1_ref, s2_ref, s3_ref,
                     *tail_refs):
    yc = _group_c(pooled_ref, wg_ref, scale_ref)
    tiles = []
    for p in range(N_PAIRS):
        lanes = slice(p * LANES, (p + 1) * LANES)
        lses = (s1_ref[p], s2_ref[p], s3_ref[p])
        top = jnp.maximum(jnp.maximum(lses[0], lses[1]), lses[2])
        es = [jnp.exp(s - top) for s in lses]
        num = es[0] * o1_ref[:, lanes].astype(_F32)
        num = num + es[1] * o2_ref[:, lanes].astype(_F32)
        num = num + es[2] * o3_ref[:, lanes].astype(_F32)
        tiles.append(num / (es[0] + es[1] + es[2]))
    yd = jnp.concatenate(tiles, axis=1)
    _layer_tail(x_ref, yc, yd.astype(_BF), tail_refs)


def _row_spec(tm, width):
    return pl.BlockSpec((tm, width), lambda i: (i, 0))


def _resident_spec(shape):
    nd = len(shape)
    return pl.BlockSpec(shape, lambda *_: (0,) * nd, pipeline_mode=pl.Buffered(1))


def _tail_call(kernel_fn, name, x, mixer_args, mixer_specs, tail_params, tm):
    n, dm = x.shape
    return pl.pallas_call(
        kernel_fn,
        grid=(n // tm,),
        in_specs=[_row_spec(tm, dm)] + mixer_specs + [_resident_spec(p.shape) for p in tail_params],
        out_specs=_row_spec(tm, dm),
        out_shape=jax.ShapeDtypeStruct((n, dm), _F32),
        compiler_params=_cparams(1),
        name=name,
    )(x, *mixer_args, *tail_params)


def _even_tail(x, ya, yb, tail_params, tm):
    return _tail_call(_even_tail_kernel, "even_tail", x, [ya, yb], [_row_spec(tm, MIX_W)] * 2, tail_params, tm)


def _odd_tail(x, pooled, wg, scale, outs, stats, tail_params, tm):
    tiles_per_seq = stats[0].shape[2] // tm
    stat_spec = pl.BlockSpec((None, N_PAIRS, tm, LANES), lambda i: (i // tiles_per_seq, 0, i % tiles_per_seq, 0))
    specs = ([_row_spec(tm, MIX_W), _resident_spec(wg.shape), _resident_spec(scale.shape)]
             + [_row_spec(tm, MIX_W)] * 3 + [stat_spec] * 3)
    return _tail_call(_odd_tail_kernel, "odd_tail", x, [pooled, wg, scale, *outs, *stats], specs, tail_params, tm)


def _sample_odd_tail(x, pooled, wg, scale, yd, tail_params, tm):
    specs = [_row_spec(tm, MIX_W), _resident_spec(wg.shape), _resident_spec(scale.shape), _row_spec(tm, MIX_W)]
    return _tail_call(_sample_odd_tail_kernel, "sample_odd_tail", x, [pooled, wg, scale, yd], specs, tail_params, tm)


def _fill_ext(ext_ref, hist, cur, t):
    nb = ext_ref.shape[0]
    ext_ref[:, 0:8, :] = jnp.zeros((nb, 8, MIX_W), _F32)
    ext_ref[:, 8:EXT0, :] = hist
    ext_ref[:, EXT0:EXT0 + t, :] = cur


def _conv_body(h, gb, gc, c_hist, w_ref, ext_ref, yb_ref, ctail_ref):
    t = h.shape[1]
    c = gc * h
    _fill_ext(ext_ref, c_hist, c, t)
    conv = ext_ref[:, EXT0 - 2:EXT0 - 2 + t, :] * w_ref[0:1, :]
    conv = conv + ext_ref[:, EXT0 - 1:EXT0 - 1 + t, :] * w_ref[1:2, :]
    conv = conv + c * w_ref[2:3, :]
    yb_ref[...] = (gb * conv).astype(yb_ref.dtype)
    ctail_ref[...] = ext_ref[:, EXT0 + t - 8:EXT0 + t, :]


def _prompt_conv_kernel(h_ref, gb_ref, gc_ref, hp_ref, gcp_ref, w_ref, yb_ref, ctail_ref, ext_ref):
    c_hist = jnp.where(pl.program_id(1) > 0, hp_ref[...].astype(_F32) * gcp_ref[...].astype(_F32), 0.0)
    _conv_body(h_ref[...].astype(_F32), gb_ref[...].astype(_F32), gc_ref[...].astype(_F32), c_hist,
               w_ref, ext_ref, yb_ref, ctail_ref)


def _sample_conv_kernel(h_ref, gb_ref, gc_ref, hist_ref, w_ref, yb_ref, ctail_ref, ext_ref):
    _conv_body(h_ref[...], gb_ref[...], gc_ref[...], hist_ref[...], w_ref, ext_ref, yb_ref, ctail_ref)


def _pool_body(u, hist, pos0, ext_ref, s2_ref, s4_ref, s8_ref, out_ref):
    nb, t, _ = u.shape
    _fill_ext(ext_ref, hist, u, t)
    hi = EXT0 + t
    zeros8 = jnp.zeros((nb, 8, MIX_W), _F32)
    s2_ref[:, 0:8, :] = zeros8
    s4_ref[:, 0:8, :] = zeros8
    s8_ref[:, 0:8, :] = zeros8
    s2_ref[:, 8:hi, :] = ext_ref[:, 8:hi, :] + ext_ref[:, 7:hi - 1, :]
    s4_ref[:, 8:hi, :] = s2_ref[:, 8:hi, :] + s2_ref[:, 6:hi - 2, :]
    s8_ref[:, 8:hi, :] = s4_ref[:, 8:hi, :] + s4_ref[:, 4:hi - 4, :]
    sums = (
        s2_ref[:, EXT0:hi, 0:LANES],
        s4_ref[:, EXT0:hi, LANES:2 * LANES],
        s8_ref[:, EXT0:hi, 2 * LANES:3 * LANES],
        s8_ref[:, EXT0:hi, 3 * LANES:] + s8_ref[:, EXT0 - 8:hi - 8, 3 * LANES:],
    )
    pos = (pos0 + lax.broadcasted_iota(jnp.int32, (1, t, LANES), 1) + 1).astype(_F32)
    tiles = []
    for g, (w, s) in enumerate(zip(POOL_WINDOWS, sums)):
        cnt = jnp.minimum(pos, float(w))
        tiles.append(s / cnt - u[:, :, g * LANES:(g + 1) * LANES])
    out_ref[...] = jnp.concatenate(tiles, axis=2).astype(out_ref.dtype)


def _prompt_pool_kernel(u_ref, up_ref, out_ref, ext_ref, s2_ref, s4_ref, s8_ref, *, tm):
    i = pl.program_id(1)
    hist = jnp.where(i > 0, up_ref[...].astype(_F32), 0.0)
    _pool_body(u_ref[...].astype(_F32), hist, i * tm, ext_ref, s2_ref, s4_ref, s8_ref, out_ref)


def _sample_pool_kernel(u_ref, hist_ref, out_ref, ext_ref, s2_ref, s4_ref, s8_ref):
    _pool_body(u_ref[...], hist_ref[...], PAST_LEN, ext_ref, s2_ref, s4_ref, s8_ref, out_ref)


def _prompt_conv(p3, conv_w, tm):
    bsz, s, _ = p3.shape
    col = lambda c: pl.BlockSpec((1, tm, MIX_W), lambda b, i: (b, i, c))
    prev = lambda c: pl.BlockSpec((1, HALO, MIX_W), lambda b, i: (b, jnp.maximum(i * (tm // HALO) - 1, 0), c))
    return pl.pallas_call(
        _prompt_conv_kernel,
        grid=(bsz, s // tm),
        in_specs=[col(1), col(2), col(3), prev(1), prev(3), _const_spec(conv_w.shape)],
        out_specs=[pl.BlockSpec((1, tm, MIX_W), lambda b, i: (b, i, 0)),
                   pl.BlockSpec((1, 8, MIX_W), lambda b, i: (b, i, 0))],
        out_shape=[jax.ShapeDtypeStruct((bsz, s, MIX_W), _BF),
                   jax.ShapeDtypeStruct((bsz, (s // tm) * 8, MIX_W), _F32)],
        scratch_shapes=[pltpu.VMEM((1, EXT0 + tm, MIX_W), _F32)],
        compiler_params=_cparams(2),
        name="prompt_conv",
    )(p3, p3, p3, p3, p3, conv_w)


def _sample_conv(p3, hist, conv_w, bt):
    bsz, t, _ = p3.shape
    col = lambda c: pl.BlockSpec((bt, t, MIX_W), lambda b: (b, 0, c))
    return pl.pallas_call(
        _sample_conv_kernel,
        grid=(bsz // bt,),
        in_specs=[col(1), col(2), col(3), pl.BlockSpec((bt, HALO, MIX_W), lambda b: (b, 0, 0)),
                  _const_spec(conv_w.shape)],
        out_specs=[pl.BlockSpec((bt, t, MIX_W), lambda b: (b, 0, 0)),
                   pl.BlockSpec((bt, 8, MIX_W), lambda b: (b, 0, 0))],
        out_shape=[jax.ShapeDtypeStruct((bsz, t, MIX_W), _BF), jax.ShapeDtypeStruct((bsz, 8, MIX_W), _F32)],
        scratch_shapes=[pltpu.VMEM((bt, EXT0 + t, MIX_W), _F32)],
        compiler_params=_cparams(1),
        name="sample_conv",
    )(p3, p3, p3, hist, conv_w)


def _prompt_pool(p3, tm):
    bsz, s, _ = p3.shape
    scratch = pltpu.VMEM((1, EXT0 + tm, MIX_W), _F32)
    return pl.pallas_call(
        functools.partial(_prompt_pool_kernel, tm=tm),
        grid=(bsz, s // tm),
        in_specs=[pl.BlockSpec((1, tm, MIX_W), lambda b, i: (b, i, 0)),
                  pl.BlockSpec((1, HALO, MIX_W), lambda b, i: (b, jnp.maximum(i * (tm // HALO) - 1, 0), 0))],
        out_specs=pl.BlockSpec((1, tm, MIX_W), lambda b, i: (b, i, 0)),
        out_shape=jax.ShapeDtypeStruct((bsz, s, MIX_W), _BF),
        scratch_shapes=[scratch] * 4,
        compiler_params=_cparams(2),
        name="prompt_pool",
    )(p3, p3)


def _sample_pool(p3, hist, bt):
    bsz, t, _ = p3.shape
    scratch = pltpu.VMEM((bt, EXT0 + t, MIX_W), _F32)
    return pl.pallas_call(
        _sample_pool_kernel,
        grid=(bsz // bt,),
        in_specs=[pl.BlockSpec((bt, t, MIX_W), lambda b: (b, 0, 0)),
                  pl.BlockSpec((bt, HALO, MIX_W), lambda b: (b, 0, 0))],
        out_specs=pl.BlockSpec((bt, t, MIX_W), lambda b: (b, 0, 0)),
        out_shape=jax.ShapeDtypeStruct((bsz, t, MIX_W), _BF),
        scratch_shapes=[scratch] * 4,
        compiler_params=_cparams(1),
        name="sample_pool",
    )(p3, hist)


def _split_heads(q_pair):
    left = _lane_is_left()
    zero = jnp.zeros_like(q_pair)
    return jnp.concatenate([jnp.where(left, q_pair, zero), jnp.where(left, zero, q_pair)], axis=0)


def _band_attn_kernel(*refs, tq, dilation, kv_lane, has_sink, want_stat):
    refs = list(refs)
    q_ref, kc_ref, kp_ref, vc_ref, vp_ref, bias_ref = refs[:6]
    rest = refs[6:]
    sink_ref = rest.pop(0) if has_sink else None
    o_ref = rest.pop(0)
    st_ref = rest.pop(0) if want_stat else None
    stage_ref = rest.pop(0) if dilation > 1 else None

    first = pl.program_id(1) == 0
    res = pl.program_id(2)
    left = _lane_is_left()
    prev_cols = lax.broadcasted_iota(jnp.int32, (1, 2 * BAND), 1) < BAND
    top_rows = lax.broadcasted_iota(jnp.int32, (2 * BAND, 1), 0) < BAND
    ones = jnp.ones((2 * BAND, LANES), _BF)

    for j in range(tq // BAND):
        rows = slice(j * BAND, (j + 1) * BAND)
        out_rows = rows if dilation == 1 else pl.ds(j * BAND * dilation + res, BAND, stride=dilation)
        for p in range(N_PAIRS):
            kl = slice(kv_lane[p], kv_lane[p] + LANES)
            if j == 0:
                k_prev, v_prev = kp_ref[:, kl], vp_ref[:, kl]
            else:
                k_prev, v_prev = kc_ref[(j - 1) * BAND:j * BAND, kl], vc_ref[(j - 1) * BAND:j * BAND, kl]
            k2 = jnp.concatenate([k_prev, kc_ref[rows, kl]], axis=0)
            v2 = jnp.concatenate([v_prev, vc_ref[rows, kl]], axis=0)
            q2 = _split_heads(q_ref[rows, p * LANES:(p + 1) * LANES])
            s = lax.dot_general(q2, k2, (((1,), (1,)), ((), ())), preferred_element_type=_F32)
            s = s + bias_ref[p]
            if j == 0:
                s = jnp.where(jnp.logical_and(first, prev_cols), _NEG_INF, s)
            m = jnp.max(s, axis=1, keepdims=True)
            prob = jnp.exp(s - m).astype(_BF)
            r = jnp.dot(prob, jnp.concatenate([v2, ones], axis=1), preferred_element_type=_F32)
            pv, l = r[:, :LANES], r[:, LANES:]
            if has_sink:
                ha, hb = A_PAIR_HEADS[p]
                sink = jnp.where(top_rows, sink_ref[ha], sink_ref[hb])
                m2 = jnp.maximum(m, sink)
                a = jnp.exp(m - m2)
                o = pv * a / (l * a + jnp.exp(sink - m2))
            else:
                o = pv / l
            o_pair = jnp.where(left, o[:BAND], o[BAND:])
            if dilation == 1:
                o_ref[rows, p * LANES:(p + 1) * LANES] = o_pair.astype(o_ref.dtype)
            else:
                stage_ref[p, out_rows, :] = o_pair
            if want_stat:
                lse = m + jnp.log(l)
                st_ref[p, out_rows, :] = jnp.where(left, lse[:BAND], lse[BAND:])

    if dilation > 1:
        @pl.when(res == dilation - 1)
        def _():
            for p in range(N_PAIRS):
                o_ref[:, p * LANES:(p + 1) * LANES] = stage_ref[p].astype(o_ref.dtype)


def _band_bias(pair_heads, dilation):
    slopes = _alibi_slopes(N_HEADS)
    qi = np.arange(BAND)[:, None]
    kj = np.arange(2 * BAND)[None, :]
    dist = qi + BAND - kj
    valid = (dist >= 0) & (dist <= BAND)
    out = np.empty((len(pair_heads), 2 * BAND, 2 * BAND), np.float32)
    for p, heads in enumerate(pair_heads):
        for half, h in enumerate(heads):
            bias = -np.float32(slopes[h]) * (dist * dilation).astype(np.float32)
            out[p, half * BAND:(half + 1) * BAND] = np.where(valid, bias, -np.inf)
    return jnp.asarray(out)


BAND_TOKENS = 4096


def _band_attn(arr, q_col, k_col, v_col, kv_width, kv_lane, pair_heads, sinks, want_stat):
    bsz, dilation, n, _ = arr.shape
    seq = n * dilation
    tq = min(512, n, BAND_TOKENS // dilation)
    sub = tq // BAND
    assert q_col % MIX_W == 0 and k_col % kv_width == 0 and v_col % kv_width == 0

    def cur(col, w):
        return pl.BlockSpec((None, None, tq, w), lambda b, i, r: (b, r, i, col // w))

    def prev(col, w):
        return pl.BlockSpec((None, None, BAND, w), lambda b, i, r: (b, r, jnp.maximum(i * sub - 1, 0), col // w))

    in_specs = [cur(q_col, MIX_W), cur(k_col, kv_width), prev(k_col, kv_width), cur(v_col, kv_width),
                prev(v_col, kv_width), _const_spec((N_PAIRS, 2 * BAND, 2 * BAND))]
    args = [arr, arr, arr, arr, arr, _band_bias(pair_heads, dilation)]
    if sinks is not None:
        in_specs.append(pl.BlockSpec(memory_space=pltpu.SMEM))
        args.append(sinks)
    out_specs = [pl.BlockSpec((None, tq * dilation, MIX_W), lambda b, i, r: (b, i, 0))]
    out_shape = [jax.ShapeDtypeStruct((bsz, seq, MIX_W), _BF)]
    if want_stat:
        out_specs.append(pl.BlockSpec((None, N_PAIRS, tq * dilation, LANES), lambda b, i, r: (b, 0, i, 0)))
        out_shape.append(jax.ShapeDtypeStruct((bsz, N_PAIRS, seq, LANES), _F32))
    scratch = [pltpu.VMEM((N_PAIRS, tq * dilation, LANES), _F32)] if dilation > 1 else []
    res = pl.pallas_call(
        functools.partial(_band_attn_kernel, tq=tq, dilation=dilation, kv_lane=kv_lane, has_sink=sinks is not None,
                          want_stat=want_stat),
        grid=(bsz, n // tq, dilation),
        in_specs=in_specs,
        out_specs=out_specs,
        out_shape=out_shape,
        scratch_shapes=scratch,
        compiler_params=_cparams(3),
        name=f"band_attn_d{dilation}",
    )(*args)
    o = res[0].reshape(bsz * seq, MIX_W)
    return (o, res[1]) if want_stat else o


def _sample_attn_unit(b, pair0, refs, sink_ref, *, n_hist, t_new, q_pairs_of_kv, pair_heads):
    q_ref, kn_ref, vn_ref, kc_ref, vc_ref, bias_ref, mult_ref, y_ref, ko_ref, vo_ref = refs
    left = _lane_is_left()
    new_lanes = lax.broadcasted_iota(jnp.int32, (1, LANES), 1) >= LANES - t_new
    top_rows = lax.broadcasted_iota(jnp.int32, (2 * t_new, 1), 0) < t_new
    zpad = jnp.zeros((LANES - t_new, LANES), _F32)
    mult = mult_ref[...]
    for kvp, q_pairs in enumerate(q_pairs_of_kv):
        heads = slice(2 * kvp, 2 * kvp + 2)
        lanes = slice(kvp * LANES, (kvp + 1) * LANES)
        ext = []
        for c_ref, n_ref, o_ref in ((kc_ref, kn_ref, ko_ref), (vc_ref, vn_ref, vo_ref)):
            old = c_ref[b, heads].reshape(LANES, n_hist)
            new = jnp.concatenate([zpad, n_ref[b, :, lanes]], axis=0).T
            rolled = pltpu.roll(old, n_hist - t_new, axis=1)
            tail = jnp.where(new_lanes, new, rolled[:, n_hist - LANES:])
            out = tail if n_hist == LANES else jnp.concatenate([rolled[:, :n_hist - LANES], tail], axis=1)
            o_ref[b, heads] = out.reshape(2, HEAD_DIM, n_hist)
            ext.append(jnp.concatenate([old.astype(_BF), new.astype(_BF)], axis=1))
        k_ext, v_ext = ext
        for p in q_pairs:
            q2 = _split_heads(q_ref[b, :, p * LANES:(p + 1) * LANES].astype(_BF))
            s = jnp.dot(q2, k_ext, preferred_element_type=_F32) + bias_ref[pair0 + p]
            m = jnp.max(s, axis=1, keepdims=True)
            prob = (mult * jnp.exp(s - m)).astype(_BF)
            l = jnp.sum(prob.astype(_F32), axis=1, keepdims=True)
            pv = lax.dot_general(prob, v_ext, (((1,), (1,)), ((), ())), preferred_element_type=_F32)
            if sink_ref is not None:
                ha, hb = pair_heads[p]
                sink = jnp.where(top_rows, sink_ref[ha], sink_ref[hb])
                m2 = jnp.maximum(m, sink)
                a = jnp.exp(m - m2)
                o = pv * a / (l * a + jnp.exp(sink - m2))
            else:
                o = pv / l
            y_ref[b, :, p * LANES:(p + 1) * LANES] = jnp.where(left, o[:t_new], o[t_new:]).astype(y_ref.dtype)


def _sample_attn_kernel(*refs, has_sink, **statics):
    refs = list(refs)
    sink_ref = refs.pop(7) if has_sink else None

    def one_batch(b, carry):
        _sample_attn_unit(b, 0, refs, sink_ref, **statics)
        return carry

    lax.fori_loop(0, refs[0].shape[0], one_batch, 0)


def _sample_tables(pair_heads, branches, n_hist, t_new):
    slopes = _alibi_slopes(N_HEADS)
    key_pos = np.concatenate([np.arange(n_hist), n_hist + np.arange(LANES) - (LANES - t_new)])
    is_key = np.concatenate([np.ones(n_hist, bool), np.arange(LANES) >= LANES - t_new])
    delta = (n_hist + np.arange(t_new))[:, None] - key_pos[None, :]
    mult = np.zeros(delta.shape, np.float32)
    for window, dil in branches:
        mult += ((delta >= 0) & (delta % dil == 0) & (delta <= window) & is_key[None, :]).astype(np.float32)
    bias = np.empty((len(pair_heads), 2 * t_new, key_pos.size), np.float32)
    for p, heads in enumerate(pair_heads):
        for half, h in enumerate(heads):
            b = -np.float32(slopes[h]) * delta.astype(np.float32)
            bias[p, half * t_new:(half + 1) * t_new] = np.where(mult > 0, b, -np.inf)
    return jnp.asarray(bias), jnp.asarray(np.concatenate([mult, mult], axis=0))


UNIT_HEADS = 4


def _tail_cache_kernel(x_ref, ya_ref, yb_ref, wo_ref, bo_ref, g1_ref, b1_ref, w1_ref, w2_ref, g2_ref, b2_ref,
                       q_ref, kn_ref, vn_ref, kc_ref, vc_ref, bias_ref, mult_ref,
                       o_ref, y_ref, ko_ref, vo_ref, xb_ref, acc_ref, *, units_per_row, **statics):
    c = pl.program_id(1)
    n_chunks = pl.num_programs(1)
    unit = pl.program_id(0) * n_chunks + c

    @pl.when(c == 0)
    def _():
        mix = jnp.dot(ya_ref[...], wo_ref[:MIX_W, :], preferred_element_type=_F32)
        mix = mix + jnp.dot(yb_ref[...], wo_ref[MIX_W:, :], preferred_element_type=_F32) + bo_ref[...]
        x1 = _layer_norm(DEEPNORM_ALPHA * x_ref[...] + mix, g1_ref[...], b1_ref[...])
        xb_ref[...] = x1.astype(_BF)
        acc_ref[...] = DEEPNORM_ALPHA * x1

    h = jnp.dot(xb_ref[...], w1_ref[c], preferred_element_type=_F32)
    h = jnp.square(jnp.maximum(h, 0.0)).astype(_BF)
    acc_ref[...] += jnp.dot(h, w2_ref[c], preferred_element_type=_F32)
    pair0 = (unit % units_per_row) * (UNIT_HEADS // 2)
    _sample_attn_unit(0, pair0, (q_ref, kn_ref, vn_ref, kc_ref, vc_ref, bias_ref, mult_ref, y_ref, ko_ref, vo_ref),
                      None, **statics)

    @pl.when(c == n_chunks - 1)
    def _():
        o_ref[...] = _layer_norm(acc_ref[...], g2_ref[...], b2_ref[...])


def _even_tail_with_cache(x, ya, yb, tail_params, p3, q_col, k_col, v_col, pair_heads, branches, k_cache, v_cache,
                          tm, n_chunks):
    n, dm = x.shape
    bsz, t_new, _ = p3.shape
    _, n_hist, kvh, _ = k_cache.shape
    units_per_row = kvh // UNIT_HEADS
    unit_w = UNIT_HEADS * HEAD_DIM
    assert (n // tm) * n_chunks == bsz * units_per_row and kvh == N_HEADS
    wo, bo, g1, b1, w1, w2, g2, b2 = tail_params
    dh = w1.shape[1]
    w1c = jnp.transpose(w1.reshape(dm, n_chunks, dh // n_chunks), (1, 0, 2))
    w2c = w2.reshape(n_chunks, dh // n_chunks, dm)
    kt = jnp.transpose(k_cache, (0, 2, 3, 1))
    vt = jnp.transpose(v_cache, (0, 2, 3, 1))
    bias, mult = _sample_tables(pair_heads, branches, n_hist, t_new)

    row = lambda w: pl.BlockSpec((tm, w), lambda t, c: (t, 0))
    unit_of = lambda t, c: t * n_chunks + c
    new = lambda col: pl.BlockSpec(
        (1, t_new, unit_w),
        lambda t, c: (unit_of(t, c) // units_per_row, 0, col // unit_w + unit_of(t, c) % units_per_row))
    cache = pl.BlockSpec((1, UNIT_HEADS, HEAD_DIM, n_hist),
                         lambda t, c: (unit_of(t, c) // units_per_row, unit_of(t, c) % units_per_row, 0, 0))
    params = (wo, bo, g1, b1, w1c, w2c, g2, b2)
    out, y, ko, vo = pl.pallas_call(
        functools.partial(_tail_cache_kernel, units_per_row=units_per_row, n_hist=n_hist, t_new=t_new,
                          q_pairs_of_kv=tuple((p,) for p in range(UNIT_HEADS // 2)), pair_heads=pair_heads),
        grid=(n // tm, n_chunks),
        in_specs=[row(dm), row(MIX_W), row(MIX_W)] + [_resident_spec(p.shape) for p in params]
        + [new(q_col), new(k_col), new(v_col), cache, cache, _resident_spec(bias.shape), _resident_spec(mult.shape)],
        out_specs=[row(dm), new(0), cache, cache],
        out_shape=[jax.ShapeDtypeStruct((n, dm), _F32), jax.ShapeDtypeStruct((bsz, t_new, MIX_W), _BF),
                   jax.ShapeDtypeStruct(kt.shape, _F32), jax.ShapeDtypeStruct(vt.shape, _F32)],
        scratch_shapes=[pltpu.VMEM((tm, dm), _BF), pltpu.VMEM((tm, dm), _F32)],
        compiler_params=pltpu.CompilerParams(dimension_semantics=("arbitrary", "arbitrary"),
                                             vmem_limit_bytes=FUSED_VMEM_LIMIT),
        name="even_tail_with_cache",
    )(x, ya, yb, *params, p3, p3, p3, kt, vt, bias, mult)
    return out, y, jnp.transpose(ko, (0, 3, 1, 2)), jnp.transpose(vo, (0, 3, 1, 2))


def _sample_attn(p3, q_col, k_col, v_col, pair_heads, branches, k_cache, v_cache, sinks, bt):
    bsz, t_new, _ = p3.shape
    _, n_hist, kvh, _ = k_cache.shape
    kv_width = kvh * HEAD_DIM
    kt = jnp.transpose(k_cache, (0, 2, 3, 1))
    vt = jnp.transpose(v_cache, (0, 2, 3, 1))
    n_kv_pairs = kvh // 2
    q_pairs_of_kv = tuple(tuple(p for p in range(N_PAIRS) if p % n_kv_pairs == kvp) for kvp in range(n_kv_pairs))
    bias, mult = _sample_tables(pair_heads, branches, n_hist, t_new)
    new = lambda col, w: pl.BlockSpec((bt, t_new, w), lambda b: (b, 0, col // w))
    cache = pl.BlockSpec((bt, kvh, HEAD_DIM, n_hist), lambda b: (b, 0, 0, 0))
    in_specs = [new(q_col, MIX_W), new(k_col, kv_width), new(v_col, kv_width), cache, cache,
                _const_spec(bias.shape), _const_spec(mult.shape)]
    args = [p3, p3, p3, kt, vt, bias, mult]
    if sinks is not None:
        in_specs.append(pl.BlockSpec(memory_space=pltpu.SMEM))
        args.append(sinks)
    y, ko, vo = pl.pallas_call(
        functools.partial(_sample_attn_kernel, n_hist=n_hist, t_new=t_new, q_pairs_of_kv=q_pairs_of_kv,
                          pair_heads=pair_heads, has_sink=sinks is not None),
        grid=(bsz // bt,),
        in_specs=in_specs,
        out_specs=[pl.BlockSpec((bt, t_new, MIX_W), lambda b: (b, 0, 0)), cache, cache],
        out_shape=[jax.ShapeDtypeStruct((bsz, t_new, MIX_W), _BF),
                   jax.ShapeDtypeStruct(kt.shape, _F32), jax.ShapeDtypeStruct(vt.shape, _F32)],
        compiler_params=_cparams(1),
        name=f"sample_attn_{n_hist}",
    )(*args)
    return y, jnp.transpose(ko, (0, 3, 1, 2)), jnp.transpose(vo, (0, 3, 1, 2))


A_Q = N_HEADS * HEAD_DIM
A_KV = A_KV_HEADS * HEAD_DIM
E_Q, E_H, E_GB, E_GC, E_K, E_V = 0, 512, 1024, 1536, 2048, 2176
O_U, O_Q, O_K, O_V = 0, 512, 1024, 1536


def _prep_layer_weights(even_w_in, even_b_in, even_w_out, odd_w_in, odd_b_in, c_w_group):
    q_cols = np.concatenate([h * HEAD_DIM + np.arange(HEAD_DIM) for h in A_HEAD_ORDER])
    o1, o2, o3 = A_Q, A_Q + A_KV, A_Q + 2 * A_KV
    order = np.concatenate([q_cols, np.arange(o3, o3 + 3 * MIX_W), np.arange(o1, o3)])
    scale = np.ones((order.size,), np.float32)
    scale[:A_Q] = HEAD_DIM ** -0.5
    ew = (even_w_in[:, order] * scale).astype(_BF)
    eb = (even_b_in[order] * scale)[None, :]
    ewo = jnp.concatenate([even_w_out[q_cols], even_w_out[A_Q:]], axis=0).astype(_BF)
    oscale = np.ones((odd_w_in.shape[1],), np.float32)
    oscale[O_Q:O_K] = HEAD_DIM ** -0.5
    ow = (odd_w_in * oscale).astype(_BF)
    ob = (odd_b_in * oscale)[None, :]
    groups, gw, _ = c_w_group.shape
    wg = jnp.zeros((MIX_W, MIX_W), _F32)
    for g in range(groups):
        wg = wg.at[g * gw:(g + 1) * gw, g * gw:(g + 1) * gw].set(c_w_group[g])
    return ew, eb, ewo, ow, ob, wg.astype(_BF)


def _row(v):
    return v[None, :]


def _forward(xp, xs, caches, wts, tm, bt_attn, bt_shift, n_chunks):
    (even_w_in, even_b_in, a_sinks, b_conv_w, even_w_out, even_b_out, odd_w_in, odd_b_in, c_w_group, c_scale,
     odd_w_out, odd_b_out, mlp_w1, mlp_w2, ln1_g, ln1_b, ln2_g, ln2_b) = wts
    cache_a_k, cache_a_v, state_b_conv, state_c_pool, cache_d_k, cache_d_v = caches
    bsz, seq, dm = xp.shape
    bs, ts, _ = xs.shape
    n, ns = bsz * seq, bs * ts
    tms = min(tm, ns)
    ew, eb, ewo, ow, ob, wg = _prep_layer_weights(even_w_in[0], even_b_in[0], even_w_out[0], odd_w_in[0],
                                                  odd_b_in[0], c_w_group[0])
    owo = odd_w_out[0].astype(_BF)
    w1 = mlp_w1.astype(_BF)
    w2 = mlp_w2.astype(_BF)
    tails = [(wo, _row(bo), _row(ln1_g[i]), _row(ln1_b[i]), w1[i], w2[i], _row(ln2_g[i]), _row(ln2_b[i]))
             for i, (wo, bo) in enumerate(((ewo, even_b_out[0]), (owo, odd_b_out[0])))]
    a_kv_lane = (0,) * N_PAIRS
    d_kv_lane = tuple(p * LANES for p in range(N_PAIRS))

    xs = xs.reshape(ns, dm)
    pe3 = _proj(xs, ew, eb, _F32, tms).reshape(bs, ts, -1)
    ya, ak_s, av_s = _sample_attn(pe3, E_Q, E_K, E_V, A_PAIR_HEADS, ((A_WINDOW, 1),), cache_a_k[0], cache_a_v[0],
                                  a_sinks[0], bt_attn)
    c_hist = jnp.pad(state_b_conv[0], ((0, 0), (HALO - (CONV_WIDTH - 1), 0), (0, 0)))
    yb, ctail = _sample_conv(pe3, c_hist, b_conv_w[0], bt_shift)
    bc_s = ctail[:, -(CONV_WIDTH - 1):, :]
    xs = _even_tail(xs, ya.reshape(ns, MIX_W), yb.reshape(ns, MIX_W), tails[0], tms)
    po3s = _proj(xs, ow, ob, _F32, tms).reshape(bs, ts, -1)
    u_hist = jnp.pad(state_c_pool[0], ((0, 0), (HALO - (POOL_MAX - 1), 0), (0, 0)))
    pooled_s = _sample_pool(po3s, u_hist, bt_shift).reshape(ns, MIX_W)
    cp_s = jnp.concatenate([state_c_pool[0], po3s[:, :, O_U:O_Q]], axis=1)[:, -(POOL_MAX - 1):]

    x = xp.reshape(n, dm)
    pe3 = _proj(x, ew, eb, _BF, tm).reshape(bsz, seq, -1)
    ya = _band_attn(pe3[:, None], E_Q, E_K, E_V, LANES, a_kv_lane, A_PAIR_HEADS, a_sinks[0], False)
    yb, ctail = _prompt_conv(pe3, b_conv_w[0], tm)
    n_keep = min(A_WINDOW, seq)
    a_k = pe3[:, seq - n_keep:, E_K:E_V].astype(_F32).reshape(bsz, n_keep, A_KV_HEADS, HEAD_DIM)
    a_v = pe3[:, seq - n_keep:, E_V:].astype(_F32).reshape(bsz, n_keep, A_KV_HEADS, HEAD_DIM)
    b_conv = ctail[:, -(CONV_WIDTH - 1):, :]
    x, yd, dk_s, dv_s = _even_tail_with_cache(x, ya, yb.reshape(n, MIX_W), tails[0], po3s, O_Q, O_K, O_V,
                                              D_PAIR_HEADS, D_BRANCHES, cache_d_k[0], cache_d_v[0], tm, n_chunks)
    xs = _sample_odd_tail(xs, pooled_s, wg, _row(c_scale[0]), yd.reshape(ns, MIX_W), tails[1], tms)

    dils = tuple(d for _, d in D_BRANCHES if d > 1)
    po3, *regrouped = _proj_dilated(x.reshape(bsz, seq, dm), ow, ob, tm, O_Q, dils)
    outs, stats = [], []
    for _, dil in D_BRANCHES:
        if dil == 1:
            o, st = _band_attn(po3[:, None], O_Q, O_K, O_V, MIX_W, d_kv_lane, D_PAIR_HEADS, None, True)
        else:
            o, st = _band_attn(regrouped[dils.index(dil)], 0, O_K - O_Q, O_V - O_Q, MIX_W, d_kv_lane,
                               D_PAIR_HEADS, None, True)
        outs.append(o)
        stats.append(st)
    pooled = _prompt_pool(po3, tm).reshape(n, MIX_W)
    n_keep = min(D_BRANCHES[-1][0], seq)
    c_pool = po3[:, seq - (POOL_MAX - 1):, O_U:O_Q].astype(_F32)
    d_k = po3[:, seq - n_keep:, O_K:O_V].astype(_F32).reshape(bsz, n_keep, N_HEADS, HEAD_DIM)
    d_v = po3[:, seq - n_keep:, O_V:].astype(_F32).reshape(bsz, n_keep, N_HEADS, HEAD_DIM)
    x = _odd_tail(x, pooled, wg, _row(c_scale[0]), outs, stats, tails[1], tm)
    return (x.reshape(bsz, seq, dm), xs.reshape(bs, ts, dm), a_k[None], a_v[None], b_conv[None], c_pool[None],
            d_k[None], d_v[None], ak_s[None], av_s[None], bc_s[None], cp_s[None], dk_s[None], dv_s[None])


def kernel(x_prompt, x_sample, cache_a_k, cache_a_v, state_b_conv, state_c_pool, cache_d_k, cache_d_v, even_w_in, even_b_in, a_sinks, b_conv_w, even_w_out, even_b_out, odd_w_in, odd_b_in, c_w_group, c_scale, odd_w_out, odd_b_out, mlp_w1, mlp_w2, ln1_g, ln1_b, ln2_g, ln2_b):
    wts = (even_w_in, even_b_in, a_sinks, b_conv_w, even_w_out, even_b_out, odd_w_in, odd_b_in, c_w_group, c_scale,
           odd_w_out, odd_b_out, mlp_w1, mlp_w2, ln1_g, ln1_b, ln2_g, ln2_b)
    caches = (cache_a_k, cache_a_v, state_b_conv, state_c_pool, cache_d_k, cache_d_v)
    return _forward(x_prompt, x_sample, caches, wts, tm=512, bt_attn=16, bt_shift=32, n_chunks=4)
```

```python
import functools

import numpy as np
import jax
import jax.numpy as jnp
from jax import lax
from jax.experimental import pallas as pl
from jax.experimental.pallas import tpu as pltpu

HEAD_DIM = 64
N_HEADS = 8
A_KV_HEADS = 2
A_WINDOW = 128
D_BRANCHES = ((128, 1), (512, 4), (2048, 16))
CONV_WIDTH = 3
POOL_WINDOWS = (2, 4, 8, 16)
POOL_MAX = 16
DEPTH = 2
PAST_LEN = 16384
DEEPNORM_ALPHA = (2 * DEPTH) ** 0.25
LN_EPS = 1e-5

MIX_W = N_HEADS * HEAD_DIM
LANES = 128
N_PAIRS = MIX_W // LANES
BAND = 128
HALO = 16
EXT0 = 24
VMEM_LIMIT = 56 * 1024 * 1024
FUSED_VMEM_LIMIT = 62 * 1024 * 1024

A_HEAD_ORDER = (0, 4, 1, 5, 2, 6, 3, 7)
A_PAIR_HEADS = tuple((p, p + 4) for p in range(N_PAIRS))
D_PAIR_HEADS = tuple((2 * p, 2 * p + 1) for p in range(N_PAIRS))

_BF = jnp.bfloat16
_F32 = jnp.float32
_NEG_INF = float("-inf")


def _alibi_slopes(n_heads):
    return 2.0 ** (-8.0 * np.arange(1, n_heads + 1) / n_heads)


def _cparams(n_axes):
    return pltpu.CompilerParams(dimension_semantics=("arbitrary",) * n_axes, vmem_limit_bytes=VMEM_LIMIT)


def _const_spec(shape):
    nd = len(shape)
    return pl.BlockSpec(shape, lambda *_: (0,) * nd)


def _layer_norm(y, g, b):
    mu = jnp.mean(y, axis=-1, keepdims=True)
    yc = y - mu
    var = jnp.mean(yc * yc, axis=-1, keepdims=True)
    return yc * lax.rsqrt(var + LN_EPS) * g + b


def _proj_kernel(x_ref, w_ref, b_ref, o_ref, *, tn):
    x = x_ref[...].astype(_BF)
    for j in range(o_ref.shape[1] // tn):
        cols = slice(j * tn, (j + 1) * tn)
        acc = jnp.dot(x, w_ref[:, cols], preferred_element_type=_F32)
        o_ref[:, cols] = (acc + b_ref[:, cols]).astype(o_ref.dtype)


def _proj(x, w, b, out_dtype, tm):
    n, k = x.shape
    m = w.shape[1]
    return pl.pallas_call(
        functools.partial(_proj_kernel, tn=256),
        grid=(n // tm,),
        in_specs=[pl.BlockSpec((tm, k), lambda i: (i, 0)), _const_spec((k, m)), _const_spec((1, m))],
        out_specs=pl.BlockSpec((tm, m), lambda i: (i, 0)),
        out_shape=jax.ShapeDtypeStruct((n, m), out_dtype),
        compiler_params=_cparams(1),
        name="proj",
    )(x, w, b)


def _carried_history(ext_ref, tm):
    @pl.when(pl.program_id(1) == 0)
    def _():
        ext_ref[:, tm + EXT0 - HALO:tm + EXT0, :] = jnp.zeros((1, HALO, MIX_W), _F32)

    return ext_ref[:, tm + EXT0 - HALO:tm + EXT0, :]


def _proj_conv_kernel(x_ref, w_ref, b_ref, cw_ref, qkv_ref, yb_ref, ctail_ref, hg_ref, ext_ref, *, tn):
    tm = x_ref.shape[0]
    c_hist = _carried_history(ext_ref, tm)
    x = x_ref[...].astype(_BF)
    gates = slice(MIX_W, 4 * MIX_W)
    for j in range(w_ref.shape[1] // tn):
        lo = j * tn
        acc = jnp.dot(x, w_ref[:, lo:lo + tn], preferred_element_type=_F32) + b_ref[:, lo:lo + tn]
        if lo < gates.start:
            qkv_ref[:, lo:lo + tn] = acc.astype(qkv_ref.dtype)
        elif lo < gates.stop:
            hg_ref[:, lo - gates.start:lo - gates.start + tn] = acc
        else:
            qkv_ref[:, lo - 3 * MIX_W:lo - 3 * MIX_W + tn] = acc.astype(qkv_ref.dtype)
    h, gb, gc = (hg_ref[:, k * MIX_W:(k + 1) * MIX_W][None] for k in range(3))
    _conv_body(h, gb, gc, c_hist, cw_ref, ext_ref, yb_ref, ctail_ref)


def _proj_conv(x3, w, b, conv_w, tm):
    bsz, seq, k = x3.shape
    m = w.shape[1]
    qkv_w = m - 3 * MIX_W
    return pl.pallas_call(
        functools.partial(_proj_conv_kernel, tn=256),
        grid=(bsz, seq // tm),
        in_specs=[pl.BlockSpec((None, tm, k), lambda bi, i: (bi, i, 0)), _const_spec((k, m)), _const_spec((1, m)),
                  _const_spec(conv_w.shape)],
        out_specs=[pl.BlockSpec((None, tm, qkv_w), lambda bi, i: (bi, i, 0)),
                   pl.BlockSpec((1, tm, MIX_W), lambda bi, i: (bi, i, 0)),
                   pl.BlockSpec((1, 8, MIX_W), lambda bi, i: (bi, i, 0))],
        out_shape=[jax.ShapeDtypeStruct((bsz, seq, qkv_w), _BF), jax.ShapeDtypeStruct((bsz, seq, MIX_W), _BF),
                   jax.ShapeDtypeStruct((bsz, (seq // tm) * 8, MIX_W), _F32)],
        scratch_shapes=[pltpu.VMEM((tm, 3 * MIX_W), _F32), pltpu.VMEM((1, EXT0 + tm, MIX_W), _F32)],
        compiler_params=_cparams(2),
        name="proj_conv",
    )(x3, w, b, conv_w)


def _proj_dilated_kernel(x_ref, w_ref, b_ref, o_ref, d4_ref, d16_ref, pooled_ref, utail_ref,
                         stage_ref, stage4_ref, u_ref, ext_ref, s2_ref, s4_ref, s8_ref, *, tn, first_col):
    tm = x_ref.shape[0]
    u_hist = _carried_history(ext_ref, tm)
    x = x_ref[...].astype(_BF)
    for j in range(w_ref.shape[1] // tn):
        lo = j * tn
        acc = jnp.dot(x, w_ref[:, lo:lo + tn], preferred_element_type=_F32) + b_ref[:, lo:lo + tn]
        if lo < first_col:
            u_ref[:, lo:lo + tn] = acc
        else:
            o_ref[:, lo - first_col:lo - first_col + tn] = acc.astype(o_ref.dtype)
            for h in range(tn // LANES):
                stage_ref[(lo - first_col) // LANES + h] = acc[:, h * LANES:(h + 1) * LANES]
    _pool_body(u_ref[...][None], u_hist, pl.program_id(1) * tm, ext_ref, s2_ref, s4_ref, s8_ref, pooled_ref)
    utail_ref[...] = ext_ref[:, tm + EXT0 - HALO:tm + EXT0, :]
    q4, q16 = tm // 4, tm // 16
    for s in range(stage_ref.shape[0]):
        lanes = slice(s * LANES, (s + 1) * LANES)
        for r in range(4):
            rows = stage_ref[s, pl.ds(r, q4, stride=4), :]
            stage4_ref[s, r * q4:(r + 1) * q4, :] = rows
            d4_ref[r, :, lanes] = rows.astype(d4_ref.dtype)
        for r in range(4):
            for k in range(4):
                d16_ref[r + 4 * k, :, lanes] = stage4_ref[s, pl.ds(r * q4 + k, q16, stride=4), :].astype(d16_ref.dtype)


def _proj_dilated(x3, w, b, tm, first_col, dilations):
    bsz, seq, k = x3.shape
    m = w.shape[1]
    wd = m - first_col
    assert first_col == MIX_W and tuple(dilations) == (4, 16) and tm % 256 == 0
    out_specs = [pl.BlockSpec((None, tm, wd), lambda bi, i: (bi, i, 0))]
    out_shape = [jax.ShapeDtypeStruct((bsz, seq, wd), _BF)]
    for d in dilations:
        out_specs.append(pl.BlockSpec((None, d, tm // d, wd), lambda bi, i: (bi, 0, i, 0)))
        out_shape.append(jax.ShapeDtypeStruct((bsz, d, seq // d, wd), _BF))
    out_specs += [pl.BlockSpec((1, tm, MIX_W), lambda bi, i: (bi, i, 0)),
                  pl.BlockSpec((1, HALO, MIX_W), lambda bi, i: (bi, i, 0))]
    out_shape += [jax.ShapeDtypeStruct((bsz, seq, MIX_W), _BF),
                  jax.ShapeDtypeStruct((bsz, (seq // tm) * HALO, MIX_W), _F32)]
    shift = pltpu.VMEM((1, EXT0 + tm, MIX_W), _F32)
    return pl.pallas_call(
        functools.partial(_proj_dilated_kernel, tn=256, first_col=first_col),
        grid=(bsz, seq // tm),
        in_specs=[pl.BlockSpec((None, tm, k), lambda bi, i: (bi, i, 0)), _const_spec((k, m)), _const_spec((1, m))],
        out_specs=out_specs,
        out_shape=out_shape,
        scratch_shapes=[pltpu.VMEM((wd // LANES, tm, LANES), _F32)] * 2 + [pltpu.VMEM((tm, MIX_W), _F32)] + [shift] * 4,
        compiler_params=_cparams(2),
        name="proj_dilated",
    )(x3, w, b)


MLP_CHUNK = 512


def _lane_is_left():
    return lax.broadcasted_iota(jnp.int32, (1, LANES), 1) < HEAD_DIM


def _layer_tail(x_ref, left, right, tail_refs):
    wo_ref, bo_ref, g1_ref, b1_ref, w1_ref, w2_ref, g2_ref, b2_ref, o_ref = tail_refs
    mix = jnp.dot(left, wo_ref[:MIX_W, :], preferred_element_type=_F32)
    mix = mix + jnp.dot(right, wo_ref[MIX_W:, :], preferred_element_type=_F32) + bo_ref[...]
    x = _layer_norm(DEEPNORM_ALPHA * x_ref[...] + mix, g1_ref[...], b1_ref[...])
    xb = x.astype(_BF)
    acc = jnp.zeros(x.shape, _F32)
    for c in range(w1_ref.shape[1] // MLP_CHUNK):
        cols = slice(c * MLP_CHUNK, (c + 1) * MLP_CHUNK)
        h = jnp.dot(xb, w1_ref[:, cols], preferred_element_type=_F32)
        h = jnp.square(jnp.maximum(h, 0.0)).astype(_BF)
        acc = acc + jnp.dot(h, w2_ref[cols, :], preferred_element_type=_F32)
    o_ref[...] = _layer_norm(DEEPNORM_ALPHA * x + acc, g2_ref[...], b2_ref[...])


def _even_tail_kernel(x_ref, ya_ref, yb_ref, *tail_refs):
    _layer_tail(x_ref, ya_ref[...], yb_ref[...], tail_refs)


def _group_c(pooled_ref, wg_ref, scale_ref):
    return (jnp.dot(pooled_ref[...], wg_ref[...], preferred_element_type=_F32) * scale_ref[...]).astype(_BF)


def _sample_odd_tail_kernel(x_ref, pooled_ref, wg_ref, scale_ref, yd_ref, *tail_refs):
    _layer_tail(x_ref, _group_c(pooled_ref, wg_ref, scale_ref), yd_ref[...], tail_refs)


def _odd_tail_kernel(x_ref, pooled_ref, wg_ref, scale_ref, o1_ref, o2_ref, o3_ref, s1_ref, s2_ref, s3_ref,
                     *tail_refs):
    yc = _group_c(pooled_ref, wg_ref, scale_ref)
    tiles = []
    for p in range(N_PAIRS):
        lanes = slice(p * LANES, (p + 1) * LANES)
        lses = (s1_ref[p], s2_ref[p], s3_ref[p])
        top = jnp.maximum(jnp.maximum(lses[0], lses[1]), lses[2])
        es = [jnp.exp(s - top) for s in lses]
        num = es[0] * o1_ref[:, lanes].astype(_F32)
        num = num + es[1] * o2_ref[:, lanes].astype(_F32)
        num = num + es[2] * o3_ref[:, lanes].astype(_F32)
        tiles.append(num / (es[0] + es[1] + es[2]))
    yd = jnp.concatenate(tiles, axis=1)
    _layer_tail(x_ref, yc, yd.astype(_BF), tail_refs)


def _row_spec(tm, width):
    return pl.BlockSpec((tm, width), lambda i: (i, 0))


def _resident_spec(shape):
    nd = len(shape)
    return pl.BlockSpec(shape, lambda *_: (0,) * nd, pipeline_mode=pl.Buffered(1))


def _tail_call(kernel_fn, name, x, mixer_args, mixer_specs, tail_params, tm):
    n, dm = x.shape
    return pl.pallas_call(
        kernel_fn,
        grid=(n // tm,),
        in_specs=[_row_spec(tm, dm)] + mixer_specs + [_resident_spec(p.shape) for p in tail_params],
        out_specs=_row_spec(tm, dm),
        out_shape=jax.ShapeDtypeStruct((n, dm), _F32),
        compiler_params=_cparams(1),
        name=name,
    )(x, *mixer_args, *tail_params)


def _even_tail(x, ya, yb, tail_params, tm):
    return _tail_call(_even_tail_kernel, "even_tail", x, [ya, yb], [_row_spec(tm, MIX_W)] * 2, tail_params, tm)


def _odd_tail(x, pooled, wg, scale, outs, stats, tail_params, tm):
    tiles_per_seq = stats[0].shape[2] // tm
    stat_spec = pl.BlockSpec((None, N_PAIRS, tm, LANES), lambda i: (i // tiles_per_seq, 0, i % tiles_per_seq, 0))
    specs = ([_row_spec(tm, MIX_W), _resident_spec(wg.shape), _resident_spec(scale.shape)]
             + [_row_spec(tm, MIX_W)] * 3 + [stat_spec] * 3)
    return _tail_call(_odd_tail_kernel, "odd_tail", x, [pooled, wg, scale, *outs, *stats], specs, tail_params, tm)


def _sample_odd_tail(x, pooled, wg, scale, yd, tail_params, tm):
    specs = [_row_spec(tm, MIX_W), _resident_spec(wg.shape), _resident_spec(scale.shape), _row_spec(tm, MIX_W)]
    return _tail_call(_sample_odd_tail_kernel, "sample_odd_tail", x, [pooled, wg, scale, yd], specs, tail_params, tm)


def _fill_ext(ext_ref, hist, cur, t):
    nb = ext_ref.shape[0]
    ext_ref[:, 0:8, :] = jnp.zeros((nb, 8, MIX_W), _F32)
    ext_ref[:, 8:EXT0, :] = hist
    ext_ref[:, EXT0:EXT0 + t, :] = cur


def _conv_body(h, gb, gc, c_hist, w_ref, ext_ref, yb_ref, ctail_ref):
    t = h.shape[1]
    c = gc * h
    _fill_ext(ext_ref, c_hist, c, t)
    conv = ext_ref[:, EXT0 - 2:EXT0 - 2 + t, :] * w_ref[0:1, :]
    conv = conv + ext_ref[:, EXT0 - 1:EXT0 - 1 + t, :] * w_ref[1:2, :]
    conv = conv + c * w_ref[2:3, :]
    yb_ref[...] = (gb * conv).astype(yb_ref.dtype)
    ctail_ref[...] = ext_ref[:, EXT0 + t - 8:EXT0 + t, :]


def _sample_conv_kernel(h_ref, gb_ref, gc_ref, hist_ref, w_ref, yb_ref, ctail_ref, ext_ref):
    _conv_body(h_ref[...], gb_ref[...], gc_ref[...], hist_ref[...], w_ref, ext_ref, yb_ref, ctail_ref)


def _pool_body(u, hist, pos0, ext_ref, s2_ref, s4_ref, s8_ref, out_ref):
    nb, t, _ = u.shape
    _fill_ext(ext_ref, hist, u, t)
    hi = EXT0 + t
    zeros8 = jnp.zeros((nb, 8, MIX_W), _F32)
    s2_ref[:, 0:8, :] = zeros8
    s4_ref[:, 0:8, :] = zeros8
    s8_ref[:, 0:8, :] = zeros8
    s2_ref[:, 8:hi, :] = ext_ref[:, 8:hi, :] + ext_ref[:, 7:hi - 1, :]
    s4_ref[:, 8:hi, :] = s2_ref[:, 8:hi, :] + s2_ref[:, 6:hi - 2, :]
    s8_ref[:, 8:hi, :] = s4_ref[:, 8:hi, :] + s4_ref[:, 4:hi - 4, :]
    sums = (
        s2_ref[:, EXT0:hi, 0:LANES],
        s4_ref[:, EXT0:hi, LANES:2 * LANES],
        s8_ref[:, EXT0:hi, 2 * LANES:3 * LANES],
        s8_ref[:, EXT0:hi, 3 * LANES:] + s8_ref[:, EXT0 - 8:hi - 8, 3 * LANES:],
    )
    pos = (pos0 + lax.broadcasted_iota(jnp.int32, (1, t, LANES), 1) + 1).astype(_F32)
    tiles = []
    for g, (w, s) in enumerate(zip(POOL_WINDOWS, sums)):
        cnt = jnp.minimum(pos, float(w))
        tiles.append(s / cnt - u[:, :, g * LANES:(g + 1) * LANES])
    out_ref[...] = jnp.concatenate(tiles, axis=2).astype(out_ref.dtype)


def _sample_pool_kernel(u_ref, hist_ref, out_ref, ext_ref, s2_ref, s4_ref, s8_ref):
    _pool_body(u_ref[...], hist_ref[...], PAST_LEN, ext_ref, s2_ref, s4_ref, s8_ref, out_ref)


def _sample_conv(p3, hist, conv_w, bt):
    bsz, t, _ = p3.shape
    col = lambda c: pl.BlockSpec((bt, t, MIX_W), lambda b: (b, 0, c))
    return pl.pallas_call(
        _sample_conv_kernel,
        grid=(bsz // bt,),
        in_specs=[col(1), col(2), col(3), pl.BlockSpec((bt, HALO, MIX_W), lambda b: (b, 0, 0)),
                  _const_spec(conv_w.shape)],
        out_specs=[pl.BlockSpec((bt, t, MIX_W), lambda b: (b, 0, 0)),
                   pl.BlockSpec((bt, 8, MIX_W), lambda b: (b, 0, 0))],
        out_shape=[jax.ShapeDtypeStruct((bsz, t, MIX_W), _BF), jax.ShapeDtypeStruct((bsz, 8, MIX_W), _F32)],
        scratch_shapes=[pltpu.VMEM((bt, EXT0 + t, MIX_W), _F32)],
        compiler_params=_cparams(1),
        name="sample_conv",
    )(p3, p3, p3, hist, conv_w)


def _sample_pool(p3, hist, bt):
    bsz, t, _ = p3.shape
    scratch = pltpu.VMEM((bt, EXT0 + t, MIX_W), _F32)
    return pl.pallas_call(
        _sample_pool_kernel,
        grid=(bsz // bt,),
        in_specs=[pl.BlockSpec((bt, t, MIX_W), lambda b: (b, 0, 0)),
                  pl.BlockSpec((bt, HALO, MIX_W), lambda b: (b, 0, 0))],
        out_specs=pl.BlockSpec((bt, t, MIX_W), lambda b: (b, 0, 0)),
        out_shape=jax.ShapeDtypeStruct((bsz, t, MIX_W), _BF),
        scratch_shapes=[scratch] * 4,
        compiler_params=_cparams(1),
        name="sample_pool",
    )(p3, hist)


def _split_heads(q_pair):
    left = _lane_is_left()
    zero = jnp.zeros_like(q_pair)
    return jnp.concatenate([jnp.where(left, q_pair, zero), jnp.where(left, zero, q_pair)], axis=0)


def _band_attn_kernel(*refs, tq, dilation, kv_lane, has_sink, want_stat):
    refs = list(refs)
    q_ref, kc_ref, kp_ref, vc_ref, vp_ref, bias_ref = refs[:6]
    rest = refs[6:]
    sink_ref = rest.pop(0) if has_sink else None
    o_ref = rest.pop(0)
    st_ref = rest.pop(0) if want_stat else None
    stage_ref = rest.pop(0) if dilation > 1 else None

    first = pl.program_id(1) == 0
    res = pl.program_id(2)
    left = _lane_is_left()
    prev_cols = lax.broadcasted_iota(jnp.int32, (1, 2 * BAND), 1) < BAND
    top_rows = lax.broadcasted_iota(jnp.int32, (2 * BAND, 1), 0) < BAND
    ones = jnp.ones((2 * BAND, LANES), _BF)

    for j in range(tq // BAND):
        rows = slice(j * BAND, (j + 1) * BAND)
        out_rows = rows if dilation == 1 else pl.ds(j * BAND * dilation + res, BAND, stride=dilation)
        for p in range(N_PAIRS):
            kl = slice(kv_lane[p], kv_lane[p] + LANES)
            if j == 0:
                k_prev, v_prev = kp_ref[:, kl], vp_ref[:, kl]
            else:
                k_prev, v_prev = kc_ref[(j - 1) * BAND:j * BAND, kl], vc_ref[(j - 1) * BAND:j * BAND, kl]
            k2 = jnp.concatenate([k_prev, kc_ref[rows, kl]], axis=0)
            v2 = jnp.concatenate([v_prev, vc_ref[rows, kl]], axis=0)
            q2 = _split_heads(q_ref[rows, p * LANES:(p + 1) * LANES])
            s = lax.dot_general(q2, k2, (((1,), (1,)), ((), ())), preferred_element_type=_F32)
            s = s + bias_ref[p]
            if j == 0:
                s = jnp.where(jnp.logical_and(first, prev_cols), _NEG_INF, s)
            m = jnp.max(s, axis=1, keepdims=True)
            prob = jnp.exp(s - m).astype(_BF)
            r = jnp.dot(prob, jnp.concatenate([v2, ones], axis=1), preferred_element_type=_F32)
            pv, l = r[:, :LANES], r[:, LANES:]
            if has_sink:
                ha, hb = A_PAIR_HEADS[p]
                sink = jnp.where(top_rows, sink_ref[ha], sink_ref[hb])
                m2 = jnp.maximum(m, sink)
                a = jnp.exp(m - m2)
                o = pv * a / (l * a + jnp.exp(sink - m2))
            else:
                o = pv / l
            o_pair = jnp.where(left, o[:BAND], o[BAND:])
            if dilation == 1:
                o_ref[rows, p * LANES:(p + 1) * LANES] = o_pair.astype(o_ref.dtype)
            else:
                stage_ref[p, out_rows, :] = o_pair
            if want_stat:
                lse = m + jnp.log(l)
                st_ref[p, out_rows, :] = jnp.where(left, lse[:BAND], lse[BAND:])

    if dilation > 1:
        @pl.when(res == dilation - 1)
        def _():
            for p in range(N_PAIRS):
                o_ref[:, p * LANES:(p + 1) * LANES] = stage_ref[p].astype(o_ref.dtype)


def _band_bias(pair_heads, dilation):
    slopes = _alibi_slopes(N_HEADS)
    qi = np.arange(BAND)[:, None]
    kj = np.arange(2 * BAND)[None, :]
    dist = qi + BAND - kj
    valid = (dist >= 0) & (dist <= BAND)
    out = np.empty((len(pair_heads), 2 * BAND, 2 * BAND), np.float32)
    for p, heads in enumerate(pair_heads):
        for half, h in enumerate(heads):
            bias = -np.float32(slopes[h]) * (dist * dilation).astype(np.float32)
            out[p, half * BAND:(half + 1) * BAND] = np.where(valid, bias, -np.inf)
    return jnp.asarray(out)


BAND_TOKENS = 4096


def _band_attn(arr, q_col, k_col, v_col, kv_width, kv_lane, pair_heads, sinks, want_stat):
    bsz, dilation, n, _ = arr.shape
    seq = n * dilation
    tq = min(512, n, BAND_TOKENS // dilation)
    sub = tq // BAND
    assert q_col % MIX_W == 0 and k_col % kv_width == 0 and v_col % kv_width == 0

    def cur(col, w):
        return pl.BlockSpec((None, None, tq, w), lambda b, i, r: (b, r, i, col // w))

    def prev(col, w):
        return pl.BlockSpec((None, None, BAND, w), lambda b, i, r: (b, r, jnp.maximum(i * sub - 1, 0), col // w))

    in_specs = [cur(q_col, MIX_W), cur(k_col, kv_width), prev(k_col, kv_width), cur(v_col, kv_width),
                prev(v_col, kv_width), _const_spec((N_PAIRS, 2 * BAND, 2 * BAND))]
    args = [arr, arr, arr, arr, arr, _band_bias(pair_heads, dilation)]
    if sinks is not None:
        in_specs.append(pl.BlockSpec(memory_space=pltpu.SMEM))
        args.append(sinks)
    out_specs = [pl.BlockSpec((None, tq * dilation, MIX_W), lambda b, i, r: (b, i, 0))]
    out_shape = [jax.ShapeDtypeStruct((bsz, seq, MIX_W), _BF)]
    if want_stat:
        out_specs.append(pl.BlockSpec((None, N_PAIRS, tq * dilation, LANES), lambda b, i, r: (b, 0, i, 0)))
        out_shape.append(jax.ShapeDtypeStruct((bsz, N_PAIRS, seq, LANES), _F32))
    scratch = [pltpu.VMEM((N_PAIRS, tq * dilation, LANES), _F32)] if dilation > 1 else []
    res = pl.pallas_call(
        functools.partial(_band_attn_kernel, tq=tq, dilation=dilation, kv_lane=kv_lane, has_sink=sinks is not None,
                          want_stat=want_stat),
        grid=(bsz, n // tq, dilation),
        in_specs=in_specs,
        out_specs=out_specs,
        out_shape=out_shape,
        scratch_shapes=scratch,
        compiler_params=_cparams(3),
        name=f"band_attn_d{dilation}",
    )(*args)
    o = res[0].reshape(bsz * seq, MIX_W)
    return (o, res[1]) if want_stat else o


def _sample_attn_unit(b, pair0, refs, sink_ref, *, n_hist, t_new, q_pairs_of_kv, pair_heads):
    q_ref, kn_ref, vn_ref, kc_ref, vc_ref, bias_ref, mult_ref, y_ref, ko_ref, vo_ref = refs
    left = _lane_is_left()
    new_lanes = lax.broadcasted_iota(jnp.int32, (1, LANES), 1) >= LANES - t_new
    top_rows = lax.broadcasted_iota(jnp.int32, (2 * t_new, 1), 0) < t_new
    zpad = jnp.zeros((LANES - t_new, LANES), _F32)
    mult = mult_ref[...]
    for kvp, q_pairs in enumerate(q_pairs_of_kv):
        heads = slice(2 * kvp, 2 * kvp + 2)
        lanes = slice(kvp * LANES, (kvp + 1) * LANES)
        ext = []
        for c_ref, n_ref, o_ref in ((kc_ref, kn_ref, ko_ref), (vc_ref, vn_ref, vo_ref)):
            old = c_ref[b, heads].reshape(LANES, n_hist)
            new = jnp.concatenate([zpad, n_ref[b, :, lanes]], axis=0).T
            rolled = pltpu.roll(old, n_hist - t_new, axis=1)
            tail = jnp.where(new_lanes, new, rolled[:, n_hist - LANES:])
            out = tail if n_hist == LANES else jnp.concatenate([rolled[:, :n_hist - LANES], tail], axis=1)
            o_ref[b, heads] = out.reshape(2, HEAD_DIM, n_hist)
            ext.append(jnp.concatenate([old.astype(_BF), new.astype(_BF)], axis=1))
        k_ext, v_ext = ext
        for p in q_pairs:
            q2 = _split_heads(q_ref[b, :, p * LANES:(p + 1) * LANES].astype(_BF))
            s = jnp.dot(q2, k_ext, preferred_element_type=_F32) + bias_ref[pair0 + p]
            m = jnp.max(s, axis=1, keepdims=True)
            prob = (mult * jnp.exp(s - m)).astype(_BF)
            l = jnp.sum(prob.astype(_F32), axis=1, keepdims=True)
            pv = lax.dot_general(prob, v_ext, (((1,), (1,)), ((), ())), preferred_element_type=_F32)
            if sink_ref is not None:
                ha, hb = pair_heads[p]
                sink = jnp.where(top_rows, sink_ref[ha], sink_ref[hb])
                m2 = jnp.maximum(m, sink)
                a = jnp.exp(m - m2)
                o = pv * a / (l * a + jnp.exp(sink - m2))
            else:
                o = pv / l
            y_ref[b, :, p * LANES:(p + 1) * LANES] = jnp.where(left, o[:t_new], o[t_new:]).astype(y_ref.dtype)


def _sample_attn_kernel(*refs, has_sink, **statics):
    refs = list(refs)
    sink_ref = refs.pop(7) if has_sink else None

    def one_batch(b, carry):
        _sample_attn_unit(b, 0, refs, sink_ref, **statics)
        return carry

    lax.fori_loop(0, refs[0].shape[0], one_batch, 0)


def _sample_tables(pair_heads, branches, n_hist, t_new):
    slopes = _alibi_slopes(N_HEADS)
    key_pos = np.concatenate([np.arange(n_hist), n_hist + np.arange(LANES) - (LANES - t_new)])
    is_key = np.concatenate([np.ones(n_hist, bool), np.arange(LANES) >= LANES - t_new])
    delta = (n_hist + np.arange(t_new))[:, None] - key_pos[None, :]
    mult = np.zeros(delta.shape, np.float32)
    for window, dil in branches:
        mult += ((delta >= 0) & (delta % dil == 0) & (delta <= window) & is_key[None, :]).astype(np.float32)
    bias = np.empty((len(pair_heads), 2 * t_new, key_pos.size), np.float32)
    for p, heads in enumerate(pair_heads):
        for half, h in enumerate(heads):
            b = -np.float32(slopes[h]) * delta.astype(np.float32)
            bias[p, half * t_new:(half + 1) * t_new] = np.where(mult > 0, b, -np.inf)
    return jnp.asarray(bias), jnp.asarray(np.concatenate([mult, mult], axis=0))


UNIT_HEADS = 4


def _tail_cache_kernel(x_ref, ya_ref, yb_ref, wo_ref, bo_ref, g1_ref, b1_ref, w1_ref, w2_ref, g2_ref, b2_ref,
                       q_ref, kn_ref, vn_ref, kc_ref, vc_ref, bias_ref, mult_ref,
                       o_ref, y_ref, ko_ref, vo_ref, xb_ref, acc_ref, *, units_per_row, **statics):
    c = pl.program_id(1)
    n_chunks = pl.num_programs(1)
    unit = pl.program_id(0) * n_chunks + c

    @pl.when(c == 0)
    def _():
        mix = jnp.dot(ya_ref[...], wo_ref[:MIX_W, :], preferred_element_type=_F32)
        mix = mix + jnp.dot(yb_ref[...], wo_ref[MIX_W:, :], preferred_element_type=_F32) + bo_ref[...]
        x1 = _layer_norm(DEEPNORM_ALPHA * x_ref[...] + mix, g1_ref[...], b1_ref[...])
        xb_ref[...] = x1.astype(_BF)
        acc_ref[...] = DEEPNORM_ALPHA * x1

    h = jnp.dot(xb_ref[...], w1_ref[c], preferred_element_type=_F32)
    h = jnp.square(jnp.maximum(h, 0.0)).astype(_BF)
    acc_ref[...] += jnp.dot(h, w2_ref[c], preferred_element_type=_F32)
    pair0 = (unit % units_per_row) * (UNIT_HEADS // 2)
    _sample_attn_unit(0, pair0, (q_ref, kn_ref, vn_ref, kc_ref, vc_ref, bias_ref, mult_ref, y_ref, ko_ref, vo_ref),
                      None, **statics)

    @pl.when(c == n_chunks - 1)
    def _():
        o_ref[...] = _layer_norm(acc_ref[...], g2_ref[...], b2_ref[...])


def _even_tail_with_cache(x, ya, yb, tail_params, p3, q_col, k_col, v_col, pair_heads, branches, k_cache, v_cache,
                          tm, n_chunks):
    n, dm = x.shape
    bsz, t_new, _ = p3.shape
    _, n_hist, kvh, _ = k_cache.shape
    units_per_row = kvh // UNIT_HEADS
    unit_w = UNIT_HEADS * HEAD_DIM
    assert (n // tm) * n_chunks == bsz * units_per_row and kvh == N_HEADS
    wo, bo, g1, b1, w1, w2, g2, b2 = tail_params
    dh = w1.shape[1]
    w1c = jnp.transpose(w1.reshape(dm, n_chunks, dh // n_chunks), (1, 0, 2))
    w2c = w2.reshape(n_chunks, dh // n_chunks, dm)
    kt = jnp.transpose(k_cache, (0, 2, 3, 1))
    vt = jnp.transpose(v_cache, (0, 2, 3, 1))
    bias, mult = _sample_tables(pair_heads, branches, n_hist, t_new)

    row = lambda w: pl.BlockSpec((tm, w), lambda t, c: (t, 0))
    unit_of = lambda t, c: t * n_chunks + c
    new = lambda col: pl.BlockSpec(
        (1, t_new, unit_w),
        lambda t, c: (unit_of(t, c) // units_per_row, 0, col // unit_w + unit_of(t, c) % units_per_row))
    cache = pl.BlockSpec((1, UNIT_HEADS, HEAD_DIM, n_hist),
                         lambda t, c: (unit_of(t, c) // units_per_row, unit_of(t, c) % units_per_row, 0, 0))
    params = (wo, bo, g1, b1, w1c, w2c, g2, b2)
    out, y, ko, vo = pl.pallas_call(
        functools.partial(_tail_cache_kernel, units_per_row=units_per_row, n_hist=n_hist, t_new=t_new,
                          q_pairs_of_kv=tuple((p,) for p in range(UNIT_HEADS // 2)), pair_heads=pair_heads),
        grid=(n // tm, n_chunks),
        in_specs=[row(dm), row(MIX_W), row(MIX_W)] + [_resident_spec(p.shape) for p in params]
        + [new(q_col), new(k_col), new(v_col), cache, cache, _resident_spec(bias.shape), _resident_spec(mult.shape)],
        out_specs=[row(dm), new(0), cache, cache],
        out_shape=[jax.ShapeDtypeStruct((n, dm), _F32), jax.ShapeDtypeStruct((bsz, t_new, MIX_W), _BF),
                   jax.ShapeDtypeStruct(kt.shape, _F32), jax.ShapeDtypeStruct(vt.shape, _F32)],
        scratch_shapes=[pltpu.VMEM((tm, dm), _BF), pltpu.VMEM((tm, dm), _F32)],
        compiler_params=pltpu.CompilerParams(dimension_semantics=("arbitrary", "arbitrary"),
                                             vmem_limit_bytes=FUSED_VMEM_LIMIT),
        name="even_tail_with_cache",
    )(x, ya, yb, *params, p3, p3, p3, kt, vt, bias, mult)
    return out, y, jnp.transpose(ko, (0, 3, 1, 2)), jnp.transpose(vo, (0, 3, 1, 2))


def _sample_attn(p3, q_col, k_col, v_col, pair_heads, branches, k_cache, v_cache, sinks, bt):
    bsz, t_new, _ = p3.shape
    _, n_hist, kvh, _ = k_cache.shape
    kv_width = kvh * HEAD_DIM
    kt = jnp.transpose(k_cache, (0, 2, 3, 1))
    vt = jnp.transpose(v_cache, (0, 2, 3, 1))
    n_kv_pairs = kvh // 2
    q_pairs_of_kv = tuple(tuple(p for p in range(N_PAIRS) if p % n_kv_pairs == kvp) for kvp in range(n_kv_pairs))
    bias, mult = _sample_tables(pair_heads, branches, n_hist, t_new)
    new = lambda col, w: pl.BlockSpec((bt, t_new, w), lambda b: (b, 0, col // w))
    cache = pl.BlockSpec((bt, kvh, HEAD_DIM, n_hist), lambda b: (b, 0, 0, 0))
    in_specs = [new(q_col, MIX_W), new(k_col, kv_width), new(v_col, kv_width), cache, cache,
                _const_spec(bias.shape), _const_spec(mult.shape)]
    args = [p3, p3, p3, kt, vt, bias, mult]
    if sinks is not None:
        in_specs.append(pl.BlockSpec(memory_space=pltpu.SMEM))
        args.append(sinks)
    y, ko, vo = pl.pallas_call(
        functools.partial(_sample_attn_kernel, n_hist=n_hist, t_new=t_new, q_pairs_of_kv=q_pairs_of_kv,
                          pair_heads=pair_heads, has_sink=sinks is not None),
        grid=(bsz // bt,),
        in_specs=in_specs,
        out_specs=[pl.BlockSpec((bt, t_new, MIX_W), lambda b: (b, 0, 0)), cache, cache],
        out_shape=[jax.ShapeDtypeStruct((bsz, t_new, MIX_W), _BF),
                   jax.ShapeDtypeStruct(kt.shape, _F32), jax.ShapeDtypeStruct(vt.shape, _F32)],
        compiler_params=_cparams(1),
        name=f"sample_attn_{n_hist}",
    )(*args)
    return y, jnp.transpose(ko, (0, 3, 1, 2)), jnp.transpose(vo, (0, 3, 1, 2))


A_Q = N_HEADS * HEAD_DIM
A_KV = A_KV_HEADS * HEAD_DIM
E_Q, E_H, E_GB, E_GC, E_K, E_V = 0, 512, 1024, 1536, 2048, 2176
O_U, O_Q, O_K, O_V = 0, 512, 1024, 1536


def _prep_layer_weights(even_w_in, even_b_in, even_w_out, odd_w_in, odd_b_in, c_w_group):
    q_cols = np.concatenate([h * HEAD_DIM + np.arange(HEAD_DIM) for h in A_HEAD_ORDER])
    o1, o2, o3 = A_Q, A_Q + A_KV, A_Q + 2 * A_KV
    order = np.concatenate([q_cols, np.arange(o3, o3 + 3 * MIX_W), np.arange(o1, o3)])
    scale = np.ones((order.size,), np.float32)
    scale[:A_Q] = HEAD_DIM ** -0.5
    ew = (even_w_in[:, order] * scale).astype(_BF)
    eb = (even_b_in[order] * scale)[None, :]
    ewo = jnp.concatenate([even_w_out[q_cols], even_w_out[A_Q:]], axis=0).astype(_BF)
    oscale = np.ones((odd_w_in.shape[1],), np.float32)
    oscale[O_Q:O_K] = HEAD_DIM ** -0.5
    ow = (odd_w_in * oscale).astype(_BF)
    ob = (odd_b_in * oscale)[None, :]
    groups, gw, _ = c_w_group.shape
    wg = jnp.zeros((MIX_W, MIX_W), _F32)
    for g in range(groups):
        wg = wg.at[g * gw:(g + 1) * gw, g * gw:(g + 1) * gw].set(c_w_group[g])
    return ew, eb, ewo, ow, ob, wg.astype(_BF)


def _row(v):
    return v[None, :]


def _forward(xp, xs, caches, wts, tm, bt_attn, bt_shift, n_chunks):
    (even_w_in, even_b_in, a_sinks, b_conv_w, even_w_out, even_b_out, odd_w_in, odd_b_in, c_w_group, c_scale,
     odd_w_out, odd_b_out, mlp_w1, mlp_w2, ln1_g, ln1_b, ln2_g, ln2_b) = wts
    cache_a_k, cache_a_v, state_b_conv, state_c_pool, cache_d_k, cache_d_v = caches
    bsz, seq, dm = xp.shape
    bs, ts, _ = xs.shape
    n, ns = bsz * seq, bs * ts
    tms = min(tm, ns)
    ew, eb, ewo, ow, ob, wg = _prep_layer_weights(even_w_in[0], even_b_in[0], even_w_out[0], odd_w_in[0],
                                                  odd_b_in[0], c_w_group[0])
    owo = odd_w_out[0].astype(_BF)
    w1 = mlp_w1.astype(_BF)
    w2 = mlp_w2.astype(_BF)
    tails = [(wo, _row(bo), _row(ln1_g[i]), _row(ln1_b[i]), w1[i], w2[i], _row(ln2_g[i]), _row(ln2_b[i]))
             for i, (wo, bo) in enumerate(((ewo, even_b_out[0]), (owo, odd_b_out[0])))]
    a_kv_lane = (0,) * N_PAIRS
    d_kv_lane = tuple(p * LANES for p in range(N_PAIRS))

    xs = xs.reshape(ns, dm)
    pe3 = _proj(xs, ew, eb, _F32, tms).reshape(bs, ts, -1)
    ya, ak_s, av_s = _sample_attn(pe3, E_Q, E_K, E_V, A_PAIR_HEADS, ((A_WINDOW, 1),), cache_a_k[0], cache_a_v[0],
                                  a_sinks[0], bt_attn)
    c_hist = jnp.pad(state_b_conv[0], ((0, 0), (HALO - (CONV_WIDTH - 1), 0), (0, 0)))
    yb, ctail = _sample_conv(pe3, c_hist, b_conv_w[0], bt_shift)
    bc_s = ctail[:, -(CONV_WIDTH - 1):, :]
    xs = _even_tail(xs, ya.reshape(ns, MIX_W), yb.reshape(ns, MIX_W), tails[0], tms)
    po3s = _proj(xs, ow, ob, _F32, tms).reshape(bs, ts, -1)
    u_hist = jnp.pad(state_c_pool[0], ((0, 0), (HALO - (POOL_MAX - 1), 0), (0, 0)))
    pooled_s = _sample_pool(po3s, u_hist, bt_shift).reshape(ns, MIX_W)
    cp_s = jnp.concatenate([state_c_pool[0], po3s[:, :, O_U:O_Q]], axis=1)[:, -(POOL_MAX - 1):]

    x = xp.reshape(n, dm)
    qkv, yb, ctail = _proj_conv(xp, ew, eb, b_conv_w[0], tm)
    k_col, v_col = MIX_W, MIX_W + A_KV
    ya = _band_attn(qkv[:, None], 0, k_col, v_col, LANES, a_kv_lane, A_PAIR_HEADS, a_sinks[0], False)
    n_keep = min(A_WINDOW, seq)
    a_k = qkv[:, seq - n_keep:, k_col:v_col].astype(_F32).reshape(bsz, n_keep, A_KV_HEADS, HEAD_DIM)
    a_v = qkv[:, seq - n_keep:, v_col:].astype(_F32).reshape(bsz, n_keep, A_KV_HEADS, HEAD_DIM)
    b_conv = ctail[:, -(CONV_WIDTH - 1):, :]
    x, yd, dk_s, dv_s = _even_tail_with_cache(x, ya, yb.reshape(n, MIX_W), tails[0], po3s, O_Q, O_K, O_V,
                                              D_PAIR_HEADS, D_BRANCHES, cache_d_k[0], cache_d_v[0], tm, n_chunks)
    xs = _sample_odd_tail(xs, pooled_s, wg, _row(c_scale[0]), yd.reshape(ns, MIX_W), tails[1], tms)

    dils = tuple(d for _, d in D_BRANCHES if d > 1)
    qkv, qkv4, qkv16, pooled, utail = _proj_dilated(x.reshape(bsz, seq, dm), ow, ob, tm, O_Q, dils)
    k_col, v_col = O_K - O_Q, O_V - O_Q
    outs, stats = [], []
    for arr in (qkv[:, None], qkv4, qkv16):
        o, st = _band_attn(arr, 0, k_col, v_col, MIX_W, d_kv_lane, D_PAIR_HEADS, None, True)
        outs.append(o)
        stats.append(st)
    n_keep = min(D_BRANCHES[-1][0], seq)
    c_pool = utail[:, -(POOL_MAX - 1):, :]
    d_k = qkv[:, seq - n_keep:, k_col:v_col].astype(_F32).reshape(bsz, n_keep, N_HEADS, HEAD_DIM)
    d_v = qkv[:, seq - n_keep:, v_col:].astype(_F32).reshape(bsz, n_keep, N_HEADS, HEAD_DIM)
    x = _odd_tail(x, pooled.reshape(n, MIX_W), wg, _row(c_scale[0]), outs, stats, tails[1], tm)
    return (x.reshape(bsz, seq, dm), xs.reshape(bs, ts, dm), a_k[None], a_v[None], b_conv[None], c_pool[None],
            d_k[None], d_v[None], ak_s[None], av_s[None], bc_s[None], cp_s[None], dk_s[None], dv_s[None])


def kernel(x_prompt, x_sample, cache_a_k, cache_a_v, state_b_conv, state_c_pool, cache_d_k, cache_d_v, even_w_in, even_b_in, a_sinks, b_conv_w, even_w_out, even_b_out, odd_w_in, odd_b_in, c_w_group, c_scale, odd_w_out, odd_b_out, mlp_w1, mlp_w2, ln1_g, ln1_b, ln2_g, ln2_b):
    wts = (even_w_in, even_b_in, a_sinks, b_conv_w, even_w_out, even_b_out, odd_w_in, odd_b_in, c_w_group, c_scale,
           odd_w_out, odd_b_out, mlp_w1, mlp_w2, ln1_g, ln1_b, ln2_g, ln2_b)
    caches = (cache_a_k, cache_a_v, state_b_conv, state_c_pool, cache_d_k, cache_d_v)
    return _forward(x_prompt, x_sample, caches, wts, tm=512, bt_attn=16, bt_shift=32, n_chunks=4)
```

```python
import functools

import numpy as np
import jax
import jax.numpy as jnp
from jax import lax
from jax.experimental import pallas as pl
from jax.experimental.pallas import tpu as pltpu

HEAD_DIM = 64
N_HEADS = 8
A_KV_HEADS = 2
A_WINDOW = 128
D_BRANCHES = ((128, 1), (512, 4), (2048, 16))
CONV_WIDTH = 3
POOL_WINDOWS = (2, 4, 8, 16)
POOL_MAX = 16
DEPTH = 2
PAST_LEN = 16384
DEEPNORM_ALPHA = (2 * DEPTH) ** 0.25
LN_EPS = 1e-5

MIX_W = N_HEADS * HEAD_DIM
LANES = 128
N_PAIRS = MIX_W // LANES
BAND = 128
HALO = 16
EXT0 = 24
VMEM_LIMIT = 56 * 1024 * 1024
FUSED_VMEM_LIMIT = 62 * 1024 * 1024

A_HEAD_ORDER = (0, 4, 1, 5, 2, 6, 3, 7)
A_PAIR_HEADS = tuple((p, p + 4) for p in range(N_PAIRS))
D_PAIR_HEADS = tuple((2 * p, 2 * p + 1) for p in range(N_PAIRS))

_BF = jnp.bfloat16
_F32 = jnp.float32
_NEG_INF = float("-inf")


def _alibi_slopes(n_heads):
    return 2.0 ** (-8.0 * np.arange(1, n_heads + 1) / n_heads)


def _cparams(n_axes):
    return pltpu.CompilerParams(dimension_semantics=("arbitrary",) * n_axes, vmem_limit_bytes=VMEM_LIMIT)


def _const_spec(shape):
    nd = len(shape)
    return pl.BlockSpec(shape, lambda *_: (0,) * nd)


def _layer_norm(y, g, b):
    mu = jnp.mean(y, axis=-1, keepdims=True)
    yc = y - mu
    var = jnp.mean(yc * yc, axis=-1, keepdims=True)
    return yc * lax.rsqrt(var + LN_EPS) * g + b


def _proj_kernel(x_ref, w_ref, b_ref, o_ref, *, tn):
    x = x_ref[...].astype(_BF)
    for j in range(o_ref.shape[1] // tn):
        cols = slice(j * tn, (j + 1) * tn)
        acc = jnp.dot(x, w_ref[:, cols], preferred_element_type=_F32)
        o_ref[:, cols] = (acc + b_ref[:, cols]).astype(o_ref.dtype)


def _proj(x, w, b, out_dtype, tm):
    n, k = x.shape
    m = w.shape[1]
    return pl.pallas_call(
        functools.partial(_proj_kernel, tn=256),
        grid=(n // tm,),
        in_specs=[pl.BlockSpec((tm, k), lambda i: (i, 0)), _const_spec((k, m)), _const_spec((1, m))],
        out_specs=pl.BlockSpec((tm, m), lambda i: (i, 0)),
        out_shape=jax.ShapeDtypeStruct((n, m), out_dtype),
        compiler_params=_cparams(1),
        name="proj",
    )(x, w, b)


def _carried_history(ext_ref, tm):
    @pl.when(pl.program_id(1) == 0)
    def _():
        ext_ref[:, tm + EXT0 - HALO:tm + EXT0, :] = jnp.zeros((1, HALO, MIX_W), _F32)

    return ext_ref[:, tm + EXT0 - HALO:tm + EXT0, :]


def _proj_conv_kernel(x_ref, w_ref, b_ref, cw_ref, qkv_ref, yb_ref, ctail_ref, hg_ref, ext_ref, *, tn):
    tm = x_ref.shape[0]
    c_hist = _carried_history(ext_ref, tm)
    x = x_ref[...].astype(_BF)
    gates = slice(MIX_W, 4 * MIX_W)
    for j in range(w_ref.shape[1] // tn):
        lo = j * tn
        acc = jnp.dot(x, w_ref[:, lo:lo + tn], preferred_element_type=_F32) + b_ref[:, lo:lo + tn]
        if lo < gates.start:
            qkv_ref[:, lo:lo + tn] = acc.astype(qkv_ref.dtype)
        elif lo < gates.stop:
            hg_ref[:, lo - gates.start:lo - gates.start + tn] = acc
        else:
            qkv_ref[:, lo - 3 * MIX_W:lo - 3 * MIX_W + tn] = acc.astype(qkv_ref.dtype)
    h, gb, gc = (hg_ref[:, k * MIX_W:(k + 1) * MIX_W][None] for k in range(3))
    _conv_body(h, gb, gc, c_hist, cw_ref, ext_ref, yb_ref, ctail_ref)


def _proj_conv(x3, w, b, conv_w, tm):
    bsz, seq, k = x3.shape
    m = w.shape[1]
    qkv_w = m - 3 * MIX_W
    return pl.pallas_call(
        functools.partial(_proj_conv_kernel, tn=256),
        grid=(bsz, seq // tm),
        in_specs=[pl.BlockSpec((None, tm, k), lambda bi, i: (bi, i, 0)), _const_spec((k, m)), _const_spec((1, m)),
                  _const_spec(conv_w.shape)],
        out_specs=[pl.BlockSpec((None, tm, qkv_w), lambda bi, i: (bi, i, 0)),
                   pl.BlockSpec((1, tm, MIX_W), lambda bi, i: (bi, i, 0)),
                   pl.BlockSpec((1, 8, MIX_W), lambda bi, i: (bi, i, 0))],
        out_shape=[jax.ShapeDtypeStruct((bsz, seq, qkv_w), _BF), jax.ShapeDtypeStruct((bsz, seq, MIX_W), _BF),
                   jax.ShapeDtypeStruct((bsz, (seq // tm) * 8, MIX_W), _F32)],
        scratch_shapes=[pltpu.VMEM((tm, 3 * MIX_W), _F32), pltpu.VMEM((1, EXT0 + tm, MIX_W), _F32)],
        compiler_params=_cparams(2),
        name="proj_conv",
    )(x3, w, b, conv_w)


def _proj_dilated_kernel(x_ref, w_ref, b_ref, o_ref, d4_ref, d16_ref, pooled_ref, utail_ref,
                         stage_ref, stage4_ref, u_ref, ext_ref, s2_ref, s4_ref, s8_ref, *, tn, first_col):
    tm = x_ref.shape[0]
    u_hist = _carried_history(ext_ref, tm)
    x = x_ref[...].astype(_BF)
    for j in range(w_ref.shape[1] // tn):
        lo = j * tn
        acc = jnp.dot(x, w_ref[:, lo:lo + tn], preferred_element_type=_F32) + b_ref[:, lo:lo + tn]
        if lo < first_col:
            u_ref[:, lo:lo + tn] = acc
        else:
            o_ref[:, lo - first_col:lo - first_col + tn] = acc.astype(o_ref.dtype)
            for h in range(tn // LANES):
                stage_ref[(lo - first_col) // LANES + h] = acc[:, h * LANES:(h + 1) * LANES]
    _pool_body(u_ref[...][None], u_hist, pl.program_id(1) * tm, ext_ref, s2_ref, s4_ref, s8_ref, pooled_ref)
    utail_ref[...] = ext_ref[:, tm + EXT0 - HALO:tm + EXT0, :]
    q4, q16 = tm // 4, tm // 16
    for s in range(stage_ref.shape[0]):
        lanes = slice(s * LANES, (s + 1) * LANES)
        for r in range(4):
            rows = stage_ref[s, pl.ds(r, q4, stride=4), :]
            stage4_ref[s, r * q4:(r + 1) * q4, :] = rows
            d4_ref[r, :, lanes] = rows.astype(d4_ref.dtype)
        for r in range(4):
            for k in range(4):
                d16_ref[r + 4 * k, :, lanes] = stage4_ref[s, pl.ds(r * q4 + k, q16, stride=4), :].astype(d16_ref.dtype)


def _proj_dilated(x3, w, b, tm, first_col, dilations):
    bsz, seq, k = x3.shape
    m = w.shape[1]
    wd = m - first_col
    assert first_col == MIX_W and tuple(dilations) == (4, 16) and tm % 256 == 0
    out_specs = [pl.BlockSpec((None, tm, wd), lambda bi, i: (bi, i, 0))]
    out_shape = [jax.ShapeDtypeStruct((bsz, seq, wd), _BF)]
    for d in dilations:
        out_specs.append(pl.BlockSpec((None, d, tm // d, wd), lambda bi, i: (bi, 0, i, 0)))
        out_shape.append(jax.ShapeDtypeStruct((bsz, d, seq // d, wd), _BF))
    out_specs += [pl.BlockSpec((1, tm, MIX_W), lambda bi, i: (bi, i, 0)),
                  pl.BlockSpec((1, HALO, MIX_W), lambda bi, i: (bi, i, 0))]
    out_shape += [jax.ShapeDtypeStruct((bsz, seq, MIX_W), _BF),
                  jax.ShapeDtypeStruct((bsz, (seq // tm) * HALO, MIX_W), _F32)]
    shift = pltpu.VMEM((1, EXT0 + tm, MIX_W), _F32)
    return pl.pallas_call(
        functools.partial(_proj_dilated_kernel, tn=256, first_col=first_col),
        grid=(bsz, seq // tm),
        in_specs=[pl.BlockSpec((None, tm, k), lambda bi, i: (bi, i, 0)), _const_spec((k, m)), _const_spec((1, m))],
        out_specs=out_specs,
        out_shape=out_shape,
        scratch_shapes=[pltpu.VMEM((wd // LANES, tm, LANES), _F32)] * 2 + [pltpu.VMEM((tm, MIX_W), _F32)] + [shift] * 4,
        compiler_params=_cparams(2),
        name="proj_dilated",
    )(x3, w, b)


MLP_CHUNK = 512


def _lane_is_left():
    return lax.broadcasted_iota(jnp.int32, (1, LANES), 1) < HEAD_DIM


def _layer_tail(x_ref, left, right, tail_refs):
    wo_ref, bo_ref, g1_ref, b1_ref, w1_ref, w2_ref, g2_ref, b2_ref, o_ref = tail_refs
    mix = jnp.dot(left, wo_ref[:MIX_W, :], preferred_element_type=_F32)
    mix = mix + jnp.dot(right, wo_ref[MIX_W:, :], preferred_element_type=_F32) + bo_ref[...]
    x = _layer_norm(DEEPNORM_ALPHA * x_ref[...] + mix, g1_ref[...], b1_ref[...])
    xb = x.astype(_BF)
    acc = jnp.zeros(x.shape, _F32)
    for c in range(w1_ref.shape[1] // MLP_CHUNK):
        cols = slice(c * MLP_CHUNK, (c + 1) * MLP_CHUNK)
        h = jnp.dot(xb, w1_ref[:, cols], preferred_element_type=_F32)
        h = jnp.square(jnp.maximum(h, 0.0)).astype(_BF)
        acc = acc + jnp.dot(h, w2_ref[cols, :], preferred_element_type=_F32)
    o_ref[...] = _layer_norm(DEEPNORM_ALPHA * x + acc, g2_ref[...], b2_ref[...])


def _even_tail_kernel(x_ref, ya_ref, yb_ref, *tail_refs):
    _layer_tail(x_ref, ya_ref[...], yb_ref[...], tail_refs)


def _group_c(pooled_ref, wg_ref, scale_ref):
    return (jnp.dot(pooled_ref[...], wg_ref[...], preferred_element_type=_F32) * scale_ref[...]).astype(_BF)


def _sample_odd_tail_kernel(x_ref, pooled_ref, wg_ref, scale_ref, yd_ref, *tail_refs):
    _layer_tail(x_ref, _group_c(pooled_ref, wg_ref, scale_ref), yd_ref[...], tail_refs)


def _odd_tail_kernel(x_ref, pooled_ref, wg_ref, scale_ref, o1_ref, o2_ref, o3_ref, s1_ref, s2_ref, s3_ref,
                     *tail_refs):
    yc = _group_c(pooled_ref, wg_ref, scale_ref)
    tiles = []
    for p in range(N_PAIRS):
        lanes = slice(p * LANES, (p + 1) * LANES)
        lses = (s1_ref[p], s2_ref[p], s3_ref[p])
        top = jnp.maximum(jnp.maximum(lses[0], lses[1]), lses[2])
        es = [jnp.exp(s - top) for s in lses]
        num = es[0] * o1_ref[:, lanes].astype(_F32)
        num = num + es[1] * o2_ref[:, lanes].astype(_F32)
        num = num + es[2] * o3_ref[:, lanes].astype(_F32)
        tiles.append(num / (es[0] + es[1] + es[2]))
    yd = jnp.concatenate(tiles, axis=1)
    _layer_tail(x_ref, yc, yd.astype(_BF), tail_refs)


def _row_spec(tm, width):
    return pl.BlockSpec((tm, width), lambda i: (i, 0))


def _resident_spec(shape):
    nd = len(shape)
    return pl.BlockSpec(shape, lambda *_: (0,) * nd, pipeline_mode=pl.Buffered(1))


def _tail_call(kernel_fn, name, x, mixer_args, mixer_specs, tail_params, tm):
    n, dm = x.shape
    return pl.pallas_call(
        kernel_fn,
        grid=(n // tm,),
        in_specs=[_row_spec(tm, dm)] + mixer_specs + [_resident_spec(p.shape) for p in tail_params],
        out_specs=_row_spec(tm, dm),
        out_shape=jax.ShapeDtypeStruct((n, dm), _F32),
        compiler_params=_cparams(1),
        name=name,
    )(x, *mixer_args, *tail_params)


def _even_tail(x, ya, yb, tail_params, tm):
    return _tail_call(_even_tail_kernel, "even_tail", x, [ya, yb], [_row_spec(tm, MIX_W)] * 2, tail_params, tm)


def _odd_tail(x, pooled, wg, scale, outs, stats, tail_params, tm):
    tiles_per_seq = stats[0].shape[2] // tm
    stat_spec = pl.BlockSpec((None, N_PAIRS, tm, LANES), lambda i: (i // tiles_per_seq, 0, i % tiles_per_seq, 0))
    specs = ([_row_spec(tm, MIX_W), _resident_spec(wg.shape), _resident_spec(scale.shape)]
             + [_row_spec(tm, MIX_W)] * 3 + [stat_spec] * 3)
    return _tail_call(_odd_tail_kernel, "odd_tail", x, [pooled, wg, scale, *outs, *stats], specs, tail_params, tm)


def _sample_odd_tail(x, pooled, wg, scale, yd, tail_params, tm):
    specs = [_row_spec(tm, MIX_W), _resident_spec(wg.shape), _resident_spec(scale.shape), _row_spec(tm, MIX_W)]
    return _tail_call(_sample_odd_tail_kernel, "sample_odd_tail", x, [pooled, wg, scale, yd], specs, tail_params, tm)


def _fill_ext(ext_ref, hist, cur, t):
    nb = ext_ref.shape[0]
    ext_ref[:, 0:8, :] = jnp.zeros((nb, 8, MIX_W), _F32)
    ext_ref[:, 8:EXT0, :] = hist
    ext_ref[:, EXT0:EXT0 + t, :] = cur


def _conv_body(h, gb, gc, c_hist, w_ref, ext_ref, yb_ref, ctail_ref):
    t = h.shape[1]
    c = gc * h
    _fill_ext(ext_ref, c_hist, c, t)
    conv = ext_ref[:, EXT0 - 2:EXT0 - 2 + t, :] * w_ref[0:1, :]
    conv = conv + ext_ref[:, EXT0 - 1:EXT0 - 1 + t, :] * w_ref[1:2, :]
    conv = conv + c * w_ref[2:3, :]
    yb_ref[...] = (gb * conv).astype(yb_ref.dtype)
    ctail_ref[...] = ext_ref[:, EXT0 + t - 8:EXT0 + t, :]


def _sample_conv_kernel(h_ref, gb_ref, gc_ref, hist_ref, w_ref, yb_ref, ctail_ref, ext_ref):
    _conv_body(h_ref[...], gb_ref[...], gc_ref[...], hist_ref[...], w_ref, ext_ref, yb_ref, ctail_ref)


def _pool_body(u, hist, pos0, ext_ref, s2_ref, s4_ref, s8_ref, out_ref):
    nb, t, _ = u.shape
    _fill_ext(ext_ref, hist, u, t)
    hi = EXT0 + t
    zeros8 = jnp.zeros((nb, 8, MIX_W), _F32)
    s2_ref[:, 0:8, :] = zeros8
    s4_ref[:, 0:8, :] = zeros8
    s8_ref[:, 0:8, :] = zeros8
    s2_ref[:, 8:hi, :] = ext_ref[:, 8:hi, :] + ext_ref[:, 7:hi - 1, :]
    s4_ref[:, 8:hi, :] = s2_ref[:, 8:hi, :] + s2_ref[:, 6:hi - 2, :]
    s8_ref[:, 8:hi, :] = s4_ref[:, 8:hi, :] + s4_ref[:, 4:hi - 4, :]
    sums = (
        s2_ref[:, EXT0:hi, 0:LANES],
        s4_ref[:, EXT0:hi, LANES:2 * LANES],
        s8_ref[:, EXT0:hi, 2 * LANES:3 * LANES],
        s8_ref[:, EXT0:hi, 3 * LANES:] + s8_ref[:, EXT0 - 8:hi - 8, 3 * LANES:],
    )
    pos = (pos0 + lax.broadcasted_iota(jnp.int32, (1, t, LANES), 1) + 1).astype(_F32)
    tiles = []
    for g, (w, s) in enumerate(zip(POOL_WINDOWS, sums)):
        cnt = jnp.minimum(pos, float(w))
        tiles.append(s / cnt - u[:, :, g * LANES:(g + 1) * LANES])
    out_ref[...] = jnp.concatenate(tiles, axis=2).astype(out_ref.dtype)


def _sample_pool_kernel(u_ref, hist_ref, out_ref, ext_ref, s2_ref, s4_ref, s8_ref):
    _pool_body(u_ref[...], hist_ref[...], PAST_LEN, ext_ref, s2_ref, s4_ref, s8_ref, out_ref)


def _sample_conv(p3, hist, conv_w, bt):
    bsz, t, _ = p3.shape
    col = lambda c: pl.BlockSpec((bt, t, MIX_W), lambda b: (b, 0, c))
    return pl.pallas_call(
        _sample_conv_kernel,
        grid=(bsz // bt,),
        in_specs=[col(1), col(2), col(3), pl.BlockSpec((bt, HALO, MIX_W), lambda b: (b, 0, 0)),
                  _const_spec(conv_w.shape)],
        out_specs=[pl.BlockSpec((bt, t, MIX_W), lambda b: (b, 0, 0)),
                   pl.BlockSpec((bt, 8, MIX_W), lambda b: (b, 0, 0))],
        out_shape=[jax.ShapeDtypeStruct((bsz, t, MIX_W), _BF), jax.ShapeDtypeStruct((bsz, 8, MIX_W), _F32)],
        scratch_shapes=[pltpu.VMEM((bt, EXT0 + t, MIX_W), _F32)],
        compiler_params=_cparams(1),
        name="sample_conv",
    )(p3, p3, p3, hist, conv_w)


def _sample_pool(p3, hist, bt):
    bsz, t, _ = p3.shape
    scratch = pltpu.VMEM((bt, EXT0 + t, MIX_W), _F32)
    return pl.pallas_call(
        _sample_pool_kernel,
        grid=(bsz // bt,),
        in_specs=[pl.BlockSpec((bt, t, MIX_W), lambda b: (b, 0, 0)),
                  pl.BlockSpec((bt, HALO, MIX_W), lambda b: (b, 0, 0))],
        out_specs=pl.BlockSpec((bt, t, MIX_W), lambda b: (b, 0, 0)),
        out_shape=jax.ShapeDtypeStruct((bsz, t, MIX_W), _BF),
        scratch_shapes=[scratch] * 4,
        compiler_params=_cparams(1),
        name="sample_pool",
    )(p3, hist)


def _split_heads(q_pair):
    left = _lane_is_left()
    zero = jnp.zeros_like(q_pair)
    return jnp.concatenate([jnp.where(left, q_pair, zero), jnp.where(left, zero, q_pair)], axis=0)


def _band_attn_kernel(*refs, tq, dilation, kv_lane, has_sink, want_stat):
    refs = list(refs)
    q_ref, kc_ref, kp_ref, vc_ref, vp_ref, bias_ref = refs[:6]
    rest = refs[6:]
    sink_ref = rest.pop(0) if has_sink else None
    o_ref = rest.pop(0)
    st_ref = rest.pop(0) if want_stat else None
    stage_ref = rest.pop(0) if dilation > 1 else None

    first = pl.program_id(1) == 0
    res = pl.program_id(2)
    left = _lane_is_left()
    prev_cols = lax.broadcasted_iota(jnp.int32, (1, 2 * BAND), 1) < BAND
    top_rows = lax.broadcasted_iota(jnp.int32, (2 * BAND, 1), 0) < BAND
    ones = jnp.ones((2 * BAND, LANES), _BF)

    for j in range(tq // BAND):
        rows = slice(j * BAND, (j + 1) * BAND)
        out_rows = rows if dilation == 1 else pl.ds(j * BAND * dilation + res, BAND, stride=dilation)
        for p in range(N_PAIRS):
            kl = slice(kv_lane[p], kv_lane[p] + LANES)
            if j == 0:
                k_prev, v_prev = kp_ref[:, kl], vp_ref[:, kl]
            else:
                k_prev, v_prev = kc_ref[(j - 1) * BAND:j * BAND, kl], vc_ref[(j - 1) * BAND:j * BAND, kl]
            k2 = jnp.concatenate([k_prev, kc_ref[rows, kl]], axis=0)
            v2 = jnp.concatenate([v_prev, vc_ref[rows, kl]], axis=0)
            q2 = _split_heads(q_ref[rows, p * LANES:(p + 1) * LANES])
            s = lax.dot_general(q2, k2, (((1,), (1,)), ((), ())), preferred_element_type=_F32)
            s = s + bias_ref[p]
            if j == 0:
                s = jnp.where(jnp.logical_and(first, prev_cols), _NEG_INF, s)
            m = jnp.max(s, axis=1, keepdims=True)
            prob = jnp.exp(s - m).astype(_BF)
            r = jnp.dot(prob, jnp.concatenate([v2, ones], axis=1), preferred_element_type=_F32)
            pv, l = r[:, :LANES], r[:, LANES:]
            if has_sink:
                ha, hb = A_PAIR_HEADS[p]
                sink = jnp.where(top_rows, sink_ref[ha], sink_ref[hb])
                m2 = jnp.maximum(m, sink)
                a = jnp.exp(m - m2)
                o = pv * a / (l * a + jnp.exp(sink - m2))
            else:
                o = pv / l
            o_pair = jnp.where(left, o[:BAND], o[BAND:])
            if dilation == 1:
                o_ref[rows, p * LANES:(p + 1) * LANES] = o_pair.astype(o_ref.dtype)
            else:
                stage_ref[p, out_rows, :] = o_pair
            if want_stat:
                lse = m + jnp.log(l)
                st_ref[p, out_rows, :] = jnp.where(left, lse[:BAND], lse[BAND:])

    if dilation > 1:
        @pl.when(res == dilation - 1)
        def _():
            for p in range(N_PAIRS):
                o_ref[:, p * LANES:(p + 1) * LANES] = stage_ref[p].astype(o_ref.dtype)


def _band_bias(pair_heads, dilation):
    slopes = _alibi_slopes(N_HEADS)
    qi = np.arange(BAND)[:, None]
    kj = np.arange(2 * BAND)[None, :]
    dist = qi + BAND - kj
    valid = (dist >= 0) & (dist <= BAND)
    out = np.empty((len(pair_heads), 2 * BAND, 2 * BAND), np.float32)
    for p, heads in enumerate(pair_heads):
        for half, h in enumerate(heads):
            bias = -np.float32(slopes[h]) * (dist * dilation).astype(np.float32)
            out[p, half * BAND:(half + 1) * BAND] = np.where(valid, bias, -np.inf)
    return jnp.asarray(out)


BAND_TOKENS = 4096


def _band_attn(arr, q_col, k_col, v_col, kv_width, kv_lane, pair_heads, sinks, want_stat):
    bsz, dilation, n, _ = arr.shape
    seq = n * dilation
    tq = min(512, n, BAND_TOKENS // dilation)
    sub = tq // BAND
    assert q_col % MIX_W == 0 and k_col % kv_width == 0 and v_col % kv_width == 0

    def cur(col, w):
        return pl.BlockSpec((None, None, tq, w), lambda b, i, r: (b, r, i, col // w))

    def prev(col, w):
        return pl.BlockSpec((None, None, BAND, w), lambda b, i, r: (b, r, jnp.maximum(i * sub - 1, 0), col // w))

    in_specs = [cur(q_col, MIX_W), cur(k_col, kv_width), prev(k_col, kv_width), cur(v_col, kv_width),
                prev(v_col, kv_width), _const_spec((N_PAIRS, 2 * BAND, 2 * BAND))]
    args = [arr, arr, arr, arr, arr, _band_bias(pair_heads, dilation)]
    if sinks is not None:
        in_specs.append(pl.BlockSpec(memory_space=pltpu.SMEM))
        args.append(sinks)
    out_specs = [pl.BlockSpec((None, tq * dilation, MIX_W), lambda b, i, r: (b, i, 0))]
    out_shape = [jax.ShapeDtypeStruct((bsz, seq, MIX_W), _BF)]
    if want_stat:
        out_specs.append(pl.BlockSpec((None, N_PAIRS, tq * dilation, LANES), lambda b, i, r: (b, 0, i, 0)))
        out_shape.append(jax.ShapeDtypeStruct((bsz, N_PAIRS, seq, LANES), _F32))
    scratch = [pltpu.VMEM((N_PAIRS, tq * dilation, LANES), _F32)] if dilation > 1 else []
    res = pl.pallas_call(
        functools.partial(_band_attn_kernel, tq=tq, dilation=dilation, kv_lane=kv_lane, has_sink=sinks is not None,
                          want_stat=want_stat),
        grid=(bsz, n // tq, dilation),
        in_specs=in_specs,
        out_specs=out_specs,
        out_shape=out_shape,
        scratch_shapes=scratch,
        compiler_params=_cparams(3),
        name=f"band_attn_d{dilation}",
    )(*args)
    o = res[0].reshape(bsz * seq, MIX_W)
    return (o, res[1]) if want_stat else o


def _cached_attn_stages(rows, pair0, refs, sink_ref, *, n_hist, t_new, q_pairs_of_kv, pair_heads):
    q_ref, kn_ref, vn_ref, kc_ref, vc_ref, bias_ref, mult_ref, y_ref, ko_ref, vo_ref = refs
    left = _lane_is_left()
    new_lanes = lax.broadcasted_iota(jnp.int32, (1, LANES), 1) >= LANES - t_new
    top_rows = lax.broadcasted_iota(jnp.int32, (2 * t_new, 1), 0) < t_new
    zpad = jnp.zeros((LANES - t_new, LANES), _F32)
    mult = mult_ref[...]

    chains = []
    for b in rows:
        for kvp, q_pairs in enumerate(q_pairs_of_kv):
            heads = slice(2 * kvp, 2 * kvp + 2)
            lanes = slice(kvp * LANES, (kvp + 1) * LANES)
            ext = []
            for c_ref, n_ref, o_ref in ((kc_ref, kn_ref, ko_ref), (vc_ref, vn_ref, vo_ref)):
                old = c_ref[b, heads].reshape(LANES, n_hist)
                new = jnp.concatenate([zpad, n_ref[b, :, lanes]], axis=0).T
                rolled = pltpu.roll(old, n_hist - t_new, axis=1)
                tail = jnp.where(new_lanes, new, rolled[:, n_hist - LANES:])
                out = tail if n_hist == LANES else jnp.concatenate([rolled[:, :n_hist - LANES], tail], axis=1)
                o_ref[b, heads] = out.reshape(2, HEAD_DIM, n_hist)
                ext.append(jnp.concatenate([old.astype(_BF), new.astype(_BF)], axis=1))
            chains += [(b, p, ext[0], ext[1]) for p in q_pairs]

    scores = []
    for b, p, k_ext, _ in chains:
        q2 = _split_heads(q_ref[b, :, p * LANES:(p + 1) * LANES].astype(_BF))
        scores.append(jnp.dot(q2, k_ext, preferred_element_type=_F32) + bias_ref[pair0 + p])
    yield
    probs = []
    for s in scores:
        m = jnp.max(s, axis=1, keepdims=True)
        prob = (mult * jnp.exp(s - m)).astype(_BF)
        probs.append((m, prob, jnp.sum(prob.astype(_F32), axis=1, keepdims=True)))
    pvs = [lax.dot_general(prob, v_ext, (((1,), (1,)), ((), ())), preferred_element_type=_F32)
           for (_, prob, _), (_, _, _, v_ext) in zip(probs, chains)]
    yield
    for (b, p, _, _), (m, _, l), pv in zip(chains, probs, pvs):
        if sink_ref is not None:
            ha, hb = pair_heads[p]
            sink = jnp.where(top_rows, sink_ref[ha], sink_ref[hb])
            m2 = jnp.maximum(m, sink)
            a = jnp.exp(m - m2)
            o = pv * a / (l * a + jnp.exp(sink - m2))
        else:
            o = pv / l
        y_ref[b, :, p * LANES:(p + 1) * LANES] = jnp.where(left, o[:t_new], o[t_new:]).astype(y_ref.dtype)


def _sample_attn_kernel(*refs, has_sink, **statics):
    refs = list(refs)
    sink_ref = refs.pop(7) if has_sink else None
    for _ in _cached_attn_stages(range(refs[0].shape[0]), 0, refs, sink_ref, **statics):
        pass


def _sample_tables(pair_heads, branches, n_hist, t_new):
    slopes = _alibi_slopes(N_HEADS)
    key_pos = np.concatenate([np.arange(n_hist), n_hist + np.arange(LANES) - (LANES - t_new)])
    is_key = np.concatenate([np.ones(n_hist, bool), np.arange(LANES) >= LANES - t_new])
    delta = (n_hist + np.arange(t_new))[:, None] - key_pos[None, :]
    mult = np.zeros(delta.shape, np.float32)
    for window, dil in branches:
        mult += ((delta >= 0) & (delta % dil == 0) & (delta <= window) & is_key[None, :]).astype(np.float32)
    bias = np.empty((len(pair_heads), 2 * t_new, key_pos.size), np.float32)
    for p, heads in enumerate(pair_heads):
        for half, h in enumerate(heads):
            b = -np.float32(slopes[h]) * delta.astype(np.float32)
            bias[p, half * t_new:(half + 1) * t_new] = np.where(mult > 0, b, -np.inf)
    return jnp.asarray(bias), jnp.asarray(np.concatenate([mult, mult], axis=0))


UNIT_HEADS = 4


def _tail_cache_kernel(x_ref, ya_ref, yb_ref, wo_ref, bo_ref, g1_ref, b1_ref, w1_ref, w2_ref, g2_ref, b2_ref,
                       q_ref, kn_ref, vn_ref, kc_ref, vc_ref, bias_ref, mult_ref,
                       o_ref, y_ref, ko_ref, vo_ref, xb_ref, acc_ref, *, units_per_row, **statics):
    c = pl.program_id(1)
    n_chunks = pl.num_programs(1)
    unit = pl.program_id(0) * n_chunks + c

    @pl.when(c == 0)
    def _():
        mix = jnp.dot(ya_ref[...], wo_ref[:MIX_W, :], preferred_element_type=_F32)
        mix = mix + jnp.dot(yb_ref[...], wo_ref[MIX_W:, :], preferred_element_type=_F32) + bo_ref[...]
        x1 = _layer_norm(DEEPNORM_ALPHA * x_ref[...] + mix, g1_ref[...], b1_ref[...])
        xb_ref[...] = x1.astype(_BF)
        acc_ref[...] = DEEPNORM_ALPHA * x1

    pair0 = (unit % units_per_row) * (UNIT_HEADS // 2)
    cache_refs = (q_ref, kn_ref, vn_ref, kc_ref, vc_ref, bias_ref, mult_ref, y_ref, ko_ref, vo_ref)
    stages = _cached_attn_stages((0,), pair0, cache_refs, None, **statics)
    next(stages)
    h = jnp.dot(xb_ref[...], w1_ref[c], preferred_element_type=_F32)
    h = jnp.square(jnp.maximum(h, 0.0)).astype(_BF)
    next(stages)
    acc_ref[...] += jnp.dot(h, w2_ref[c], preferred_element_type=_F32)
    for _ in stages:
        pass

    @pl.when(c == n_chunks - 1)
    def _():
        o_ref[...] = _layer_norm(acc_ref[...], g2_ref[...], b2_ref[...])


def _even_tail_with_cache(x, ya, yb, tail_params, p3, q_col, k_col, v_col, pair_heads, branches, k_cache, v_cache,
                          tm, n_chunks):
    n, dm = x.shape
    bsz, t_new, _ = p3.shape
    _, n_hist, kvh, _ = k_cache.shape
    units_per_row = kvh // UNIT_HEADS
    unit_w = UNIT_HEADS * HEAD_DIM
    assert (n // tm) * n_chunks == bsz * units_per_row and kvh == N_HEADS
    wo, bo, g1, b1, w1, w2, g2, b2 = tail_params
    dh = w1.shape[1]
    w1c = jnp.transpose(w1.reshape(dm, n_chunks, dh // n_chunks), (1, 0, 2))
    w2c = w2.reshape(n_chunks, dh // n_chunks, dm)
    kt = jnp.transpose(k_cache, (0, 2, 3, 1))
    vt = jnp.transpose(v_cache, (0, 2, 3, 1))
    bias, mult = _sample_tables(pair_heads, branches, n_hist, t_new)

    row = lambda w: pl.BlockSpec((tm, w), lambda t, c: (t, 0))
    unit_of = lambda t, c: t * n_chunks + c
    new = lambda col: pl.BlockSpec(
        (1, t_new, unit_w),
        lambda t, c: (unit_of(t, c) // units_per_row, 0, col // unit_w + unit_of(t, c) % units_per_row))
    cache = pl.BlockSpec((1, UNIT_HEADS, HEAD_DIM, n_hist),
                         lambda t, c: (unit_of(t, c) // units_per_row, unit_of(t, c) % units_per_row, 0, 0))
    params = (wo, bo, g1, b1, w1c, w2c, g2, b2)
    out, y, ko, vo = pl.pallas_call(
        functools.partial(_tail_cache_kernel, units_per_row=units_per_row, n_hist=n_hist, t_new=t_new,
                          q_pairs_of_kv=tuple((p,) for p in range(UNIT_HEADS // 2)), pair_heads=pair_heads),
        grid=(n // tm, n_chunks),
        in_specs=[row(dm), row(MIX_W), row(MIX_W)] + [_resident_spec(p.shape) for p in params]
        + [new(q_col), new(k_col), new(v_col), cache, cache, _resident_spec(bias.shape), _resident_spec(mult.shape)],
        out_specs=[row(dm), new(0), cache, cache],
        out_shape=[jax.ShapeDtypeStruct((n, dm), _F32), jax.ShapeDtypeStruct((bsz, t_new, MIX_W), _BF),
                   jax.ShapeDtypeStruct(kt.shape, _F32), jax.ShapeDtypeStruct(vt.shape, _F32)],
        scratch_shapes=[pltpu.VMEM((tm, dm), _BF), pltpu.VMEM((tm, dm), _F32)],
        compiler_params=pltpu.CompilerParams(dimension_semantics=("arbitrary", "arbitrary"),
                                             vmem_limit_bytes=FUSED_VMEM_LIMIT),
        name="even_tail_with_cache",
    )(x, ya, yb, *params, p3, p3, p3, kt, vt, bias, mult)
    return out, y, jnp.transpose(ko, (0, 3, 1, 2)), jnp.transpose(vo, (0, 3, 1, 2))


def _sample_attn(p3, q_col, k_col, v_col, pair_heads, branches, k_cache, v_cache, sinks, bt):
    bsz, t_new, _ = p3.shape
    _, n_hist, kvh, _ = k_cache.shape
    kv_width = kvh * HEAD_DIM
    kt = jnp.transpose(k_cache, (0, 2, 3, 1))
    vt = jnp.transpose(v_cache, (0, 2, 3, 1))
    n_kv_pairs = kvh // 2
    q_pairs_of_kv = tuple(tuple(p for p in range(N_PAIRS) if p % n_kv_pairs == kvp) for kvp in range(n_kv_pairs))
    bias, mult = _sample_tables(pair_heads, branches, n_hist, t_new)
    new = lambda col, w: pl.BlockSpec((bt, t_new, w), lambda b: (b, 0, col // w))
    cache = pl.BlockSpec((bt, kvh, HEAD_DIM, n_hist), lambda b: (b, 0, 0, 0))
    in_specs = [new(q_col, MIX_W), new(k_col, kv_width), new(v_col, kv_width), cache, cache,
                _const_spec(bias.shape), _const_spec(mult.shape)]
    args = [p3, p3, p3, kt, vt, bias, mult]
    if sinks is not None:
        in_specs.append(pl.BlockSpec(memory_space=pltpu.SMEM))
        args.append(sinks)
    y, ko, vo = pl.pallas_call(
        functools.partial(_sample_attn_kernel, n_hist=n_hist, t_new=t_new, q_pairs_of_kv=q_pairs_of_kv,
                          pair_heads=pair_heads, has_sink=sinks is not None),
        grid=(bsz // bt,),
        in_specs=in_specs,
        out_specs=[pl.BlockSpec((bt, t_new, MIX_W), lambda b: (b, 0, 0)), cache, cache],
        out_shape=[jax.ShapeDtypeStruct((bsz, t_new, MIX_W), _BF),
                   jax.ShapeDtypeStruct(kt.shape, _F32), jax.ShapeDtypeStruct(vt.shape, _F32)],
        compiler_params=_cparams(1),
        name=f"sample_attn_{n_hist}",
    )(*args)
    return y, jnp.transpose(ko, (0, 3, 1, 2)), jnp.transpose(vo, (0, 3, 1, 2))


A_Q = N_HEADS * HEAD_DIM
A_KV = A_KV_HEADS * HEAD_DIM
E_Q, E_H, E_GB, E_GC, E_K, E_V = 0, 512, 1024, 1536, 2048, 2176
O_U, O_Q, O_K, O_V = 0, 512, 1024, 1536


def _prep_layer_weights(even_w_in, even_b_in, even_w_out, odd_w_in, odd_b_in, c_w_group):
    q_cols = np.concatenate([h * HEAD_DIM + np.arange(HEAD_DIM) for h in A_HEAD_ORDER])
    o1, o2, o3 = A_Q, A_Q + A_KV, A_Q + 2 * A_KV
    order = np.concatenate([q_cols, np.arange(o3, o3 + 3 * MIX_W), np.arange(o1, o3)])
    scale = np.ones((order.size,), np.float32)
    scale[:A_Q] = HEAD_DIM ** -0.5
    ew = (even_w_in[:, order] * scale).astype(_BF)
    eb = (even_b_in[order] * scale)[None, :]
    ewo = jnp.concatenate([even_w_out[q_cols], even_w_out[A_Q:]], axis=0).astype(_BF)
    oscale = np.ones((odd_w_in.shape[1],), np.float32)
    oscale[O_Q:O_K] = HEAD_DIM ** -0.5
    ow = (odd_w_in * oscale).astype(_BF)
    ob = (odd_b_in * oscale)[None, :]
    groups, gw, _ = c_w_group.shape
    wg = jnp.zeros((MIX_W, MIX_W), _F32)
    for g in range(groups):
        wg = wg.at[g * gw:(g + 1) * gw, g * gw:(g + 1) * gw].set(c_w_group[g])
    return ew, eb, ewo, ow, ob, wg.astype(_BF)


def _row(v):
    return v[None, :]


def _forward(xp, xs, caches, wts, tm, bt_attn, bt_shift, n_chunks):
    (even_w_in, even_b_in, a_sinks, b_conv_w, even_w_out, even_b_out, odd_w_in, odd_b_in, c_w_group, c_scale,
     odd_w_out, odd_b_out, mlp_w1, mlp_w2, ln1_g, ln1_b, ln2_g, ln2_b) = wts
    cache_a_k, cache_a_v, state_b_conv, state_c_pool, cache_d_k, cache_d_v = caches
    bsz, seq, dm = xp.shape
    bs, ts, _ = xs.shape
    n, ns = bsz * seq, bs * ts
    tms = min(tm, ns)
    ew, eb, ewo, ow, ob, wg = _prep_layer_weights(even_w_in[0], even_b_in[0], even_w_out[0], odd_w_in[0],
                                                  odd_b_in[0], c_w_group[0])
    owo = odd_w_out[0].astype(_BF)
    w1 = mlp_w1.astype(_BF)
    w2 = mlp_w2.astype(_BF)
    tails = [(wo, _row(bo), _row(ln1_g[i]), _row(ln1_b[i]), w1[i], w2[i], _row(ln2_g[i]), _row(ln2_b[i]))
             for i, (wo, bo) in enumerate(((ewo, even_b_out[0]), (owo, odd_b_out[0])))]
    a_kv_lane = (0,) * N_PAIRS
    d_kv_lane = tuple(p * LANES for p in range(N_PAIRS))

    xs = xs.reshape(ns, dm)
    pe3 = _proj(xs, ew, eb, _F32, tms).reshape(bs, ts, -1)
    ya, ak_s, av_s = _sample_attn(pe3, E_Q, E_K, E_V, A_PAIR_HEADS, ((A_WINDOW, 1),), cache_a_k[0], cache_a_v[0],
                                  a_sinks[0], bt_attn)
    c_hist = jnp.pad(state_b_conv[0], ((0, 0), (HALO - (CONV_WIDTH - 1), 0), (0, 0)))
    yb, ctail = _sample_conv(pe3, c_hist, b_conv_w[0], bt_shift)
    bc_s = ctail[:, -(CONV_WIDTH - 1):, :]
    xs = _even_tail(xs, ya.reshape(ns, MIX_W), yb.reshape(ns, MIX_W), tails[0], tms)
    po3s = _proj(xs, ow, ob, _F32, tms).reshape(bs, ts, -1)
    u_hist = jnp.pad(state_c_pool[0], ((0, 0), (HALO - (POOL_MAX - 1), 0), (0, 0)))
    pooled_s = _sample_pool(po3s, u_hist, bt_shift).reshape(ns, MIX_W)
    cp_s = jnp.concatenate([state_c_pool[0], po3s[:, :, O_U:O_Q]], axis=1)[:, -(POOL_MAX - 1):]

    x = xp.reshape(n, dm)
    qkv, yb, ctail = _proj_conv(xp, ew, eb, b_conv_w[0], tm)
    k_col, v_col = MIX_W, MIX_W + A_KV
    ya = _band_attn(qkv[:, None], 0, k_col, v_col, LANES, a_kv_lane, A_PAIR_HEADS, a_sinks[0], False)
    n_keep = min(A_WINDOW, seq)
    a_k = qkv[:, seq - n_keep:, k_col:v_col].astype(_F32).reshape(bsz, n_keep, A_KV_HEADS, HEAD_DIM)
    a_v = qkv[:, seq - n_keep:, v_col:].astype(_F32).reshape(bsz, n_keep, A_KV_HEADS, HEAD_DIM)
    b_conv = ctail[:, -(CONV_WIDTH - 1):, :]
    x, yd, dk_s, dv_s = _even_tail_with_cache(x, ya, yb.reshape(n, MIX_W), tails[0], po3s, O_Q, O_K, O_V,
                                              D_PAIR_HEADS, D_BRANCHES, cache_d_k[0], cache_d_v[0], tm, n_chunks)
    xs = _sample_odd_tail(xs, pooled_s, wg, _row(c_scale[0]), yd.reshape(ns, MIX_W), tails[1], tms)

    dils = tuple(d for _, d in D_BRANCHES if d > 1)
    qkv, qkv4, qkv16, pooled, utail = _proj_dilated(x.reshape(bsz, seq, dm), ow, ob, tm, O_Q, dils)
    k_col, v_col = O_K - O_Q, O_V - O_Q
    outs, stats = [], []
    for arr in (qkv[:, None], qkv4, qkv16):
        o, st = _band_attn(arr, 0, k_col, v_col, MIX_W, d_kv_lane, D_PAIR_HEADS, None, True)
        outs.append(o)
        stats.append(st)
    n_keep = min(D_BRANCHES[-1][0], seq)
    c_pool = utail[:, -(POOL_MAX - 1):, :]
    d_k = qkv[:, seq - n_keep:, k_col:v_col].astype(_F32).reshape(bsz, n_keep, N_HEADS, HEAD_DIM)
    d_v = qkv[:, seq - n_keep:, v_col:].astype(_F32).reshape(bsz, n_keep, N_HEADS, HEAD_DIM)
    x = _odd_tail(x, pooled.reshape(n, MIX_W), wg, _row(c_scale[0]), outs, stats, tails[1], tm)
    return (x.reshape(bsz, seq, dm), xs.reshape(bs, ts, dm), a_k[None], a_v[None], b_conv[None], c_pool[None],
            d_k[None], d_v[None], ak_s[None], av_s[None], bc_s[None], cp_s[None], dk_s[None], dv_s[None])


def kernel(x_prompt, x_sample, cache_a_k, cache_a_v, state_b_conv, state_c_pool, cache_d_k, cache_d_v, even_w_in, even_b_in, a_sinks, b_conv_w, even_w_out, even_b_out, odd_w_in, odd_b_in, c_w_group, c_scale, odd_w_out, odd_b_out, mlp_w1, mlp_w2, ln1_g, ln1_b, ln2_g, ln2_b):
    wts = (even_w_in, even_b_in, a_sinks, b_conv_w, even_w_out, even_b_out, odd_w_in, odd_b_in, c_w_group, c_scale,
           odd_w_out, odd_b_out, mlp_w1, mlp_w2, ln1_g, ln1_b, ln2_g, ln2_b)
    caches = (cache_a_k, cache_a_v, state_b_conv, state_c_pool, cache_d_k, cache_d_v)
    return _forward(x_prompt, x_sample, caches, wts, tm=512, bt_attn=8, bt_shift=32, n_chunks=4)
```

```python
import functools

import numpy as np
import jax
import jax.numpy as jnp
from jax import lax
from jax.experimental import pallas as pl
from jax.experimental.pallas import tpu as pltpu

HEAD_DIM = 64
N_HEADS = 8
A_KV_HEADS = 2
A_WINDOW = 128
D_BRANCHES = ((128, 1), (512, 4), (2048, 16))
CONV_WIDTH = 3
POOL_WINDOWS = (2, 4, 8, 16)
POOL_MAX = 16
DEPTH = 2
PAST_LEN = 16384
DEEPNORM_ALPHA = (2 * DEPTH) ** 0.25
LN_EPS = 1e-5

MIX_W = N_HEADS * HEAD_DIM
LANES = 128
N_PAIRS = MIX_W // LANES
BAND = 128
HALO = 16
EXT0 = 24
VMEM_LIMIT = 56 * 1024 * 1024
FUSED_VMEM_LIMIT = 62 * 1024 * 1024

A_HEAD_ORDER = (0, 4, 1, 5, 2, 6, 3, 7)
A_PAIR_HEADS = tuple((p, p + 4) for p in range(N_PAIRS))
D_PAIR_HEADS = tuple((2 * p, 2 * p + 1) for p in range(N_PAIRS))

_BF = jnp.bfloat16
_F32 = jnp.float32
_NEG_INF = float("-inf")


def _alibi_slopes(n_heads):
    return 2.0 ** (-8.0 * np.arange(1, n_heads + 1) / n_heads)


def _cparams(n_axes):
    return pltpu.CompilerParams(dimension_semantics=("arbitrary",) * n_axes, vmem_limit_bytes=VMEM_LIMIT)


def _const_spec(shape):
    nd = len(shape)
    return pl.BlockSpec(shape, lambda *_: (0,) * nd)


def _layer_norm(y, g, b):
    mu = jnp.mean(y, axis=-1, keepdims=True)
    yc = y - mu
    var = jnp.mean(yc * yc, axis=-1, keepdims=True)
    return yc * lax.rsqrt(var + LN_EPS) * g + b


def _proj_kernel(x_ref, w_ref, b_ref, o_ref, *, tn):
    x = x_ref[...].astype(_BF)
    for j in range(o_ref.shape[1] // tn):
        cols = slice(j * tn, (j + 1) * tn)
        acc = jnp.dot(x, w_ref[:, cols], preferred_element_type=_F32)
        o_ref[:, cols] = (acc + b_ref[:, cols]).astype(o_ref.dtype)


def _proj(x, w, b, out_dtype, tm):
    n, k = x.shape
    m = w.shape[1]
    return pl.pallas_call(
        functools.partial(_proj_kernel, tn=256),
        grid=(n // tm,),
        in_specs=[pl.BlockSpec((tm, k), lambda i: (i, 0)), _const_spec((k, m)), _const_spec((1, m))],
        out_specs=pl.BlockSpec((tm, m), lambda i: (i, 0)),
        out_shape=jax.ShapeDtypeStruct((n, m), out_dtype),
        compiler_params=_cparams(1),
        name="proj",
    )(x, w, b)


def _carried_history(ext_ref, tm):
    @pl.when(pl.program_id(1) == 0)
    def _():
        ext_ref[:, tm + EXT0 - HALO:tm + EXT0, :] = jnp.zeros((1, HALO, MIX_W), _F32)

    return ext_ref[:, tm + EXT0 - HALO:tm + EXT0, :]


def _proj_conv_kernel(x_ref, w_ref, b_ref, cw_ref, qkv_ref, yb_ref, ctail_ref, hg_ref, ext_ref, *, tn):
    tm = x_ref.shape[0]
    c_hist = _carried_history(ext_ref, tm)
    x = x_ref[...].astype(_BF)
    gates = slice(MIX_W, 4 * MIX_W)
    for j in range(w_ref.shape[1] // tn):
        lo = j * tn
        acc = jnp.dot(x, w_ref[:, lo:lo + tn], preferred_element_type=_F32) + b_ref[:, lo:lo + tn]
        if lo < gates.start:
            qkv_ref[:, lo:lo + tn] = acc.astype(qkv_ref.dtype)
        elif lo < gates.stop:
            hg_ref[:, lo - gates.start:lo - gates.start + tn] = acc
        else:
            qkv_ref[:, lo - 3 * MIX_W:lo - 3 * MIX_W + tn] = acc.astype(qkv_ref.dtype)
    h, gb, gc = (hg_ref[:, k * MIX_W:(k + 1) * MIX_W][None] for k in range(3))
    _conv_body(h, gb, gc, c_hist, cw_ref, ext_ref, yb_ref, ctail_ref)


def _proj_conv(x3, w, b, conv_w, tm):
    bsz, seq, k = x3.shape
    m = w.shape[1]
    qkv_w = m - 3 * MIX_W
    return pl.pallas_call(
        functools.partial(_proj_conv_kernel, tn=256),
        grid=(bsz, seq // tm),
        in_specs=[pl.BlockSpec((None, tm, k), lambda bi, i: (bi, i, 0)), _const_spec((k, m)), _const_spec((1, m)),
                  _const_spec(conv_w.shape)],
        out_specs=[pl.BlockSpec((None, tm, qkv_w), lambda bi, i: (bi, i, 0)),
                   pl.BlockSpec((1, tm, MIX_W), lambda bi, i: (bi, i, 0)),
                   pl.BlockSpec((1, 8, MIX_W), lambda bi, i: (bi, i, 0))],
        out_shape=[jax.ShapeDtypeStruct((bsz, seq, qkv_w), _BF), jax.ShapeDtypeStruct((bsz, seq, MIX_W), _BF),
                   jax.ShapeDtypeStruct((bsz, (seq // tm) * 8, MIX_W), _F32)],
        scratch_shapes=[pltpu.VMEM((tm, 3 * MIX_W), _F32), pltpu.VMEM((1, EXT0 + tm, MIX_W), _F32)],
        compiler_params=_cparams(2),
        name="proj_conv",
    )(x3, w, b, conv_w)


def _proj_dilated_kernel(x_ref, w_ref, b_ref, o_ref, d4_ref, d16_ref, pooled_ref, utail_ref,
                         stage_ref, stage4_ref, u_ref, ext_ref, s2_ref, s4_ref, s8_ref, *, tn, first_col):
    tm = x_ref.shape[0]
    u_hist = _carried_history(ext_ref, tm)
    x = x_ref[...].astype(_BF)
    for j in range(w_ref.shape[1] // tn):
        lo = j * tn
        acc = jnp.dot(x, w_ref[:, lo:lo + tn], preferred_element_type=_F32) + b_ref[:, lo:lo + tn]
        if lo < first_col:
            u_ref[:, lo:lo + tn] = acc
        else:
            o_ref[:, lo - first_col:lo - first_col + tn] = acc.astype(o_ref.dtype)
            for h in range(tn // LANES):
                stage_ref[(lo - first_col) // LANES + h] = acc[:, h * LANES:(h + 1) * LANES]
    _pool_body(u_ref[...][None], u_hist, pl.program_id(1) * tm, ext_ref, s2_ref, s4_ref, s8_ref, pooled_ref)
    utail_ref[...] = ext_ref[:, tm + EXT0 - HALO:tm + EXT0, :]
    q4, q16 = tm // 4, tm // 16
    for s in range(stage_ref.shape[0]):
        lanes = slice(s * LANES, (s + 1) * LANES)
        for r in range(4):
            rows = stage_ref[s, pl.ds(r, q4, stride=4), :]
            stage4_ref[s, r * q4:(r + 1) * q4, :] = rows
            d4_ref[r, :, lanes] = rows.astype(d4_ref.dtype)
        for r in range(4):
            for k in range(4):
                d16_ref[r + 4 * k, :, lanes] = stage4_ref[s, pl.ds(r * q4 + k, q16, stride=4), :].astype(d16_ref.dtype)


def _proj_dilated(x3, w, b, tm, first_col, dilations):
    bsz, seq, k = x3.shape
    m = w.shape[1]
    wd = m - first_col
    assert first_col == MIX_W and tuple(dilations) == (4, 16) and tm % 256 == 0
    out_specs = [pl.BlockSpec((None, tm, wd), lambda bi, i: (bi, i, 0))]
    out_shape = [jax.ShapeDtypeStruct((bsz, seq, wd), _BF)]
    for d in dilations:
        out_specs.append(pl.BlockSpec((None, d, tm // d, wd), lambda bi, i: (bi, 0, i, 0)))
        out_shape.append(jax.ShapeDtypeStruct((bsz, d, seq // d, wd), _BF))
    out_specs += [pl.BlockSpec((1, tm, MIX_W), lambda bi, i: (bi, i, 0)),
                  pl.BlockSpec((1, HALO, MIX_W), lambda bi, i: (bi, i, 0))]
    out_shape += [jax.ShapeDtypeStruct((bsz, seq, MIX_W), _BF),
                  jax.ShapeDtypeStruct((bsz, (seq // tm) * HALO, MIX_W), _F32)]
    shift = pltpu.VMEM((1, EXT0 + tm, MIX_W), _F32)
    return pl.pallas_call(
        functools.partial(_proj_dilated_kernel, tn=256, first_col=first_col),
        grid=(bsz, seq // tm),
        in_specs=[pl.BlockSpec((None, tm, k), lambda bi, i: (bi, i, 0)), _const_spec((k, m)), _const_spec((1, m))],
        out_specs=out_specs,
        out_shape=out_shape,
        scratch_shapes=[pltpu.VMEM((wd // LANES, tm, LANES), _F32)] * 2 + [pltpu.VMEM((tm, MIX_W), _F32)] + [shift] * 4,
        compiler_params=_cparams(2),
        name="proj_dilated",
    )(x3, w, b)


MLP_CHUNK = 512


def _lane_is_left():
    return lax.broadcasted_iota(jnp.int32, (1, LANES), 1) < HEAD_DIM


def _layer_tail(x_ref, left, right, tail_refs):
    wo_ref, bo_ref, g1_ref, b1_ref, w1_ref, w2_ref, g2_ref, b2_ref, o_ref = tail_refs
    mix = jnp.dot(left, wo_ref[:MIX_W, :], preferred_element_type=_F32)
    mix = mix + jnp.dot(right, wo_ref[MIX_W:, :], preferred_element_type=_F32) + bo_ref[...]
    x = _layer_norm(DEEPNORM_ALPHA * x_ref[...] + mix, g1_ref[...], b1_ref[...])
    xb = x.astype(_BF)
    acc = jnp.zeros(x.shape, _F32)
    for c in range(w1_ref.shape[1] // MLP_CHUNK):
        cols = slice(c * MLP_CHUNK, (c + 1) * MLP_CHUNK)
        h = jnp.dot(xb, w1_ref[:, cols], preferred_element_type=_F32)
        h = jnp.square(jnp.maximum(h, 0.0)).astype(_BF)
        acc = acc + jnp.dot(h, w2_ref[cols, :], preferred_element_type=_F32)
    o_ref[...] = _layer_norm(DEEPNORM_ALPHA * x + acc, g2_ref[...], b2_ref[...])


def _even_tail_kernel(x_ref, ya_ref, yb_ref, *tail_refs):
    _layer_tail(x_ref, ya_ref[...], yb_ref[...], tail_refs)


def _group_c(pooled_ref, wg_ref, scale_ref):
    return (jnp.dot(pooled_ref[...], wg_ref[...], preferred_element_type=_F32) * scale_ref[...]).astype(_BF)


def _sample_odd_tail_kernel(x_ref, pooled_ref, wg_ref, scale_ref, yd_ref, *tail_refs):
    _layer_tail(x_ref, _group_c(pooled_ref, wg_ref, scale_ref), yd_ref[...], tail_refs)


def _odd_tail_kernel(x_ref, pooled_ref, wg_ref, scale_ref, o1_ref, o2_ref, o3_ref, s1_ref, s2_ref, s3_ref,
                     *tail_refs):
    yc = _group_c(pooled_ref, wg_ref, scale_ref)
    tiles = []
    for p in range(N_PAIRS):
        lanes = slice(p * LANES, (p + 1) * LANES)
        lses = (s1_ref[p], s2_ref[p], s3_ref[p])
        top = jnp.maximum(jnp.maximum(lses[0], lses[1]), lses[2])
        es = [jnp.exp(s - top) for s in lses]
        num = es[0] * o1_ref[:, lanes].astype(_F32)
        num = num + es[1] * o2_ref[:, lanes].astype(_F32)
        num = num + es[2] * o3_ref[:, lanes].astype(_F32)
        tiles.append(num / (es[0] + es[1] + es[2]))
    yd = jnp.concatenate(tiles, axis=1)
    _layer_tail(x_ref, yc, yd.astype(_BF), tail_refs)


def _row_spec(tm, width):
    return pl.BlockSpec((tm, width), lambda i: (i, 0))


def _resident_spec(shape):
    nd = len(shape)
    return pl.BlockSpec(shape, lambda *_: (0,) * nd, pipeline_mode=pl.Buffered(1))


def _tail_call(kernel_fn, name, x, mixer_args, mixer_specs, tail_params, tm):
    n, dm = x.shape
    return pl.pallas_call(
        kernel_fn,
        grid=(n // tm,),
        in_specs=[_row_spec(tm, dm)] + mixer_specs + [_resident_spec(p.shape) for p in tail_params],
        out_specs=_row_spec(tm, dm),
        out_shape=jax.ShapeDtypeStruct((n, dm), _F32),
        compiler_params=_cparams(1),
        name=name,
    )(x, *mixer_args, *tail_params)


def _even_tail(x, ya, yb, tail_params, tm):
    return _tail_call(_even_tail_kernel, "even_tail", x, [ya, yb], [_row_spec(tm, MIX_W)] * 2, tail_params, tm)


def _odd_tail(x, pooled, wg, scale, outs, stats, tail_params, tm):
    tiles_per_seq = stats[0].shape[2] // tm
    stat_spec = pl.BlockSpec((None, N_PAIRS, tm, LANES), lambda i: (i // tiles_per_seq, 0, i % tiles_per_seq, 0))
    specs = ([_row_spec(tm, MIX_W), _resident_spec(wg.shape), _resident_spec(scale.shape)]
             + [_row_spec(tm, MIX_W)] * 3 + [stat_spec] * 3)
    return _tail_call(_odd_tail_kernel, "odd_tail", x, [pooled, wg, scale, *outs, *stats], specs, tail_params, tm)


def _sample_odd_tail(x, pooled, wg, scale, yd, tail_params, tm):
    specs = [_row_spec(tm, MIX_W), _resident_spec(wg.shape), _resident_spec(scale.shape), _row_spec(tm, MIX_W)]
    return _tail_call(_sample_odd_tail_kernel, "sample_odd_tail", x, [pooled, wg, scale, yd], specs, tail_params, tm)


def _fill_ext(ext_ref, hist, cur, t):
    nb = ext_ref.shape[0]
    ext_ref[:, 0:8, :] = jnp.zeros((nb, 8, MIX_W), _F32)
    ext_ref[:, 8:EXT0, :] = hist
    ext_ref[:, EXT0:EXT0 + t, :] = cur


def _conv_body(h, gb, gc, c_hist, w_ref, ext_ref, yb_ref, ctail_ref):
    t = h.shape[1]
    c = gc * h
    _fill_ext(ext_ref, c_hist, c, t)
    conv = ext_ref[:, EXT0 - 2:EXT0 - 2 + t, :] * w_ref[0:1, :]
    conv = conv + ext_ref[:, EXT0 - 1:EXT0 - 1 + t, :] * w_ref[1:2, :]
    conv = conv + c * w_ref[2:3, :]
    yb_ref[...] = (gb * conv).astype(yb_ref.dtype)
    ctail_ref[...] = ext_ref[:, EXT0 + t - 8:EXT0 + t, :]


def _sample_conv_kernel(h_ref, gb_ref, gc_ref, hist_ref, w_ref, yb_ref, ctail_ref, ext_ref):
    _conv_body(h_ref[...], gb_ref[...], gc_ref[...], hist_ref[...], w_ref, ext_ref, yb_ref, ctail_ref)


def _pool_body(u, hist, pos0, ext_ref, s2_ref, s4_ref, s8_ref, out_ref):
    nb, t, _ = u.shape
    _fill_ext(ext_ref, hist, u, t)
    hi = EXT0 + t
    zeros8 = jnp.zeros((nb, 8, MIX_W), _F32)
    s2_ref[:, 0:8, :] = zeros8
    s4_ref[:, 0:8, :] = zeros8
    s8_ref[:, 0:8, :] = zeros8
    s2_ref[:, 8:hi, :] = ext_ref[:, 8:hi, :] + ext_ref[:, 7:hi - 1, :]
    s4_ref[:, 8:hi, :] = s2_ref[:, 8:hi, :] + s2_ref[:, 6:hi - 2, :]
    s8_ref[:, 8:hi, :] = s4_ref[:, 8:hi, :] + s4_ref[:, 4:hi - 4, :]
    sums = (
        s2_ref[:, EXT0:hi, 0:LANES],
        s4_ref[:, EXT0:hi, LANES:2 * LANES],
        s8_ref[:, EXT0:hi, 2 * LANES:3 * LANES],
        s8_ref[:, EXT0:hi, 3 * LANES:] + s8_ref[:, EXT0 - 8:hi - 8, 3 * LANES:],
    )
    pos = (pos0 + lax.broadcasted_iota(jnp.int32, (1, t, LANES), 1) + 1).astype(_F32)
    tiles = []
    for g, (w, s) in enumerate(zip(POOL_WINDOWS, sums)):
        cnt = jnp.minimum(pos, float(w))
        tiles.append(s / cnt - u[:, :, g * LANES:(g + 1) * LANES])
    out_ref[...] = jnp.concatenate(tiles, axis=2).astype(out_ref.dtype)


def _sample_pool_kernel(u_ref, hist_ref, out_ref, ext_ref, s2_ref, s4_ref, s8_ref):
    _pool_body(u_ref[...], hist_ref[...], PAST_LEN, ext_ref, s2_ref, s4_ref, s8_ref, out_ref)


def _sample_conv(p3, hist, conv_w, bt):
    bsz, t, _ = p3.shape
    col = lambda c: pl.BlockSpec((bt, t, MIX_W), lambda b: (b, 0, c))
    return pl.pallas_call(
        _sample_conv_kernel,
        grid=(bsz // bt,),
        in_specs=[col(1), col(2), col(3), pl.BlockSpec((bt, HALO, MIX_W), lambda b: (b, 0, 0)),
                  _const_spec(conv_w.shape)],
        out_specs=[pl.BlockSpec((bt, t, MIX_W), lambda b: (b, 0, 0)),
                   pl.BlockSpec((bt, 8, MIX_W), lambda b: (b, 0, 0))],
        out_shape=[jax.ShapeDtypeStruct((bsz, t, MIX_W), _BF), jax.ShapeDtypeStruct((bsz, 8, MIX_W), _F32)],
        scratch_shapes=[pltpu.VMEM((bt, EXT0 + t, MIX_W), _F32)],
        compiler_params=_cparams(1),
        name="sample_conv",
    )(p3, p3, p3, hist, conv_w)


def _sample_pool(p3, hist, bt):
    bsz, t, _ = p3.shape
    scratch = pltpu.VMEM((bt, EXT0 + t, MIX_W), _F32)
    return pl.pallas_call(
        _sample_pool_kernel,
        grid=(bsz // bt,),
        in_specs=[pl.BlockSpec((bt, t, MIX_W), lambda b: (b, 0, 0)),
                  pl.BlockSpec((bt, HALO, MIX_W), lambda b: (b, 0, 0))],
        out_specs=pl.BlockSpec((bt, t, MIX_W), lambda b: (b, 0, 0)),
        out_shape=jax.ShapeDtypeStruct((bsz, t, MIX_W), _BF),
        scratch_shapes=[scratch] * 4,
        compiler_params=_cparams(1),
        name="sample_pool",
    )(p3, hist)


def _split_heads(q_pair):
    left = _lane_is_left()
    zero = jnp.zeros_like(q_pair)
    return jnp.concatenate([jnp.where(left, q_pair, zero), jnp.where(left, zero, q_pair)], axis=0)


def _band_attn_kernel(*refs, tq, dilation, kv_lane, has_sink, want_stat):
    refs = list(refs)
    q_ref, kc_ref, kp_ref, vc_ref, vp_ref, bias_ref = refs[:6]
    rest = refs[6:]
    sink_ref = rest.pop(0) if has_sink else None
    o_ref = rest.pop(0)
    st_ref = rest.pop(0) if want_stat else None
    stage_ref = rest.pop(0) if dilation > 1 else None

    first = pl.program_id(1) == 0
    res = pl.program_id(2)
    left = _lane_is_left()
    prev_cols = lax.broadcasted_iota(jnp.int32, (1, 2 * BAND), 1) < BAND
    top_rows = lax.broadcasted_iota(jnp.int32, (2 * BAND, 1), 0) < BAND
    ones = jnp.ones((2 * BAND, LANES), _BF)

    for j in range(tq // BAND):
        rows = slice(j * BAND, (j + 1) * BAND)
        out_rows = rows if dilation == 1 else pl.ds(j * BAND * dilation + res, BAND, stride=dilation)
        for p in range(N_PAIRS):
            kl = slice(kv_lane[p], kv_lane[p] + LANES)
            if j == 0:
                k_prev, v_prev = kp_ref[:, kl], vp_ref[:, kl]
            else:
                k_prev, v_prev = kc_ref[(j - 1) * BAND:j * BAND, kl], vc_ref[(j - 1) * BAND:j * BAND, kl]
            k2 = jnp.concatenate([k_prev, kc_ref[rows, kl]], axis=0)
            v2 = jnp.concatenate([v_prev, vc_ref[rows, kl]], axis=0)
            q2 = _split_heads(q_ref[rows, p * LANES:(p + 1) * LANES])
            s = lax.dot_general(q2, k2, (((1,), (1,)), ((), ())), preferred_element_type=_F32)
            s = s + bias_ref[p]
            if j == 0:
                s = jnp.where(jnp.logical_and(first, prev_cols), _NEG_INF, s)
            m = jnp.max(s, axis=1, keepdims=True)
            prob = jnp.exp(s - m).astype(_BF)
            r = jnp.dot(prob, jnp.concatenate([v2, ones], axis=1), preferred_element_type=_F32)
            pv, l = r[:, :LANES], r[:, LANES:]
            if has_sink:
                ha, hb = A_PAIR_HEADS[p]
                sink = jnp.where(top_rows, sink_ref[ha], sink_ref[hb])
                m2 = jnp.maximum(m, sink)
                a = jnp.exp(m - m2)
                o = pv * a / (l * a + jnp.exp(sink - m2))
            else:
                o = pv / l
            o_pair = jnp.where(left, o[:BAND], o[BAND:])
            if dilation == 1:
                o_ref[rows, p * LANES:(p + 1) * LANES] = o_pair.astype(o_ref.dtype)
            else:
                stage_ref[p, out_rows, :] = o_pair
            if want_stat:
                lse = m + jnp.log(l)
                st_ref[p, out_rows, :] = jnp.where(left, lse[:BAND], lse[BAND:])

    if dilation > 1:
        @pl.when(res == dilation - 1)
        def _():
            for p in range(N_PAIRS):
                o_ref[:, p * LANES:(p + 1) * LANES] = stage_ref[p].astype(o_ref.dtype)


def _band_bias(pair_heads, dilation):
    slopes = _alibi_slopes(N_HEADS)
    qi = np.arange(BAND)[:, None]
    kj = np.arange(2 * BAND)[None, :]
    dist = qi + BAND - kj
    valid = (dist >= 0) & (dist <= BAND)
    out = np.empty((len(pair_heads), 2 * BAND, 2 * BAND), np.float32)
    for p, heads in enumerate(pair_heads):
        for half, h in enumerate(heads):
            bias = -np.float32(slopes[h]) * (dist * dilation).astype(np.float32)
            out[p, half * BAND:(half + 1) * BAND] = np.where(valid, bias, -np.inf)
    return jnp.asarray(out)


BAND_TOKENS = 4096


def _band_attn(arr, q_col, k_col, v_col, kv_width, kv_lane, pair_heads, sinks, want_stat):
    bsz, dilation, n, _ = arr.shape
    seq = n * dilation
    tq = min(512, n, BAND_TOKENS // dilation)
    sub = tq // BAND
    assert q_col % MIX_W == 0 and k_col % kv_width == 0 and v_col % kv_width == 0

    def cur(col, w):
        return pl.BlockSpec((None, None, tq, w), lambda b, i, r: (b, r, i, col // w))

    def prev(col, w):
        return pl.BlockSpec((None, None, BAND, w), lambda b, i, r: (b, r, jnp.maximum(i * sub - 1, 0), col // w))

    in_specs = [cur(q_col, MIX_W), cur(k_col, kv_width), prev(k_col, kv_width), cur(v_col, kv_width),
                prev(v_col, kv_width), _const_spec((N_PAIRS, 2 * BAND, 2 * BAND))]
    args = [arr, arr, arr, arr, arr, _band_bias(pair_heads, dilation)]
    if sinks is not None:
        in_specs.append(pl.BlockSpec(memory_space=pltpu.SMEM))
        args.append(sinks)
    out_specs = [pl.BlockSpec((None, tq * dilation, MIX_W), lambda b, i, r: (b, i, 0))]
    out_shape = [jax.ShapeDtypeStruct((bsz, seq, MIX_W), _BF)]
    if want_stat:
        out_specs.append(pl.BlockSpec((None, N_PAIRS, tq * dilation, LANES), lambda b, i, r: (b, 0, i, 0)))
        out_shape.append(jax.ShapeDtypeStruct((bsz, N_PAIRS, seq, LANES), _F32))
    scratch = [pltpu.VMEM((N_PAIRS, tq * dilation, LANES), _F32)] if dilation > 1 else []
    res = pl.pallas_call(
        functools.partial(_band_attn_kernel, tq=tq, dilation=dilation, kv_lane=kv_lane, has_sink=sinks is not None,
                          want_stat=want_stat),
        grid=(bsz, n // tq, dilation),
        in_specs=in_specs,
        out_specs=out_specs,
        out_shape=out_shape,
        scratch_shapes=scratch,
        compiler_params=_cparams(3),
        name=f"band_attn_d{dilation}",
    )(*args)
    o = res[0].reshape(bsz * seq, MIX_W)
    return (o, res[1]) if want_stat else o


def _cached_attn_stages(rows, pair0, refs, sink_ref, *, n_hist, t_new, q_pairs_of_kv, pair_heads, cache_row=None):
    q_ref, kn_ref, vn_ref, kc_ref, vc_ref, bias_ref, mult_ref, y_ref, ko_ref, vo_ref = refs
    left = _lane_is_left()
    new_lanes = lax.broadcasted_iota(jnp.int32, (1, LANES), 1) >= LANES - t_new
    top_rows = lax.broadcasted_iota(jnp.int32, (2 * t_new, 1), 0) < t_new
    zpad = jnp.zeros((LANES - t_new, LANES), _F32)
    mult = mult_ref[...]

    chains = []
    for b in rows:
        for kvp, q_pairs in enumerate(q_pairs_of_kv):
            heads = slice(2 * kvp, 2 * kvp + 2)
            lanes = slice(kvp * LANES, (kvp + 1) * LANES)
            ext = []
            for c_ref, n_ref, o_ref in ((kc_ref, kn_ref, ko_ref), (vc_ref, vn_ref, vo_ref)):
                old = c_ref[b if cache_row is None else cache_row, heads].reshape(LANES, n_hist)
                new = jnp.concatenate([zpad, n_ref[b, :, lanes]], axis=0).T
                rolled = pltpu.roll(old, n_hist - t_new, axis=1)
                tail = jnp.where(new_lanes, new, rolled[:, n_hist - LANES:])
                out = tail if n_hist == LANES else jnp.concatenate([rolled[:, :n_hist - LANES], tail], axis=1)
                o_ref[b, heads] = out.reshape(2, HEAD_DIM, n_hist)
                ext.append(jnp.concatenate([old.astype(_BF), new.astype(_BF)], axis=1))
            chains += [(b, p, ext[0], ext[1]) for p in q_pairs]

    scores = []
    for b, p, k_ext, _ in chains:
        q2 = _split_heads(q_ref[b, :, p * LANES:(p + 1) * LANES].astype(_BF))
        scores.append(jnp.dot(q2, k_ext, preferred_element_type=_F32) + bias_ref[pair0 + p])
    yield
    probs = []
    for s in scores:
        m = jnp.max(s, axis=1, keepdims=True)
        prob = (mult * jnp.exp(s - m)).astype(_BF)
        probs.append((m, prob, jnp.sum(prob.astype(_F32), axis=1, keepdims=True)))
    pvs = [lax.dot_general(prob, v_ext, (((1,), (1,)), ((), ())), preferred_element_type=_F32)
           for (_, prob, _), (_, _, _, v_ext) in zip(probs, chains)]
    yield
    for (b, p, _, _), (m, _, l), pv in zip(chains, probs, pvs):
        if sink_ref is not None:
            ha, hb = pair_heads[p]
            sink = jnp.where(top_rows, sink_ref[ha], sink_ref[hb])
            m2 = jnp.maximum(m, sink)
            a = jnp.exp(m - m2)
            o = pv * a / (l * a + jnp.exp(sink - m2))
        else:
            o = pv / l
        y_ref[b, :, p * LANES:(p + 1) * LANES] = jnp.where(left, o[:t_new], o[t_new:]).astype(y_ref.dtype)


def _sample_attn_kernel(*refs, has_sink, **statics):
    refs = list(refs)
    sink_ref = refs.pop(7) if has_sink else None
    for _ in _cached_attn_stages(range(refs[0].shape[0]), 0, refs, sink_ref, **statics):
        pass


def _sample_tables(pair_heads, branches, n_hist, t_new):
    slopes = _alibi_slopes(N_HEADS)
    key_pos = np.concatenate([np.arange(n_hist), n_hist + np.arange(LANES) - (LANES - t_new)])
    is_key = np.concatenate([np.ones(n_hist, bool), np.arange(LANES) >= LANES - t_new])
    delta = (n_hist + np.arange(t_new))[:, None] - key_pos[None, :]
    mult = np.zeros(delta.shape, np.float32)
    for window, dil in branches:
        mult += ((delta >= 0) & (delta % dil == 0) & (delta <= window) & is_key[None, :]).astype(np.float32)
    bias = np.empty((len(pair_heads), 2 * t_new, key_pos.size), np.float32)
    for p, heads in enumerate(pair_heads):
        for half, h in enumerate(heads):
            b = -np.float32(slopes[h]) * delta.astype(np.float32)
            bias[p, half * t_new:(half + 1) * t_new] = np.where(mult > 0, b, -np.inf)
    return jnp.asarray(bias), jnp.asarray(np.concatenate([mult, mult], axis=0))


UNIT_HEADS = 4


CACHE_SLOTS = 3


def _tail_cache_kernel(x_ref, ya_ref, yb_ref, wo_ref, bo_ref, g1_ref, b1_ref, w1_ref, w2_ref, g2_ref, b2_ref,
                       q_ref, kn_ref, vn_ref, kc_hbm, vc_hbm, bias_ref, mult_ref,
                       o_ref, y_ref, ko_ref, vo_ref, xb_ref, acc_ref, kbuf_ref, vbuf_ref, sem_ref,
                       *, units_per_row, **statics):
    c = pl.program_id(1)
    n_chunks = pl.num_programs(1)
    unit = pl.program_id(0) * n_chunks + c
    n_units = pl.num_programs(0) * n_chunks

    def fetch(u):
        slot = lax.rem(u, CACHE_SLOTS)
        heads = pl.ds(lax.rem(u, units_per_row) * UNIT_HEADS, UNIT_HEADS)
        return [pltpu.make_async_copy(hbm.at[u // units_per_row, heads], buf.at[slot], sem_ref.at[k, slot])
                for k, (hbm, buf) in enumerate(((kc_hbm, kbuf_ref), (vc_hbm, vbuf_ref)))]

    @pl.when(unit == 0)
    def _():
        for u in range(CACHE_SLOTS - 1):
            for cp in fetch(u):
                cp.start()

    @pl.when(unit + (CACHE_SLOTS - 1) < n_units)
    def _():
        for cp in fetch(unit + (CACHE_SLOTS - 1)):
            cp.start()

    for cp in fetch(unit):
        cp.wait()

    @pl.when(c == 0)
    def _():
        mix = jnp.dot(ya_ref[...], wo_ref[:MIX_W, :], preferred_element_type=_F32)
        mix = mix + jnp.dot(yb_ref[...], wo_ref[MIX_W:, :], preferred_element_type=_F32) + bo_ref[...]
        x1 = _layer_norm(DEEPNORM_ALPHA * x_ref[...] + mix, g1_ref[...], b1_ref[...])
        xb_ref[...] = x1.astype(_BF)
        acc_ref[...] = DEEPNORM_ALPHA * x1

    pair0 = (unit % units_per_row) * (UNIT_HEADS // 2)
    cache_refs = (q_ref, kn_ref, vn_ref, kbuf_ref, vbuf_ref, bias_ref, mult_ref, y_ref, ko_ref, vo_ref)
    stages = _cached_attn_stages((0,), pair0, cache_refs, None, cache_row=lax.rem(unit, CACHE_SLOTS), **statics)
    next(stages)
    h = jnp.dot(xb_ref[...], w1_ref[c], preferred_element_type=_F32)
    h = jnp.square(jnp.maximum(h, 0.0)).astype(_BF)
    next(stages)
    acc_ref[...] += jnp.dot(h, w2_ref[c], preferred_element_type=_F32)
    for _ in stages:
        pass

    @pl.when(c == n_chunks - 1)
    def _():
        o_ref[...] = _layer_norm(acc_ref[...], g2_ref[...], b2_ref[...])


def _even_tail_with_cache(x, ya, yb, tail_params, p3, q_col, k_col, v_col, pair_heads, branches, k_cache, v_cache,
                          tm, n_chunks):
    n, dm = x.shape
    bsz, t_new, _ = p3.shape
    _, n_hist, kvh, _ = k_cache.shape
    units_per_row = kvh // UNIT_HEADS
    unit_w = UNIT_HEADS * HEAD_DIM
    assert (n // tm) * n_chunks == bsz * units_per_row >= CACHE_SLOTS and kvh == N_HEADS
    wo, bo, g1, b1, w1, w2, g2, b2 = tail_params
    dh = w1.shape[1]
    w1c = jnp.transpose(w1.reshape(dm, n_chunks, dh // n_chunks), (1, 0, 2))
    w2c = w2.reshape(n_chunks, dh // n_chunks, dm)
    kt = jnp.transpose(k_cache, (0, 2, 3, 1))
    vt = jnp.transpose(v_cache, (0, 2, 3, 1))
    bias, mult = _sample_tables(pair_heads, branches, n_hist, t_new)

    row = lambda w: pl.BlockSpec((tm, w), lambda t, c: (t, 0))
    unit_of = lambda t, c: t * n_chunks + c
    new = lambda col: pl.BlockSpec(
        (1, t_new, unit_w),
        lambda t, c: (unit_of(t, c) // units_per_row, 0, col // unit_w + unit_of(t, c) % units_per_row))
    cache = pl.BlockSpec((1, UNIT_HEADS, HEAD_DIM, n_hist),
                         lambda t, c: (unit_of(t, c) // units_per_row, unit_of(t, c) % units_per_row, 0, 0))
    params = (wo, bo, g1, b1, w1c, w2c, g2, b2)
    out, y, ko, vo = pl.pallas_call(
        functools.partial(_tail_cache_kernel, units_per_row=units_per_row, n_hist=n_hist, t_new=t_new,
                          q_pairs_of_kv=tuple((p,) for p in range(UNIT_HEADS // 2)), pair_heads=pair_heads),
        grid=(n // tm, n_chunks),
        in_specs=[row(dm), row(MIX_W), row(MIX_W)] + [_resident_spec(p.shape) for p in params]
        + [new(q_col), new(k_col), new(v_col), pl.BlockSpec(memory_space=pl.ANY), pl.BlockSpec(memory_space=pl.ANY),
           _resident_spec(bias.shape), _resident_spec(mult.shape)],
        out_specs=[row(dm), new(0), cache, cache],
        out_shape=[jax.ShapeDtypeStruct((n, dm), _F32), jax.ShapeDtypeStruct((bsz, t_new, MIX_W), _BF),
                   jax.ShapeDtypeStruct(kt.shape, _F32), jax.ShapeDtypeStruct(vt.shape, _F32)],
        scratch_shapes=[pltpu.VMEM((tm, dm), _BF), pltpu.VMEM((tm, dm), _F32)]
        + [pltpu.VMEM((CACHE_SLOTS, UNIT_HEADS, HEAD_DIM, n_hist), _F32)] * 2
        + [pltpu.SemaphoreType.DMA((2, CACHE_SLOTS))],
        compiler_params=pltpu.CompilerParams(dimension_semantics=("arbitrary", "arbitrary"),
                                             vmem_limit_bytes=FUSED_VMEM_LIMIT),
        name="even_tail_with_cache",
    )(x, ya, yb, *params, p3, p3, p3, kt, vt, bias, mult)
    return out, y, jnp.transpose(ko, (0, 3, 1, 2)), jnp.transpose(vo, (0, 3, 1, 2))


def _sample_attn(p3, q_col, k_col, v_col, pair_heads, branches, k_cache, v_cache, sinks, bt):
    bsz, t_new, _ = p3.shape
    _, n_hist, kvh, _ = k_cache.shape
    kv_width = kvh * HEAD_DIM
    kt = jnp.transpose(k_cache, (0, 2, 3, 1))
    vt = jnp.transpose(v_cache, (0, 2, 3, 1))
    n_kv_pairs = kvh // 2
    q_pairs_of_kv = tuple(tuple(p for p in range(N_PAIRS) if p % n_kv_pairs == kvp) for kvp in range(n_kv_pairs))
    bias, mult = _sample_tables(pair_heads, branches, n_hist, t_new)
    new = lambda col, w: pl.BlockSpec((bt, t_new, w), lambda b: (b, 0, col // w))
    cache = pl.BlockSpec((bt, kvh, HEAD_DIM, n_hist), lambda b: (b, 0, 0, 0))
    in_specs = [new(q_col, MIX_W), new(k_col, kv_width), new(v_col, kv_width), cache, cache,
                _const_spec(bias.shape), _const_spec(mult.shape)]
    args = [p3, p3, p3, kt, vt, bias, mult]
    if sinks is not None:
        in_specs.append(pl.BlockSpec(memory_space=pltpu.SMEM))
        args.append(sinks)
    y, ko, vo = pl.pallas_call(
        functools.partial(_sample_attn_kernel, n_hist=n_hist, t_new=t_new, q_pairs_of_kv=q_pairs_of_kv,
                          pair_heads=pair_heads, has_sink=sinks is not None),
        grid=(bsz // bt,),
        in_specs=in_specs,
        out_specs=[pl.BlockSpec((bt, t_new, MIX_W), lambda b: (b, 0, 0)), cache, cache],
        out_shape=[jax.ShapeDtypeStruct((bsz, t_new, MIX_W), _BF),
                   jax.ShapeDtypeStruct(kt.shape, _F32), jax.ShapeDtypeStruct(vt.shape, _F32)],
        compiler_params=_cparams(1),
        name=f"sample_attn_{n_hist}",
    )(*args)
    return y, jnp.transpose(ko, (0, 3, 1, 2)), jnp.transpose(vo, (0, 3, 1, 2))


A_Q = N_HEADS * HEAD_DIM
A_KV = A_KV_HEADS * HEAD_DIM
E_Q, E_H, E_GB, E_GC, E_K, E_V = 0, 512, 1024, 1536, 2048, 2176
O_U, O_Q, O_K, O_V = 0, 512, 1024, 1536


def _prep_layer_weights(even_w_in, even_b_in, even_w_out, odd_w_in, odd_b_in, c_w_group):
    q_cols = np.concatenate([h * HEAD_DIM + np.arange(HEAD_DIM) for h in A_HEAD_ORDER])
    o1, o2, o3 = A_Q, A_Q + A_KV, A_Q + 2 * A_KV
    order = np.concatenate([q_cols, np.arange(o3, o3 + 3 * MIX_W), np.arange(o1, o3)])
    scale = np.ones((order.size,), np.float32)
    scale[:A_Q] = HEAD_DIM ** -0.5
    ew = (even_w_in[:, order] * scale).astype(_BF)
    eb = (even_b_in[order] * scale)[None, :]
    ewo = jnp.concatenate([even_w_out[q_cols], even_w_out[A_Q:]], axis=0).astype(_BF)
    oscale = np.ones((odd_w_in.shape[1],), np.float32)
    oscale[O_Q:O_K] = HEAD_DIM ** -0.5
    ow = (odd_w_in * oscale).astype(_BF)
    ob = (odd_b_in * oscale)[None, :]
    groups, gw, _ = c_w_group.shape
    wg = jnp.zeros((MIX_W, MIX_W), _F32)
    for g in range(groups):
        wg = wg.at[g * gw:(g + 1) * gw, g * gw:(g + 1) * gw].set(c_w_group[g])
    return ew, eb, ewo, ow, ob, wg.astype(_BF)


def _row(v):
    return v[None, :]


def _forward(xp, xs, caches, wts, tm, bt_attn, bt_shift, n_chunks):
    (even_w_in, even_b_in, a_sinks, b_conv_w, even_w_out, even_b_out, odd_w_in, odd_b_in, c_w_group, c_scale,
     odd_w_out, odd_b_out, mlp_w1, mlp_w2, ln1_g, ln1_b, ln2_g, ln2_b) = wts
    cache_a_k, cache_a_v, state_b_conv, state_c_pool, cache_d_k, cache_d_v = caches
    bsz, seq, dm = xp.shape
    bs, ts, _ = xs.shape
    n, ns = bsz * seq, bs * ts
    tms = min(tm, ns)
    ew, eb, ewo, ow, ob, wg = _prep_layer_weights(even_w_in[0], even_b_in[0], even_w_out[0], odd_w_in[0],
                                                  odd_b_in[0], c_w_group[0])
    owo = odd_w_out[0].astype(_BF)
    w1 = mlp_w1.astype(_BF)
    w2 = mlp_w2.astype(_BF)
    tails = [(wo, _row(bo), _row(ln1_g[i]), _row(ln1_b[i]), w1[i], w2[i], _row(ln2_g[i]), _row(ln2_b[i]))
             for i, (wo, bo) in enumerate(((ewo, even_b_out[0]), (owo, odd_b_out[0])))]
    a_kv_lane = (0,) * N_PAIRS
    d_kv_lane = tuple(p * LANES for p in range(N_PAIRS))

    xs = xs.reshape(ns, dm)
    pe3 = _proj(xs, ew, eb, _F32, tms).reshape(bs, ts, -1)
    ya, ak_s, av_s = _sample_attn(pe3, E_Q, E_K, E_V, A_PAIR_HEADS, ((A_WINDOW, 1),), cache_a_k[0], cache_a_v[0],
                                  a_sinks[0], bt_attn)
    c_hist = jnp.pad(state_b_conv[0], ((0, 0), (HALO - (CONV_WIDTH - 1), 0), (0, 0)))
    yb, ctail = _sample_conv(pe3, c_hist, b_conv_w[0], bt_shift)
    bc_s = ctail[:, -(CONV_WIDTH - 1):, :]
    xs = _even_tail(xs, ya.reshape(ns, MIX_W), yb.reshape(ns, MIX_W), tails[0], tms)
    po3s = _proj(xs, ow, ob, _F32, tms).reshape(bs, ts, -1)
    u_hist = jnp.pad(state_c_pool[0], ((0, 0), (HALO - (POOL_MAX - 1), 0), (0, 0)))
    pooled_s = _sample_pool(po3s, u_hist, bt_shift).reshape(ns, MIX_W)
    cp_s = jnp.concatenate([state_c_pool[0], po3s[:, :, O_U:O_Q]], axis=1)[:, -(POOL_MAX - 1):]

    x = xp.reshape(n, dm)
    qkv, yb, ctail = _proj_conv(xp, ew, eb, b_conv_w[0], tm)
    k_col, v_col = MIX_W, MIX_W + A_KV
    ya = _band_attn(qkv[:, None], 0, k_col, v_col, LANES, a_kv_lane, A_PAIR_HEADS, a_sinks[0], False)
    n_keep = min(A_WINDOW, seq)
    a_k = qkv[:, seq - n_keep:, k_col:v_col].astype(_F32).reshape(bsz, n_keep, A_KV_HEADS, HEAD_DIM)
    a_v = qkv[:, seq - n_keep:, v_col:].astype(_F32).reshape(bsz, n_keep, A_KV_HEADS, HEAD_DIM)
    b_conv = ctail[:, -(CONV_WIDTH - 1):, :]
    x, yd, dk_s, dv_s = _even_tail_with_cache(x, ya, yb.reshape(n, MIX_W), tails[0], po3s, O_Q, O_K, O_V,
                                              D_PAIR_HEADS, D_BRANCHES, cache_d_k[0], cache_d_v[0], tm, n_chunks)
    xs = _sample_odd_tail(xs, pooled_s, wg, _row(c_scale[0]), yd.reshape(ns, MIX_W), tails[1], tms)

    dils = tuple(d for _, d in D_BRANCHES if d > 1)
    qkv, qkv4, qkv16, pooled, utail = _proj_dilated(x.reshape(bsz, seq, dm), ow, ob, tm, O_Q, dils)
    k_col, v_col = O_K - O_Q, O_V - O_Q
    outs, stats = [], []
    for arr in (qkv[:, None], qkv4, qkv16):
        o, st = _band_attn(arr, 0, k_col, v_col, MIX_W, d_kv_lane, D_PAIR_HEADS, None, True)
        outs.append(o)
        stats.append(st)
    n_keep = min(D_BRANCHES[-1][0], seq)
    c_pool = utail[:, -(POOL_MAX - 1):, :]
    d_k = qkv[:, seq - n_keep:, k_col:v_col].astype(_F32).reshape(bsz, n_keep, N_HEADS, HEAD_DIM)
    d_v = qkv[:, seq - n_keep:, v_col:].astype(_F32).reshape(bsz, n_keep, N_HEADS, HEAD_DIM)
    x = _odd_tail(x, pooled.reshape(n, MIX_W), wg, _row(c_scale[0]), outs, stats, tails[1], tm)
    return (x.reshape(bsz, seq, dm), xs.reshape(bs, ts, dm), a_k[None], a_v[None], b_conv[None], c_pool[None],
            d_k[None], d_v[None], ak_s[None], av_s[None], bc_s[None], cp_s[None], dk_s[None], dv_s[None])


def kernel(x_prompt, x_sample, cache_a_k, cache_a_v, state_b_conv, state_c_pool, cache_d_k, cache_d_v, even_w_in, even_b_in, a_sinks, b_conv_w, even_w_out, even_b_out, odd_w_in, odd_b_in, c_w_group, c_scale, odd_w_out, odd_b_out, mlp_w1, mlp_w2, ln1_g, ln1_b, ln2_g, ln2_b):
    wts = (even_w_in, even_b_in, a_sinks, b_conv_w, even_w_out, even_b_out, odd_w_in, odd_b_in, c_w_group, c_scale,
           odd_w_out, odd_b_out, mlp_w1, mlp_w2, ln1_g, ln1_b, ln2_g, ln2_b)
    caches = (cache_a_k, cache_a_v, state_b_conv, state_c_pool, cache_d_k, cache_d_v)
    return _forward(x_prompt, x_sample, caches, wts, tm=512, bt_attn=8, bt_shift=32, n_chunks=4)
```

```python
import functools

import numpy as np
import jax
import jax.numpy as jnp
from jax import lax
from jax.experimental import pallas as pl
from jax.experimental.pallas import tpu as pltpu

HEAD_DIM = 64
N_HEADS = 8
A_KV_HEADS = 2
A_WINDOW = 128
D_BRANCHES = ((128, 1), (512, 4), (2048, 16))
CONV_WIDTH = 3
POOL_WINDOWS = (2, 4, 8, 16)
POOL_MAX = 16
DEPTH = 2
PAST_LEN = 16384
DEEPNORM_ALPHA = (2 * DEPTH) ** 0.25
LN_EPS = 1e-5

MIX_W = N_HEADS * HEAD_DIM
LANES = 128
N_PAIRS = MIX_W // LANES
BAND = 128
HALO = 16
EXT0 = 24
VMEM_LIMIT = 56 * 1024 * 1024
FUSED_VMEM_LIMIT = 62 * 1024 * 1024

A_HEAD_ORDER = (0, 4, 1, 5, 2, 6, 3, 7)
A_PAIR_HEADS = tuple((p, p + 4) for p in range(N_PAIRS))
D_PAIR_HEADS = tuple((2 * p, 2 * p + 1) for p in range(N_PAIRS))

_BF = jnp.bfloat16
_F32 = jnp.float32
_NEG_INF = float("-inf")


def _alibi_slopes(n_heads):
    return 2.0 ** (-8.0 * np.arange(1, n_heads + 1) / n_heads)


def _cparams(n_axes):
    return pltpu.CompilerParams(dimension_semantics=("arbitrary",) * n_axes, vmem_limit_bytes=VMEM_LIMIT)


def _const_spec(shape):
    nd = len(shape)
    return pl.BlockSpec(shape, lambda *_: (0,) * nd)


def _layer_norm(y, g, b):
    mu = jnp.mean(y, axis=-1, keepdims=True)
    yc = y - mu
    var = jnp.mean(yc * yc, axis=-1, keepdims=True)
    return yc * lax.rsqrt(var + LN_EPS) * g + b


def _proj_kernel(x_ref, w_ref, b_ref, o_ref, *, tn):
    x = x_ref[...].astype(_BF)
    for j in range(o_ref.shape[1] // tn):
        cols = slice(j * tn, (j + 1) * tn)
        acc = jnp.dot(x, w_ref[:, cols], preferred_element_type=_F32)
        o_ref[:, cols] = (acc + b_ref[:, cols]).astype(o_ref.dtype)


def _proj(x, w, b, out_dtype, tm):
    n, k = x.shape
    m = w.shape[1]
    return pl.pallas_call(
        functools.partial(_proj_kernel, tn=256),
        grid=(n // tm,),
        in_specs=[pl.BlockSpec((tm, k), lambda i: (i, 0)), _const_spec((k, m)), _const_spec((1, m))],
        out_specs=pl.BlockSpec((tm, m), lambda i: (i, 0)),
        out_shape=jax.ShapeDtypeStruct((n, m), out_dtype),
        compiler_params=_cparams(1),
        name="proj",
    )(x, w, b)


def _carried_history(ext_ref, tm):
    @pl.when(pl.program_id(1) == 0)
    def _():
        ext_ref[:, tm + EXT0 - HALO:tm + EXT0, :] = jnp.zeros((1, HALO, MIX_W), _F32)

    return ext_ref[:, tm + EXT0 - HALO:tm + EXT0, :]


def _proj_conv_kernel(x_ref, w_ref, b_ref, cw_ref, qkv_ref, yb_ref, ctail_ref, hg_ref, ext_ref, *, tn):
    tm = x_ref.shape[0]
    c_hist = _carried_history(ext_ref, tm)
    x = x_ref[...].astype(_BF)
    gates = slice(MIX_W, 4 * MIX_W)
    for j in range(w_ref.shape[1] // tn):
        lo = j * tn
        acc = jnp.dot(x, w_ref[:, lo:lo + tn], preferred_element_type=_F32) + b_ref[:, lo:lo + tn]
        if lo < gates.start:
            qkv_ref[:, lo:lo + tn] = acc.astype(qkv_ref.dtype)
        elif lo < gates.stop:
            hg_ref[:, lo - gates.start:lo - gates.start + tn] = acc
        else:
            qkv_ref[:, lo - 3 * MIX_W:lo - 3 * MIX_W + tn] = acc.astype(qkv_ref.dtype)
    h, gb, gc = (hg_ref[:, k * MIX_W:(k + 1) * MIX_W][None] for k in range(3))
    _conv_body(h, gb, gc, c_hist, cw_ref, ext_ref, yb_ref, ctail_ref)


def _proj_conv(x3, w, b, conv_w, tm):
    bsz, seq, k = x3.shape
    m = w.shape[1]
    qkv_w = m - 3 * MIX_W
    return pl.pallas_call(
        functools.partial(_proj_conv_kernel, tn=256),
        grid=(bsz, seq // tm),
        in_specs=[pl.BlockSpec((None, tm, k), lambda bi, i: (bi, i, 0)), _const_spec((k, m)), _const_spec((1, m)),
                  _const_spec(conv_w.shape)],
        out_specs=[pl.BlockSpec((None, tm, qkv_w), lambda bi, i: (bi, i, 0)),
                   pl.BlockSpec((1, tm, MIX_W), lambda bi, i: (bi, i, 0)),
                   pl.BlockSpec((1, 8, MIX_W), lambda bi, i: (bi, i, 0))],
        out_shape=[jax.ShapeDtypeStruct((bsz, seq, qkv_w), _BF), jax.ShapeDtypeStruct((bsz, seq, MIX_W), _BF),
                   jax.ShapeDtypeStruct((bsz, (seq // tm) * 8, MIX_W), _F32)],
        scratch_shapes=[pltpu.VMEM((tm, 3 * MIX_W), _F32), pltpu.VMEM((1, EXT0 + tm, MIX_W), _F32)],
        compiler_params=_cparams(2),
        name="proj_conv",
    )(x3, w, b, conv_w)


def _proj_dilated_kernel(x_ref, w_ref, b_ref, o_ref, d4_ref, d16_ref, pooled_ref, utail_ref,
                         stage_ref, stage4_ref, u_ref, ext_ref, s2_ref, s4_ref, s8_ref, *, tn, first_col):
    tm = x_ref.shape[0]
    u_hist = _carried_history(ext_ref, tm)
    x = x_ref[...].astype(_BF)
    for j in range(w_ref.shape[1] // tn):
        lo = j * tn
        acc = jnp.dot(x, w_ref[:, lo:lo + tn], preferred_element_type=_F32) + b_ref[:, lo:lo + tn]
        if lo < first_col:
            u_ref[:, lo:lo + tn] = acc
        else:
            o_ref[:, lo - first_col:lo - first_col + tn] = acc.astype(o_ref.dtype)
            for h in range(tn // LANES):
                stage_ref[(lo - first_col) // LANES + h] = acc[:, h * LANES:(h + 1) * LANES]
    _pool_body(u_ref[...][None], u_hist, pl.program_id(1) * tm, ext_ref, s2_ref, s4_ref, s8_ref, pooled_ref)
    utail_ref[...] = ext_ref[:, tm + EXT0 - HALO:tm + EXT0, :]
    q4, q16 = tm // 4, tm // 16
    for s in range(stage_ref.shape[0]):
        lanes = slice(s * LANES, (s + 1) * LANES)
        for r in range(4):
            rows = stage_ref[s, pl.ds(r, q4, stride=4), :]
            stage4_ref[s, r * q4:(r + 1) * q4, :] = rows
            d4_ref[r, :, lanes] = rows.astype(d4_ref.dtype)
        for r in range(4):
            for k in range(4):
                d16_ref[r + 4 * k, :, lanes] = stage4_ref[s, pl.ds(r * q4 + k, q16, stride=4), :].astype(d16_ref.dtype)


def _proj_dilated(x3, w, b, tm, first_col, dilations):
    bsz, seq, k = x3.shape
    m = w.shape[1]
    wd = m - first_col
    assert first_col == MIX_W and tuple(dilations) == (4, 16) and tm % 256 == 0
    out_specs = [pl.BlockSpec((None, tm, wd), lambda bi, i: (bi, i, 0))]
    out_shape = [jax.ShapeDtypeStruct((bsz, seq, wd), _BF)]
    for d in dilations:
        out_specs.append(pl.BlockSpec((None, d, tm // d, wd), lambda bi, i: (bi, 0, i, 0)))
        out_shape.append(jax.ShapeDtypeStruct((bsz, d, seq // d, wd), _BF))
    out_specs += [pl.BlockSpec((1, tm, MIX_W), lambda bi, i: (bi, i, 0)),
                  pl.BlockSpec((1, HALO, MIX_W), lambda bi, i: (bi, i, 0))]
    out_shape += [jax.ShapeDtypeStruct((bsz, seq, MIX_W), _BF),
                  jax.ShapeDtypeStruct((bsz, (seq // tm) * HALO, MIX_W), _F32)]
    shift = pltpu.VMEM((1, EXT0 + tm, MIX_W), _F32)
    return pl.pallas_call(
        functools.partial(_proj_dilated_kernel, tn=256, first_col=first_col),
        grid=(bsz, seq // tm),
        in_specs=[pl.BlockSpec((None, tm, k), lambda bi, i: (bi, i, 0)), _const_spec((k, m)), _const_spec((1, m))],
        out_specs=out_specs,
        out_shape=out_shape,
        scratch_shapes=[pltpu.VMEM((wd // LANES, tm, LANES), _F32)] * 2 + [pltpu.VMEM((tm, MIX_W), _F32)] + [shift] * 4,
        compiler_params=_cparams(2),
        name="proj_dilated",
    )(x3, w, b)


MLP_CHUNK = 512


def _lane_is_left():
    return lax.broadcasted_iota(jnp.int32, (1, LANES), 1) < HEAD_DIM


def _layer_tail(x_ref, left, right, tail_refs):
    wo_ref, bo_ref, g1_ref, b1_ref, w1_ref, w2_ref, g2_ref, b2_ref, o_ref = tail_refs
    mix = jnp.dot(left, wo_ref[:MIX_W, :], preferred_element_type=_F32)
    mix = mix + jnp.dot(right, wo_ref[MIX_W:, :], preferred_element_type=_F32) + bo_ref[...]
    x = _layer_norm(DEEPNORM_ALPHA * x_ref[...] + mix, g1_ref[...], b1_ref[...])
    xb = x.astype(_BF)
    acc = jnp.zeros(x.shape, _F32)
    for c in range(w1_ref.shape[1] // MLP_CHUNK):
        cols = slice(c * MLP_CHUNK, (c + 1) * MLP_CHUNK)
        h = jnp.dot(xb, w1_ref[:, cols], preferred_element_type=_F32)
        h = jnp.square(jnp.maximum(h, 0.0)).astype(_BF)
        acc = acc + jnp.dot(h, w2_ref[cols, :], preferred_element_type=_F32)
    o_ref[...] = _layer_norm(DEEPNORM_ALPHA * x + acc, g2_ref[...], b2_ref[...])


def _even_tail_kernel(x_ref, ya_ref, yb_ref, *tail_refs):
    _layer_tail(x_ref, ya_ref[...], yb_ref[...], tail_refs)


def _group_c(pooled_ref, wg_ref, scale_ref):
    return (jnp.dot(pooled_ref[...], wg_ref[...], preferred_element_type=_F32) * scale_ref[...]).astype(_BF)


def _sample_odd_tail_kernel(x_ref, pooled_ref, wg_ref, scale_ref, yd_ref, *tail_refs):
    _layer_tail(x_ref, _group_c(pooled_ref, wg_ref, scale_ref), yd_ref[...], tail_refs)


def _odd_tail_kernel(x_ref, pooled_ref, wg_ref, scale_ref, o1_ref, o2_ref, o3_ref, s1_ref, s2_ref, s3_ref,
                     *tail_refs):
    yc = _group_c(pooled_ref, wg_ref, scale_ref)
    tiles = []
    for p in range(N_PAIRS):
        lanes = slice(p * LANES, (p + 1) * LANES)
        lses = (s1_ref[p], s2_ref[p], s3_ref[p])
        top = jnp.maximum(jnp.maximum(lses[0], lses[1]), lses[2])
        es = [jnp.exp(s - top) for s in lses]
        num = es[0] * o1_ref[:, lanes].astype(_F32)
        num = num + es[1] * o2_ref[:, lanes].astype(_F32)
        num = num + es[2] * o3_ref[:, lanes].astype(_F32)
        tiles.append(num / (es[0] + es[1] + es[2]))
    yd = jnp.concatenate(tiles, axis=1)
    _layer_tail(x_ref, yc, yd.astype(_BF), tail_refs)


def _row_spec(tm, width):
    return pl.BlockSpec((tm, width), lambda i: (i, 0))


def _resident_spec(shape):
    nd = len(shape)
    return pl.BlockSpec(shape, lambda *_: (0,) * nd, pipeline_mode=pl.Buffered(1))


def _tail_call(kernel_fn, name, x, mixer_args, mixer_specs, tail_params, tm):
    n, dm = x.shape
    return pl.pallas_call(
        kernel_fn,
        grid=(n // tm,),
        in_specs=[_row_spec(tm, dm)] + mixer_specs + [_resident_spec(p.shape) for p in tail_params],
        out_specs=_row_spec(tm, dm),
        out_shape=jax.ShapeDtypeStruct((n, dm), _F32),
        compiler_params=_cparams(1),
        name=name,
    )(x, *mixer_args, *tail_params)


def _even_tail(x, ya, yb, tail_params, tm):
    return _tail_call(_even_tail_kernel, "even_tail", x, [ya, yb], [_row_spec(tm, MIX_W)] * 2, tail_params, tm)


def _odd_tail(x, pooled, wg, scale, outs, stats, tail_params, tm):
    tiles_per_seq = stats[0].shape[2] // tm
    stat_spec = pl.BlockSpec((None, N_PAIRS, tm, LANES), lambda i: (i // tiles_per_seq, 0, i % tiles_per_seq, 0))
    specs = ([_row_spec(tm, MIX_W), _resident_spec(wg.shape), _resident_spec(scale.shape)]
             + [_row_spec(tm, MIX_W)] * 3 + [stat_spec] * 3)
    return _tail_call(_odd_tail_kernel, "odd_tail", x, [pooled, wg, scale, *outs, *stats], specs, tail_params, tm)


def _sample_odd_tail(x, pooled, wg, scale, yd, tail_params, tm):
    specs = [_row_spec(tm, MIX_W), _resident_spec(wg.shape), _resident_spec(scale.shape), _row_spec(tm, MIX_W)]
    return _tail_call(_sample_odd_tail_kernel, "sample_odd_tail", x, [pooled, wg, scale, yd], specs, tail_params, tm)


def _fill_ext(ext_ref, hist, cur, t):
    nb = ext_ref.shape[0]
    ext_ref[:, 0:8, :] = jnp.zeros((nb, 8, MIX_W), _F32)
    ext_ref[:, 8:EXT0, :] = hist
    ext_ref[:, EXT0:EXT0 + t, :] = cur


def _conv_body(h, gb, gc, c_hist, w_ref, ext_ref, yb_ref, ctail_ref):
    t = h.shape[1]
    c = gc * h
    _fill_ext(ext_ref, c_hist, c, t)
    conv = ext_ref[:, EXT0 - 2:EXT0 - 2 + t, :] * w_ref[0:1, :]
    conv = conv + ext_ref[:, EXT0 - 1:EXT0 - 1 + t, :] * w_ref[1:2, :]
    conv = conv + c * w_ref[2:3, :]
    yb_ref[...] = (gb * conv).astype(yb_ref.dtype)
    ctail_ref[...] = ext_ref[:, EXT0 + t - 8:EXT0 + t, :]


def _sample_conv_kernel(h_ref, gb_ref, gc_ref, hist_ref, w_ref, yb_ref, ctail_ref, ext_ref):
    _conv_body(h_ref[...], gb_ref[...], gc_ref[...], hist_ref[...], w_ref, ext_ref, yb_ref, ctail_ref)


def _pool_body(u, hist, pos0, ext_ref, s2_ref, s4_ref, s8_ref, out_ref):
    nb, t, _ = u.shape
    _fill_ext(ext_ref, hist, u, t)
    hi = EXT0 + t
    zeros8 = jnp.zeros((nb, 8, MIX_W), _F32)
    s2_ref[:, 0:8, :] = zeros8
    s4_ref[:, 0:8, :] = zeros8
    s8_ref[:, 0:8, :] = zeros8
    s2_ref[:, 8:hi, :] = ext_ref[:, 8:hi, :] + ext_ref[:, 7:hi - 1, :]
    s4_ref[:, 8:hi, :] = s2_ref[:, 8:hi, :] + s2_ref[:, 6:hi - 2, :]
    s8_ref[:, 8:hi, :] = s4_ref[:, 8:hi, :] + s4_ref[:, 4:hi - 4, :]
    sums = (
        s2_ref[:, EXT0:hi, 0:LANES],
        s4_ref[:, EXT0:hi, LANES:2 * LANES],
        s8_ref[:, EXT0:hi, 2 * LANES:3 * LANES],
        s8_ref[:, EXT0:hi, 3 * LANES:] + s8_ref[:, EXT0 - 8:hi - 8, 3 * LANES:],
    )
    pos = (pos0 + lax.broadcasted_iota(jnp.int32, (1, t, LANES), 1) + 1).astype(_F32)
    tiles = []
    for g, (w, s) in enumerate(zip(POOL_WINDOWS, sums)):
        cnt = jnp.minimum(pos, float(w))
        tiles.append(s / cnt - u[:, :, g * LANES:(g + 1) * LANES])
    out_ref[...] = jnp.concatenate(tiles, axis=2).astype(out_ref.dtype)


def _sample_pool_kernel(u_ref, hist_ref, out_ref, ext_ref, s2_ref, s4_ref, s8_ref):
    _pool_body(u_ref[...], hist_ref[...], PAST_LEN, ext_ref, s2_ref, s4_ref, s8_ref, out_ref)


def _sample_conv(p3, hist, conv_w, bt):
    bsz, t, _ = p3.shape
    col = lambda c: pl.BlockSpec((bt, t, MIX_W), lambda b: (b, 0, c))
    return pl.pallas_call(
        _sample_conv_kernel,
        grid=(bsz // bt,),
        in_specs=[col(1), col(2), col(3), pl.BlockSpec((bt, HALO, MIX_W), lambda b: (b, 0, 0)),
                  _const_spec(conv_w.shape)],
        out_specs=[pl.BlockSpec((bt, t, MIX_W), lambda b: (b, 0, 0)),
                   pl.BlockSpec((bt, 8, MIX_W), lambda b: (b, 0, 0))],
        out_shape=[jax.ShapeDtypeStruct((bsz, t, MIX_W), _BF), jax.ShapeDtypeStruct((bsz, 8, MIX_W), _F32)],
        scratch_shapes=[pltpu.VMEM((bt, EXT0 + t, MIX_W), _F32)],
        compiler_params=_cparams(1),
        name="sample_conv",
    )(p3, p3, p3, hist, conv_w)


def _sample_pool(p3, hist, bt):
    bsz, t, _ = p3.shape
    scratch = pltpu.VMEM((bt, EXT0 + t, MIX_W), _F32)
    return pl.pallas_call(
        _sample_pool_kernel,
        grid=(bsz // bt,),
        in_specs=[pl.BlockSpec((bt, t, MIX_W), lambda b: (b, 0, 0)),
                  pl.BlockSpec((bt, HALO, MIX_W), lambda b: (b, 0, 0))],
        out_specs=pl.BlockSpec((bt, t, MIX_W), lambda b: (b, 0, 0)),
        out_shape=jax.ShapeDtypeStruct((bsz, t, MIX_W), _BF),
        scratch_shapes=[scratch] * 4,
        compiler_params=_cparams(1),
        name="sample_pool",
    )(p3, hist)


def _split_heads(q_pair):
    left = _lane_is_left()
    zero = jnp.zeros_like(q_pair)
    return jnp.concatenate([jnp.where(left, q_pair, zero), jnp.where(left, zero, q_pair)], axis=0)


def _band_attn_kernel(*refs, tq, dilation, kv_lane, has_sink, want_stat):
    refs = list(refs)
    q_ref, kc_ref, kp_ref, vc_ref, vp_ref, bias_ref = refs[:6]
    rest = refs[6:]
    sink_ref = rest.pop(0) if has_sink else None
    o_ref = rest.pop(0)
    st_ref = rest.pop(0) if want_stat else None
    stage_ref = rest.pop(0) if dilation > 1 else None

    first = pl.program_id(1) == 0
    n_classes = q_ref.shape[0]
    left = _lane_is_left()
    prev_cols = lax.broadcasted_iota(jnp.int32, (1, 2 * BAND), 1) < BAND
    top_rows = lax.broadcasted_iota(jnp.int32, (2 * BAND, 1), 0) < BAND
    ones = jnp.ones((2 * BAND, LANES), _BF)

    for cl, j in ((cl, j) for cl in range(n_classes) for j in range(tq // BAND)):
        res = pl.program_id(2) * n_classes + cl
        rows = slice(j * BAND, (j + 1) * BAND)
        out_rows = rows if dilation == 1 else pl.ds(j * BAND * dilation + res, BAND, stride=dilation)
        for p in range(N_PAIRS):
            kl = slice(kv_lane[p], kv_lane[p] + LANES)
            if j == 0:
                k_prev, v_prev = kp_ref[cl, :, kl], vp_ref[cl, :, kl]
            else:
                k_prev, v_prev = kc_ref[cl, (j - 1) * BAND:j * BAND, kl], vc_ref[cl, (j - 1) * BAND:j * BAND, kl]
            k2 = jnp.concatenate([k_prev, kc_ref[cl, rows, kl]], axis=0)
            v2 = jnp.concatenate([v_prev, vc_ref[cl, rows, kl]], axis=0)
            q2 = _split_heads(q_ref[cl, rows, p * LANES:(p + 1) * LANES])
            s = lax.dot_general(q2, k2, (((1,), (1,)), ((), ())), preferred_element_type=_F32)
            s = s + bias_ref[p]
            if j == 0:
                s = jnp.where(jnp.logical_and(first, prev_cols), _NEG_INF, s)
            m = jnp.max(s, axis=1, keepdims=True)
            prob = jnp.exp(s - m).astype(_BF)
            r = jnp.dot(prob, jnp.concatenate([v2, ones], axis=1), preferred_element_type=_F32)
            pv, l = r[:, :LANES], r[:, LANES:]
            if has_sink:
                ha, hb = A_PAIR_HEADS[p]
                sink = jnp.where(top_rows, sink_ref[ha], sink_ref[hb])
                m2 = jnp.maximum(m, sink)
                a = jnp.exp(m - m2)
                o = pv * a / (l * a + jnp.exp(sink - m2))
            else:
                o = pv / l
            o_pair = jnp.where(left, o[:BAND], o[BAND:])
            if dilation == 1:
                o_ref[rows, p * LANES:(p + 1) * LANES] = o_pair.astype(o_ref.dtype)
            else:
                stage_ref[p, out_rows, :] = o_pair
            if want_stat:
                lse = m + jnp.log(l)
                st_ref[p, out_rows, :] = jnp.where(left, lse[:BAND], lse[BAND:])

    if dilation > 1:
        @pl.when(pl.program_id(2) == pl.num_programs(2) - 1)
        def _():
            for p in range(N_PAIRS):
                o_ref[:, p * LANES:(p + 1) * LANES] = stage_ref[p].astype(o_ref.dtype)


def _band_bias(pair_heads, dilation):
    slopes = _alibi_slopes(N_HEADS)
    qi = np.arange(BAND)[:, None]
    kj = np.arange(2 * BAND)[None, :]
    dist = qi + BAND - kj
    valid = (dist >= 0) & (dist <= BAND)
    out = np.empty((len(pair_heads), 2 * BAND, 2 * BAND), np.float32)
    for p, heads in enumerate(pair_heads):
        for half, h in enumerate(heads):
            bias = -np.float32(slopes[h]) * (dist * dilation).astype(np.float32)
            out[p, half * BAND:(half + 1) * BAND] = np.where(valid, bias, -np.inf)
    return jnp.asarray(out)


BAND_TOKENS = 4096
BAND_STEP_ROWS = 1024


def _band_attn(arr, q_col, k_col, v_col, kv_width, kv_lane, pair_heads, sinks, want_stat):
    bsz, dilation, n, _ = arr.shape
    seq = n * dilation
    tq = min(BAND_STEP_ROWS, n, BAND_TOKENS // dilation)
    sub = tq // BAND
    n_classes = min(dilation, BAND_STEP_ROWS // tq)
    assert q_col % MIX_W == 0 and k_col % kv_width == 0 and v_col % kv_width == 0 and dilation % n_classes == 0

    def cur(col, w):
        return pl.BlockSpec((None, n_classes, tq, w), lambda b, i, r: (b, r, i, col // w))

    def prev(col, w):
        return pl.BlockSpec((None, n_classes, BAND, w),
                            lambda b, i, r: (b, r, jnp.maximum(i * sub - 1, 0), col // w))

    in_specs = [cur(q_col, MIX_W), cur(k_col, kv_width), prev(k_col, kv_width), cur(v_col, kv_width),
                prev(v_col, kv_width), _const_spec((N_PAIRS, 2 * BAND, 2 * BAND))]
    args = [arr, arr, arr, arr, arr, _band_bias(pair_heads, dilation)]
    if sinks is not None:
        in_specs.append(pl.BlockSpec(memory_space=pltpu.SMEM))
        args.append(sinks)
    out_specs = [pl.BlockSpec((None, tq * dilation, MIX_W), lambda b, i, r: (b, i, 0))]
    out_shape = [jax.ShapeDtypeStruct((bsz, seq, MIX_W), _BF)]
    if want_stat:
        out_specs.append(pl.BlockSpec((None, N_PAIRS, tq * dilation, LANES), lambda b, i, r: (b, 0, i, 0)))
        out_shape.append(jax.ShapeDtypeStruct((bsz, N_PAIRS, seq, LANES), _F32))
    scratch = [pltpu.VMEM((N_PAIRS, tq * dilation, LANES), _F32)] if dilation > 1 else []
    res = pl.pallas_call(
        functools.partial(_band_attn_kernel, tq=tq, dilation=dilation, kv_lane=kv_lane, has_sink=sinks is not None,
                          want_stat=want_stat),
        grid=(bsz, n // tq, dilation // n_classes),
        in_specs=in_specs,
        out_specs=out_specs,
        out_shape=out_shape,
        scratch_shapes=scratch,
        compiler_params=_cparams(3),
        name=f"band_attn_d{dilation}",
    )(*args)
    o = res[0].reshape(bsz * seq, MIX_W)
    return (o, res[1]) if want_stat else o


def _cached_attn_stages(rows, pair0, refs, sink_ref, *, n_hist, t_new, q_pairs_of_kv, pair_heads, cache_row=None):
    q_ref, kn_ref, vn_ref, kc_ref, vc_ref, bias_ref, mult_ref, y_ref, ko_ref, vo_ref = refs
    left = _lane_is_left()
    new_lanes = lax.broadcasted_iota(jnp.int32, (1, LANES), 1) >= LANES - t_new
    top_rows = lax.broadcasted_iota(jnp.int32, (2 * t_new, 1), 0) < t_new
    zpad = jnp.zeros((LANES - t_new, LANES), _F32)
    mult = mult_ref[...]

    chains = []
    for b in rows:
        for kvp, q_pairs in enumerate(q_pairs_of_kv):
            heads = slice(2 * kvp, 2 * kvp + 2)
            lanes = slice(kvp * LANES, (kvp + 1) * LANES)
            ext = []
            for c_ref, n_ref, o_ref in ((kc_ref, kn_ref, ko_ref), (vc_ref, vn_ref, vo_ref)):
                old = c_ref[b if cache_row is None else cache_row, heads].reshape(LANES, n_hist)
                new = jnp.concatenate([zpad, n_ref[b, :, lanes]], axis=0).T
                rolled = pltpu.roll(old, n_hist - t_new, axis=1)
                tail = jnp.where(new_lanes, new, rolled[:, n_hist - LANES:])
                out = tail if n_hist == LANES else jnp.concatenate([rolled[:, :n_hist - LANES], tail], axis=1)
                o_ref[b, heads] = out.reshape(2, HEAD_DIM, n_hist)
                ext.append(jnp.concatenate([old.astype(_BF), new.astype(_BF)], axis=1))
            chains += [(b, p, ext[0], ext[1]) for p in q_pairs]

    scores = []
    for b, p, k_ext, _ in chains:
        q2 = _split_heads(q_ref[b, :, p * LANES:(p + 1) * LANES].astype(_BF))
        scores.append(jnp.dot(q2, k_ext, preferred_element_type=_F32) + bias_ref[pair0 + p])
    yield
    probs = []
    for s in scores:
        m = jnp.max(s, axis=1, keepdims=True)
        prob = (mult * jnp.exp(s - m)).astype(_BF)
        probs.append((m, prob, jnp.sum(prob.astype(_F32), axis=1, keepdims=True)))
    pvs = [lax.dot_general(prob, v_ext, (((1,), (1,)), ((), ())), preferred_element_type=_F32)
           for (_, prob, _), (_, _, _, v_ext) in zip(probs, chains)]
    yield
    for (b, p, _, _), (m, _, l), pv in zip(chains, probs, pvs):
        if sink_ref is not None:
            ha, hb = pair_heads[p]
            sink = jnp.where(top_rows, sink_ref[ha], sink_ref[hb])
            m2 = jnp.maximum(m, sink)
            a = jnp.exp(m - m2)
            o = pv * a / (l * a + jnp.exp(sink - m2))
        else:
            o = pv / l
        y_ref[b, :, p * LANES:(p + 1) * LANES] = jnp.where(left, o[:t_new], o[t_new:]).astype(y_ref.dtype)


def _sample_attn_kernel(*refs, has_sink, **statics):
    refs = list(refs)
    sink_ref = refs.pop(7) if has_sink else None
    for _ in _cached_attn_stages(range(refs[0].shape[0]), 0, refs, sink_ref, **statics):
        pass


def _sample_tables(pair_heads, branches, n_hist, t_new):
    slopes = _alibi_slopes(N_HEADS)
    key_pos = np.concatenate([np.arange(n_hist), n_hist + np.arange(LANES) - (LANES - t_new)])
    is_key = np.concatenate([np.ones(n_hist, bool), np.arange(LANES) >= LANES - t_new])
    delta = (n_hist + np.arange(t_new))[:, None] - key_pos[None, :]
    mult = np.zeros(delta.shape, np.float32)
    for window, dil in branches:
        mult += ((delta >= 0) & (delta % dil == 0) & (delta <= window) & is_key[None, :]).astype(np.float32)
    bias = np.empty((len(pair_heads), 2 * t_new, key_pos.size), np.float32)
    for p, heads in enumerate(pair_heads):
        for half, h in enumerate(heads):
            b = -np.float32(slopes[h]) * delta.astype(np.float32)
            bias[p, half * t_new:(half + 1) * t_new] = np.where(mult > 0, b, -np.inf)
    return jnp.asarray(bias), jnp.asarray(np.concatenate([mult, mult], axis=0))


UNIT_HEADS = 4


CACHE_SLOTS = 3


def _tail_cache_kernel(x_ref, ya_ref, yb_ref, wo_ref, bo_ref, g1_ref, b1_ref, w1_ref, w2_ref, g2_ref, b2_ref,
                       q_ref, kn_ref, vn_ref, kc_hbm, vc_hbm, bias_ref, mult_ref,
                       o_ref, y_ref, ko_ref, vo_ref, xb_ref, acc_ref, kbuf_ref, vbuf_ref, sem_ref,
                       *, units_per_row, **statics):
    c = pl.program_id(1)
    n_chunks = pl.num_programs(1)
    unit = pl.program_id(0) * n_chunks + c
    n_units = pl.num_programs(0) * n_chunks

    def fetch(u):
        slot = lax.rem(u, CACHE_SLOTS)
        heads = pl.ds(lax.rem(u, units_per_row) * UNIT_HEADS, UNIT_HEADS)
        return [pltpu.make_async_copy(hbm.at[u // units_per_row, heads], buf.at[slot], sem_ref.at[k, slot])
                for k, (hbm, buf) in enumerate(((kc_hbm, kbuf_ref), (vc_hbm, vbuf_ref)))]

    @pl.when(unit == 0)
    def _():
        for u in range(CACHE_SLOTS - 1):
            for cp in fetch(u):
                cp.start()

    @pl.when(unit + (CACHE_SLOTS - 1) < n_units)
    def _():
        for cp in fetch(unit + (CACHE_SLOTS - 1)):
            cp.start()

    for cp in fetch(unit):
        cp.wait()

    @pl.when(c == 0)
    def _():
        mix = jnp.dot(ya_ref[...], wo_ref[:MIX_W, :], preferred_element_type=_F32)
        mix = mix + jnp.dot(yb_ref[...], wo_ref[MIX_W:, :], preferred_element_type=_F32) + bo_ref[...]
        x1 = _layer_norm(DEEPNORM_ALPHA * x_ref[...] + mix, g1_ref[...], b1_ref[...])
        xb_ref[...] = x1.astype(_BF)
        acc_ref[...] = DEEPNORM_ALPHA * x1

    pair0 = (unit % units_per_row) * (UNIT_HEADS // 2)
    cache_refs = (q_ref, kn_ref, vn_ref, kbuf_ref, vbuf_ref, bias_ref, mult_ref, y_ref, ko_ref, vo_ref)
    stages = _cached_attn_stages((0,), pair0, cache_refs, None, cache_row=lax.rem(unit, CACHE_SLOTS), **statics)
    next(stages)
    h = jnp.dot(xb_ref[...], w1_ref[c], preferred_element_type=_F32)
    h = jnp.square(jnp.maximum(h, 0.0)).astype(_BF)
    next(stages)
    acc_ref[...] += jnp.dot(h, w2_ref[c], preferred_element_type=_F32)
    for _ in stages:
        pass

    @pl.when(c == n_chunks - 1)
    def _():
        o_ref[...] = _layer_norm(acc_ref[...], g2_ref[...], b2_ref[...])


def _even_tail_with_cache(x, ya, yb, tail_params, p3, q_col, k_col, v_col, pair_heads, branches, k_cache, v_cache,
                          tm, n_chunks):
    n, dm = x.shape
    bsz, t_new, _ = p3.shape
    _, n_hist, kvh, _ = k_cache.shape
    units_per_row = kvh // UNIT_HEADS
    unit_w = UNIT_HEADS * HEAD_DIM
    assert (n // tm) * n_chunks == bsz * units_per_row >= CACHE_SLOTS and kvh == N_HEADS
    wo, bo, g1, b1, w1, w2, g2, b2 = tail_params
    dh = w1.shape[1]
    w1c = jnp.transpose(w1.reshape(dm, n_chunks, dh // n_chunks), (1, 0, 2)).astype(_BF)
    w2c = w2.reshape(n_chunks, dh // n_chunks, dm)
    kt = jnp.transpose(k_cache, (0, 2, 3, 1))
    vt = jnp.transpose(v_cache, (0, 2, 3, 1))
    bias, mult = _sample_tables(pair_heads, branches, n_hist, t_new)

    row = lambda w: pl.BlockSpec((tm, w), lambda t, c: (t, 0))
    unit_of = lambda t, c: t * n_chunks + c
    new = lambda col: pl.BlockSpec(
        (1, t_new, unit_w),
        lambda t, c: (unit_of(t, c) // units_per_row, 0, col // unit_w + unit_of(t, c) % units_per_row))
    cache = pl.BlockSpec((1, UNIT_HEADS, HEAD_DIM, n_hist),
                         lambda t, c: (unit_of(t, c) // units_per_row, unit_of(t, c) % units_per_row, 0, 0))
    params = (wo, bo, g1, b1, w1c, w2c, g2, b2)
    out, y, ko, vo = pl.pallas_call(
        functools.partial(_tail_cache_kernel, units_per_row=units_per_row, n_hist=n_hist, t_new=t_new,
                          q_pairs_of_kv=tuple((p,) for p in range(UNIT_HEADS // 2)), pair_heads=pair_heads),
        grid=(n // tm, n_chunks),
        in_specs=[row(dm), row(MIX_W), row(MIX_W)] + [_resident_spec(p.shape) for p in params]
        + [new(q_col), new(k_col), new(v_col), pl.BlockSpec(memory_space=pl.ANY), pl.BlockSpec(memory_space=pl.ANY),
           _resident_spec(bias.shape), _resident_spec(mult.shape)],
        out_specs=[row(dm), new(0), cache, cache],
        out_shape=[jax.ShapeDtypeStruct((n, dm), _F32), jax.ShapeDtypeStruct((bsz, t_new, MIX_W), _BF),
                   jax.ShapeDtypeStruct(kt.shape, _F32), jax.ShapeDtypeStruct(vt.shape, _F32)],
        scratch_shapes=[pltpu.VMEM((tm, dm), _BF), pltpu.VMEM((tm, dm), _F32)]
        + [pltpu.VMEM((CACHE_SLOTS, UNIT_HEADS, HEAD_DIM, n_hist), _F32)] * 2
        + [pltpu.SemaphoreType.DMA((2, CACHE_SLOTS))],
        compiler_params=pltpu.CompilerParams(dimension_semantics=("arbitrary", "arbitrary"),
                                             vmem_limit_bytes=FUSED_VMEM_LIMIT),
        name="even_tail_with_cache",
    )(x, ya, yb, *params, p3, p3, p3, kt, vt, bias, mult)
    return out, y, jnp.transpose(ko, (0, 3, 1, 2)), jnp.transpose(vo, (0, 3, 1, 2))


def _sample_attn(p3, q_col, k_col, v_col, pair_heads, branches, k_cache, v_cache, sinks, bt):
    bsz, t_new, _ = p3.shape
    _, n_hist, kvh, _ = k_cache.shape
    kv_width = kvh * HEAD_DIM
    kt = jnp.transpose(k_cache, (0, 2, 3, 1))
    vt = jnp.transpose(v_cache, (0, 2, 3, 1))
    n_kv_pairs = kvh // 2
    q_pairs_of_kv = tuple(tuple(p for p in range(N_PAIRS) if p % n_kv_pairs == kvp) for kvp in range(n_kv_pairs))
    bias, mult = _sample_tables(pair_heads, branches, n_hist, t_new)
    new = lambda col, w: pl.BlockSpec((bt, t_new, w), lambda b: (b, 0, col // w))
    cache = pl.BlockSpec((bt, kvh, HEAD_DIM, n_hist), lambda b: (b, 0, 0, 0))
    in_specs = [new(q_col, MIX_W), new(k_col, kv_width), new(v_col, kv_width), cache, cache,
                _const_spec(bias.shape), _const_spec(mult.shape)]
    args = [p3, p3, p3, kt, vt, bias, mult]
    if sinks is not None:
        in_specs.append(pl.BlockSpec(memory_space=pltpu.SMEM))
        args.append(sinks)
    y, ko, vo = pl.pallas_call(
        functools.partial(_sample_attn_kernel, n_hist=n_hist, t_new=t_new, q_pairs_of_kv=q_pairs_of_kv,
                          pair_heads=pair_heads, has_sink=sinks is not None),
        grid=(bsz // bt,),
        in_specs=in_specs,
        out_specs=[pl.BlockSpec((bt, t_new, MIX_W), lambda b: (b, 0, 0)), cache, cache],
        out_shape=[jax.ShapeDtypeStruct((bsz, t_new, MIX_W), _BF),
                   jax.ShapeDtypeStruct(kt.shape, _F32), jax.ShapeDtypeStruct(vt.shape, _F32)],
        compiler_params=_cparams(1),
        name=f"sample_attn_{n_hist}",
    )(*args)
    return y, jnp.transpose(ko, (0, 3, 1, 2)), jnp.transpose(vo, (0, 3, 1, 2))


A_Q = N_HEADS * HEAD_DIM
A_KV = A_KV_HEADS * HEAD_DIM
E_Q, E_H, E_GB, E_GC, E_K, E_V = 0, 512, 1024, 1536, 2048, 2176
O_U, O_Q, O_K, O_V = 0, 512, 1024, 1536


def _prep_layer_weights(even_w_in, even_b_in, even_w_out, odd_w_in, odd_b_in, c_w_group):
    q_cols = np.concatenate([h * HEAD_DIM + np.arange(HEAD_DIM) for h in A_HEAD_ORDER])
    o1, o2, o3 = A_Q, A_Q + A_KV, A_Q + 2 * A_KV
    order = np.concatenate([q_cols, np.arange(o3, o3 + 3 * MIX_W), np.arange(o1, o3)])
    scale = np.ones((order.size,), np.float32)
    scale[:A_Q] = HEAD_DIM ** -0.5
    ew = (even_w_in[:, order] * scale).astype(_BF)
    eb = (even_b_in[order] * scale)[None, :]
    ewo = jnp.concatenate([even_w_out[q_cols], even_w_out[A_Q:]], axis=0).astype(_BF)
    oscale = np.ones((odd_w_in.shape[1],), np.float32)
    oscale[O_Q:O_K] = HEAD_DIM ** -0.5
    ow = (odd_w_in * oscale).astype(_BF)
    ob = (odd_b_in * oscale)[None, :]
    groups, gw, _ = c_w_group.shape
    wg = jnp.zeros((MIX_W, MIX_W), _F32)
    for g in range(groups):
        wg = wg.at[g * gw:(g + 1) * gw, g * gw:(g + 1) * gw].set(c_w_group[g])
    return ew, eb, ewo, ow, ob, wg.astype(_BF)


def _row(v):
    return v[None, :]


def _forward(xp, xs, caches, wts, tm, bt_attn, bt_shift, n_chunks):
    (even_w_in, even_b_in, a_sinks, b_conv_w, even_w_out, even_b_out, odd_w_in, odd_b_in, c_w_group, c_scale,
     odd_w_out, odd_b_out, mlp_w1, mlp_w2, ln1_g, ln1_b, ln2_g, ln2_b) = wts
    cache_a_k, cache_a_v, state_b_conv, state_c_pool, cache_d_k, cache_d_v = caches
    bsz, seq, dm = xp.shape
    bs, ts, _ = xs.shape
    n, ns = bsz * seq, bs * ts
    tms = min(tm, ns)
    ew, eb, ewo, ow, ob, wg = _prep_layer_weights(even_w_in[0], even_b_in[0], even_w_out[0], odd_w_in[0],
                                                  odd_b_in[0], c_w_group[0])
    owo = odd_w_out[0].astype(_BF)
    w1 = mlp_w1.astype(_BF)
    w2 = mlp_w2.astype(_BF)
    tails = [(wo, _row(bo), _row(ln1_g[i]), _row(ln1_b[i]), w1[i], w2[i], _row(ln2_g[i]), _row(ln2_b[i]))
             for i, (wo, bo) in enumerate(((ewo, even_b_out[0]), (owo, odd_b_out[0])))]
    a_kv_lane = (0,) * N_PAIRS
    d_kv_lane = tuple(p * LANES for p in range(N_PAIRS))

    xs = xs.reshape(ns, dm)
    pe3 = _proj(xs, ew, eb, _F32, tms).reshape(bs, ts, -1)
    ya, ak_s, av_s = _sample_attn(pe3, E_Q, E_K, E_V, A_PAIR_HEADS, ((A_WINDOW, 1),), cache_a_k[0], cache_a_v[0],
                                  a_sinks[0], bt_attn)
    c_hist = jnp.pad(state_b_conv[0], ((0, 0), (HALO - (CONV_WIDTH - 1), 0), (0, 0)))
    yb, ctail = _sample_conv(pe3, c_hist, b_conv_w[0], bt_shift)
    bc_s = ctail[:, -(CONV_WIDTH - 1):, :]
    xs = _even_tail(xs, ya.reshape(ns, MIX_W), yb.reshape(ns, MIX_W), tails[0], tms)
    po3s = _proj(xs, ow, ob, _F32, tms).reshape(bs, ts, -1)
    u_hist = jnp.pad(state_c_pool[0], ((0, 0), (HALO - (POOL_MAX - 1), 0), (0, 0)))
    pooled_s = _sample_pool(po3s, u_hist, bt_shift).reshape(ns, MIX_W)
    cp_s = jnp.concatenate([state_c_pool[0], po3s[:, :, O_U:O_Q]], axis=1)[:, -(POOL_MAX - 1):]

    x = xp.reshape(n, dm)
    qkv, yb, ctail = _proj_conv(xp, ew, eb, b_conv_w[0], tm)
    k_col, v_col = MIX_W, MIX_W + A_KV
    ya = _band_attn(qkv[:, None], 0, k_col, v_col, LANES, a_kv_lane, A_PAIR_HEADS, a_sinks[0], False)
    n_keep = min(A_WINDOW, seq)
    a_k = qkv[:, seq - n_keep:, k_col:v_col].astype(_F32).reshape(bsz, n_keep, A_KV_HEADS, HEAD_DIM)
    a_v = qkv[:, seq - n_keep:, v_col:].astype(_F32).reshape(bsz, n_keep, A_KV_HEADS, HEAD_DIM)
    b_conv = ctail[:, -(CONV_WIDTH - 1):, :]
    fused_tail = tails[0][:4] + (mlp_w1[0],) + tails[0][5:]
    x, yd, dk_s, dv_s = _even_tail_with_cache(x, ya, yb.reshape(n, MIX_W), fused_tail, po3s, O_Q, O_K, O_V,
                                              D_PAIR_HEADS, D_BRANCHES, cache_d_k[0], cache_d_v[0], tm, n_chunks)
    xs = _sample_odd_tail(xs, pooled_s, wg, _row(c_scale[0]), yd.reshape(ns, MIX_W), tails[1], tms)

    dils = tuple(d for _, d in D_BRANCHES if d > 1)
    qkv, qkv4, qkv16, pooled, utail = _proj_dilated(x.reshape(bsz, seq, dm), ow, ob, tm, O_Q, dils)
    k_col, v_col = O_K - O_Q, O_V - O_Q
    outs, stats = [], []
    for arr in (qkv[:, None], qkv4, qkv16):
        o, st = _band_attn(arr, 0, k_col, v_col, MIX_W, d_kv_lane, D_PAIR_HEADS, None, True)
        outs.append(o)
        stats.append(st)
    n_keep = min(D_BRANCHES[-1][0], seq)
    c_pool = utail[:, -(POOL_MAX - 1):, :]
    d_k = qkv[:, seq - n_keep:, k_col:v_col].astype(_F32).reshape(bsz, n_keep, N_HEADS, HEAD_DIM)
    d_v = qkv[:, seq - n_keep:, v_col:].astype(_F32).reshape(bsz, n_keep, N_HEADS, HEAD_DIM)
    x = _odd_tail(x, pooled.reshape(n, MIX_W), wg, _row(c_scale[0]), outs, stats, tails[1], tm)
    return (x.reshape(bsz, seq, dm), xs.reshape(bs, ts, dm), a_k[None], a_v[None], b_conv[None], c_pool[None],
            d_k[None], d_v[None], ak_s[None], av_s[None], bc_s[None], cp_s[None], dk_s[None], dv_s[None])


def kernel(x_prompt, x_sample, cache_a_k, cache_a_v, state_b_conv, state_c_pool, cache_d_k, cache_d_v, even_w_in, even_b_in, a_sinks, b_conv_w, even_w_out, even_b_out, odd_w_in, odd_b_in, c_w_group, c_scale, odd_w_out, odd_b_out, mlp_w1, mlp_w2, ln1_g, ln1_b, ln2_g, ln2_b):
    wts = (even_w_in, even_b_in, a_sinks, b_conv_w, even_w_out, even_b_out, odd_w_in, odd_b_in, c_w_group, c_scale,
           odd_w_out, odd_b_out, mlp_w1, mlp_w2, ln1_g, ln1_b, ln2_g, ln2_b)
    caches = (cache_a_k, cache_a_v, state_b_conv, state_c_pool, cache_d_k, cache_d_v)
    return _forward(x_prompt, x_sample, caches, wts, tm=512, bt_attn=8, bt_shift=32, n_chunks=4)
```

```python
import functools

import numpy as np
import jax
import jax.numpy as jnp
from jax import lax
from jax.experimental import pallas as pl
from jax.experimental.pallas import tpu as pltpu

HEAD_DIM = 64
N_HEADS = 8
A_KV_HEADS = 2
A_WINDOW = 128
D_BRANCHES = ((128, 1), (512, 4), (2048, 16))
CONV_WIDTH = 3
POOL_WINDOWS = (2, 4, 8, 16)
POOL_MAX = 16
DEPTH = 2
PAST_LEN = 16384
DEEPNORM_ALPHA = (2 * DEPTH) ** 0.25
LN_EPS = 1e-5

MIX_W = N_HEADS * HEAD_DIM
LANES = 128
N_PAIRS = MIX_W // LANES
BAND = 128
HALO = 16
EXT0 = 24
VMEM_LIMIT = 56 * 1024 * 1024
FUSED_VMEM_LIMIT = 62 * 1024 * 1024

A_HEAD_ORDER = (0, 4, 1, 5, 2, 6, 3, 7)
A_PAIR_HEADS = tuple((p, p + 4) for p in range(N_PAIRS))
D_PAIR_HEADS = tuple((2 * p, 2 * p + 1) for p in range(N_PAIRS))

_BF = jnp.bfloat16
_F32 = jnp.float32
_NEG_INF = float("-inf")


def _alibi_slopes(n_heads):
    return 2.0 ** (-8.0 * np.arange(1, n_heads + 1) / n_heads)


def _cparams(n_axes):
    return pltpu.CompilerParams(dimension_semantics=("arbitrary",) * n_axes, vmem_limit_bytes=VMEM_LIMIT)


def _const_spec(shape):
    nd = len(shape)
    return pl.BlockSpec(shape, lambda *_: (0,) * nd)


def _layer_norm(y, g, b):
    mu = jnp.mean(y, axis=-1, keepdims=True)
    yc = y - mu
    var = jnp.mean(yc * yc, axis=-1, keepdims=True)
    return yc * lax.rsqrt(var + LN_EPS) * g + b


def _proj_kernel(x_ref, w_ref, b_ref, o_ref, *, tn):
    x = x_ref[...].astype(_BF)
    for j in range(o_ref.shape[1] // tn):
        cols = slice(j * tn, (j + 1) * tn)
        acc = jnp.dot(x, w_ref[:, cols], preferred_element_type=_F32)
        o_ref[:, cols] = (acc + b_ref[:, cols]).astype(o_ref.dtype)


def _proj(x, w, b, out_dtype, tm):
    n, k = x.shape
    m = w.shape[1]
    return pl.pallas_call(
        functools.partial(_proj_kernel, tn=256),
        grid=(n // tm,),
        in_specs=[pl.BlockSpec((tm, k), lambda i: (i, 0)), _const_spec((k, m)), _const_spec((1, m))],
        out_specs=pl.BlockSpec((tm, m), lambda i: (i, 0)),
        out_shape=jax.ShapeDtypeStruct((n, m), out_dtype),
        compiler_params=_cparams(1),
        name="proj",
    )(x, w, b)


def _carried_history(ext_ref, tm):
    @pl.when(pl.program_id(1) == 0)
    def _():
        ext_ref[:, tm + EXT0 - HALO:tm + EXT0, :] = jnp.zeros((1, HALO, MIX_W), _F32)

    return ext_ref[:, tm + EXT0 - HALO:tm + EXT0, :]


def _proj_conv_kernel(x_ref, w_ref, b_ref, cw_ref, qkv_ref, yb_ref, ctail_ref, hg_ref, ext_ref, *, tn):
    tm = x_ref.shape[0]
    c_hist = _carried_history(ext_ref, tm)
    x = x_ref[...].astype(_BF)
    gates = slice(MIX_W, 4 * MIX_W)
    for j in range(w_ref.shape[1] // tn):
        lo = j * tn
        acc = jnp.dot(x, w_ref[:, lo:lo + tn], preferred_element_type=_F32) + b_ref[:, lo:lo + tn]
        if lo < gates.start:
            qkv_ref[:, lo:lo + tn] = acc.astype(qkv_ref.dtype)
        elif lo < gates.stop:
            hg_ref[:, lo - gates.start:lo - gates.start + tn] = acc
        else:
            qkv_ref[:, lo - 3 * MIX_W:lo - 3 * MIX_W + tn] = acc.astype(qkv_ref.dtype)
    h, gb, gc = (hg_ref[:, k * MIX_W:(k + 1) * MIX_W][None] for k in range(3))
    _conv_body(h, gb, gc, c_hist, cw_ref, ext_ref, yb_ref, ctail_ref)


def _proj_conv(x3, w, b, conv_w, tm):
    bsz, seq, k = x3.shape
    m = w.shape[1]
    qkv_w = m - 3 * MIX_W
    return pl.pallas_call(
        functools.partial(_proj_conv_kernel, tn=256),
        grid=(bsz, seq // tm),
        in_specs=[pl.BlockSpec((None, tm, k), lambda bi, i: (bi, i, 0)), _const_spec((k, m)), _const_spec((1, m)),
                  _const_spec(conv_w.shape)],
        out_specs=[pl.BlockSpec((None, tm, qkv_w), lambda bi, i: (bi, i, 0)),
                   pl.BlockSpec((1, tm, MIX_W), lambda bi, i: (bi, i, 0)),
                   pl.BlockSpec((1, 8, MIX_W), lambda bi, i: (bi, i, 0))],
        out_shape=[jax.ShapeDtypeStruct((bsz, seq, qkv_w), _BF), jax.ShapeDtypeStruct((bsz, seq, MIX_W), _BF),
                   jax.ShapeDtypeStruct((bsz, (seq // tm) * 8, MIX_W), _F32)],
        scratch_shapes=[pltpu.VMEM((tm, 3 * MIX_W), _F32), pltpu.VMEM((1, EXT0 + tm, MIX_W), _F32)],
        compiler_params=_cparams(2),
        name="proj_conv",
    )(x3, w, b, conv_w)


def _proj_dilated_kernel(x_ref, w_ref, b_ref, o_ref, d4_ref, d16_ref, pooled_ref, utail_ref,
                         stage_ref, stage4_ref, u_ref, ext_ref, s2_ref, s4_ref, s8_ref, *, tn, first_col):
    tm = x_ref.shape[0]
    u_hist = _carried_history(ext_ref, tm)
    x = x_ref[...].astype(_BF)
    for j in range(w_ref.shape[1] // tn):
        lo = j * tn
        acc = jnp.dot(x, w_ref[:, lo:lo + tn], preferred_element_type=_F32) + b_ref[:, lo:lo + tn]
        if lo < first_col:
            u_ref[:, lo:lo + tn] = acc
        else:
            o_ref[:, lo - first_col:lo - first_col + tn] = acc.astype(o_ref.dtype)
            for h in range(tn // LANES):
                stage_ref[(lo - first_col) // LANES + h] = acc[:, h * LANES:(h + 1) * LANES]
    _pool_body(u_ref[...][None], u_hist, pl.program_id(1) * tm, ext_ref, s2_ref, s4_ref, s8_ref, pooled_ref)
    utail_ref[...] = ext_ref[:, tm + EXT0 - HALO:tm + EXT0, :]
    q4, q16 = tm // 4, tm // 16
    for s in range(stage_ref.shape[0]):
        lanes = slice(s * LANES, (s + 1) * LANES)
        for r in range(4):
            rows = stage_ref[s, pl.ds(r, q4, stride=4), :]
            stage4_ref[s, r * q4:(r + 1) * q4, :] = rows
            d4_ref[r, :, lanes] = rows.astype(d4_ref.dtype)
        for r in range(4):
            for k in range(4):
                d16_ref[r + 4 * k, :, lanes] = stage4_ref[s, pl.ds(r * q4 + k, q16, stride=4), :].astype(d16_ref.dtype)


def _proj_dilated(x3, w, b, tm, first_col, dilations):
    bsz, seq, k = x3.shape
    m = w.shape[1]
    wd = m - first_col
    assert first_col == MIX_W and tuple(dilations) == (4, 16) and tm % 256 == 0
    out_specs = [pl.BlockSpec((None, tm, wd), lambda bi, i: (bi, i, 0))]
    out_shape = [jax.ShapeDtypeStruct((bsz, seq, wd), _BF)]
    for d in dilations:
        out_specs.append(pl.BlockSpec((None, d, tm // d, wd), lambda bi, i: (bi, 0, i, 0)))
        out_shape.append(jax.ShapeDtypeStruct((bsz, d, seq // d, wd), _BF))
    out_specs += [pl.BlockSpec((1, tm, MIX_W), lambda bi, i: (bi, i, 0)),
                  pl.BlockSpec((1, HALO, MIX_W), lambda bi, i: (bi, i, 0))]
    out_shape += [jax.ShapeDtypeStruct((bsz, seq, MIX_W), _BF),
                  jax.ShapeDtypeStruct((bsz, (seq // tm) * HALO, MIX_W), _F32)]
    shift = pltpu.VMEM((1, EXT0 + tm, MIX_W), _F32)
    return pl.pallas_call(
        functools.partial(_proj_dilated_kernel, tn=256, first_col=first_col),
        grid=(bsz, seq // tm),
        in_specs=[pl.BlockSpec((None, tm, k), lambda bi, i: (bi, i, 0)), _const_spec((k, m)), _const_spec((1, m))],
        out_specs=out_specs,
        out_shape=out_shape,
        scratch_shapes=[pltpu.VMEM((wd // LANES, tm, LANES), _F32)] * 2 + [pltpu.VMEM((tm, MIX_W), _F32)] + [shift] * 4,
        compiler_params=_cparams(2),
        name="proj_dilated",
    )(x3, w, b)


MLP_CHUNK = 512


def _lane_is_left():
    return lax.broadcasted_iota(jnp.int32, (1, LANES), 1) < HEAD_DIM


def _layer_tail(x_ref, left, right, tail_refs):
    wo_ref, bo_ref, g1_ref, b1_ref, w1_ref, w2_ref, g2_ref, b2_ref, o_ref = tail_refs
    mix = jnp.dot(left, wo_ref[:MIX_W, :], preferred_element_type=_F32)
    mix = mix + jnp.dot(right, wo_ref[MIX_W:, :], preferred_element_type=_F32) + bo_ref[...]
    x = _layer_norm(DEEPNORM_ALPHA * x_ref[...] + mix, g1_ref[...], b1_ref[...])
    xb = x.astype(_BF)
    acc = jnp.zeros(x.shape, _F32)
    for c in range(w1_ref.shape[1] // MLP_CHUNK):
        cols = slice(c * MLP_CHUNK, (c + 1) * MLP_CHUNK)
        h = jnp.dot(xb, w1_ref[:, cols], preferred_element_type=_F32)
        h = jnp.square(jnp.maximum(h, 0.0)).astype(_BF)
        acc = acc + jnp.dot(h, w2_ref[cols, :], preferred_element_type=_F32)
    o_ref[...] = _layer_norm(DEEPNORM_ALPHA * x + acc, g2_ref[...], b2_ref[...])


def _even_tail_kernel(x_ref, ya_ref, yb_ref, *tail_refs):
    _layer_tail(x_ref, ya_ref[...], yb_ref[...], tail_refs)


def _group_c(pooled_ref, wg_ref, scale_ref):
    return (jnp.dot(pooled_ref[...], wg_ref[...], preferred_element_type=_F32) * scale_ref[...]).astype(_BF)


def _sample_odd_tail_kernel(x_ref, pooled_ref, wg_ref, scale_ref, yd_ref, *tail_refs):
    _layer_tail(x_ref, _group_c(pooled_ref, wg_ref, scale_ref), yd_ref[...], tail_refs)


def _odd_tail_kernel(x_ref, pooled_ref, wg_ref, scale_ref, o1_ref, o2_ref, o3_ref, s1_ref, s2_ref, s3_ref,
                     *tail_refs):
    yc = _group_c(pooled_ref, wg_ref, scale_ref)
    tiles = []
    for p in range(N_PAIRS):
        lanes = slice(p * LANES, (p + 1) * LANES)
        lses = (s1_ref[p], s2_ref[p], s3_ref[p])
        top = jnp.maximum(jnp.maximum(lses[0], lses[1]), lses[2])
        es = [jnp.exp(s - top) for s in lses]
        num = es[0] * o1_ref[:, lanes].astype(_F32)
        num = num + es[1] * o2_ref[:, lanes].astype(_F32)
        num = num + es[2] * o3_ref[:, lanes].astype(_F32)
        tiles.append(num / (es[0] + es[1] + es[2]))
    yd = jnp.concatenate(tiles, axis=1)
    _layer_tail(x_ref, yc, yd.astype(_BF), tail_refs)


def _row_spec(tm, width):
    return pl.BlockSpec((tm, width), lambda i: (i, 0))


def _resident_spec(shape):
    nd = len(shape)
    return pl.BlockSpec(shape, lambda *_: (0,) * nd, pipeline_mode=pl.Buffered(1))


def _tail_call(kernel_fn, name, x, mixer_args, mixer_specs, tail_params, tm):
    n, dm = x.shape
    return pl.pallas_call(
        kernel_fn,
        grid=(n // tm,),
        in_specs=[_row_spec(tm, dm)] + mixer_specs + [_resident_spec(p.shape) for p in tail_params],
        out_specs=_row_spec(tm, dm),
        out_shape=jax.ShapeDtypeStruct((n, dm), _F32),
        compiler_params=_cparams(1),
        name=name,
    )(x, *mixer_args, *tail_params)


def _even_tail(x, ya, yb, tail_params, tm):
    return _tail_call(_even_tail_kernel, "even_tail", x, [ya, yb], [_row_spec(tm, MIX_W)] * 2, tail_params, tm)


def _odd_tail(x, pooled, wg, scale, outs, stats, tail_params, tm):
    tiles_per_seq = stats[0].shape[2] // tm
    stat_spec = pl.BlockSpec((None, N_PAIRS, tm, LANES), lambda i: (i // tiles_per_seq, 0, i % tiles_per_seq, 0))
    specs = ([_row_spec(tm, MIX_W), _resident_spec(wg.shape), _resident_spec(scale.shape)]
             + [_row_spec(tm, MIX_W)] * 3 + [stat_spec] * 3)
    return _tail_call(_odd_tail_kernel, "odd_tail", x, [pooled, wg, scale, *outs, *stats], specs, tail_params, tm)


def _sample_odd_tail(x, pooled, wg, scale, yd, tail_params, tm):
    specs = [_row_spec(tm, MIX_W), _resident_spec(wg.shape), _resident_spec(scale.shape), _row_spec(tm, MIX_W)]
    return _tail_call(_sample_odd_tail_kernel, "sample_odd_tail", x, [pooled, wg, scale, yd], specs, tail_params, tm)


def _fill_ext(ext_ref, hist, cur, t):
    nb = ext_ref.shape[0]
    ext_ref[:, 0:8, :] = jnp.zeros((nb, 8, MIX_W), _F32)
    ext_ref[:, 8:EXT0, :] = hist
    ext_ref[:, EXT0:EXT0 + t, :] = cur


def _conv_body(h, gb, gc, c_hist, w_ref, ext_ref, yb_ref, ctail_ref):
    t = h.shape[1]
    c = gc * h
    _fill_ext(ext_ref, c_hist, c, t)
    conv = ext_ref[:, EXT0 - 2:EXT0 - 2 + t, :] * w_ref[0:1, :]
    conv = conv + ext_ref[:, EXT0 - 1:EXT0 - 1 + t, :] * w_ref[1:2, :]
    conv = conv + c * w_ref[2:3, :]
    yb_ref[...] = (gb * conv).astype(yb_ref.dtype)
    ctail_ref[...] = ext_ref[:, EXT0 + t - 8:EXT0 + t, :]


def _sample_conv_kernel(h_ref, gb_ref, gc_ref, hist_ref, w_ref, yb_ref, ctail_ref, ext_ref):
    _conv_body(h_ref[...], gb_ref[...], gc_ref[...], hist_ref[...], w_ref, ext_ref, yb_ref, ctail_ref)


def _pool_body(u, hist, pos0, ext_ref, s2_ref, s4_ref, s8_ref, out_ref):
    nb, t, _ = u.shape
    _fill_ext(ext_ref, hist, u, t)
    hi = EXT0 + t
    zeros8 = jnp.zeros((nb, 8, MIX_W), _F32)
    s2_ref[:, 0:8, :] = zeros8
    s4_ref[:, 0:8, :] = zeros8
    s8_ref[:, 0:8, :] = zeros8
    s2_ref[:, 8:hi, :] = ext_ref[:, 8:hi, :] + ext_ref[:, 7:hi - 1, :]
    s4_ref[:, 8:hi, :] = s2_ref[:, 8:hi, :] + s2_ref[:, 6:hi - 2, :]
    s8_ref[:, 8:hi, :] = s4_ref[:, 8:hi, :] + s4_ref[:, 4:hi - 4, :]
    sums = (
        s2_ref[:, EXT0:hi, 0:LANES],
        s4_ref[:, EXT0:hi, LANES:2 * LANES],
        s8_ref[:, EXT0:hi, 2 * LANES:3 * LANES],
        s8_ref[:, EXT0:hi, 3 * LANES:] + s8_ref[:, EXT0 - 8:hi - 8, 3 * LANES:],
    )
    pos = (pos0 + lax.broadcasted_iota(jnp.int32, (1, t, LANES), 1) + 1).astype(_F32)
    tiles = []
    for g, (w, s) in enumerate(zip(POOL_WINDOWS, sums)):
        cnt = jnp.minimum(pos, float(w))
        tiles.append(s / cnt - u[:, :, g * LANES:(g + 1) * LANES])
    out_ref[...] = jnp.concatenate(tiles, axis=2).astype(out_ref.dtype)


def _sample_pool_kernel(u_ref, hist_ref, out_ref, ext_ref, s2_ref, s4_ref, s8_ref):
    _pool_body(u_ref[...], hist_ref[...], PAST_LEN, ext_ref, s2_ref, s4_ref, s8_ref, out_ref)


def _sample_conv(p3, hist, conv_w, bt):
    bsz, t, _ = p3.shape
    col = lambda c: pl.BlockSpec((bt, t, MIX_W), lambda b: (b, 0, c))
    return pl.pallas_call(
        _sample_conv_kernel,
        grid=(bsz // bt,),
        in_specs=[col(1), col(2), col(3), pl.BlockSpec((bt, HALO, MIX_W), lambda b: (b, 0, 0)),
                  _const_spec(conv_w.shape)],
        out_specs=[pl.BlockSpec((bt, t, MIX_W), lambda b: (b, 0, 0)),
                   pl.BlockSpec((bt, 8, MIX_W), lambda b: (b, 0, 0))],
        out_shape=[jax.ShapeDtypeStruct((bsz, t, MIX_W), _BF), jax.ShapeDtypeStruct((bsz, 8, MIX_W), _F32)],
        scratch_shapes=[pltpu.VMEM((bt, EXT0 + t, MIX_W), _F32)],
        compiler_params=_cparams(1),
        name="sample_conv",
    )(p3, p3, p3, hist, conv_w)


def _sample_pool(p3, hist, bt):
    bsz, t, _ = p3.shape
    scratch = pltpu.VMEM((bt, EXT0 + t, MIX_W), _F32)
    return pl.pallas_call(
        _sample_pool_kernel,
        grid=(bsz // bt,),
        in_specs=[pl.BlockSpec((bt, t, MIX_W), lambda b: (b, 0, 0)),
                  pl.BlockSpec((bt, HALO, MIX_W), lambda b: (b, 0, 0))],
        out_specs=pl.BlockSpec((bt, t, MIX_W), lambda b: (b, 0, 0)),
        out_shape=jax.ShapeDtypeStruct((bsz, t, MIX_W), _BF),
        scratch_shapes=[scratch] * 4,
        compiler_params=_cparams(1),
        name="sample_pool",
    )(p3, hist)


def _split_heads(q_pair):
    left = _lane_is_left()
    zero = jnp.zeros_like(q_pair)
    return jnp.concatenate([jnp.where(left, q_pair, zero), jnp.where(left, zero, q_pair)], axis=0)


def _band_attn_kernel(*refs, tq, dilation, kv_lane, has_sink, want_stat):
    refs = list(refs)
    q_ref, kc_ref, kp_ref, vc_ref, vp_ref, bias_ref = refs[:6]
    rest = refs[6:]
    sink_ref = rest.pop(0) if has_sink else None
    o_ref = rest.pop(0)
    st_ref = rest.pop(0) if want_stat else None
    stage_ref = rest.pop(0) if dilation > 1 else None

    first = pl.program_id(1) == 0
    n_classes = q_ref.shape[0]
    left = _lane_is_left()
    prev_cols = lax.broadcasted_iota(jnp.int32, (1, 2 * BAND), 1) < BAND
    top_rows = lax.broadcasted_iota(jnp.int32, (2 * BAND, 1), 0) < BAND
    ones = jnp.ones((2 * BAND, LANES), _BF)

    for cl, j in ((cl, j) for cl in range(n_classes) for j in range(tq // BAND)):
        res = pl.program_id(2) * n_classes + cl
        rows = slice(j * BAND, (j + 1) * BAND)
        out_rows = rows if dilation == 1 else pl.ds(j * BAND * dilation + res, BAND, stride=dilation)
        for p in range(N_PAIRS):
            kl = slice(kv_lane[p], kv_lane[p] + LANES)
            if j == 0:
                k_prev, v_prev = kp_ref[cl, :, kl], vp_ref[cl, :, kl]
            else:
                k_prev, v_prev = kc_ref[cl, (j - 1) * BAND:j * BAND, kl], vc_ref[cl, (j - 1) * BAND:j * BAND, kl]
            k2 = jnp.concatenate([k_prev, kc_ref[cl, rows, kl]], axis=0)
            v2 = jnp.concatenate([v_prev, vc_ref[cl, rows, kl]], axis=0)
            q2 = _split_heads(q_ref[cl, rows, p * LANES:(p + 1) * LANES])
            s = lax.dot_general(q2, k2, (((1,), (1,)), ((), ())), preferred_element_type=_F32)
            s = s + bias_ref[p]
            if j == 0:
                s = jnp.where(jnp.logical_and(first, prev_cols), _NEG_INF, s)
            m = jnp.max(s, axis=1, keepdims=True)
            prob = jnp.exp(s - m).astype(_BF)
            r = jnp.dot(prob, jnp.concatenate([v2, ones], axis=1), preferred_element_type=_F32)
            pv, l = r[:, :LANES], r[:, LANES:]
            if has_sink:
                ha, hb = A_PAIR_HEADS[p]
                sink = jnp.where(top_rows, sink_ref[ha], sink_ref[hb])
                m2 = jnp.maximum(m, sink)
                a = jnp.exp(m - m2)
                o = pv * a / (l * a + jnp.exp(sink - m2))
            else:
                o = pv / l
            o_pair = jnp.where(left, o[:BAND], o[BAND:])
            if dilation == 1:
                o_ref[rows, p * LANES:(p + 1) * LANES] = o_pair.astype(o_ref.dtype)
            else:
                stage_ref[p, out_rows, :] = o_pair
            if want_stat:
                lse = m + jnp.log(l)
                st_ref[p, out_rows, :] = jnp.where(left, lse[:BAND], lse[BAND:])

    if dilation > 1:
        @pl.when(pl.program_id(2) == pl.num_programs(2) - 1)
        def _():
            for p in range(N_PAIRS):
                o_ref[:, p * LANES:(p + 1) * LANES] = stage_ref[p].astype(o_ref.dtype)


def _band_bias(pair_heads, dilation):
    slopes = _alibi_slopes(N_HEADS)
    qi = np.arange(BAND)[:, None]
    kj = np.arange(2 * BAND)[None, :]
    dist = qi + BAND - kj
    valid = (dist >= 0) & (dist <= BAND)
    out = np.empty((len(pair_heads), 2 * BAND, 2 * BAND), np.float32)
    for p, heads in enumerate(pair_heads):
        for half, h in enumerate(heads):
            bias = -np.float32(slopes[h]) * (dist * dilation).astype(np.float32)
            out[p, half * BAND:(half + 1) * BAND] = np.where(valid, bias, -np.inf)
    return jnp.asarray(out)


BAND_TOKENS = 4096
BAND_STEP_ROWS = 1024


def _band_attn(arr, q_col, k_col, v_col, kv_width, kv_lane, pair_heads, sinks, want_stat):
    bsz, dilation, n, _ = arr.shape
    seq = n * dilation
    tq = min(BAND_STEP_ROWS, n, BAND_TOKENS // dilation)
    sub = tq // BAND
    n_classes = min(dilation, BAND_STEP_ROWS // tq)
    assert q_col % MIX_W == 0 and k_col % kv_width == 0 and v_col % kv_width == 0 and dilation % n_classes == 0

    def cur(col, w):
        return pl.BlockSpec((None, n_classes, tq, w), lambda b, i, r: (b, r, i, col // w))

    def prev(col, w):
        return pl.BlockSpec((None, n_classes, BAND, w),
                            lambda b, i, r: (b, r, jnp.maximum(i * sub - 1, 0), col // w))

    in_specs = [cur(q_col, MIX_W), cur(k_col, kv_width), prev(k_col, kv_width), cur(v_col, kv_width),
                prev(v_col, kv_width), _const_spec((N_PAIRS, 2 * BAND, 2 * BAND))]
    args = [arr, arr, arr, arr, arr, _band_bias(pair_heads, dilation)]
    if sinks is not None:
        in_specs.append(pl.BlockSpec(memory_space=pltpu.SMEM))
        args.append(sinks)
    out_specs = [pl.BlockSpec((None, tq * dilation, MIX_W), lambda b, i, r: (b, i, 0))]
    out_shape = [jax.ShapeDtypeStruct((bsz, seq, MIX_W), _BF)]
    if want_stat:
        out_specs.append(pl.BlockSpec((None, N_PAIRS, tq * dilation, LANES), lambda b, i, r: (b, 0, i, 0)))
        out_shape.append(jax.ShapeDtypeStruct((bsz, N_PAIRS, seq, LANES), _F32))
    scratch = [pltpu.VMEM((N_PAIRS, tq * dilation, LANES), _F32)] if dilation > 1 else []
    res = pl.pallas_call(
        functools.partial(_band_attn_kernel, tq=tq, dilation=dilation, kv_lane=kv_lane, has_sink=sinks is not None,
                          want_stat=want_stat),
        grid=(bsz, n // tq, dilation // n_classes),
        in_specs=in_specs,
        out_specs=out_specs,
        out_shape=out_shape,
        scratch_shapes=scratch,
        compiler_params=_cparams(3),
        name=f"band_attn_d{dilation}",
    )(*args)
    o = res[0].reshape(bsz * seq, MIX_W)
    return (o, res[1]) if want_stat else o


def _cached_attn_stages(rows, pair0, refs, sink_ref, *, n_hist, t_new, q_pairs_of_kv, pair_heads, cache_row=None):
    q_ref, kn_ref, vn_ref, kc_ref, vc_ref, bias_ref, mult_ref, y_ref, ko_ref, vo_ref = refs
    left = _lane_is_left()
    new_lanes = lax.broadcasted_iota(jnp.int32, (1, LANES), 1) >= LANES - t_new
    top_rows = lax.broadcasted_iota(jnp.int32, (2 * t_new, 1), 0) < t_new
    zpad = jnp.zeros((LANES - t_new, LANES), _F32)
    mult = mult_ref[...]

    chains = []
    for b in rows:
        for kvp, q_pairs in enumerate(q_pairs_of_kv):
            heads = slice(2 * kvp, 2 * kvp + 2)
            lanes = slice(kvp * LANES, (kvp + 1) * LANES)
            ext = []
            for c_ref, n_ref, o_ref in ((kc_ref, kn_ref, ko_ref), (vc_ref, vn_ref, vo_ref)):
                old = c_ref[b if cache_row is None else cache_row, heads].reshape(LANES, n_hist)
                new = jnp.concatenate([zpad, n_ref[b, :, lanes]], axis=0).T
                rolled = pltpu.roll(old, n_hist - t_new, axis=1)
                tail = jnp.where(new_lanes, new, rolled[:, n_hist - LANES:])
                out = tail if n_hist == LANES else jnp.concatenate([rolled[:, :n_hist - LANES], tail], axis=1)
                o_ref[b, heads] = out.reshape(2, HEAD_DIM, n_hist)
                ext.append(jnp.concatenate([old.astype(_BF), new.astype(_BF)], axis=1))
            chains += [(b, p, ext[0], ext[1]) for p in q_pairs]

    scores = []
    for b, p, k_ext, _ in chains:
        q2 = _split_heads(q_ref[b, :, p * LANES:(p + 1) * LANES].astype(_BF))
        scores.append(jnp.dot(q2, k_ext, preferred_element_type=_F32) + bias_ref[pair0 + p])
    yield
    probs = []
    for s in scores:
        m = jnp.max(s, axis=1, keepdims=True)
        prob = (mult * jnp.exp(s - m)).astype(_BF)
        probs.append((m, prob, jnp.sum(prob.astype(_F32), axis=1, keepdims=True)))
    pvs = [lax.dot_general(prob, v_ext, (((1,), (1,)), ((), ())), preferred_element_type=_F32)
           for (_, prob, _), (_, _, _, v_ext) in zip(probs, chains)]
    yield
    for (b, p, _, _), (m, _, l), pv in zip(chains, probs, pvs):
        if sink_ref is not None:
            ha, hb = pair_heads[p]
            sink = jnp.where(top_rows, sink_ref[ha], sink_ref[hb])
            m2 = jnp.maximum(m, sink)
            a = jnp.exp(m - m2)
            o = pv * a / (l * a + jnp.exp(sink - m2))
        else:
            o = pv / l
        y_ref[b, :, p * LANES:(p + 1) * LANES] = jnp.where(left, o[:t_new], o[t_new:]).astype(y_ref.dtype)


def _sample_attn_kernel(*refs, has_sink, **statics):
    refs = list(refs)
    sink_ref = refs.pop(7) if has_sink else None
    for _ in _cached_attn_stages(range(refs[0].shape[0]), 0, refs, sink_ref, **statics):
        pass


def _sample_tables(pair_heads, branches, n_hist, t_new):
    slopes = _alibi_slopes(N_HEADS)
    key_pos = np.concatenate([np.arange(n_hist), n_hist + np.arange(LANES) - (LANES - t_new)])
    is_key = np.concatenate([np.ones(n_hist, bool), np.arange(LANES) >= LANES - t_new])
    delta = (n_hist + np.arange(t_new))[:, None] - key_pos[None, :]
    mult = np.zeros(delta.shape, np.float32)
    for window, dil in branches:
        mult += ((delta >= 0) & (delta % dil == 0) & (delta <= window) & is_key[None, :]).astype(np.float32)
    bias = np.empty((len(pair_heads), 2 * t_new, key_pos.size), np.float32)
    for p, heads in enumerate(pair_heads):
        for half, h in enumerate(heads):
            b = -np.float32(slopes[h]) * delta.astype(np.float32)
            bias[p, half * t_new:(half + 1) * t_new] = np.where(mult > 0, b, -np.inf)
    return jnp.asarray(bias), jnp.asarray(np.concatenate([mult, mult], axis=0))


UNIT_HEADS = 4


CACHE_SLOTS = 3


def _tail_cache_kernel(x_ref, ya_ref, yb_ref, wo_ref, bo_ref, g1_ref, b1_ref, w1_ref, w2_ref, g2_ref, b2_ref,
                       q_ref, kn_ref, vn_ref, kc_hbm, vc_hbm, bias_ref, mult_ref,
                       o_ref, y_ref, ko_ref, vo_ref, xb_ref, acc_ref, kbuf_ref, vbuf_ref, sem_ref,
                       *, units_per_row, **statics):
    c = pl.program_id(1)
    n_chunks = pl.num_programs(1)
    unit = pl.program_id(0) * n_chunks + c
    n_units = pl.num_programs(0) * n_chunks

    def fetch(u):
        slot = lax.rem(u, CACHE_SLOTS)
        heads = pl.ds(lax.rem(u, units_per_row) * UNIT_HEADS, UNIT_HEADS)
        return [pltpu.make_async_copy(hbm.at[u // units_per_row, heads], buf.at[slot], sem_ref.at[k, slot])
                for k, (hbm, buf) in enumerate(((kc_hbm, kbuf_ref), (vc_hbm, vbuf_ref)))]

    @pl.when(unit == 0)
    def _():
        for u in range(CACHE_SLOTS - 1):
            for cp in fetch(u):
                cp.start()

    @pl.when(unit + (CACHE_SLOTS - 1) < n_units)
    def _():
        for cp in fetch(unit + (CACHE_SLOTS - 1)):
            cp.start()

    for cp in fetch(unit):
        cp.wait()

    @pl.when(c == 0)
    def _():
        mix = jnp.dot(ya_ref[...], wo_ref[:MIX_W, :], preferred_element_type=_F32)
        mix = mix + jnp.dot(yb_ref[...], wo_ref[MIX_W:, :], preferred_element_type=_F32) + bo_ref[...]
        x1 = _layer_norm(DEEPNORM_ALPHA * x_ref[...] + mix, g1_ref[...], b1_ref[...])
        xb_ref[...] = x1.astype(_BF)
        acc_ref[...] = DEEPNORM_ALPHA * x1

    pair0 = (unit % units_per_row) * (UNIT_HEADS // 2)
    cache_refs = (q_ref, kn_ref, vn_ref, kbuf_ref, vbuf_ref, bias_ref, mult_ref, y_ref, ko_ref, vo_ref)
    stages = _cached_attn_stages((0,), pair0, cache_refs, None, cache_row=lax.rem(unit, CACHE_SLOTS), **statics)
    next(stages)
    h = jnp.dot(xb_ref[...], w1_ref[c], preferred_element_type=_F32)
    h = jnp.square(jnp.maximum(h, 0.0)).astype(_BF)
    next(stages)
    acc_ref[...] += jnp.dot(h, w2_ref[c], preferred_element_type=_F32)
    for _ in stages:
        pass

    @pl.when(c == n_chunks - 1)
    def _():
        o_ref[...] = _layer_norm(acc_ref[...], g2_ref[...], b2_ref[...])


def _even_tail_with_cache(x, ya, yb, tail_params, p3, q_col, k_col, v_col, pair_heads, branches, k_cache, v_cache,
                          tm, n_chunks):
    n, dm = x.shape
    bsz, t_new, _ = p3.shape
    _, n_hist, kvh, _ = k_cache.shape
    units_per_row = kvh // UNIT_HEADS
    unit_w = UNIT_HEADS * HEAD_DIM
    assert (n // tm) * n_chunks == bsz * units_per_row >= CACHE_SLOTS and kvh == N_HEADS
    wo, bo, g1, b1, w1, w2, g2, b2 = tail_params
    dh = w1.shape[1]
    w1c = jnp.transpose(w1.reshape(dm, n_chunks, dh // n_chunks), (1, 0, 2)).astype(_BF)
    w2c = w2.reshape(n_chunks, dh // n_chunks, dm)
    kt = jnp.transpose(k_cache, (0, 2, 3, 1))
    vt = jnp.transpose(v_cache, (0, 2, 3, 1))
    bias, mult = _sample_tables(pair_heads, branches, n_hist, t_new)

    row = lambda w: pl.BlockSpec((tm, w), lambda t, c: (t, 0))
    unit_of = lambda t, c: t * n_chunks + c
    new = lambda col: pl.BlockSpec(
        (1, t_new, unit_w),
        lambda t, c: (unit_of(t, c) // units_per_row, 0, col // unit_w + unit_of(t, c) % units_per_row))
    cache = pl.BlockSpec((1, UNIT_HEADS, HEAD_DIM, n_hist),
                         lambda t, c: (unit_of(t, c) // units_per_row, unit_of(t, c) % units_per_row, 0, 0))
    params = (wo, bo, g1, b1, w1c, w2c, g2, b2)
    out, y, ko, vo = pl.pallas_call(
        functools.partial(_tail_cache_kernel, units_per_row=units_per_row, n_hist=n_hist, t_new=t_new,
                          q_pairs_of_kv=tuple((p,) for p in range(UNIT_HEADS // 2)), pair_heads=pair_heads),
        grid=(n // tm, n_chunks),
        in_specs=[row(dm), row(MIX_W), row(MIX_W)] + [_resident_spec(p.shape) for p in params]
        + [new(q_col), new(k_col), new(v_col), pl.BlockSpec(memory_space=pl.ANY), pl.BlockSpec(memory_space=pl.ANY),
           _resident_spec(bias.shape), _resident_spec(mult.shape)],
        out_specs=[row(dm), new(0), cache, cache],
        out_shape=[jax.ShapeDtypeStruct((n, dm), _F32), jax.ShapeDtypeStruct((bsz, t_new, MIX_W), _BF),
                   jax.ShapeDtypeStruct(kt.shape, _F32), jax.ShapeDtypeStruct(vt.shape, _F32)],
        scratch_shapes=[pltpu.VMEM((tm, dm), _BF), pltpu.VMEM((tm, dm), _F32)]
        + [pltpu.VMEM((CACHE_SLOTS, UNIT_HEADS, HEAD_DIM, n_hist), _F32)] * 2
        + [pltpu.SemaphoreType.DMA((2, CACHE_SLOTS))],
        compiler_params=pltpu.CompilerParams(dimension_semantics=("arbitrary", "arbitrary"),
                                             vmem_limit_bytes=FUSED_VMEM_LIMIT),
        name="even_tail_with_cache",
    )(x, ya, yb, *params, p3, p3, p3, kt, vt, bias, mult)
    return out, y, jnp.transpose(ko, (0, 3, 1, 2)), jnp.transpose(vo, (0, 3, 1, 2))


def _sample_attn(p3, q_col, k_col, v_col, pair_heads, branches, k_cache, v_cache, sinks, bt):
    bsz, t_new, _ = p3.shape
    _, n_hist, kvh, _ = k_cache.shape
    kv_width = kvh * HEAD_DIM
    kt = jnp.transpose(k_cache, (0, 2, 3, 1))
    vt = jnp.transpose(v_cache, (0, 2, 3, 1))
    n_kv_pairs = kvh // 2
    q_pairs_of_kv = tuple(tuple(p for p in range(N_PAIRS) if p % n_kv_pairs == kvp) for kvp in range(n_kv_pairs))
    bias, mult = _sample_tables(pair_heads, branches, n_hist, t_new)
    new = lambda col, w: pl.BlockSpec((bt, t_new, w), lambda b: (b, 0, col // w))
    cache = pl.BlockSpec((bt, kvh, HEAD_DIM, n_hist), lambda b: (b, 0, 0, 0))
    in_specs = [new(q_col, MIX_W), new(k_col, kv_width), new(v_col, kv_width), cache, cache,
                _const_spec(bias.shape), _const_spec(mult.shape)]
    args = [p3, p3, p3, kt, vt, bias, mult]
    if sinks is not None:
        in_specs.append(pl.BlockSpec(memory_space=pltpu.SMEM))
        args.append(sinks)
    y, ko, vo = pl.pallas_call(
        functools.partial(_sample_attn_kernel, n_hist=n_hist, t_new=t_new, q_pairs_of_kv=q_pairs_of_kv,
                          pair_heads=pair_heads, has_sink=sinks is not None),
        grid=(bsz // bt,),
        in_specs=in_specs,
        out_specs=[pl.BlockSpec((bt, t_new, MIX_W), lambda b: (b, 0, 0)), cache, cache],
        out_shape=[jax.ShapeDtypeStruct((bsz, t_new, MIX_W), _BF),
                   jax.ShapeDtypeStruct(kt.shape, _F32), jax.ShapeDtypeStruct(vt.shape, _F32)],
        compiler_params=_cparams(1),
        name=f"sample_attn_{n_hist}",
    )(*args)
    return y, jnp.transpose(ko, (0, 3, 1, 2)), jnp.transpose(vo, (0, 3, 1, 2))


A_Q = N_HEADS * HEAD_DIM
A_KV = A_KV_HEADS * HEAD_DIM
E_Q, E_H, E_GB, E_GC, E_K, E_V = 0, 512, 1024, 1536, 2048, 2176
O_U, O_Q, O_K, O_V = 0, 512, 1024, 1536


def _prep_layer_weights(even_w_in, even_b_in, even_w_out, odd_w_in, odd_b_in, c_w_group):
    q_cols = np.concatenate([h * HEAD_DIM + np.arange(HEAD_DIM) for h in A_HEAD_ORDER])
    o1, o2, o3 = A_Q, A_Q + A_KV, A_Q + 2 * A_KV
    order = np.concatenate([q_cols, np.arange(o3, o3 + 3 * MIX_W), np.arange(o1, o3)])
    scale = np.ones((order.size,), np.float32)
    scale[:A_Q] = HEAD_DIM ** -0.5
    ew = (even_w_in[:, order] * scale).astype(_BF)
    eb = (even_b_in[order] * scale)[None, :]
    ewo = jnp.concatenate([even_w_out[q_cols], even_w_out[A_Q:]], axis=0).astype(_BF)
    oscale = np.ones((odd_w_in.shape[1],), np.float32)
    oscale[O_Q:O_K] = HEAD_DIM ** -0.5
    ow = (odd_w_in * oscale).astype(_BF)
    ob = (odd_b_in * oscale)[None, :]
    groups, gw, _ = c_w_group.shape
    wg = jnp.zeros((MIX_W, MIX_W), _F32)
    for g in range(groups):
        wg = wg.at[g * gw:(g + 1) * gw, g * gw:(g + 1) * gw].set(c_w_group[g])
    return ew, eb, ewo, ow, ob, wg.astype(_BF)


def _row(v):
    return v[None, :]


def _forward(xp, xs, caches, wts, tm, tm_proj, bt_attn, bt_shift, n_chunks):
    (even_w_in, even_b_in, a_sinks, b_conv_w, even_w_out, even_b_out, odd_w_in, odd_b_in, c_w_group, c_scale,
     odd_w_out, odd_b_out, mlp_w1, mlp_w2, ln1_g, ln1_b, ln2_g, ln2_b) = wts
    cache_a_k, cache_a_v, state_b_conv, state_c_pool, cache_d_k, cache_d_v = caches
    bsz, seq, dm = xp.shape
    bs, ts, _ = xs.shape
    n, ns = bsz * seq, bs * ts
    tms = min(tm, ns)
    ew, eb, ewo, ow, ob, wg = _prep_layer_weights(even_w_in[0], even_b_in[0], even_w_out[0], odd_w_in[0],
                                                  odd_b_in[0], c_w_group[0])
    owo = odd_w_out[0].astype(_BF)
    w1 = mlp_w1.astype(_BF)
    w2 = mlp_w2.astype(_BF)
    tails = [(wo, _row(bo), _row(ln1_g[i]), _row(ln1_b[i]), w1[i], w2[i], _row(ln2_g[i]), _row(ln2_b[i]))
             for i, (wo, bo) in enumerate(((ewo, even_b_out[0]), (owo, odd_b_out[0])))]
    a_kv_lane = (0,) * N_PAIRS
    d_kv_lane = tuple(p * LANES for p in range(N_PAIRS))

    xs = xs.reshape(ns, dm)
    pe3 = _proj(xs, ew, eb, _F32, tms).reshape(bs, ts, -1)
    ya, ak_s, av_s = _sample_attn(pe3, E_Q, E_K, E_V, A_PAIR_HEADS, ((A_WINDOW, 1),), cache_a_k[0], cache_a_v[0],
                                  a_sinks[0], bt_attn)
    c_hist = jnp.pad(state_b_conv[0], ((0, 0), (HALO - (CONV_WIDTH - 1), 0), (0, 0)))
    yb, ctail = _sample_conv(pe3, c_hist, b_conv_w[0], bt_shift)
    bc_s = ctail[:, -(CONV_WIDTH - 1):, :]
    xs = _even_tail(xs, ya.reshape(ns, MIX_W), yb.reshape(ns, MIX_W), tails[0], tms)
    po3s = _proj(xs, ow, ob, _F32, tms).reshape(bs, ts, -1)
    u_hist = jnp.pad(state_c_pool[0], ((0, 0), (HALO - (POOL_MAX - 1), 0), (0, 0)))
    pooled_s = _sample_pool(po3s, u_hist, bt_shift).reshape(ns, MIX_W)
    cp_s = jnp.concatenate([state_c_pool[0], po3s[:, :, O_U:O_Q]], axis=1)[:, -(POOL_MAX - 1):]

    x = xp.reshape(n, dm)
    qkv, yb, ctail = _proj_conv(xp, ew, eb, b_conv_w[0], tm_proj)
    k_col, v_col = MIX_W, MIX_W + A_KV
    ya = _band_attn(qkv[:, None], 0, k_col, v_col, LANES, a_kv_lane, A_PAIR_HEADS, a_sinks[0], False)
    n_keep = min(A_WINDOW, seq)
    a_k = qkv[:, seq - n_keep:, k_col:v_col].astype(_F32).reshape(bsz, n_keep, A_KV_HEADS, HEAD_DIM)
    a_v = qkv[:, seq - n_keep:, v_col:].astype(_F32).reshape(bsz, n_keep, A_KV_HEADS, HEAD_DIM)
    b_conv = ctail[:, -(CONV_WIDTH - 1):, :]
    fused_tail = tails[0][:4] + (mlp_w1[0],) + tails[0][5:]
    x, yd, dk_s, dv_s = _even_tail_with_cache(x, ya, yb.reshape(n, MIX_W), fused_tail, po3s, O_Q, O_K, O_V,
                                              D_PAIR_HEADS, D_BRANCHES, cache_d_k[0], cache_d_v[0], tm, n_chunks)
    xs = _sample_odd_tail(xs, pooled_s, wg, _row(c_scale[0]), yd.reshape(ns, MIX_W), tails[1], tms)

    dils = tuple(d for _, d in D_BRANCHES if d > 1)
    qkv, qkv4, qkv16, pooled, utail = _proj_dilated(x.reshape(bsz, seq, dm), ow, ob, tm_proj, O_Q, dils)
    k_col, v_col = O_K - O_Q, O_V - O_Q
    outs, stats = [], []
    for arr in (qkv[:, None], qkv4, qkv16):
        o, st = _band_attn(arr, 0, k_col, v_col, MIX_W, d_kv_lane, D_PAIR_HEADS, None, True)
        outs.append(o)
        stats.append(st)
    n_keep = min(D_BRANCHES[-1][0], seq)
    c_pool = utail[:, -(POOL_MAX - 1):, :]
    d_k = qkv[:, seq - n_keep:, k_col:v_col].astype(_F32).reshape(bsz, n_keep, N_HEADS, HEAD_DIM)
    d_v = qkv[:, seq - n_keep:, v_col:].astype(_F32).reshape(bsz, n_keep, N_HEADS, HEAD_DIM)
    x = _odd_tail(x, pooled.reshape(n, MIX_W), wg, _row(c_scale[0]), outs, stats, tails[1], tm)
    return (x.reshape(bsz, seq, dm), xs.reshape(bs, ts, dm), a_k[None], a_v[None], b_conv[None], c_pool[None],
            d_k[None], d_v[None], ak_s[None], av_s[None], bc_s[None], cp_s[None], dk_s[None], dv_s[None])


def kernel(x_prompt, x_sample, cache_a_k, cache_a_v, state_b_conv, state_c_pool, cache_d_k, cache_d_v, even_w_in, even_b_in, a_sinks, b_conv_w, even_w_out, even_b_out, odd_w_in, odd_b_in, c_w_group, c_scale, odd_w_out, odd_b_out, mlp_w1, mlp_w2, ln1_g, ln1_b, ln2_g, ln2_b):
    wts = (even_w_in, even_b_in, a_sinks, b_conv_w, even_w_out, even_b_out, odd_w_in, odd_b_in, c_w_group, c_scale,
           odd_w_out, odd_b_out, mlp_w1, mlp_w2, ln1_g, ln1_b, ln2_g, ln2_b)
    caches = (cache_a_k, cache_a_v, state_b_conv, state_c_pool, cache_d_k, cache_d_v)
    return _forward(x_prompt, x_sample, caches, wts, tm=512, tm_proj=1024, bt_attn=8, bt_shift=32, n_chunks=4)
```

```python
import functools

import numpy as np
import jax
import jax.numpy as jnp
from jax import lax
from jax.experimental import pallas as pl
from jax.experimental.pallas import tpu as pltpu

HEAD_DIM = 64
N_HEADS = 8
A_KV_HEADS = 2
A_WINDOW = 128
D_BRANCHES = ((128, 1), (512, 4), (2048, 16))
CONV_WIDTH = 3
POOL_WINDOWS = (2, 4, 8, 16)
POOL_MAX = 16
DEPTH = 2
PAST_LEN = 16384
DEEPNORM_ALPHA = (2 * DEPTH) ** 0.25
LN_EPS = 1e-5

MIX_W = N_HEADS * HEAD_DIM
LANES = 128
N_PAIRS = MIX_W // LANES
BAND = 128
HALO = 16
EXT0 = 24
VMEM_LIMIT = 56 * 1024 * 1024
FUSED_VMEM_LIMIT = 62 * 1024 * 1024

A_HEAD_ORDER = (0, 4, 1, 5, 2, 6, 3, 7)
A_PAIR_HEADS = tuple((p, p + 4) for p in range(N_PAIRS))
D_PAIR_HEADS = tuple((2 * p, 2 * p + 1) for p in range(N_PAIRS))

_BF = jnp.bfloat16
_F32 = jnp.float32
_NEG_INF = float("-inf")


def _alibi_slopes(n_heads):
    return 2.0 ** (-8.0 * np.arange(1, n_heads + 1) / n_heads)


def _cparams(n_axes):
    return pltpu.CompilerParams(dimension_semantics=("arbitrary",) * n_axes, vmem_limit_bytes=VMEM_LIMIT)


def _const_spec(shape):
    nd = len(shape)
    return pl.BlockSpec(shape, lambda *_: (0,) * nd)


def _layer_norm(y, g, b):
    mu = jnp.mean(y, axis=-1, keepdims=True)
    yc = y - mu
    var = jnp.mean(yc * yc, axis=-1, keepdims=True)
    return yc * lax.rsqrt(var + LN_EPS) * g + b


def _proj_kernel(x_ref, w_ref, b_ref, o_ref, *, tn):
    x = x_ref[...].astype(_BF)
    for j in range(o_ref.shape[1] // tn):
        cols = slice(j * tn, (j + 1) * tn)
        acc = jnp.dot(x, w_ref[:, cols], preferred_element_type=_F32)
        o_ref[:, cols] = (acc + b_ref[:, cols]).astype(o_ref.dtype)


def _proj(x, w, b, out_dtype, tm):
    n, k = x.shape
    m = w.shape[1]
    return pl.pallas_call(
        functools.partial(_proj_kernel, tn=256),
        grid=(n // tm,),
        in_specs=[pl.BlockSpec((tm, k), lambda i: (i, 0)), _const_spec((k, m)), _const_spec((1, m))],
        out_specs=pl.BlockSpec((tm, m), lambda i: (i, 0)),
        out_shape=jax.ShapeDtypeStruct((n, m), out_dtype),
        compiler_params=_cparams(1),
        name="proj",
    )(x, w, b)


def _carried_history(ext_ref, tm):
    @pl.when(pl.program_id(1) == 0)
    def _():
        ext_ref[:, tm + EXT0 - HALO:tm + EXT0, :] = jnp.zeros((1, HALO, MIX_W), _F32)

    return ext_ref[:, tm + EXT0 - HALO:tm + EXT0, :]


def _proj_conv_kernel(x_ref, w_ref, b_ref, cw_ref, qkv_ref, yb_ref, ctail_ref, hg_ref, ext_ref, *, tn):
    tm = x_ref.shape[0]
    c_hist = _carried_history(ext_ref, tm)
    x = x_ref[...].astype(_BF)
    gates = slice(MIX_W, 4 * MIX_W)
    for j in range(w_ref.shape[1] // tn):
        lo = j * tn
        acc = jnp.dot(x, w_ref[:, lo:lo + tn], preferred_element_type=_F32) + b_ref[:, lo:lo + tn]
        if lo < gates.start:
            qkv_ref[:, lo:lo + tn] = acc.astype(qkv_ref.dtype)
        elif lo < gates.stop:
            hg_ref[:, lo - gates.start:lo - gates.start + tn] = acc
        else:
            qkv_ref[:, lo - 3 * MIX_W:lo - 3 * MIX_W + tn] = acc.astype(qkv_ref.dtype)
    h, gb, gc = (hg_ref[:, k * MIX_W:(k + 1) * MIX_W][None] for k in range(3))
    _conv_body(h, gb, gc, c_hist, cw_ref, ext_ref, yb_ref, ctail_ref)


def _proj_conv(x3, w, b, conv_w, tm):
    bsz, seq, k = x3.shape
    m = w.shape[1]
    qkv_w = m - 3 * MIX_W
    return pl.pallas_call(
        functools.partial(_proj_conv_kernel, tn=256),
        grid=(bsz, seq // tm),
        in_specs=[pl.BlockSpec((None, tm, k), lambda bi, i: (bi, i, 0)), _const_spec((k, m)), _const_spec((1, m)),
                  _const_spec(conv_w.shape)],
        out_specs=[pl.BlockSpec((None, tm, qkv_w), lambda bi, i: (bi, i, 0)),
                   pl.BlockSpec((1, tm, MIX_W), lambda bi, i: (bi, i, 0)),
                   pl.BlockSpec((1, 8, MIX_W), lambda bi, i: (bi, i, 0))],
        out_shape=[jax.ShapeDtypeStruct((bsz, seq, qkv_w), _BF), jax.ShapeDtypeStruct((bsz, seq, MIX_W), _BF),
                   jax.ShapeDtypeStruct((bsz, (seq // tm) * 8, MIX_W), _F32)],
        scratch_shapes=[pltpu.VMEM((tm, 3 * MIX_W), _F32), pltpu.VMEM((1, EXT0 + tm, MIX_W), _F32)],
        compiler_params=_cparams(2),
        name="proj_conv",
    )(x3, w, b, conv_w)


def _proj_dilated_kernel(x_ref, w_ref, b_ref, o_ref, d4_ref, d16_ref, pooled_ref, utail_ref, kf_ref, vf_ref,
                         stage_ref, stage4_ref, u_ref, ext_ref, s2_ref, s4_ref, s8_ref, *, tn, first_col):
    tm = x_ref.shape[0]
    u_hist = _carried_history(ext_ref, tm)
    x = x_ref[...].astype(_BF)
    for j in range(w_ref.shape[1] // tn):
        lo = j * tn
        acc = jnp.dot(x, w_ref[:, lo:lo + tn], preferred_element_type=_F32) + b_ref[:, lo:lo + tn]
        if lo < first_col:
            u_ref[:, lo:lo + tn] = acc
        else:
            rel = lo - first_col
            o_ref[:, rel:rel + tn] = acc.astype(o_ref.dtype)
            for h in range(tn // LANES):
                stage_ref[rel // LANES + h] = acc[:, h * LANES:(h + 1) * LANES]
            if rel >= MIX_W:
                f_ref = kf_ref if rel < 2 * MIX_W else vf_ref
                f_ref[:, rel % MIX_W:rel % MIX_W + tn] = acc
    _pool_body(u_ref[...][None], u_hist, pl.program_id(1) * tm, ext_ref, s2_ref, s4_ref, s8_ref, pooled_ref)
    utail_ref[...] = ext_ref[:, tm + EXT0 - HALO:tm + EXT0, :]
    q4, q16 = tm // 4, tm // 16
    for s in range(stage_ref.shape[0]):
        lanes = slice(s * LANES, (s + 1) * LANES)
        for r in range(4):
            rows = stage_ref[s, pl.ds(r, q4, stride=4), :]
            stage4_ref[s, r * q4:(r + 1) * q4, :] = rows
            d4_ref[r, :, lanes] = rows.astype(d4_ref.dtype)
        for r in range(4):
            for k in range(4):
                d16_ref[r + 4 * k, :, lanes] = stage4_ref[s, pl.ds(r * q4 + k, q16, stride=4), :].astype(d16_ref.dtype)


def _proj_dilated(x3, w, b, tm, first_col, dilations, n_keep):
    bsz, seq, k = x3.shape
    m = w.shape[1]
    wd = m - first_col
    assert first_col == MIX_W and wd == 3 * MIX_W and tuple(dilations) == (4, 16) and tm % 256 == 0
    assert n_keep % tm == 0 and seq % tm == 0
    skipped = (seq - n_keep) // tm
    out_specs = [pl.BlockSpec((None, tm, wd), lambda bi, i: (bi, i, 0))]
    out_shape = [jax.ShapeDtypeStruct((bsz, seq, wd), _BF)]
    for d in dilations:
        out_specs.append(pl.BlockSpec((None, d, tm // d, wd), lambda bi, i: (bi, 0, i, 0)))
        out_shape.append(jax.ShapeDtypeStruct((bsz, d, seq // d, wd), _BF))
    out_specs += [pl.BlockSpec((1, tm, MIX_W), lambda bi, i: (bi, i, 0)),
                  pl.BlockSpec((1, HALO, MIX_W), lambda bi, i: (bi, i, 0))]
    out_shape += [jax.ShapeDtypeStruct((bsz, seq, MIX_W), _BF),
                  jax.ShapeDtypeStruct((bsz, (seq // tm) * HALO, MIX_W), _F32)]
    out_specs += [pl.BlockSpec((None, tm, MIX_W), lambda bi, i: (bi, jnp.maximum(i - skipped, 0), 0))] * 2
    out_shape += [jax.ShapeDtypeStruct((bsz, n_keep, MIX_W), _F32)] * 2
    shift = pltpu.VMEM((1, EXT0 + tm, MIX_W), _F32)
    return pl.pallas_call(
        functools.partial(_proj_dilated_kernel, tn=256, first_col=first_col),
        grid=(bsz, seq // tm),
        in_specs=[pl.BlockSpec((None, tm, k), lambda bi, i: (bi, i, 0)), _const_spec((k, m)), _const_spec((1, m))],
        out_specs=out_specs,
        out_shape=out_shape,
        scratch_shapes=[pltpu.VMEM((wd // LANES, tm, LANES), _F32)] * 2 + [pltpu.VMEM((tm, MIX_W), _F32)] + [shift] * 4,
        compiler_params=_cparams(2),
        name="proj_dilated",
    )(x3, w, b)


MLP_CHUNK = 512


def _lane_is_left():
    return lax.broadcasted_iota(jnp.int32, (1, LANES), 1) < HEAD_DIM


def _layer_tail(x_ref, left, right, tail_refs):
    wo_ref, bo_ref, g1_ref, b1_ref, w1_ref, w2_ref, g2_ref, b2_ref, o_ref = tail_refs
    mix = jnp.dot(left, wo_ref[:MIX_W, :], preferred_element_type=_F32)
    mix = mix + jnp.dot(right, wo_ref[MIX_W:, :], preferred_element_type=_F32) + bo_ref[...]
    x = _layer_norm(DEEPNORM_ALPHA * x_ref[...] + mix, g1_ref[...], b1_ref[...])
    xb = x.astype(_BF)
    acc = jnp.zeros(x.shape, _F32)
    for c in range(w1_ref.shape[1] // MLP_CHUNK):
        cols = slice(c * MLP_CHUNK, (c + 1) * MLP_CHUNK)
        h = jnp.dot(xb, w1_ref[:, cols], preferred_element_type=_F32)
        h = jnp.square(jnp.maximum(h, 0.0)).astype(_BF)
        acc = acc + jnp.dot(h, w2_ref[cols, :], preferred_element_type=_F32)
    o_ref[...] = _layer_norm(DEEPNORM_ALPHA * x + acc, g2_ref[...], b2_ref[...])


def _even_tail_kernel(x_ref, ya_ref, yb_ref, *tail_refs):
    _layer_tail(x_ref, ya_ref[...], yb_ref[...], tail_refs)


def _group_c(pooled_ref, wg_ref, scale_ref):
    return (jnp.dot(pooled_ref[...], wg_ref[...], preferred_element_type=_F32) * scale_ref[...]).astype(_BF)


def _sample_odd_tail_kernel(x_ref, pooled_ref, wg_ref, scale_ref, yd_ref, *tail_refs):
    _layer_tail(x_ref, _group_c(pooled_ref, wg_ref, scale_ref), yd_ref[...], tail_refs)


def _odd_tail_kernel(x_ref, pooled_ref, wg_ref, scale_ref, o1_ref, o2_ref, o3_ref, s1_ref, s2_ref, s3_ref,
                     *tail_refs):
    yc = _group_c(pooled_ref, wg_ref, scale_ref)
    tiles = []
    for p in range(N_PAIRS):
        lanes = slice(p * LANES, (p + 1) * LANES)
        lses = (s1_ref[p], s2_ref[p], s3_ref[p])
        top = jnp.maximum(jnp.maximum(lses[0], lses[1]), lses[2])
        es = [jnp.exp(s - top) for s in lses]
        num = es[0] * o1_ref[:, lanes].astype(_F32)
        num = num + es[1] * o2_ref[:, lanes].astype(_F32)
        num = num + es[2] * o3_ref[:, lanes].astype(_F32)
        tiles.append(num / (es[0] + es[1] + es[2]))
    yd = jnp.concatenate(tiles, axis=1)
    _layer_tail(x_ref, yc, yd.astype(_BF), tail_refs)


def _row_spec(tm, width):
    return pl.BlockSpec((tm, width), lambda i: (i, 0))


def _resident_spec(shape):
    nd = len(shape)
    return pl.BlockSpec(shape, lambda *_: (0,) * nd, pipeline_mode=pl.Buffered(1))


def _tail_call(kernel_fn, name, x, mixer_args, mixer_specs, tail_params, tm):
    n, dm = x.shape
    return pl.pallas_call(
        kernel_fn,
        grid=(n // tm,),
        in_specs=[_row_spec(tm, dm)] + mixer_specs + [_resident_spec(p.shape) for p in tail_params],
        out_specs=_row_spec(tm, dm),
        out_shape=jax.ShapeDtypeStruct((n, dm), _F32),
        compiler_params=_cparams(1),
        name=name,
    )(x, *mixer_args, *tail_params)


def _even_tail(x, ya, yb, tail_params, tm):
    return _tail_call(_even_tail_kernel, "even_tail", x, [ya, yb], [_row_spec(tm, MIX_W)] * 2, tail_params, tm)


def _odd_tail(x, pooled, wg, scale, outs, stats, tail_params, tm):
    tiles_per_seq = stats[0].shape[2] // tm
    stat_spec = pl.BlockSpec((None, N_PAIRS, tm, LANES), lambda i: (i // tiles_per_seq, 0, i % tiles_per_seq, 0))
    specs = ([_row_spec(tm, MIX_W), _resident_spec(wg.shape), _resident_spec(scale.shape)]
             + [_row_spec(tm, MIX_W)] * 3 + [stat_spec] * 3)
    return _tail_call(_odd_tail_kernel, "odd_tail", x, [pooled, wg, scale, *outs, *stats], specs, tail_params, tm)


def _sample_odd_tail(x, pooled, wg, scale, yd, tail_params, tm):
    specs = [_row_spec(tm, MIX_W), _resident_spec(wg.shape), _resident_spec(scale.shape), _row_spec(tm, MIX_W)]
    return _tail_call(_sample_odd_tail_kernel, "sample_odd_tail", x, [pooled, wg, scale, yd], specs, tail_params, tm)


def _fill_ext(ext_ref, hist, cur, t):
    nb = ext_ref.shape[0]
    ext_ref[:, 0:8, :] = jnp.zeros((nb, 8, MIX_W), _F32)
    ext_ref[:, 8:EXT0, :] = hist
    ext_ref[:, EXT0:EXT0 + t, :] = cur


def _conv_body(h, gb, gc, c_hist, w_ref, ext_ref, yb_ref, ctail_ref):
    t = h.shape[1]
    c = gc * h
    _fill_ext(ext_ref, c_hist, c, t)
    conv = ext_ref[:, EXT0 - 2:EXT0 - 2 + t, :] * w_ref[0:1, :]
    conv = conv + ext_ref[:, EXT0 - 1:EXT0 - 1 + t, :] * w_ref[1:2, :]
    conv = conv + c * w_ref[2:3, :]
    yb_ref[...] = (gb * conv).astype(yb_ref.dtype)
    ctail_ref[...] = ext_ref[:, EXT0 + t - 8:EXT0 + t, :]


def _sample_conv_kernel(h_ref, gb_ref, gc_ref, hist_ref, w_ref, yb_ref, ctail_ref, ext_ref):
    _conv_body(h_ref[...], gb_ref[...], gc_ref[...], hist_ref[...], w_ref, ext_ref, yb_ref, ctail_ref)


def _pool_body(u, hist, pos0, ext_ref, s2_ref, s4_ref, s8_ref, out_ref):
    nb, t, _ = u.shape
    _fill_ext(ext_ref, hist, u, t)
    hi = EXT0 + t
    zeros8 = jnp.zeros((nb, 8, MIX_W), _F32)
    s2_ref[:, 0:8, :] = zeros8
    s4_ref[:, 0:8, :] = zeros8
    s8_ref[:, 0:8, :] = zeros8
    s2_ref[:, 8:hi, :] = ext_ref[:, 8:hi, :] + ext_ref[:, 7:hi - 1, :]
    s4_ref[:, 8:hi, :] = s2_ref[:, 8:hi, :] + s2_ref[:, 6:hi - 2, :]
    s8_ref[:, 8:hi, :] = s4_ref[:, 8:hi, :] + s4_ref[:, 4:hi - 4, :]
    sums = (
        s2_ref[:, EXT0:hi, 0:LANES],
        s4_ref[:, EXT0:hi, LANES:2 * LANES],
        s8_ref[:, EXT0:hi, 2 * LANES:3 * LANES],
        s8_ref[:, EXT0:hi, 3 * LANES:] + s8_ref[:, EXT0 - 8:hi - 8, 3 * LANES:],
    )
    pos = (pos0 + lax.broadcasted_iota(jnp.int32, (1, t, LANES), 1) + 1).astype(_F32)
    tiles = []
    for g, (w, s) in enumerate(zip(POOL_WINDOWS, sums)):
        cnt = jnp.minimum(pos, float(w))
        tiles.append(s / cnt - u[:, :, g * LANES:(g + 1) * LANES])
    out_ref[...] = jnp.concatenate(tiles, axis=2).astype(out_ref.dtype)


def _sample_pool_kernel(u_ref, hist_ref, out_ref, ext_ref, s2_ref, s4_ref, s8_ref):
    _pool_body(u_ref[...], hist_ref[...], PAST_LEN, ext_ref, s2_ref, s4_ref, s8_ref, out_ref)


def _sample_conv(p3, hist, conv_w, bt):
    bsz, t, _ = p3.shape
    col = lambda c: pl.BlockSpec((bt, t, MIX_W), lambda b: (b, 0, c))
    return pl.pallas_call(
        _sample_conv_kernel,
        grid=(bsz // bt,),
        in_specs=[col(1), col(2), col(3), pl.BlockSpec((bt, HALO, MIX_W), lambda b: (b, 0, 0)),
                  _const_spec(conv_w.shape)],
        out_specs=[pl.BlockSpec((bt, t, MIX_W), lambda b: (b, 0, 0)),
                   pl.BlockSpec((bt, 8, MIX_W), lambda b: (b, 0, 0))],
        out_shape=[jax.ShapeDtypeStruct((bsz, t, MIX_W), _BF), jax.ShapeDtypeStruct((bsz, 8, MIX_W), _F32)],
        scratch_shapes=[pltpu.VMEM((bt, EXT0 + t, MIX_W), _F32)],
        compiler_params=_cparams(1),
        name="sample_conv",
    )(p3, p3, p3, hist, conv_w)


def _sample_pool(p3, hist, bt):
    bsz, t, _ = p3.shape
    scratch = pltpu.VMEM((bt, EXT0 + t, MIX_W), _F32)
    return pl.pallas_call(
        _sample_pool_kernel,
        grid=(bsz // bt,),
        in_specs=[pl.BlockSpec((bt, t, MIX_W), lambda b: (b, 0, 0)),
                  pl.BlockSpec((bt, HALO, MIX_W), lambda b: (b, 0, 0))],
        out_specs=pl.BlockSpec((bt, t, MIX_W), lambda b: (b, 0, 0)),
        out_shape=jax.ShapeDtypeStruct((bsz, t, MIX_W), _BF),
        scratch_shapes=[scratch] * 4,
        compiler_params=_cparams(1),
        name="sample_pool",
    )(p3, hist)


def _split_heads(q_pair):
    left = _lane_is_left()
    zero = jnp.zeros_like(q_pair)
    return jnp.concatenate([jnp.where(left, q_pair, zero), jnp.where(left, zero, q_pair)], axis=0)


def _band_attn_kernel(*refs, tq, dilation, kv_lane, has_sink, want_stat):
    refs = list(refs)
    q_ref, kc_ref, kp_ref, vc_ref, vp_ref, bias_ref = refs[:6]
    rest = refs[6:]
    sink_ref = rest.pop(0) if has_sink else None
    o_ref = rest.pop(0)
    st_ref = rest.pop(0) if want_stat else None
    stage_ref = rest.pop(0) if dilation > 1 else None

    first = pl.program_id(1) == 0
    n_classes = q_ref.shape[0]
    left = _lane_is_left()
    prev_cols = lax.broadcasted_iota(jnp.int32, (1, 2 * BAND), 1) < BAND
    top_rows = lax.broadcasted_iota(jnp.int32, (2 * BAND, 1), 0) < BAND
    ones = jnp.ones((2 * BAND, LANES), _BF)

    for cl, j in ((cl, j) for cl in range(n_classes) for j in range(tq // BAND)):
        res = pl.program_id(2) * n_classes + cl
        rows = slice(j * BAND, (j + 1) * BAND)
        out_rows = rows if dilation == 1 else pl.ds(j * BAND * dilation + res, BAND, stride=dilation)
        for p in range(N_PAIRS):
            kl = slice(kv_lane[p], kv_lane[p] + LANES)
            if j == 0:
                k_prev, v_prev = kp_ref[cl, :, kl], vp_ref[cl, :, kl]
            else:
                k_prev, v_prev = kc_ref[cl, (j - 1) * BAND:j * BAND, kl], vc_ref[cl, (j - 1) * BAND:j * BAND, kl]
            k2 = jnp.concatenate([k_prev, kc_ref[cl, rows, kl]], axis=0)
            v2 = jnp.concatenate([v_prev, vc_ref[cl, rows, kl]], axis=0)
            q2 = _split_heads(q_ref[cl, rows, p * LANES:(p + 1) * LANES])
            s = lax.dot_general(q2, k2, (((1,), (1,)), ((), ())), preferred_element_type=_F32)
            s = s + bias_ref[p]
            if j == 0:
                s = jnp.where(jnp.logical_and(first, prev_cols), _NEG_INF, s)
            m = jnp.max(s, axis=1, keepdims=True)
            prob = jnp.exp(s - m).astype(_BF)
            r = jnp.dot(prob, jnp.concatenate([v2, ones], axis=1), preferred_element_type=_F32)
            pv, l = r[:, :LANES], r[:, LANES:]
            if has_sink:
                ha, hb = A_PAIR_HEADS[p]
                sink = jnp.where(top_rows, sink_ref[ha], sink_ref[hb])
                m2 = jnp.maximum(m, sink)
                a = jnp.exp(m - m2)
                o = pv * a / (l * a + jnp.exp(sink - m2))
            else:
                o = pv / l
            o_pair = jnp.where(left, o[:BAND], o[BAND:])
            if dilation == 1:
                o_ref[rows, p * LANES:(p + 1) * LANES] = o_pair.astype(o_ref.dtype)
            else:
                stage_ref[p, out_rows, :] = o_pair
            if want_stat:
                lse = m + jnp.log(l)
                st_ref[p, out_rows, :] = jnp.where(left, lse[:BAND], lse[BAND:])

    if dilation > 1:
        @pl.when(pl.program_id(2) == pl.num_programs(2) - 1)
        def _():
            for p in range(N_PAIRS):
                o_ref[:, p * LANES:(p + 1) * LANES] = stage_ref[p].astype(o_ref.dtype)


def _band_bias(pair_heads, dilation):
    slopes = _alibi_slopes(N_HEADS)
    qi = np.arange(BAND)[:, None]
    kj = np.arange(2 * BAND)[None, :]
    dist = qi + BAND - kj
    valid = (dist >= 0) & (dist <= BAND)
    out = np.empty((len(pair_heads), 2 * BAND, 2 * BAND), np.float32)
    for p, heads in enumerate(pair_heads):
        for half, h in enumerate(heads):
            bias = -np.float32(slopes[h]) * (dist * dilation).astype(np.float32)
            out[p, half * BAND:(half + 1) * BAND] = np.where(valid, bias, -np.inf)
    return jnp.asarray(out)


BAND_TOKENS = 4096
BAND_STEP_ROWS = 1024


def _band_attn(arr, q_col, k_col, v_col, kv_width, kv_lane, pair_heads, sinks, want_stat):
    bsz, dilation, n, _ = arr.shape
    seq = n * dilation
    tq = min(BAND_STEP_ROWS, n, BAND_TOKENS // dilation)
    sub = tq // BAND
    n_classes = min(dilation, BAND_STEP_ROWS // tq)
    assert q_col % MIX_W == 0 and k_col % kv_width == 0 and v_col % kv_width == 0 and dilation % n_classes == 0

    def cur(col, w):
        return pl.BlockSpec((None, n_classes, tq, w), lambda b, i, r: (b, r, i, col // w))

    def prev(col, w):
        return pl.BlockSpec((None, n_classes, BAND, w),
                            lambda b, i, r: (b, r, jnp.maximum(i * sub - 1, 0), col // w))

    in_specs = [cur(q_col, MIX_W), cur(k_col, kv_width), prev(k_col, kv_width), cur(v_col, kv_width),
                prev(v_col, kv_width), _const_spec((N_PAIRS, 2 * BAND, 2 * BAND))]
    args = [arr, arr, arr, arr, arr, _band_bias(pair_heads, dilation)]
    if sinks is not None:
        in_specs.append(pl.BlockSpec(memory_space=pltpu.SMEM))
        args.append(sinks)
    out_specs = [pl.BlockSpec((None, tq * dilation, MIX_W), lambda b, i, r: (b, i, 0))]
    out_shape = [jax.ShapeDtypeStruct((bsz, seq, MIX_W), _BF)]
    if want_stat:
        out_specs.append(pl.BlockSpec((None, N_PAIRS, tq * dilation, LANES), lambda b, i, r: (b, 0, i, 0)))
        out_shape.append(jax.ShapeDtypeStruct((bsz, N_PAIRS, seq, LANES), _F32))
    scratch = [pltpu.VMEM((N_PAIRS, tq * dilation, LANES), _F32)] if dilation > 1 else []
    res = pl.pallas_call(
        functools.partial(_band_attn_kernel, tq=tq, dilation=dilation, kv_lane=kv_lane, has_sink=sinks is not None,
                          want_stat=want_stat),
        grid=(bsz, n // tq, dilation // n_classes),
        in_specs=in_specs,
        out_specs=out_specs,
        out_shape=out_shape,
        scratch_shapes=scratch,
        compiler_params=_cparams(3),
        name=f"band_attn_d{dilation}",
    )(*args)
    o = res[0].reshape(bsz * seq, MIX_W)
    return (o, res[1]) if want_stat else o


def _cached_attn_stages(rows, pair0, refs, sink_ref, *, n_hist, t_new, q_pairs_of_kv, pair_heads, cache_row=None):
    q_ref, kn_ref, vn_ref, kc_ref, vc_ref, bias_ref, mult_ref, y_ref, ko_ref, vo_ref = refs
    left = _lane_is_left()
    new_lanes = lax.broadcasted_iota(jnp.int32, (1, LANES), 1) >= LANES - t_new
    top_rows = lax.broadcasted_iota(jnp.int32, (2 * t_new, 1), 0) < t_new
    zpad = jnp.zeros((LANES - t_new, LANES), _F32)
    mult = mult_ref[...]

    chains = []
    for b in rows:
        for kvp, q_pairs in enumerate(q_pairs_of_kv):
            heads = slice(2 * kvp, 2 * kvp + 2)
            lanes = slice(kvp * LANES, (kvp + 1) * LANES)
            ext = []
            for c_ref, n_ref, o_ref in ((kc_ref, kn_ref, ko_ref), (vc_ref, vn_ref, vo_ref)):
                old = c_ref[b if cache_row is None else cache_row, heads].reshape(LANES, n_hist)
                new = jnp.concatenate([zpad, n_ref[b, :, lanes]], axis=0).T
                rolled = pltpu.roll(old, n_hist - t_new, axis=1)
                tail = jnp.where(new_lanes, new, rolled[:, n_hist - LANES:])
                out = tail if n_hist == LANES else jnp.concatenate([rolled[:, :n_hist - LANES], tail], axis=1)
                o_ref[b, heads] = out.reshape(2, HEAD_DIM, n_hist)
                ext.append(jnp.concatenate([old.astype(_BF), new.astype(_BF)], axis=1))
            chains += [(b, p, ext[0], ext[1]) for p in q_pairs]

    scores = []
    for b, p, k_ext, _ in chains:
        q2 = _split_heads(q_ref[b, :, p * LANES:(p + 1) * LANES].astype(_BF))
        scores.append(jnp.dot(q2, k_ext, preferred_element_type=_F32) + bias_ref[pair0 + p])
    yield
    probs = []
    for s in scores:
        m = jnp.max(s, axis=1, keepdims=True)
        prob = (mult * jnp.exp(s - m)).astype(_BF)
        probs.append((m, prob, jnp.sum(prob.astype(_F32), axis=1, keepdims=True)))
    pvs = [lax.dot_general(prob, v_ext, (((1,), (1,)), ((), ())), preferred_element_type=_F32)
           for (_, prob, _), (_, _, _, v_ext) in zip(probs, chains)]
    yield
    for (b, p, _, _), (m, _, l), pv in zip(chains, probs, pvs):
        if sink_ref is not None:
            ha, hb = pair_heads[p]
            sink = jnp.where(top_rows, sink_ref[ha], sink_ref[hb])
            m2 = jnp.maximum(m, sink)
            a = jnp.exp(m - m2)
            o = pv * a / (l * a + jnp.exp(sink - m2))
        else:
            o = pv / l
        y_ref[b, :, p * LANES:(p + 1) * LANES] = jnp.where(left, o[:t_new], o[t_new:]).astype(y_ref.dtype)


def _sample_attn_kernel(*refs, has_sink, **statics):
    refs = list(refs)
    sink_ref = refs.pop(7) if has_sink else None
    for _ in _cached_attn_stages(range(refs[0].shape[0]), 0, refs, sink_ref, **statics):
        pass


def _sample_tables(pair_heads, branches, n_hist, t_new):
    slopes = _alibi_slopes(N_HEADS)
    key_pos = np.concatenate([np.arange(n_hist), n_hist + np.arange(LANES) - (LANES - t_new)])
    is_key = np.concatenate([np.ones(n_hist, bool), np.arange(LANES) >= LANES - t_new])
    delta = (n_hist + np.arange(t_new))[:, None] - key_pos[None, :]
    mult = np.zeros(delta.shape, np.float32)
    for window, dil in branches:
        mult += ((delta >= 0) & (delta % dil == 0) & (delta <= window) & is_key[None, :]).astype(np.float32)
    bias = np.empty((len(pair_heads), 2 * t_new, key_pos.size), np.float32)
    for p, heads in enumerate(pair_heads):
        for half, h in enumerate(heads):
            b = -np.float32(slopes[h]) * delta.astype(np.float32)
            bias[p, half * t_new:(half + 1) * t_new] = np.where(mult > 0, b, -np.inf)
    return jnp.asarray(bias), jnp.asarray(np.concatenate([mult, mult], axis=0))


UNIT_HEADS = 4


CACHE_SLOTS = 3


def _tail_cache_kernel(x_ref, ya_ref, yb_ref, wo_ref, bo_ref, g1_ref, b1_ref, w1_ref, w2_ref, g2_ref, b2_ref,
                       q_ref, kn_ref, vn_ref, kc_hbm, vc_hbm, bias_ref, mult_ref,
                       o_ref, y_ref, ko_ref, vo_ref, xb_ref, acc_ref, kbuf_ref, vbuf_ref, sem_ref,
                       *, units_per_row, **statics):
    c = pl.program_id(1)
    n_chunks = pl.num_programs(1)
    unit = pl.program_id(0) * n_chunks + c
    n_units = pl.num_programs(0) * n_chunks

    def fetch(u):
        slot = lax.rem(u, CACHE_SLOTS)
        heads = pl.ds(lax.rem(u, units_per_row) * UNIT_HEADS, UNIT_HEADS)
        return [pltpu.make_async_copy(hbm.at[u // units_per_row, heads], buf.at[slot], sem_ref.at[k, slot])
                for k, (hbm, buf) in enumerate(((kc_hbm, kbuf_ref), (vc_hbm, vbuf_ref)))]

    @pl.when(unit == 0)
    def _():
        for u in range(CACHE_SLOTS - 1):
            for cp in fetch(u):
                cp.start()

    @pl.when(unit + (CACHE_SLOTS - 1) < n_units)
    def _():
        for cp in fetch(unit + (CACHE_SLOTS - 1)):
            cp.start()

    for cp in fetch(unit):
        cp.wait()

    @pl.when(c == 0)
    def _():
        mix = jnp.dot(ya_ref[...], wo_ref[:MIX_W, :], preferred_element_type=_F32)
        mix = mix + jnp.dot(yb_ref[...], wo_ref[MIX_W:, :], preferred_element_type=_F32) + bo_ref[...]
        x1 = _layer_norm(DEEPNORM_ALPHA * x_ref[...] + mix, g1_ref[...], b1_ref[...])
        xb_ref[...] = x1.astype(_BF)
        acc_ref[...] = DEEPNORM_ALPHA * x1

    pair0 = (unit % units_per_row) * (UNIT_HEADS // 2)
    cache_refs = (q_ref, kn_ref, vn_ref, kbuf_ref, vbuf_ref, bias_ref, mult_ref, y_ref, ko_ref, vo_ref)
    stages = _cached_attn_stages((0,), pair0, cache_refs, None, cache_row=lax.rem(unit, CACHE_SLOTS), **statics)
    next(stages)
    h = jnp.dot(xb_ref[...], w1_ref[c], preferred_element_type=_F32)
    h = jnp.square(jnp.maximum(h, 0.0)).astype(_BF)
    next(stages)
    acc_ref[...] += jnp.dot(h, w2_ref[c], preferred_element_type=_F32)
    for _ in stages:
        pass

    @pl.when(c == n_chunks - 1)
    def _():
        o_ref[...] = _layer_norm(acc_ref[...], g2_ref[...], b2_ref[...])


def _even_tail_with_cache(x, ya, yb, tail_params, p3, q_col, k_col, v_col, pair_heads, branches, k_cache, v_cache,
                          tm, n_chunks):
    n, dm = x.shape
    bsz, t_new, _ = p3.shape
    _, n_hist, kvh, _ = k_cache.shape
    units_per_row = kvh // UNIT_HEADS
    unit_w = UNIT_HEADS * HEAD_DIM
    assert (n // tm) * n_chunks == bsz * units_per_row >= CACHE_SLOTS and kvh == N_HEADS
    wo, bo, g1, b1, w1, w2, g2, b2 = tail_params
    dh = w1.shape[1]
    w1c = jnp.transpose(w1.reshape(dm, n_chunks, dh // n_chunks), (1, 0, 2)).astype(_BF)
    w2c = w2.reshape(n_chunks, dh // n_chunks, dm)
    kt = jnp.transpose(k_cache, (0, 2, 3, 1))
    vt = jnp.transpose(v_cache, (0, 2, 3, 1))
    bias, mult = _sample_tables(pair_heads, branches, n_hist, t_new)

    row = lambda w: pl.BlockSpec((tm, w), lambda t, c: (t, 0))
    unit_of = lambda t, c: t * n_chunks + c
    new = lambda col: pl.BlockSpec(
        (1, t_new, unit_w),
        lambda t, c: (unit_of(t, c) // units_per_row, 0, col // unit_w + unit_of(t, c) % units_per_row))
    cache = pl.BlockSpec((1, UNIT_HEADS, HEAD_DIM, n_hist),
                         lambda t, c: (unit_of(t, c) // units_per_row, unit_of(t, c) % units_per_row, 0, 0))
    params = (wo, bo, g1, b1, w1c, w2c, g2, b2)
    out, y, ko, vo = pl.pallas_call(
        functools.partial(_tail_cache_kernel, units_per_row=units_per_row, n_hist=n_hist, t_new=t_new,
                          q_pairs_of_kv=tuple((p,) for p in range(UNIT_HEADS // 2)), pair_heads=pair_heads),
        grid=(n // tm, n_chunks),
        in_specs=[row(dm), row(MIX_W), row(MIX_W)] + [_resident_spec(p.shape) for p in params]
        + [new(q_col), new(k_col), new(v_col), pl.BlockSpec(memory_space=pl.ANY), pl.BlockSpec(memory_space=pl.ANY),
           _resident_spec(bias.shape), _resident_spec(mult.shape)],
        out_specs=[row(dm), new(0), cache, cache],
        out_shape=[jax.ShapeDtypeStruct((n, dm), _F32), jax.ShapeDtypeStruct((bsz, t_new, MIX_W), _BF),
                   jax.ShapeDtypeStruct(kt.shape, _F32), jax.ShapeDtypeStruct(vt.shape, _F32)],
        scratch_shapes=[pltpu.VMEM((tm, dm), _BF), pltpu.VMEM((tm, dm), _F32)]
        + [pltpu.VMEM((CACHE_SLOTS, UNIT_HEADS, HEAD_DIM, n_hist), _F32)] * 2
        + [pltpu.SemaphoreType.DMA((2, CACHE_SLOTS))],
        compiler_params=pltpu.CompilerParams(dimension_semantics=("arbitrary", "arbitrary"),
                                             vmem_limit_bytes=FUSED_VMEM_LIMIT),
        name="even_tail_with_cache",
    )(x, ya, yb, *params, p3, p3, p3, kt, vt, bias, mult)
    return out, y, jnp.transpose(ko, (0, 3, 1, 2)), jnp.transpose(vo, (0, 3, 1, 2))


def _sample_attn(p3, q_col, k_col, v_col, pair_heads, branches, k_cache, v_cache, sinks, bt):
    bsz, t_new, _ = p3.shape
    _, n_hist, kvh, _ = k_cache.shape
    kv_width = kvh * HEAD_DIM
    kt = jnp.transpose(k_cache, (0, 2, 3, 1))
    vt = jnp.transpose(v_cache, (0, 2, 3, 1))
    n_kv_pairs = kvh // 2
    q_pairs_of_kv = tuple(tuple(p for p in range(N_PAIRS) if p % n_kv_pairs == kvp) for kvp in range(n_kv_pairs))
    bias, mult = _sample_tables(pair_heads, branches, n_hist, t_new)
    new = lambda col, w: pl.BlockSpec((bt, t_new, w), lambda b: (b, 0, col // w))
    cache = pl.BlockSpec((bt, kvh, HEAD_DIM, n_hist), lambda b: (b, 0, 0, 0))
    in_specs = [new(q_col, MIX_W), new(k_col, kv_width), new(v_col, kv_width), cache, cache,
                _const_spec(bias.shape), _const_spec(mult.shape)]
    args = [p3, p3, p3, kt, vt, bias, mult]
    if sinks is not None:
        in_specs.append(pl.BlockSpec(memory_space=pltpu.SMEM))
        args.append(sinks)
    y, ko, vo = pl.pallas_call(
        functools.partial(_sample_attn_kernel, n_hist=n_hist, t_new=t_new, q_pairs_of_kv=q_pairs_of_kv,
                          pair_heads=pair_heads, has_sink=sinks is not None),
        grid=(bsz // bt,),
        in_specs=in_specs,
        out_specs=[pl.BlockSpec((bt, t_new, MIX_W), lambda b: (b, 0, 0)), cache, cache],
        out_shape=[jax.ShapeDtypeStruct((bsz, t_new, MIX_W), _BF),
                   jax.ShapeDtypeStruct(kt.shape, _F32), jax.ShapeDtypeStruct(vt.shape, _F32)],
        compiler_params=_cparams(1),
        name=f"sample_attn_{n_hist}",
    )(*args)
    return y, jnp.transpose(ko, (0, 3, 1, 2)), jnp.transpose(vo, (0, 3, 1, 2))


A_Q = N_HEADS * HEAD_DIM
A_KV = A_KV_HEADS * HEAD_DIM
E_Q, E_H, E_GB, E_GC, E_K, E_V = 0, 512, 1024, 1536, 2048, 2176
O_U, O_Q, O_K, O_V = 0, 512, 1024, 1536


def _prep_layer_weights(even_w_in, even_b_in, even_w_out, odd_w_in, odd_b_in, c_w_group):
    q_cols = np.concatenate([h * HEAD_DIM + np.arange(HEAD_DIM) for h in A_HEAD_ORDER])
    o1, o2, o3 = A_Q, A_Q + A_KV, A_Q + 2 * A_KV
    order = np.concatenate([q_cols, np.arange(o3, o3 + 3 * MIX_W), np.arange(o1, o3)])
    scale = np.ones((order.size,), np.float32)
    scale[:A_Q] = HEAD_DIM ** -0.5
    ew = (even_w_in[:, order] * scale).astype(_BF)
    eb = (even_b_in[order] * scale)[None, :]
    ewo = jnp.concatenate([even_w_out[q_cols], even_w_out[A_Q:]], axis=0).astype(_BF)
    oscale = np.ones((odd_w_in.shape[1],), np.float32)
    oscale[O_Q:O_K] = HEAD_DIM ** -0.5
    ow = (odd_w_in * oscale).astype(_BF)
    ob = (odd_b_in * oscale)[None, :]
    groups, gw, _ = c_w_group.shape
    wg = jnp.zeros((MIX_W, MIX_W), _F32)
    for g in range(groups):
        wg = wg.at[g * gw:(g + 1) * gw, g * gw:(g + 1) * gw].set(c_w_group[g])
    return ew, eb, ewo, ow, ob, wg.astype(_BF)


def _row(v):
    return v[None, :]


def _forward(xp, xs, caches, wts, tm, tm_proj, bt_attn, bt_shift, n_chunks):
    (even_w_in, even_b_in, a_sinks, b_conv_w, even_w_out, even_b_out, odd_w_in, odd_b_in, c_w_group, c_scale,
     odd_w_out, odd_b_out, mlp_w1, mlp_w2, ln1_g, ln1_b, ln2_g, ln2_b) = wts
    cache_a_k, cache_a_v, state_b_conv, state_c_pool, cache_d_k, cache_d_v = caches
    bsz, seq, dm = xp.shape
    bs, ts, _ = xs.shape
    n, ns = bsz * seq, bs * ts
    tms = min(tm, ns)
    ew, eb, ewo, ow, ob, wg = _prep_layer_weights(even_w_in[0], even_b_in[0], even_w_out[0], odd_w_in[0],
                                                  odd_b_in[0], c_w_group[0])
    owo = odd_w_out[0].astype(_BF)
    w1 = mlp_w1.astype(_BF)
    w2 = mlp_w2.astype(_BF)
    tails = [(wo, _row(bo), _row(ln1_g[i]), _row(ln1_b[i]), w1[i], w2[i], _row(ln2_g[i]), _row(ln2_b[i]))
             for i, (wo, bo) in enumerate(((ewo, even_b_out[0]), (owo, odd_b_out[0])))]
    a_kv_lane = (0,) * N_PAIRS
    d_kv_lane = tuple(p * LANES for p in range(N_PAIRS))

    xs = xs.reshape(ns, dm)
    pe3 = _proj(xs, ew, eb, _F32, tms).reshape(bs, ts, -1)
    ya, ak_s, av_s = _sample_attn(pe3, E_Q, E_K, E_V, A_PAIR_HEADS, ((A_WINDOW, 1),), cache_a_k[0], cache_a_v[0],
                                  a_sinks[0], bt_attn)
    c_hist = jnp.pad(state_b_conv[0], ((0, 0), (HALO - (CONV_WIDTH - 1), 0), (0, 0)))
    yb, ctail = _sample_conv(pe3, c_hist, b_conv_w[0], bt_shift)
    bc_s = ctail[:, -(CONV_WIDTH - 1):, :]
    xs = _even_tail(xs, ya.reshape(ns, MIX_W), yb.reshape(ns, MIX_W), tails[0], tms)
    po3s = _proj(xs, ow, ob, _F32, tms).reshape(bs, ts, -1)
    u_hist = jnp.pad(state_c_pool[0], ((0, 0), (HALO - (POOL_MAX - 1), 0), (0, 0)))
    pooled_s = _sample_pool(po3s, u_hist, bt_shift).reshape(ns, MIX_W)
    cp_s = jnp.concatenate([state_c_pool[0], po3s[:, :, O_U:O_Q]], axis=1)[:, -(POOL_MAX - 1):]

    x = xp.reshape(n, dm)
    qkv, yb, ctail = _proj_conv(xp, ew, eb, b_conv_w[0], tm_proj)
    k_col, v_col = MIX_W, MIX_W + A_KV
    ya = _band_attn(qkv[:, None], 0, k_col, v_col, LANES, a_kv_lane, A_PAIR_HEADS, a_sinks[0], False)
    n_keep = min(A_WINDOW, seq)
    a_k = qkv[:, seq - n_keep:, k_col:v_col].astype(_F32).reshape(bsz, n_keep, A_KV_HEADS, HEAD_DIM)
    a_v = qkv[:, seq - n_keep:, v_col:].astype(_F32).reshape(bsz, n_keep, A_KV_HEADS, HEAD_DIM)
    b_conv = ctail[:, -(CONV_WIDTH - 1):, :]
    fused_tail = tails[0][:4] + (mlp_w1[0],) + tails[0][5:]
    x, yd, dk_s, dv_s = _even_tail_with_cache(x, ya, yb.reshape(n, MIX_W), fused_tail, po3s, O_Q, O_K, O_V,
                                              D_PAIR_HEADS, D_BRANCHES, cache_d_k[0], cache_d_v[0], tm, n_chunks)
    xs = _sample_odd_tail(xs, pooled_s, wg, _row(c_scale[0]), yd.reshape(ns, MIX_W), tails[1], tms)

    dils = tuple(d for _, d in D_BRANCHES if d > 1)
    n_keep = min(D_BRANCHES[-1][0], seq)
    qkv, qkv4, qkv16, pooled, utail, k_keep, v_keep = _proj_dilated(x.reshape(bsz, seq, dm), ow, ob, tm, O_Q, dils,
                                                                    n_keep)
    k_col, v_col = O_K - O_Q, O_V - O_Q
    outs, stats = [], []
    for arr in (qkv[:, None], qkv4, qkv16):
        o, st = _band_attn(arr, 0, k_col, v_col, MIX_W, d_kv_lane, D_PAIR_HEADS, None, True)
        outs.append(o)
        stats.append(st)
    c_pool = utail[:, -(POOL_MAX - 1):, :]
    d_k = k_keep.reshape(bsz, n_keep, N_HEADS, HEAD_DIM)
    d_v = v_keep.reshape(bsz, n_keep, N_HEADS, HEAD_DIM)
    x = _odd_tail(x, pooled.reshape(n, MIX_W), wg, _row(c_scale[0]), outs, stats, tails[1], tm)
    return (x.reshape(bsz, seq, dm), xs.reshape(bs, ts, dm), a_k[None], a_v[None], b_conv[None], c_pool[None],
            d_k[None], d_v[None], ak_s[None], av_s[None], bc_s[None], cp_s[None], dk_s[None], dv_s[None])


def kernel(x_prompt, x_sample, cache_a_k, cache_a_v, state_b_conv, state_c_pool, cache_d_k, cache_d_v, even_w_in, even_b_in, a_sinks, b_conv_w, even_w_out, even_b_out, odd_w_in, odd_b_in, c_w_group, c_scale, odd_w_out, odd_b_out, mlp_w1, mlp_w2, ln1_g, ln1_b, ln2_g, ln2_b):
    wts = (even_w_in, even_b_in, a_sinks, b_conv_w, even_w_out, even_b_out, odd_w_in, odd_b_in, c_w_group, c_scale,
           odd_w_out, odd_b_out, mlp_w1, mlp_w2, ln1_g, ln1_b, ln2_g, ln2_b)
    caches = (cache_a_k, cache_a_v, state_b_conv, state_c_pool, cache_d_k, cache_d_v)
    return _forward(x_prompt, x_sample, caches, wts, tm=512, tm_proj=1024, bt_attn=8, bt_shift=32, n_chunks=4)
```

```python
import functools

import numpy as np
import jax
import jax.numpy as jnp
from jax import lax
from jax.experimental import pallas as pl
from jax.experimental.pallas import tpu as pltpu

HEAD_DIM = 64
N_HEADS = 8
A_KV_HEADS = 2
A_WINDOW = 128
D_BRANCHES = ((128, 1), (512, 4), (2048, 16))
CONV_WIDTH = 3
POOL_WINDOWS = (2, 4, 8, 16)
POOL_MAX = 16
DEPTH = 2
PAST_LEN = 16384
DEEPNORM_ALPHA = (2 * DEPTH) ** 0.25
LN_EPS = 1e-5

MIX_W = N_HEADS * HEAD_DIM
LANES = 128
N_PAIRS = MIX_W // LANES
BAND = 128
HALO = 16
EXT0 = 24
VMEM_LIMIT = 56 * 1024 * 1024
FUSED_VMEM_LIMIT = 62 * 1024 * 1024

A_HEAD_ORDER = (0, 4, 1, 5, 2, 6, 3, 7)
A_PAIR_HEADS = tuple((p, p + 4) for p in range(N_PAIRS))
D_PAIR_HEADS = tuple((2 * p, 2 * p + 1) for p in range(N_PAIRS))

_BF = jnp.bfloat16
_F32 = jnp.float32
_NEG_INF = float("-inf")


def _alibi_slopes(n_heads):
    return 2.0 ** (-8.0 * np.arange(1, n_heads + 1) / n_heads)


def _cparams(n_axes):
    return pltpu.CompilerParams(dimension_semantics=("arbitrary",) * n_axes, vmem_limit_bytes=VMEM_LIMIT)


def _const_spec(shape):
    nd = len(shape)
    return pl.BlockSpec(shape, lambda *_: (0,) * nd)


def _layer_norm(y, g, b):
    mu = jnp.mean(y, axis=-1, keepdims=True)
    yc = y - mu
    var = jnp.mean(yc * yc, axis=-1, keepdims=True)
    return yc * lax.rsqrt(var + LN_EPS) * g + b


def _proj_kernel(x_ref, w_ref, b_ref, o_ref, *, tn):
    x = x_ref[...].astype(_BF)
    for j in range(o_ref.shape[1] // tn):
        cols = slice(j * tn, (j + 1) * tn)
        acc = jnp.dot(x, w_ref[:, cols], preferred_element_type=_F32)
        o_ref[:, cols] = (acc + b_ref[:, cols]).astype(o_ref.dtype)


def _proj(x, w, b, out_dtype, tm):
    n, k = x.shape
    m = w.shape[1]
    return pl.pallas_call(
        functools.partial(_proj_kernel, tn=256),
        grid=(n // tm,),
        in_specs=[pl.BlockSpec((tm, k), lambda i: (i, 0)), _const_spec((k, m)), _const_spec((1, m))],
        out_specs=pl.BlockSpec((tm, m), lambda i: (i, 0)),
        out_shape=jax.ShapeDtypeStruct((n, m), out_dtype),
        compiler_params=_cparams(1),
        name="proj",
    )(x, w, b)


def _carried_history(ext_ref, tm):
    @pl.when(pl.program_id(1) == 0)
    def _():
        ext_ref[:, tm + EXT0 - HALO:tm + EXT0, :] = jnp.zeros((1, HALO, MIX_W), _F32)

    return ext_ref[:, tm + EXT0 - HALO:tm + EXT0, :]


def _proj_conv_kernel(x_ref, w_ref, b_ref, cw_ref, qkv_ref, yb_ref, ctail_ref, hg_ref, ext_ref, *, tn):
    tm = x_ref.shape[0]
    c_hist = _carried_history(ext_ref, tm)
    x = x_ref[...].astype(_BF)
    gates = slice(MIX_W, 4 * MIX_W)
    for j in range(w_ref.shape[1] // tn):
        lo = j * tn
        acc = jnp.dot(x, w_ref[:, lo:lo + tn], preferred_element_type=_F32) + b_ref[:, lo:lo + tn]
        if lo < gates.start:
            qkv_ref[:, lo:lo + tn] = acc.astype(qkv_ref.dtype)
        elif lo < gates.stop:
            hg_ref[:, lo - gates.start:lo - gates.start + tn] = acc
        else:
            qkv_ref[:, lo - 3 * MIX_W:lo - 3 * MIX_W + tn] = acc.astype(qkv_ref.dtype)
    h, gb, gc = (hg_ref[:, k * MIX_W:(k + 1) * MIX_W][None] for k in range(3))
    _conv_body(h, gb, gc, c_hist, cw_ref, ext_ref, yb_ref, ctail_ref)


def _proj_conv(x3, w, b, conv_w, tm):
    bsz, seq, k = x3.shape
    m = w.shape[1]
    qkv_w = m - 3 * MIX_W
    return pl.pallas_call(
        functools.partial(_proj_conv_kernel, tn=256),
        grid=(bsz, seq // tm),
        in_specs=[pl.BlockSpec((None, tm, k), lambda bi, i: (bi, i, 0)), _const_spec((k, m)), _const_spec((1, m)),
                  _const_spec(conv_w.shape)],
        out_specs=[pl.BlockSpec((None, tm, qkv_w), lambda bi, i: (bi, i, 0)),
                   pl.BlockSpec((1, tm, MIX_W), lambda bi, i: (bi, i, 0)),
                   pl.BlockSpec((1, 8, MIX_W), lambda bi, i: (bi, i, 0))],
        out_shape=[jax.ShapeDtypeStruct((bsz, seq, qkv_w), _BF), jax.ShapeDtypeStruct((bsz, seq, MIX_W), _BF),
                   jax.ShapeDtypeStruct((bsz, (seq // tm) * 8, MIX_W), _F32)],
        scratch_shapes=[pltpu.VMEM((tm, 3 * MIX_W), _F32), pltpu.VMEM((1, EXT0 + tm, MIX_W), _F32)],
        compiler_params=_cparams(2),
        name="proj_conv",
    )(x3, w, b, conv_w)


def _proj_dilated_kernel(x_ref, w_ref, b_ref, o_ref, d4_ref, d16_ref, pooled_ref, utail_ref, kf_ref, vf_ref,
                         stage_ref, stage4_ref, u_ref, ext_ref, s2_ref, s4_ref, s8_ref, *, tn, first_col):
    tm = x_ref.shape[0]
    u_hist = _carried_history(ext_ref, tm)
    x = x_ref[...].astype(_BF)
    for j in range(w_ref.shape[1] // tn):
        lo = j * tn
        acc = jnp.dot(x, w_ref[:, lo:lo + tn], preferred_element_type=_F32) + b_ref[:, lo:lo + tn]
        if lo < first_col:
            u_ref[:, lo:lo + tn] = acc
        else:
            rel = lo - first_col
            o_ref[:, rel:rel + tn] = acc.astype(o_ref.dtype)
            for h in range(tn // LANES):
                stage_ref[rel // LANES + h] = acc[:, h * LANES:(h + 1) * LANES]
            if rel >= MIX_W:
                f_ref = kf_ref if rel < 2 * MIX_W else vf_ref
                f_ref[:, rel % MIX_W:rel % MIX_W + tn] = acc
    _pool_body(u_ref[...][None], u_hist, pl.program_id(1) * tm, ext_ref, s2_ref, s4_ref, s8_ref, pooled_ref)
    utail_ref[...] = ext_ref[:, tm + EXT0 - HALO:tm + EXT0, :]
    q4, q16 = tm // 4, tm // 16
    for s in range(stage_ref.shape[0]):
        lanes = slice(s * LANES, (s + 1) * LANES)
        for r in range(4):
            rows = stage_ref[s, pl.ds(r, q4, stride=4), :]
            stage4_ref[s, r * q4:(r + 1) * q4, :] = rows
            d4_ref[r, :, lanes] = rows.astype(d4_ref.dtype)
        for r in range(4):
            for k in range(4):
                d16_ref[r + 4 * k, :, lanes] = stage4_ref[s, pl.ds(r * q4 + k, q16, stride=4), :].astype(d16_ref.dtype)


def _proj_dilated(x3, w, b, tm, first_col, dilations, n_keep):
    bsz, seq, k = x3.shape
    m = w.shape[1]
    wd = m - first_col
    assert first_col == MIX_W and wd == 3 * MIX_W and tuple(dilations) == (4, 16) and tm % 256 == 0
    assert n_keep % tm == 0 and seq % tm == 0
    skipped = (seq - n_keep) // tm
    out_specs = [pl.BlockSpec((None, tm, wd), lambda bi, i: (bi, i, 0))]
    out_shape = [jax.ShapeDtypeStruct((bsz, seq, wd), _BF)]
    for d in dilations:
        out_specs.append(pl.BlockSpec((None, d, tm // d, wd), lambda bi, i: (bi, 0, i, 0)))
        out_shape.append(jax.ShapeDtypeStruct((bsz, d, seq // d, wd), _BF))
    out_specs += [pl.BlockSpec((1, tm, MIX_W), lambda bi, i: (bi, i, 0)),
                  pl.BlockSpec((1, HALO, MIX_W), lambda bi, i: (bi, i, 0))]
    out_shape += [jax.ShapeDtypeStruct((bsz, seq, MIX_W), _BF),
                  jax.ShapeDtypeStruct((bsz, (seq // tm) * HALO, MIX_W), _F32)]
    out_specs += [pl.BlockSpec((None, tm, MIX_W), lambda bi, i: (bi, jnp.maximum(i - skipped, 0), 0))] * 2
    out_shape += [jax.ShapeDtypeStruct((bsz, n_keep, MIX_W), _F32)] * 2
    shift = pltpu.VMEM((1, EXT0 + tm, MIX_W), _F32)
    return pl.pallas_call(
        functools.partial(_proj_dilated_kernel, tn=256, first_col=first_col),
        grid=(bsz, seq // tm),
        in_specs=[pl.BlockSpec((None, tm, k), lambda bi, i: (bi, i, 0)), _const_spec((k, m)), _const_spec((1, m))],
        out_specs=out_specs,
        out_shape=out_shape,
        scratch_shapes=[pltpu.VMEM((wd // LANES, tm, LANES), _F32)] * 2 + [pltpu.VMEM((tm, MIX_W), _F32)] + [shift] * 4,
        compiler_params=_cparams(2),
        name="proj_dilated",
    )(x3, w, b)


MLP_CHUNK = 512


def _lane_is_left():
    return lax.broadcasted_iota(jnp.int32, (1, LANES), 1) < HEAD_DIM


def _layer_tail(x_ref, left, right, tail_refs):
    wo_ref, bo_ref, g1_ref, b1_ref, w1_ref, w2_ref, g2_ref, b2_ref, o_ref = tail_refs
    mix = jnp.dot(left, wo_ref[:MIX_W, :], preferred_element_type=_F32)
    mix = mix + jnp.dot(right, wo_ref[MIX_W:, :], preferred_element_type=_F32) + bo_ref[...]
    x = _layer_norm(DEEPNORM_ALPHA * x_ref[...] + mix, g1_ref[...], b1_ref[...])
    xb = x.astype(_BF)
    acc = jnp.zeros(x.shape, _F32)
    for c in range(w1_ref.shape[1] // MLP_CHUNK):
        cols = slice(c * MLP_CHUNK, (c + 1) * MLP_CHUNK)
        h = jnp.dot(xb, w1_ref[:, cols], preferred_element_type=_F32)
        h = jnp.square(jnp.maximum(h, 0.0)).astype(_BF)
        acc = acc + jnp.dot(h, w2_ref[cols, :], preferred_element_type=_F32)
    o_ref[...] = _layer_norm(DEEPNORM_ALPHA * x + acc, g2_ref[...], b2_ref[...])


def _even_tail_kernel(x_ref, ya_ref, yb_ref, *tail_refs):
    _layer_tail(x_ref, ya_ref[...], yb_ref[...], tail_refs)


def _group_c(pooled_ref, wg_ref, scale_ref):
    return (jnp.dot(pooled_ref[...], wg_ref[...], preferred_element_type=_F32) * scale_ref[...]).astype(_BF)


def _sample_odd_tail_kernel(x_ref, pooled_ref, wg_ref, scale_ref, yd_ref, *tail_refs):
    _layer_tail(x_ref, _group_c(pooled_ref, wg_ref, scale_ref), yd_ref[...], tail_refs)


def _odd_tail_kernel(x_ref, pooled_ref, wg_ref, scale_ref, o1_ref, o2_ref, o3_ref, s1_ref, s2_ref, s3_ref,
                     *tail_refs):
    yc = _group_c(pooled_ref, wg_ref, scale_ref)
    tiles = []
    for p in range(N_PAIRS):
        lanes = slice(p * LANES, (p + 1) * LANES)
        lses = (s1_ref[p], s2_ref[p], s3_ref[p])
        top = jnp.maximum(jnp.maximum(lses[0], lses[1]), lses[2])
        es = [jnp.exp(s - top) for s in lses]
        num = es[0] * o1_ref[:, lanes].astype(_F32)
        num = num + es[1] * o2_ref[:, lanes].astype(_F32)
        num = num + es[2] * o3_ref[:, lanes].astype(_F32)
        tiles.append(num / (es[0] + es[1] + es[2]))
    yd = jnp.concatenate(tiles, axis=1)
    _layer_tail(x_ref, yc, yd.astype(_BF), tail_refs)


def _row_spec(tm, width):
    return pl.BlockSpec((tm, width), lambda i: (i, 0))


def _resident_spec(shape):
    nd = len(shape)
    return pl.BlockSpec(shape, lambda *_: (0,) * nd, pipeline_mode=pl.Buffered(1))


def _tail_call(kernel_fn, name, x, mixer_args, mixer_specs, tail_params, tm):
    n, dm = x.shape
    return pl.pallas_call(
        kernel_fn,
        grid=(n // tm,),
        in_specs=[_row_spec(tm, dm)] + mixer_specs + [_resident_spec(p.shape) for p in tail_params],
        out_specs=_row_spec(tm, dm),
        out_shape=jax.ShapeDtypeStruct((n, dm), _F32),
        compiler_params=_cparams(1),
        name=name,
    )(x, *mixer_args, *tail_params)


def _even_tail(x, ya, yb, tail_params, tm):
    return _tail_call(_even_tail_kernel, "even_tail", x, [ya, yb], [_row_spec(tm, MIX_W)] * 2, tail_params, tm)


def _odd_tail(x, pooled, wg, scale, outs, stats, tail_params, tm):
    tiles_per_seq = stats[0].shape[2] // tm
    stat_spec = pl.BlockSpec((None, N_PAIRS, tm, LANES), lambda i: (i // tiles_per_seq, 0, i % tiles_per_seq, 0))
    specs = ([_row_spec(tm, MIX_W), _resident_spec(wg.shape), _resident_spec(scale.shape)]
             + [_row_spec(tm, MIX_W)] * 3 + [stat_spec] * 3)
    return _tail_call(_odd_tail_kernel, "odd_tail", x, [pooled, wg, scale, *outs, *stats], specs, tail_params, tm)


def _sample_odd_tail(x, pooled, wg, scale, yd, tail_params, tm):
    specs = [_row_spec(tm, MIX_W), _resident_spec(wg.shape), _resident_spec(scale.shape), _row_spec(tm, MIX_W)]
    return _tail_call(_sample_odd_tail_kernel, "sample_odd_tail", x, [pooled, wg, scale, yd], specs, tail_params, tm)


def _fill_ext(ext_ref, hist, cur, t):
    nb = ext_ref.shape[0]
    ext_ref[:, 0:8, :] = jnp.zeros((nb, 8, MIX_W), _F32)
    ext_ref[:, 8:EXT0, :] = hist
    ext_ref[:, EXT0:EXT0 + t, :] = cur


def _conv_body(h, gb, gc, c_hist, w_ref, ext_ref, yb_ref, ctail_ref):
    t = h.shape[1]
    c = gc * h
    _fill_ext(ext_ref, c_hist, c, t)
    conv = ext_ref[:, EXT0 - 2:EXT0 - 2 + t, :] * w_ref[0:1, :]
    conv = conv + ext_ref[:, EXT0 - 1:EXT0 - 1 + t, :] * w_ref[1:2, :]
    conv = conv + c * w_ref[2:3, :]
    yb_ref[...] = (gb * conv).astype(yb_ref.dtype)
    ctail_ref[...] = ext_ref[:, EXT0 + t - 8:EXT0 + t, :]


def _sample_conv_kernel(h_ref, gb_ref, gc_ref, hist_ref, w_ref, yb_ref, ctail_ref, ext_ref):
    _conv_body(h_ref[...], gb_ref[...], gc_ref[...], hist_ref[...], w_ref, ext_ref, yb_ref, ctail_ref)


def _pool_body(u, hist, pos0, ext_ref, s2_ref, s4_ref, s8_ref, out_ref):
    nb, t, _ = u.shape
    _fill_ext(ext_ref, hist, u, t)
    hi = EXT0 + t
    zeros8 = jnp.zeros((nb, 8, MIX_W), _F32)
    s2_ref[:, 0:8, :] = zeros8
    s4_ref[:, 0:8, :] = zeros8
    s8_ref[:, 0:8, :] = zeros8
    s2_ref[:, 8:hi, :] = ext_ref[:, 8:hi, :] + ext_ref[:, 7:hi - 1, :]
    s4_ref[:, 8:hi, :] = s2_ref[:, 8:hi, :] + s2_ref[:, 6:hi - 2, :]
    s8_ref[:, 8:hi, :] = s4_ref[:, 8:hi, :] + s4_ref[:, 4:hi - 4, :]
    sums = (
        s2_ref[:, EXT0:hi, 0:LANES],
        s4_ref[:, EXT0:hi, LANES:2 * LANES],
        s8_ref[:, EXT0:hi, 2 * LANES:3 * LANES],
        s8_ref[:, EXT0:hi, 3 * LANES:] + s8_ref[:, EXT0 - 8:hi - 8, 3 * LANES:],
    )
    pos = (pos0 + lax.broadcasted_iota(jnp.int32, (1, t, LANES), 1) + 1).astype(_F32)
    tiles = []
    for g, (w, s) in enumerate(zip(POOL_WINDOWS, sums)):
        cnt = jnp.minimum(pos, float(w))
        tiles.append(s / cnt - u[:, :, g * LANES:(g + 1) * LANES])
    out_ref[...] = jnp.concatenate(tiles, axis=2).astype(out_ref.dtype)


def _sample_pool_kernel(u_ref, hist_ref, out_ref, ext_ref, s2_ref, s4_ref, s8_ref):
    _pool_body(u_ref[...], hist_ref[...], PAST_LEN, ext_ref, s2_ref, s4_ref, s8_ref, out_ref)


def _sample_conv(p3, hist, conv_w, bt):
    bsz, t, _ = p3.shape
    col = lambda c: pl.BlockSpec((bt, t, MIX_W), lambda b: (b, 0, c))
    return pl.pallas_call(
        _sample_conv_kernel,
        grid=(bsz // bt,),
        in_specs=[col(1), col(2), col(3), pl.BlockSpec((bt, HALO, MIX_W), lambda b: (b, 0, 0)),
                  _const_spec(conv_w.shape)],
        out_specs=[pl.BlockSpec((bt, t, MIX_W), lambda b: (b, 0, 0)),
                   pl.BlockSpec((bt, 8, MIX_W), lambda b: (b, 0, 0))],
        out_shape=[jax.ShapeDtypeStruct((bsz, t, MIX_W), _BF), jax.ShapeDtypeStruct((bsz, 8, MIX_W), _F32)],
        scratch_shapes=[pltpu.VMEM((bt, EXT0 + t, MIX_W), _F32)],
        compiler_params=_cparams(1),
        name="sample_conv",
    )(p3, p3, p3, hist, conv_w)


def _sample_pool(p3, hist, bt):
    bsz, t, _ = p3.shape
    scratch = pltpu.VMEM((bt, EXT0 + t, MIX_W), _F32)
    return pl.pallas_call(
        _sample_pool_kernel,
        grid=(bsz // bt,),
        in_specs=[pl.BlockSpec((bt, t, MIX_W), lambda b: (b, 0, 0)),
                  pl.BlockSpec((bt, HALO, MIX_W), lambda b: (b, 0, 0))],
        out_specs=pl.BlockSpec((bt, t, MIX_W), lambda b: (b, 0, 0)),
        out_shape=jax.ShapeDtypeStruct((bsz, t, MIX_W), _BF),
        scratch_shapes=[scratch] * 4,
        compiler_params=_cparams(1),
        name="sample_pool",
    )(p3, hist)


def _split_heads(q_pair):
    left = _lane_is_left()
    zero = jnp.zeros_like(q_pair)
    return jnp.concatenate([jnp.where(left, q_pair, zero), jnp.where(left, zero, q_pair)], axis=0)


def _band_attn_kernel(*refs, tq, dilation, kv_lane, has_sink, want_stat):
    refs = list(refs)
    q_ref, kc_ref, kp_ref, vc_ref, vp_ref, bias_ref = refs[:6]
    rest = refs[6:]
    sink_ref = rest.pop(0) if has_sink else None
    o_ref = rest.pop(0)
    st_ref = rest.pop(0) if want_stat else None
    stage_ref = rest.pop(0) if dilation > 1 else None

    first = pl.program_id(1) == 0
    n_classes = q_ref.shape[0]
    left = _lane_is_left()
    prev_cols = lax.broadcasted_iota(jnp.int32, (1, 2 * BAND), 1) < BAND
    top_rows = lax.broadcasted_iota(jnp.int32, (2 * BAND, 1), 0) < BAND
    ones = jnp.ones((2 * BAND, LANES), _BF)

    for cl, j in ((cl, j) for cl in range(n_classes) for j in range(tq // BAND)):
        res = pl.program_id(2) * n_classes + cl
        rows = slice(j * BAND, (j + 1) * BAND)
        out_rows = rows if dilation == 1 else pl.ds(j * BAND * dilation + res, BAND, stride=dilation)
        for p in range(N_PAIRS):
            kl = slice(kv_lane[p], kv_lane[p] + LANES)
            if j == 0:
                k_prev, v_prev = kp_ref[cl, :, kl], vp_ref[cl, :, kl]
            else:
                k_prev, v_prev = kc_ref[cl, (j - 1) * BAND:j * BAND, kl], vc_ref[cl, (j - 1) * BAND:j * BAND, kl]
            k2 = jnp.concatenate([k_prev, kc_ref[cl, rows, kl]], axis=0)
            v2 = jnp.concatenate([v_prev, vc_ref[cl, rows, kl]], axis=0)
            q2 = _split_heads(q_ref[cl, rows, p * LANES:(p + 1) * LANES])
            s = lax.dot_general(q2, k2, (((1,), (1,)), ((), ())), preferred_element_type=_F32)
            s = s + bias_ref[p]
            if j == 0:
                s = jnp.where(jnp.logical_and(first, prev_cols), _NEG_INF, s)
            m = jnp.max(s, axis=1, keepdims=True)
            prob = jnp.exp(s - m).astype(_BF)
            r = jnp.dot(prob, jnp.concatenate([v2, ones], axis=1), preferred_element_type=_F32)
            pv, l = r[:, :LANES], r[:, LANES:]
            if has_sink:
                ha, hb = A_PAIR_HEADS[p]
                sink = jnp.where(top_rows, sink_ref[ha], sink_ref[hb])
                m2 = jnp.maximum(m, sink)
                a = jnp.exp(m - m2)
                o = pv * a / (l * a + jnp.exp(sink - m2))
            else:
                o = pv / l
            o_pair = jnp.where(left, o[:BAND], o[BAND:])
            if dilation == 1:
                o_ref[rows, p * LANES:(p + 1) * LANES] = o_pair.astype(o_ref.dtype)
            else:
                stage_ref[p, out_rows, :] = o_pair
            if want_stat:
                lse = m + jnp.log(l)
                st_ref[p, out_rows, :] = jnp.where(left, lse[:BAND], lse[BAND:])

    if dilation > 1:
        @pl.when(pl.program_id(2) == pl.num_programs(2) - 1)
        def _():
            for p in range(N_PAIRS):
                o_ref[:, p * LANES:(p + 1) * LANES] = stage_ref[p].astype(o_ref.dtype)


def _band_bias(pair_heads, dilation):
    slopes = _alibi_slopes(N_HEADS)
    qi = np.arange(BAND)[:, None]
    kj = np.arange(2 * BAND)[None, :]
    dist = qi + BAND - kj
    valid = (dist >= 0) & (dist <= BAND)
    out = np.empty((len(pair_heads), 2 * BAND, 2 * BAND), np.float32)
    for p, heads in enumerate(pair_heads):
        for half, h in enumerate(heads):
            bias = -np.float32(slopes[h]) * (dist * dilation).astype(np.float32)
            out[p, half * BAND:(half + 1) * BAND] = np.where(valid, bias, -np.inf)
    return jnp.asarray(out)


BAND_TOKENS = 4096
BAND_STEP_ROWS = 1024


def _band_attn(arr, q_col, k_col, v_col, kv_width, kv_lane, pair_heads, sinks, want_stat):
    bsz, dilation, n, _ = arr.shape
    seq = n * dilation
    tq = min(BAND_STEP_ROWS, n, BAND_TOKENS // dilation)
    sub = tq // BAND
    n_classes = min(dilation, BAND_STEP_ROWS // tq)
    assert q_col % MIX_W == 0 and k_col % kv_width == 0 and v_col % kv_width == 0 and dilation % n_classes == 0

    def cur(col, w):
        return pl.BlockSpec((None, n_classes, tq, w), lambda b, i, r: (b, r, i, col // w))

    def prev(col, w):
        return pl.BlockSpec((None, n_classes, BAND, w),
                            lambda b, i, r: (b, r, jnp.maximum(i * sub - 1, 0), col // w))

    in_specs = [cur(q_col, MIX_W), cur(k_col, kv_width), prev(k_col, kv_width), cur(v_col, kv_width),
                prev(v_col, kv_width), _const_spec((N_PAIRS, 2 * BAND, 2 * BAND))]
    args = [arr, arr, arr, arr, arr, _band_bias(pair_heads, dilation)]
    if sinks is not None:
        in_specs.append(pl.BlockSpec(memory_space=pltpu.SMEM))
        args.append(sinks)
    out_specs = [pl.BlockSpec((None, tq * dilation, MIX_W), lambda b, i, r: (b, i, 0))]
    out_shape = [jax.ShapeDtypeStruct((bsz, seq, MIX_W), _BF)]
    if want_stat:
        out_specs.append(pl.BlockSpec((None, N_PAIRS, tq * dilation, LANES), lambda b, i, r: (b, 0, i, 0)))
        out_shape.append(jax.ShapeDtypeStruct((bsz, N_PAIRS, seq, LANES), _F32))
    scratch = [pltpu.VMEM((N_PAIRS, tq * dilation, LANES), _F32)] if dilation > 1 else []
    res = pl.pallas_call(
        functools.partial(_band_attn_kernel, tq=tq, dilation=dilation, kv_lane=kv_lane, has_sink=sinks is not None,
                          want_stat=want_stat),
        grid=(bsz, n // tq, dilation // n_classes),
        in_specs=in_specs,
        out_specs=out_specs,
        out_shape=out_shape,
        scratch_shapes=scratch,
        compiler_params=_cparams(3),
        name=f"band_attn_d{dilation}",
    )(*args)
    o = res[0].reshape(bsz * seq, MIX_W)
    return (o, res[1]) if want_stat else o


def _cached_attn_stages(rows, pair0, refs, sink_ref, *, n_hist, t_new, q_pairs_of_kv, pair_heads, cache_row=None):
    q_ref, kn_ref, vn_ref, kc_ref, vc_ref, bias_ref, mult_ref, y_ref, ko_ref, vo_ref = refs
    left = _lane_is_left()
    new_lanes = lax.broadcasted_iota(jnp.int32, (1, LANES), 1) >= LANES - t_new
    top_rows = lax.broadcasted_iota(jnp.int32, (2 * t_new, 1), 0) < t_new
    zpad = jnp.zeros((LANES - t_new, LANES), _F32)
    mult = mult_ref[...]

    chains = []
    for b in rows:
        for kvp, q_pairs in enumerate(q_pairs_of_kv):
            heads = slice(2 * kvp, 2 * kvp + 2)
            lanes = slice(kvp * LANES, (kvp + 1) * LANES)
            ext = []
            for c_ref, n_ref, o_ref in ((kc_ref, kn_ref, ko_ref), (vc_ref, vn_ref, vo_ref)):
                old = c_ref[b if cache_row is None else cache_row, heads].reshape(LANES, n_hist)
                new = jnp.concatenate([zpad, n_ref[b, :, lanes]], axis=0).T
                rolled = pltpu.roll(old, n_hist - t_new, axis=1)
                tail = jnp.where(new_lanes, new, rolled[:, n_hist - LANES:])
                out = tail if n_hist == LANES else jnp.concatenate([rolled[:, :n_hist - LANES], tail], axis=1)
                o_ref[b, heads] = out.reshape(2, HEAD_DIM, n_hist)
                ext.append(jnp.concatenate([old.astype(_BF), new.astype(_BF)], axis=1))
            chains += [(b, p, ext[0], ext[1]) for p in q_pairs]

    scores = []
    for b, p, k_ext, _ in chains:
        q2 = _split_heads(q_ref[b, :, p * LANES:(p + 1) * LANES].astype(_BF))
        scores.append(jnp.dot(q2, k_ext, preferred_element_type=_F32) + bias_ref[pair0 + p])
    yield
    probs = []
    for s in scores:
        m = jnp.max(s, axis=1, keepdims=True)
        prob = (mult * jnp.exp(s - m)).astype(_BF)
        probs.append((m, prob, jnp.sum(prob.astype(_F32), axis=1, keepdims=True)))
    pvs = [lax.dot_general(prob, v_ext, (((1,), (1,)), ((), ())), preferred_element_type=_F32)
           for (_, prob, _), (_, _, _, v_ext) in zip(probs, chains)]
    yield
    for (b, p, _, _), (m, _, l), pv in zip(chains, probs, pvs):
        if sink_ref is not None:
            ha, hb = pair_heads[p]
            sink = jnp.where(top_rows, sink_ref[ha], sink_ref[hb])
            m2 = jnp.maximum(m, sink)
            a = jnp.exp(m - m2)
            o = pv * a / (l * a + jnp.exp(sink - m2))
        else:
            o = pv / l
        y_ref[b, :, p * LANES:(p + 1) * LANES] = jnp.where(left, o[:t_new], o[t_new:]).astype(y_ref.dtype)


def _sample_attn_kernel(*refs, has_sink, **statics):
    refs = list(refs)
    sink_ref = refs.pop(7) if has_sink else None
    for _ in _cached_attn_stages(range(refs[0].shape[0]), 0, refs, sink_ref, **statics):
        pass


def _sample_tables(pair_heads, branches, n_hist, t_new):
    slopes = _alibi_slopes(N_HEADS)
    key_pos = np.concatenate([np.arange(n_hist), n_hist + np.arange(LANES) - (LANES - t_new)])
    is_key = np.concatenate([np.ones(n_hist, bool), np.arange(LANES) >= LANES - t_new])
    delta = (n_hist + np.arange(t_new))[:, None] - key_pos[None, :]
    mult = np.zeros(delta.shape, np.float32)
    for window, dil in branches:
        mult += ((delta >= 0) & (delta % dil == 0) & (delta <= window) & is_key[None, :]).astype(np.float32)
    bias = np.empty((len(pair_heads), 2 * t_new, key_pos.size), np.float32)
    for p, heads in enumerate(pair_heads):
        for half, h in enumerate(heads):
            b = -np.float32(slopes[h]) * delta.astype(np.float32)
            bias[p, half * t_new:(half + 1) * t_new] = np.where(mult > 0, b, -np.inf)
    return jnp.asarray(bias), jnp.asarray(np.concatenate([mult, mult], axis=0))


UNIT_HEADS = 4


CACHE_SLOTS = 3


def _tail_cache_kernel(x_ref, ya_ref, yb_ref, wo_ref, bo_ref, g1_ref, b1_ref, w1_ref, w2_ref, g2_ref, b2_ref,
                       q_ref, kn_ref, vn_ref, kc_hbm, vc_hbm, bias_ref, mult_ref,
                       o_ref, y_ref, ko_ref, vo_ref, xb_ref, acc_ref, kbuf_ref, vbuf_ref, sem_ref,
                       *, units_per_row, **statics):
    c = pl.program_id(1)
    n_chunks = pl.num_programs(1)
    unit = pl.program_id(0) * n_chunks + c
    n_units = pl.num_programs(0) * n_chunks

    def fetch(u):
        slot = lax.rem(u, CACHE_SLOTS)
        heads = pl.ds(lax.rem(u, units_per_row) * UNIT_HEADS, UNIT_HEADS)
        return [pltpu.make_async_copy(hbm.at[u // units_per_row, heads], buf.at[slot], sem_ref.at[k, slot])
                for k, (hbm, buf) in enumerate(((kc_hbm, kbuf_ref), (vc_hbm, vbuf_ref)))]

    @pl.when(unit == 0)
    def _():
        for u in range(CACHE_SLOTS - 1):
            for cp in fetch(u):
                cp.start()

    @pl.when(unit + (CACHE_SLOTS - 1) < n_units)
    def _():
        for cp in fetch(unit + (CACHE_SLOTS - 1)):
            cp.start()

    for cp in fetch(unit):
        cp.wait()

    @pl.when(c == 0)
    def _():
        mix = jnp.dot(ya_ref[...], wo_ref[:MIX_W, :], preferred_element_type=_F32)
        mix = mix + jnp.dot(yb_ref[...], wo_ref[MIX_W:, :], preferred_element_type=_F32) + bo_ref[...]
        x1 = _layer_norm(DEEPNORM_ALPHA * x_ref[...] + mix, g1_ref[...], b1_ref[...])
        xb_ref[...] = x1.astype(_BF)
        acc_ref[...] = DEEPNORM_ALPHA * x1

    pair0 = (unit % units_per_row) * (UNIT_HEADS // 2)
    cache_refs = (q_ref, kn_ref, vn_ref, kbuf_ref, vbuf_ref, bias_ref, mult_ref, y_ref, ko_ref, vo_ref)
    stages = _cached_attn_stages((0,), pair0, cache_refs, None, cache_row=lax.rem(unit, CACHE_SLOTS), **statics)
    next(stages)
    h = jnp.dot(xb_ref[...], w1_ref[c], preferred_element_type=_F32)
    h = jnp.square(jnp.maximum(h, 0.0)).astype(_BF)
    next(stages)
    acc_ref[...] += jnp.dot(h, w2_ref[c], preferred_element_type=_F32)
    for _ in stages:
        pass

    @pl.when(c == n_chunks - 1)
    def _():
        o_ref[...] = _layer_norm(acc_ref[...], g2_ref[...], b2_ref[...])


def _even_tail_with_cache(x, ya, yb, tail_params, p3, q_col, k_col, v_col, pair_heads, branches, k_cache, v_cache,
                          tm, n_chunks):
    n, dm = x.shape
    bsz, t_new, _ = p3.shape
    _, n_hist, kvh, _ = k_cache.shape
    units_per_row = kvh // UNIT_HEADS
    unit_w = UNIT_HEADS * HEAD_DIM
    assert (n // tm) * n_chunks == bsz * units_per_row >= CACHE_SLOTS and kvh == N_HEADS
    wo, bo, g1, b1, w1, w2, g2, b2 = tail_params
    dh = w1.shape[1]
    w1c = jnp.transpose(w1.reshape(dm, n_chunks, dh // n_chunks), (1, 0, 2)).astype(_BF)
    w2c = w2.reshape(n_chunks, dh // n_chunks, dm)
    kt = jnp.transpose(k_cache, (0, 2, 3, 1))
    vt = jnp.transpose(v_cache, (0, 2, 3, 1))
    bias, mult = _sample_tables(pair_heads, branches, n_hist, t_new)

    row = lambda w: pl.BlockSpec((tm, w), lambda t, c: (t, 0))
    unit_of = lambda t, c: t * n_chunks + c
    new = lambda col: pl.BlockSpec(
        (1, t_new, unit_w),
        lambda t, c: (unit_of(t, c) // units_per_row, 0, col // unit_w + unit_of(t, c) % units_per_row))
    cache = pl.BlockSpec((1, UNIT_HEADS, HEAD_DIM, n_hist),
                         lambda t, c: (unit_of(t, c) // units_per_row, unit_of(t, c) % units_per_row, 0, 0))
    params = (wo, bo, g1, b1, w1c, w2c, g2, b2)
    out, y, ko, vo = pl.pallas_call(
        functools.partial(_tail_cache_kernel, units_per_row=units_per_row, n_hist=n_hist, t_new=t_new,
                          q_pairs_of_kv=tuple((p,) for p in range(UNIT_HEADS // 2)), pair_heads=pair_heads),
        grid=(n // tm, n_chunks),
        in_specs=[row(dm), row(MIX_W), row(MIX_W)] + [_resident_spec(p.shape) for p in params]
        + [new(q_col), new(k_col), new(v_col), pl.BlockSpec(memory_space=pl.ANY), pl.BlockSpec(memory_space=pl.ANY),
           _resident_spec(bias.shape), _resident_spec(mult.shape)],
        out_specs=[row(dm), new(0), cache, cache],
        out_shape=[jax.ShapeDtypeStruct((n, dm), _F32), jax.ShapeDtypeStruct((bsz, t_new, MIX_W), _BF),
                   jax.ShapeDtypeStruct(kt.shape, _F32), jax.ShapeDtypeStruct(vt.shape, _F32)],
        scratch_shapes=[pltpu.VMEM((tm, dm), _BF), pltpu.VMEM((tm, dm), _F32)]
        + [pltpu.VMEM((CACHE_SLOTS, UNIT_HEADS, HEAD_DIM, n_hist), _F32)] * 2
        + [pltpu.SemaphoreType.DMA((2, CACHE_SLOTS))],
        compiler_params=pltpu.CompilerParams(dimension_semantics=("arbitrary", "arbitrary"),
                                             vmem_limit_bytes=FUSED_VMEM_LIMIT),
        name="even_tail_with_cache",
    )(x, ya, yb, *params, p3, p3, p3, kt, vt, bias, mult)
    return out, y, jnp.transpose(ko, (0, 3, 1, 2)), jnp.transpose(vo, (0, 3, 1, 2))


def _sample_attn(p3, q_col, k_col, v_col, pair_heads, branches, k_cache, v_cache, sinks, bt):
    bsz, t_new, _ = p3.shape
    _, n_hist, kvh, _ = k_cache.shape
    kv_width = kvh * HEAD_DIM
    kt = jnp.transpose(k_cache, (0, 2, 3, 1))
    vt = jnp.transpose(v_cache, (0, 2, 3, 1))
    n_kv_pairs = kvh // 2
    q_pairs_of_kv = tuple(tuple(p for p in range(N_PAIRS) if p % n_kv_pairs == kvp) for kvp in range(n_kv_pairs))
    bias, mult = _sample_tables(pair_heads, branches, n_hist, t_new)
    new = lambda col, w: pl.BlockSpec((bt, t_new, w), lambda b: (b, 0, col // w))
    cache = pl.BlockSpec((bt, kvh, HEAD_DIM, n_hist), lambda b: (b, 0, 0, 0))
    in_specs = [new(q_col, MIX_W), new(k_col, kv_width), new(v_col, kv_width), cache, cache,
                _const_spec(bias.shape), _const_spec(mult.shape)]
    args = [p3, p3, p3, kt, vt, bias, mult]
    if sinks is not None:
        in_specs.append(pl.BlockSpec(memory_space=pltpu.SMEM))
        args.append(sinks)
    y, ko, vo = pl.pallas_call(
        functools.partial(_sample_attn_kernel, n_hist=n_hist, t_new=t_new, q_pairs_of_kv=q_pairs_of_kv,
                          pair_heads=pair_heads, has_sink=sinks is not None),
        grid=(bsz // bt,),
        in_specs=in_specs,
        out_specs=[pl.BlockSpec((bt, t_new, MIX_W), lambda b: (b, 0, 0)), cache, cache],
        out_shape=[jax.ShapeDtypeStruct((bsz, t_new, MIX_W), _BF),
                   jax.ShapeDtypeStruct(kt.shape, _F32), jax.ShapeDtypeStruct(vt.shape, _F32)],
        compiler_params=_cparams(1),
        name=f"sample_attn_{n_hist}",
    )(*args)
    return y, jnp.transpose(ko, (0, 3, 1, 2)), jnp.transpose(vo, (0, 3, 1, 2))


A_Q = N_HEADS * HEAD_DIM
A_KV = A_KV_HEADS * HEAD_DIM
E_Q, E_H, E_GB, E_GC, E_K, E_V = 0, 512, 1024, 1536, 2048, 2176
O_U, O_Q, O_K, O_V = 0, 512, 1024, 1536


def _prep_layer_weights(even_w_in, even_b_in, even_w_out, odd_w_in, odd_b_in, c_w_group):
    q_cols = np.concatenate([h * HEAD_DIM + np.arange(HEAD_DIM) for h in A_HEAD_ORDER])
    o1, o2, o3 = A_Q, A_Q + A_KV, A_Q + 2 * A_KV
    order = np.concatenate([q_cols, np.arange(o3, o3 + 3 * MIX_W), np.arange(o1, o3)])
    scale = np.ones((order.size,), np.float32)
    scale[:A_Q] = HEAD_DIM ** -0.5
    ew = (even_w_in[:, order] * scale).astype(_BF)
    eb = (even_b_in[order] * scale)[None, :]
    ewo = jnp.concatenate([even_w_out[q_cols], even_w_out[A_Q:]], axis=0).astype(_BF)
    oscale = np.ones((odd_w_in.shape[1],), np.float32)
    oscale[O_Q:O_K] = HEAD_DIM ** -0.5
    ow = (odd_w_in * oscale).astype(_BF)
    ob = (odd_b_in * oscale)[None, :]
    groups, gw, _ = c_w_group.shape
    wg = jnp.zeros((MIX_W, MIX_W), _F32)
    for g in range(groups):
        wg = wg.at[g * gw:(g + 1) * gw, g * gw:(g + 1) * gw].set(c_w_group[g])
    return ew, eb, ewo, ow, ob, wg.astype(_BF)


def _row(v):
    return v[None, :]


def _forward(xp, xs, caches, wts, tm, tm_proj, bt_attn, bt_shift, n_chunks):
    (even_w_in, even_b_in, a_sinks, b_conv_w, even_w_out, even_b_out, odd_w_in, odd_b_in, c_w_group, c_scale,
     odd_w_out, odd_b_out, mlp_w1, mlp_w2, ln1_g, ln1_b, ln2_g, ln2_b) = wts
    cache_a_k, cache_a_v, state_b_conv, state_c_pool, cache_d_k, cache_d_v = caches
    bsz, seq, dm = xp.shape
    bs, ts, _ = xs.shape
    n, ns = bsz * seq, bs * ts
    tms = min(tm_proj, ns)
    ew, eb, ewo, ow, ob, wg = _prep_layer_weights(even_w_in[0], even_b_in[0], even_w_out[0], odd_w_in[0],
                                                  odd_b_in[0], c_w_group[0])
    owo = odd_w_out[0].astype(_BF)
    w1 = mlp_w1.astype(_BF)
    w2 = mlp_w2.astype(_BF)
    tails = [(wo, _row(bo), _row(ln1_g[i]), _row(ln1_b[i]), w1[i], w2[i], _row(ln2_g[i]), _row(ln2_b[i]))
             for i, (wo, bo) in enumerate(((ewo, even_b_out[0]), (owo, odd_b_out[0])))]
    a_kv_lane = (0,) * N_PAIRS
    d_kv_lane = tuple(p * LANES for p in range(N_PAIRS))

    xs = xs.reshape(ns, dm)
    pe3 = _proj(xs, ew, eb, _F32, tms).reshape(bs, ts, -1)
    ya, ak_s, av_s = _sample_attn(pe3, E_Q, E_K, E_V, A_PAIR_HEADS, ((A_WINDOW, 1),), cache_a_k[0], cache_a_v[0],
                                  a_sinks[0], bt_attn)
    c_hist = jnp.pad(state_b_conv[0], ((0, 0), (HALO - (CONV_WIDTH - 1), 0), (0, 0)))
    yb, ctail = _sample_conv(pe3, c_hist, b_conv_w[0], bt_shift)
    bc_s = ctail[:, -(CONV_WIDTH - 1):, :]
    xs = _even_tail(xs, ya.reshape(ns, MIX_W), yb.reshape(ns, MIX_W), tails[0], tms)
    po3s = _proj(xs, ow, ob, _F32, tms).reshape(bs, ts, -1)
    u_hist = jnp.pad(state_c_pool[0], ((0, 0), (HALO - (POOL_MAX - 1), 0), (0, 0)))
    pooled_s = _sample_pool(po3s, u_hist, bt_shift).reshape(ns, MIX_W)
    cp_s = jnp.concatenate([state_c_pool[0], po3s[:, :, O_U:O_Q]], axis=1)[:, -(POOL_MAX - 1):]

    x = xp.reshape(n, dm)
    qkv, yb, ctail = _proj_conv(xp, ew, eb, b_conv_w[0], tm_proj)
    k_col, v_col = MIX_W, MIX_W + A_KV
    ya = _band_attn(qkv[:, None], 0, k_col, v_col, LANES, a_kv_lane, A_PAIR_HEADS, a_sinks[0], False)
    n_keep = min(A_WINDOW, seq)
    a_k = qkv[:, seq - n_keep:, k_col:v_col].astype(_F32).reshape(bsz, n_keep, A_KV_HEADS, HEAD_DIM)
    a_v = qkv[:, seq - n_keep:, v_col:].astype(_F32).reshape(bsz, n_keep, A_KV_HEADS, HEAD_DIM)
    b_conv = ctail[:, -(CONV_WIDTH - 1):, :]
    fused_tail = tails[0][:4] + (mlp_w1[0],) + tails[0][5:]
    x, yd, dk_s, dv_s = _even_tail_with_cache(x, ya, yb.reshape(n, MIX_W), fused_tail, po3s, O_Q, O_K, O_V,
                                              D_PAIR_HEADS, D_BRANCHES, cache_d_k[0], cache_d_v[0], tm, n_chunks)
    xs = _sample_odd_tail(xs, pooled_s, wg, _row(c_scale[0]), yd.reshape(ns, MIX_W), tails[1], tms)

    dils = tuple(d for _, d in D_BRANCHES if d > 1)
    n_keep = min(D_BRANCHES[-1][0], seq)
    qkv, qkv4, qkv16, pooled, utail, k_keep, v_keep = _proj_dilated(x.reshape(bsz, seq, dm), ow, ob, tm, O_Q, dils,
                                                                    n_keep)
    k_col, v_col = O_K - O_Q, O_V - O_Q
    outs, stats = [], []
    for arr in (qkv[:, None], qkv4, qkv16):
        o, st = _band_attn(arr, 0, k_col, v_col, MIX_W, d_kv_lane, D_PAIR_HEADS, None, True)
        outs.append(o)
        stats.append(st)
    c_pool = utail[:, -(POOL_MAX - 1):, :]
    d_k = k_keep.reshape(bsz, n_keep, N_HEADS, HEAD_DIM)
    d_v = v_keep.reshape(bsz, n_keep, N_HEADS, HEAD_DIM)
    x = _odd_tail(x, pooled.reshape(n, MIX_W), wg, _row(c_scale[0]), outs, stats, tails[1], tm)
    return (x.reshape(bsz, seq, dm), xs.reshape(bs, ts, dm), a_k[None], a_v[None], b_conv[None], c_pool[None],
            d_k[None], d_v[None], ak_s[None], av_s[None], bc_s[None], cp_s[None], dk_s[None], dv_s[None])


def kernel(x_prompt, x_sample, cache_a_k, cache_a_v, state_b_conv, state_c_pool, cache_d_k, cache_d_v, even_w_in, even_b_in, a_sinks, b_conv_w, even_w_out, even_b_out, odd_w_in, odd_b_in, c_w_group, c_scale, odd_w_out, odd_b_out, mlp_w1, mlp_w2, ln1_g, ln1_b, ln2_g, ln2_b):
    wts = (even_w_in, even_b_in, a_sinks, b_conv_w, even_w_out, even_b_out, odd_w_in, odd_b_in, c_w_group, c_scale,
           odd_w_out, odd_b_out, mlp_w1, mlp_w2, ln1_g, ln1_b, ln2_g, ln2_b)
    caches = (cache_a_k, cache_a_v, state_b_conv, state_c_pool, cache_d_k, cache_d_v)
    return _forward(x_prompt, x_sample, caches, wts, tm=512, tm_proj=1024, bt_attn=8, bt_shift=32, n_chunks=4)
```

```python
import functools

import numpy as np
import jax
import jax.numpy as jnp
from jax import lax
from jax.experimental import pallas as pl
from jax.experimental.pallas import tpu as pltpu

HEAD_DIM = 64
N_HEADS = 8
A_KV_HEADS = 2
A_WINDOW = 128
D_BRANCHES = ((128, 1), (512, 4), (2048, 16))
CONV_WIDTH = 3
POOL_WINDOWS = (2, 4, 8, 16)
POOL_MAX = 16
DEPTH = 2
PAST_LEN = 16384
DEEPNORM_ALPHA = (2 * DEPTH) ** 0.25
LN_EPS = 1e-5
LOG2_E = float(np.log2(np.e))

MIX_W = N_HEADS * HEAD_DIM
LANES = 128
N_PAIRS = MIX_W // LANES
BAND = 128
HALO = 16
EXT0 = 24
VMEM_LIMIT = 56 * 1024 * 1024
FUSED_VMEM_LIMIT = 62 * 1024 * 1024

A_HEAD_ORDER = (0, 4, 1, 5, 2, 6, 3, 7)
A_PAIR_HEADS = tuple((p, p + 4) for p in range(N_PAIRS))
D_PAIR_HEADS = tuple((2 * p, 2 * p + 1) for p in range(N_PAIRS))

_BF = jnp.bfloat16
_F32 = jnp.float32
_NEG_INF = float("-inf")


def _alibi_slopes(n_heads):
    return 2.0 ** (-8.0 * np.arange(1, n_heads + 1) / n_heads)


def _cparams(n_axes):
    return pltpu.CompilerParams(dimension_semantics=("arbitrary",) * n_axes, vmem_limit_bytes=VMEM_LIMIT)


def _const_spec(shape):
    nd = len(shape)
    return pl.BlockSpec(shape, lambda *_: (0,) * nd)


def _layer_norm(y, g, b):
    mu = jnp.mean(y, axis=-1, keepdims=True)
    yc = y - mu
    var = jnp.mean(yc * yc, axis=-1, keepdims=True)
    return yc * lax.rsqrt(var + LN_EPS) * g + b


def _proj_kernel(x_ref, w_ref, b_ref, o_ref, *, tn):
    x = x_ref[...].astype(_BF)
    for j in range(o_ref.shape[1] // tn):
        cols = slice(j * tn, (j + 1) * tn)
        acc = jnp.dot(x, w_ref[:, cols], preferred_element_type=_F32)
        o_ref[:, cols] = (acc + b_ref[:, cols]).astype(o_ref.dtype)


def _proj(x, w, b, out_dtype, tm):
    n, k = x.shape
    m = w.shape[1]
    return pl.pallas_call(
        functools.partial(_proj_kernel, tn=256),
        grid=(n // tm,),
        in_specs=[pl.BlockSpec((tm, k), lambda i: (i, 0)), _const_spec((k, m)), _const_spec((1, m))],
        out_specs=pl.BlockSpec((tm, m), lambda i: (i, 0)),
        out_shape=jax.ShapeDtypeStruct((n, m), out_dtype),
        compiler_params=_cparams(1),
        name="proj",
    )(x, w, b)


def _carried_history(ext_ref, tm):
    @pl.when(pl.program_id(1) == 0)
    def _():
        ext_ref[:, tm + EXT0 - HALO:tm + EXT0, :] = jnp.zeros((1, HALO, MIX_W), _F32)

    return ext_ref[:, tm + EXT0 - HALO:tm + EXT0, :]


def _proj_conv_kernel(x_ref, w_ref, b_ref, cw_ref, qkv_ref, yb_ref, ctail_ref, hg_ref, ext_ref, *, tn):
    tm = x_ref.shape[0]
    c_hist = _carried_history(ext_ref, tm)
    x = x_ref[...].astype(_BF)
    gates = slice(MIX_W, 4 * MIX_W)
    for j in range(w_ref.shape[1] // tn):
        lo = j * tn
        acc = jnp.dot(x, w_ref[:, lo:lo + tn], preferred_element_type=_F32) + b_ref[:, lo:lo + tn]
        if lo < gates.start:
            qkv_ref[:, lo:lo + tn] = acc.astype(qkv_ref.dtype)
        elif lo < gates.stop:
            hg_ref[:, lo - gates.start:lo - gates.start + tn] = acc
        else:
            qkv_ref[:, lo - 3 * MIX_W:lo - 3 * MIX_W + tn] = acc.astype(qkv_ref.dtype)
    h, gb, gc = (hg_ref[:, k * MIX_W:(k + 1) * MIX_W][None] for k in range(3))
    _conv_body(h, gb, gc, c_hist, cw_ref, ext_ref, yb_ref, ctail_ref)


def _proj_conv(x3, w, b, conv_w, tm):
    bsz, seq, k = x3.shape
    m = w.shape[1]
    qkv_w = m - 3 * MIX_W
    return pl.pallas_call(
        functools.partial(_proj_conv_kernel, tn=256),
        grid=(bsz, seq // tm),
        in_specs=[pl.BlockSpec((None, tm, k), lambda bi, i: (bi, i, 0)), _const_spec((k, m)), _const_spec((1, m)),
                  _const_spec(conv_w.shape)],
        out_specs=[pl.BlockSpec((None, tm, qkv_w), lambda bi, i: (bi, i, 0)),
                   pl.BlockSpec((1, tm, MIX_W), lambda bi, i: (bi, i, 0)),
                   pl.BlockSpec((1, 8, MIX_W), lambda bi, i: (bi, i, 0))],
        out_shape=[jax.ShapeDtypeStruct((bsz, seq, qkv_w), _BF), jax.ShapeDtypeStruct((bsz, seq, MIX_W), _BF),
                   jax.ShapeDtypeStruct((bsz, (seq // tm) * 8, MIX_W), _F32)],
        scratch_shapes=[pltpu.VMEM((tm, 3 * MIX_W), _F32), pltpu.VMEM((1, EXT0 + tm, MIX_W), _F32)],
        compiler_params=_cparams(2),
        name="proj_conv",
    )(x3, w, b, conv_w)


def _proj_dilated_kernel(x_ref, w_ref, b_ref, o_ref, d4_ref, d16_ref, pooled_ref, utail_ref, kf_ref, vf_ref,
                         stage_ref, stage4_ref, u_ref, ext_ref, s2_ref, s4_ref, s8_ref, *, tn, first_col):
    tm = x_ref.shape[0]
    u_hist = _carried_history(ext_ref, tm)
    x = x_ref[...].astype(_BF)
    for j in range(w_ref.shape[1] // tn):
        lo = j * tn
        acc = jnp.dot(x, w_ref[:, lo:lo + tn], preferred_element_type=_F32) + b_ref[:, lo:lo + tn]
        if lo < first_col:
            u_ref[:, lo:lo + tn] = acc
        else:
            rel = lo - first_col
            o_ref[:, rel:rel + tn] = acc.astype(o_ref.dtype)
            for h in range(tn // LANES):
                stage_ref[rel // LANES + h] = acc[:, h * LANES:(h + 1) * LANES]
            if rel >= MIX_W:
                f_ref = kf_ref if rel < 2 * MIX_W else vf_ref
                f_ref[:, rel % MIX_W:rel % MIX_W + tn] = acc
    _pool_body(u_ref[...][None], u_hist, pl.program_id(1) * tm, ext_ref, s2_ref, s4_ref, s8_ref, pooled_ref)
    utail_ref[...] = ext_ref[:, tm + EXT0 - HALO:tm + EXT0, :]
    q4, q16 = tm // 4, tm // 16
    for s in range(stage_ref.shape[0]):
        lanes = slice(s * LANES, (s + 1) * LANES)
        for r in range(4):
            rows = stage_ref[s, pl.ds(r, q4, stride=4), :]
            stage4_ref[s, r * q4:(r + 1) * q4, :] = rows
            d4_ref[r, :, lanes] = rows.astype(d4_ref.dtype)
        for r in range(4):
            for k in range(4):
                d16_ref[r + 4 * k, :, lanes] = stage4_ref[s, pl.ds(r * q4 + k, q16, stride=4), :].astype(d16_ref.dtype)


def _proj_dilated(x3, w, b, tm, first_col, dilations, n_keep):
    bsz, seq, k = x3.shape
    m = w.shape[1]
    wd = m - first_col
    assert first_col == MIX_W and wd == 3 * MIX_W and tuple(dilations) == (4, 16) and tm % 256 == 0
    assert n_keep % tm == 0 and seq % tm == 0
    skipped = (seq - n_keep) // tm
    out_specs = [pl.BlockSpec((None, tm, wd), lambda bi, i: (bi, i, 0))]
    out_shape = [jax.ShapeDtypeStruct((bsz, seq, wd), _BF)]
    for d in dilations:
        out_specs.append(pl.BlockSpec((None, d, tm // d, wd), lambda bi, i: (bi, 0, i, 0)))
        out_shape.append(jax.ShapeDtypeStruct((bsz, d, seq // d, wd), _BF))
    out_specs += [pl.BlockSpec((1, tm, MIX_W), lambda bi, i: (bi, i, 0)),
                  pl.BlockSpec((1, HALO, MIX_W), lambda bi, i: (bi, i, 0))]
    out_shape += [jax.ShapeDtypeStruct((bsz, seq, MIX_W), _BF),
                  jax.ShapeDtypeStruct((bsz, (seq // tm) * HALO, MIX_W), _F32)]
    out_specs += [pl.BlockSpec((None, tm, MIX_W), lambda bi, i: (bi, jnp.maximum(i - skipped, 0), 0))] * 2
    out_shape += [jax.ShapeDtypeStruct((bsz, n_keep, MIX_W), _F32)] * 2
    shift = pltpu.VMEM((1, EXT0 + tm, MIX_W), _F32)
    return pl.pallas_call(
        functools.partial(_proj_dilated_kernel, tn=256, first_col=first_col),
        grid=(bsz, seq // tm),
        in_specs=[pl.BlockSpec((None, tm, k), lambda bi, i: (bi, i, 0)), _const_spec((k, m)), _const_spec((1, m))],
        out_specs=out_specs,
        out_shape=out_shape,
        scratch_shapes=[pltpu.VMEM((wd // LANES, tm, LANES), _F32)] * 2 + [pltpu.VMEM((tm, MIX_W), _F32)] + [shift] * 4,
        compiler_params=_cparams(2),
        name="proj_dilated",
    )(x3, w, b)


MLP_CHUNK = 512


def _lane_is_left():
    return lax.broadcasted_iota(jnp.int32, (1, LANES), 1) < HEAD_DIM


def _layer_tail(x_ref, left, right, tail_refs):
    wo_ref, bo_ref, g1_ref, b1_ref, w1_ref, w2_ref, g2_ref, b2_ref, o_ref = tail_refs
    mix = jnp.dot(left, wo_ref[:MIX_W, :], preferred_element_type=_F32)
    mix = mix + jnp.dot(right, wo_ref[MIX_W:, :], preferred_element_type=_F32) + bo_ref[...]
    x = _layer_norm(DEEPNORM_ALPHA * x_ref[...] + mix, g1_ref[...], b1_ref[...])
    xb = x.astype(_BF)
    acc = jnp.zeros(x.shape, _F32)
    for c in range(w1_ref.shape[1] // MLP_CHUNK):
        cols = slice(c * MLP_CHUNK, (c + 1) * MLP_CHUNK)
        h = jnp.dot(xb, w1_ref[:, cols], preferred_element_type=_F32)
        h = jnp.square(jnp.maximum(h, 0.0)).astype(_BF)
        acc = acc + jnp.dot(h, w2_ref[cols, :], preferred_element_type=_F32)
    o_ref[...] = _layer_norm(DEEPNORM_ALPHA * x + acc, g2_ref[...], b2_ref[...])


def _even_tail_kernel(x_ref, ya_ref, yb_ref, *tail_refs):
    _layer_tail(x_ref, ya_ref[...], yb_ref[...], tail_refs)


def _group_c(pooled_ref, wg_ref, scale_ref):
    return (jnp.dot(pooled_ref[...], wg_ref[...], preferred_element_type=_F32) * scale_ref[...]).astype(_BF)


def _sample_odd_tail_kernel(x_ref, pooled_ref, wg_ref, scale_ref, yd_ref, *tail_refs):
    _layer_tail(x_ref, _group_c(pooled_ref, wg_ref, scale_ref), yd_ref[...], tail_refs)


def _odd_tail_kernel(x_ref, pooled_ref, wg_ref, scale_ref, o1_ref, o2_ref, o3_ref, s1_ref, s2_ref, s3_ref,
                     *tail_refs):
    yc = _group_c(pooled_ref, wg_ref, scale_ref)
    tiles = []
    for p in range(N_PAIRS):
        lanes = slice(p * LANES, (p + 1) * LANES)
        lses = (s1_ref[p], s2_ref[p], s3_ref[p])
        top = jnp.maximum(jnp.maximum(lses[0], lses[1]), lses[2])
        es = [jnp.exp2(s - top) for s in lses]
        num = es[0] * o1_ref[:, lanes].astype(_F32)
        num = num + es[1] * o2_ref[:, lanes].astype(_F32)
        num = num + es[2] * o3_ref[:, lanes].astype(_F32)
        tiles.append(num / (es[0] + es[1] + es[2]))
    yd = jnp.concatenate(tiles, axis=1)
    _layer_tail(x_ref, yc, yd.astype(_BF), tail_refs)


def _row_spec(tm, width):
    return pl.BlockSpec((tm, width), lambda i: (i, 0))


def _resident_spec(shape):
    nd = len(shape)
    return pl.BlockSpec(shape, lambda *_: (0,) * nd, pipeline_mode=pl.Buffered(1))


def _tail_call(kernel_fn, name, x, mixer_args, mixer_specs, tail_params, tm):
    n, dm = x.shape
    return pl.pallas_call(
        kernel_fn,
        grid=(n // tm,),
        in_specs=[_row_spec(tm, dm)] + mixer_specs + [_resident_spec(p.shape) for p in tail_params],
        out_specs=_row_spec(tm, dm),
        out_shape=jax.ShapeDtypeStruct((n, dm), _F32),
        compiler_params=_cparams(1),
        name=name,
    )(x, *mixer_args, *tail_params)


def _even_tail(x, ya, yb, tail_params, tm):
    return _tail_call(_even_tail_kernel, "even_tail", x, [ya, yb], [_row_spec(tm, MIX_W)] * 2, tail_params, tm)


def _odd_tail(x, pooled, wg, scale, outs, stats, tail_params, tm):
    tiles_per_seq = stats[0].shape[2] // tm
    stat_spec = pl.BlockSpec((None, N_PAIRS, tm, LANES), lambda i: (i // tiles_per_seq, 0, i % tiles_per_seq, 0))
    specs = ([_row_spec(tm, MIX_W), _resident_spec(wg.shape), _resident_spec(scale.shape)]
             + [_row_spec(tm, MIX_W)] * 3 + [stat_spec] * 3)
    return _tail_call(_odd_tail_kernel, "odd_tail", x, [pooled, wg, scale, *outs, *stats], specs, tail_params, tm)


def _sample_odd_tail(x, pooled, wg, scale, yd, tail_params, tm):
    specs = [_row_spec(tm, MIX_W), _resident_spec(wg.shape), _resident_spec(scale.shape), _row_spec(tm, MIX_W)]
    return _tail_call(_sample_odd_tail_kernel, "sample_odd_tail", x, [pooled, wg, scale, yd], specs, tail_params, tm)


def _fill_ext(ext_ref, hist, cur, t):
    nb = ext_ref.shape[0]
    ext_ref[:, 0:8, :] = jnp.zeros((nb, 8, MIX_W), _F32)
    ext_ref[:, 8:EXT0, :] = hist
    ext_ref[:, EXT0:EXT0 + t, :] = cur


def _conv_body(h, gb, gc, c_hist, w_ref, ext_ref, yb_ref, ctail_ref):
    t = h.shape[1]
    c = gc * h
    _fill_ext(ext_ref, c_hist, c, t)
    conv = ext_ref[:, EXT0 - 2:EXT0 - 2 + t, :] * w_ref[0:1, :]
    conv = conv + ext_ref[:, EXT0 - 1:EXT0 - 1 + t, :] * w_ref[1:2, :]
    conv = conv + c * w_ref[2:3, :]
    yb_ref[...] = (gb * conv).astype(yb_ref.dtype)
    ctail_ref[...] = ext_ref[:, EXT0 + t - 8:EXT0 + t, :]


def _sample_conv_kernel(h_ref, gb_ref, gc_ref, hist_ref, w_ref, yb_ref, ctail_ref, ext_ref):
    _conv_body(h_ref[...], gb_ref[...], gc_ref[...], hist_ref[...], w_ref, ext_ref, yb_ref, ctail_ref)


def _pool_body(u, hist, pos0, ext_ref, s2_ref, s4_ref, s8_ref, out_ref):
    nb, t, _ = u.shape
    _fill_ext(ext_ref, hist, u, t)
    hi = EXT0 + t
    zeros8 = jnp.zeros((nb, 8, MIX_W), _F32)
    s2_ref[:, 0:8, :] = zeros8
    s4_ref[:, 0:8, :] = zeros8
    s8_ref[:, 0:8, :] = zeros8
    s2_ref[:, 8:hi, :] = ext_ref[:, 8:hi, :] + ext_ref[:, 7:hi - 1, :]
    s4_ref[:, 8:hi, :] = s2_ref[:, 8:hi, :] + s2_ref[:, 6:hi - 2, :]
    s8_ref[:, 8:hi, :] = s4_ref[:, 8:hi, :] + s4_ref[:, 4:hi - 4, :]
    sums = (
        s2_ref[:, EXT0:hi, 0:LANES],
        s4_ref[:, EXT0:hi, LANES:2 * LANES],
        s8_ref[:, EXT0:hi, 2 * LANES:3 * LANES],
        s8_ref[:, EXT0:hi, 3 * LANES:] + s8_ref[:, EXT0 - 8:hi - 8, 3 * LANES:],
    )
    pos = (pos0 + lax.broadcasted_iota(jnp.int32, (1, t, LANES), 1) + 1).astype(_F32)
    tiles = []
    for g, (w, s) in enumerate(zip(POOL_WINDOWS, sums)):
        cnt = jnp.minimum(pos, float(w))
        tiles.append(s / cnt - u[:, :, g * LANES:(g + 1) * LANES])
    out_ref[...] = jnp.concatenate(tiles, axis=2).astype(out_ref.dtype)


def _sample_pool_kernel(u_ref, hist_ref, out_ref, ext_ref, s2_ref, s4_ref, s8_ref):
    _pool_body(u_ref[...], hist_ref[...], PAST_LEN, ext_ref, s2_ref, s4_ref, s8_ref, out_ref)


def _sample_conv(p3, hist, conv_w, bt):
    bsz, t, _ = p3.shape
    col = lambda c: pl.BlockSpec((bt, t, MIX_W), lambda b: (b, 0, c))
    return pl.pallas_call(
        _sample_conv_kernel,
        grid=(bsz // bt,),
        in_specs=[col(1), col(2), col(3), pl.BlockSpec((bt, HALO, MIX_W), lambda b: (b, 0, 0)),
                  _const_spec(conv_w.shape)],
        out_specs=[pl.BlockSpec((bt, t, MIX_W), lambda b: (b, 0, 0)),
                   pl.BlockSpec((bt, 8, MIX_W), lambda b: (b, 0, 0))],
        out_shape=[jax.ShapeDtypeStruct((bsz, t, MIX_W), _BF), jax.ShapeDtypeStruct((bsz, 8, MIX_W), _F32)],
        scratch_shapes=[pltpu.VMEM((bt, EXT0 + t, MIX_W), _F32)],
        compiler_params=_cparams(1),
        name="sample_conv",
    )(p3, p3, p3, hist, conv_w)


def _sample_pool(p3, hist, bt):
    bsz, t, _ = p3.shape
    scratch = pltpu.VMEM((bt, EXT0 + t, MIX_W), _F32)
    return pl.pallas_call(
        _sample_pool_kernel,
        grid=(bsz // bt,),
        in_specs=[pl.BlockSpec((bt, t, MIX_W), lambda b: (b, 0, 0)),
                  pl.BlockSpec((bt, HALO, MIX_W), lambda b: (b, 0, 0))],
        out_specs=pl.BlockSpec((bt, t, MIX_W), lambda b: (b, 0, 0)),
        out_shape=jax.ShapeDtypeStruct((bsz, t, MIX_W), _BF),
        scratch_shapes=[scratch] * 4,
        compiler_params=_cparams(1),
        name="sample_pool",
    )(p3, hist)


def _split_heads(q_pair):
    left = _lane_is_left()
    zero = jnp.zeros_like(q_pair)
    return jnp.concatenate([jnp.where(left, q_pair, zero), jnp.where(left, zero, q_pair)], axis=0)


def _band_attn_kernel(*refs, tq, dilation, kv_lane, has_sink, want_stat):
    refs = list(refs)
    q_ref, kc_ref, kp_ref, vc_ref, vp_ref, bias_ref = refs[:6]
    rest = refs[6:]
    sink_ref = rest.pop(0) if has_sink else None
    o_ref = rest.pop(0)
    st_ref = rest.pop(0) if want_stat else None
    stage_ref = rest.pop(0) if dilation > 1 else None

    first = pl.program_id(1) == 0
    n_classes = q_ref.shape[0]
    left = _lane_is_left()
    prev_cols = lax.broadcasted_iota(jnp.int32, (1, 2 * BAND), 1) < BAND
    top_rows = lax.broadcasted_iota(jnp.int32, (2 * BAND, 1), 0) < BAND
    ones = jnp.ones((2 * BAND, LANES), _BF)

    for cl, j in ((cl, j) for cl in range(n_classes) for j in range(tq // BAND)):
        res = pl.program_id(2) * n_classes + cl
        rows = slice(j * BAND, (j + 1) * BAND)
        out_rows = rows if dilation == 1 else pl.ds(j * BAND * dilation + res, BAND, stride=dilation)
        for p in range(N_PAIRS):
            kl = slice(kv_lane[p], kv_lane[p] + LANES)
            if j == 0:
                k_prev, v_prev = kp_ref[cl, :, kl], vp_ref[cl, :, kl]
            else:
                k_prev, v_prev = kc_ref[cl, (j - 1) * BAND:j * BAND, kl], vc_ref[cl, (j - 1) * BAND:j * BAND, kl]
            k2 = jnp.concatenate([k_prev, kc_ref[cl, rows, kl]], axis=0)
            v2 = jnp.concatenate([v_prev, vc_ref[cl, rows, kl]], axis=0)
            q2 = _split_heads(q_ref[cl, rows, p * LANES:(p + 1) * LANES])
            s = lax.dot_general(q2, k2, (((1,), (1,)), ((), ())), preferred_element_type=_F32)
            s = s + bias_ref[p]
            if j == 0:
                s = jnp.where(jnp.logical_and(first, prev_cols), _NEG_INF, s)
            m = jnp.max(s, axis=1, keepdims=True)
            prob = jnp.exp2(s - m).astype(_BF)
            r = jnp.dot(prob, jnp.concatenate([v2, ones], axis=1), preferred_element_type=_F32)
            pv, l = r[:, :LANES], r[:, LANES:]
            if has_sink:
                ha, hb = A_PAIR_HEADS[p]
                sink = jnp.where(top_rows, sink_ref[ha], sink_ref[hb])
                m2 = jnp.maximum(m, sink)
                a = jnp.exp2(m - m2)
                o = pv * a / (l * a + jnp.exp2(sink - m2))
            else:
                o = pv / l
            o_pair = jnp.where(left, o[:BAND], o[BAND:])
            if dilation == 1:
                o_ref[rows, p * LANES:(p + 1) * LANES] = o_pair.astype(o_ref.dtype)
            else:
                stage_ref[p, out_rows, :] = o_pair
            if want_stat:
                lse = m + jnp.log2(l)
                st_ref[p, out_rows, :] = jnp.where(left, lse[:BAND], lse[BAND:])

    if dilation > 1:
        @pl.when(pl.program_id(2) == pl.num_programs(2) - 1)
        def _():
            for p in range(N_PAIRS):
                o_ref[:, p * LANES:(p + 1) * LANES] = stage_ref[p].astype(o_ref.dtype)


def _band_bias(pair_heads, dilation):
    slopes = _alibi_slopes(N_HEADS)
    qi = np.arange(BAND)[:, None]
    kj = np.arange(2 * BAND)[None, :]
    dist = qi + BAND - kj
    valid = (dist >= 0) & (dist <= BAND)
    out = np.empty((len(pair_heads), 2 * BAND, 2 * BAND), np.float32)
    for p, heads in enumerate(pair_heads):
        for half, h in enumerate(heads):
            bias = -np.float32(slopes[h] * LOG2_E) * (dist * dilation).astype(np.float32)
            out[p, half * BAND:(half + 1) * BAND] = np.where(valid, bias, -np.inf)
    return jnp.asarray(out)


BAND_TOKENS = 4096
BAND_STEP_ROWS = 1024


def _band_attn(arr, q_col, k_col, v_col, kv_width, kv_lane, pair_heads, sinks, want_stat):
    bsz, dilation, n, _ = arr.shape
    seq = n * dilation
    tq = min(BAND_STEP_ROWS, n, BAND_TOKENS // dilation)
    sub = tq // BAND
    n_classes = min(dilation, BAND_STEP_ROWS // tq)
    assert q_col % MIX_W == 0 and k_col % kv_width == 0 and v_col % kv_width == 0 and dilation % n_classes == 0

    def cur(col, w):
        return pl.BlockSpec((None, n_classes, tq, w), lambda b, i, r: (b, r, i, col // w))

    def prev(col, w):
        return pl.BlockSpec((None, n_classes, BAND, w),
                            lambda b, i, r: (b, r, jnp.maximum(i * sub - 1, 0), col // w))

    in_specs = [cur(q_col, MIX_W), cur(k_col, kv_width), prev(k_col, kv_width), cur(v_col, kv_width),
                prev(v_col, kv_width), _const_spec((N_PAIRS, 2 * BAND, 2 * BAND))]
    args = [arr, arr, arr, arr, arr, _band_bias(pair_heads, dilation)]
    if sinks is not None:
        in_specs.append(pl.BlockSpec(memory_space=pltpu.SMEM))
        args.append(sinks)
    out_specs = [pl.BlockSpec((None, tq * dilation, MIX_W), lambda b, i, r: (b, i, 0))]
    out_shape = [jax.ShapeDtypeStruct((bsz, seq, MIX_W), _BF)]
    if want_stat:
        out_specs.append(pl.BlockSpec((None, N_PAIRS, tq * dilation, LANES), lambda b, i, r: (b, 0, i, 0)))
        out_shape.append(jax.ShapeDtypeStruct((bsz, N_PAIRS, seq, LANES), _F32))
    scratch = [pltpu.VMEM((N_PAIRS, tq * dilation, LANES), _F32)] if dilation > 1 else []
    res = pl.pallas_call(
        functools.partial(_band_attn_kernel, tq=tq, dilation=dilation, kv_lane=kv_lane, has_sink=sinks is not None,
                          want_stat=want_stat),
        grid=(bsz, n // tq, dilation // n_classes),
        in_specs=in_specs,
        out_specs=out_specs,
        out_shape=out_shape,
        scratch_shapes=scratch,
        compiler_params=_cparams(3),
        name=f"band_attn_d{dilation}",
    )(*args)
    o = res[0].reshape(bsz * seq, MIX_W)
    return (o, res[1]) if want_stat else o


def _cached_attn_stages(rows, pair0, refs, sink_ref, *, n_hist, t_new, q_pairs_of_kv, pair_heads, cache_row=None):
    q_ref, kn_ref, vn_ref, kc_ref, vc_ref, bias_ref, mult_ref, y_ref, ko_ref, vo_ref = refs
    left = _lane_is_left()
    new_lanes = lax.broadcasted_iota(jnp.int32, (1, LANES), 1) >= LANES - t_new
    top_rows = lax.broadcasted_iota(jnp.int32, (2 * t_new, 1), 0) < t_new
    zpad = jnp.zeros((LANES - t_new, LANES), _F32)
    mult = mult_ref[...]

    chains = []
    for b in rows:
        for kvp, q_pairs in enumerate(q_pairs_of_kv):
            heads = slice(2 * kvp, 2 * kvp + 2)
            lanes = slice(kvp * LANES, (kvp + 1) * LANES)
            ext = []
            for c_ref, n_ref, o_ref in ((kc_ref, kn_ref, ko_ref), (vc_ref, vn_ref, vo_ref)):
                old = c_ref[b if cache_row is None else cache_row, heads].reshape(LANES, n_hist)
                new = jnp.concatenate([zpad, n_ref[b, :, lanes]], axis=0).T
                rolled = pltpu.roll(old, n_hist - t_new, axis=1)
                tail = jnp.where(new_lanes, new, rolled[:, n_hist - LANES:])
                out = tail if n_hist == LANES else jnp.concatenate([rolled[:, :n_hist - LANES], tail], axis=1)
                o_ref[b, heads] = out.reshape(2, HEAD_DIM, n_hist)
                ext.append(jnp.concatenate([old.astype(_BF), new.astype(_BF)], axis=1))
            chains += [(b, p, ext[0], ext[1]) for p in q_pairs]

    scores = []
    for b, p, k_ext, _ in chains:
        q2 = _split_heads(q_ref[b, :, p * LANES:(p + 1) * LANES].astype(_BF))
        scores.append(jnp.dot(q2, k_ext, preferred_element_type=_F32) + bias_ref[pair0 + p])
    yield
    probs = []
    for s in scores:
        m = jnp.max(s, axis=1, keepdims=True)
        prob = (mult * jnp.exp2(s - m)).astype(_BF)
        probs.append((m, prob, jnp.sum(prob.astype(_F32), axis=1, keepdims=True)))
    pvs = [lax.dot_general(prob, v_ext, (((1,), (1,)), ((), ())), preferred_element_type=_F32)
           for (_, prob, _), (_, _, _, v_ext) in zip(probs, chains)]
    yield
    for (b, p, _, _), (m, _, l), pv in zip(chains, probs, pvs):
        if sink_ref is not None:
            ha, hb = pair_heads[p]
            sink = jnp.where(top_rows, sink_ref[ha], sink_ref[hb])
            m2 = jnp.maximum(m, sink)
            a = jnp.exp2(m - m2)
            o = pv * a / (l * a + jnp.exp2(sink - m2))
        else:
            o = pv / l
        y_ref[b, :, p * LANES:(p + 1) * LANES] = jnp.where(left, o[:t_new], o[t_new:]).astype(y_ref.dtype)


def _sample_attn_kernel(*refs, has_sink, **statics):
    refs = list(refs)
    sink_ref = refs.pop(7) if has_sink else None
    for _ in _cached_attn_stages(range(refs[0].shape[0]), 0, refs, sink_ref, **statics):
        pass


def _sample_tables(pair_heads, branches, n_hist, t_new):
    slopes = _alibi_slopes(N_HEADS)
    key_pos = np.concatenate([np.arange(n_hist), n_hist + np.arange(LANES) - (LANES - t_new)])
    is_key = np.concatenate([np.ones(n_hist, bool), np.arange(LANES) >= LANES - t_new])
    delta = (n_hist + np.arange(t_new))[:, None] - key_pos[None, :]
    mult = np.zeros(delta.shape, np.float32)
    for window, dil in branches:
        mult += ((delta >= 0) & (delta % dil == 0) & (delta <= window) & is_key[None, :]).astype(np.float32)
    bias = np.empty((len(pair_heads), 2 * t_new, key_pos.size), np.float32)
    for p, heads in enumerate(pair_heads):
        for half, h in enumerate(heads):
            b = -np.float32(slopes[h] * LOG2_E) * delta.astype(np.float32)
            bias[p, half * t_new:(half + 1) * t_new] = np.where(mult > 0, b, -np.inf)
    return jnp.asarray(bias), jnp.asarray(np.concatenate([mult, mult], axis=0))


UNIT_HEADS = 4


CACHE_SLOTS = 3


def _tail_cache_kernel(x_ref, ya_ref, yb_ref, wo_ref, bo_ref, g1_ref, b1_ref, w1_ref, w2_ref, g2_ref, b2_ref,
                       q_ref, kn_ref, vn_ref, kc_hbm, vc_hbm, bias_ref, mult_ref,
                       o_ref, y_ref, ko_ref, vo_ref, xb_ref, acc_ref, kbuf_ref, vbuf_ref, sem_ref,
                       *, units_per_row, **statics):
    c = pl.program_id(1)
    n_chunks = pl.num_programs(1)
    unit = pl.program_id(0) * n_chunks + c
    n_units = pl.num_programs(0) * n_chunks

    def fetch(u):
        slot = lax.rem(u, CACHE_SLOTS)
        heads = pl.ds(lax.rem(u, units_per_row) * UNIT_HEADS, UNIT_HEADS)
        return [pltpu.make_async_copy(hbm.at[u // units_per_row, heads], buf.at[slot], sem_ref.at[k, slot])
                for k, (hbm, buf) in enumerate(((kc_hbm, kbuf_ref), (vc_hbm, vbuf_ref)))]

    @pl.when(unit == 0)
    def _():
        for u in range(CACHE_SLOTS - 1):
            for cp in fetch(u):
                cp.start()

    @pl.when(unit + (CACHE_SLOTS - 1) < n_units)
    def _():
        for cp in fetch(unit + (CACHE_SLOTS - 1)):
            cp.start()

    for cp in fetch(unit):
        cp.wait()

    @pl.when(c == 0)
    def _():
        mix = jnp.dot(ya_ref[...], wo_ref[:MIX_W, :], preferred_element_type=_F32)
        mix = mix + jnp.dot(yb_ref[...], wo_ref[MIX_W:, :], preferred_element_type=_F32) + bo_ref[...]
        x1 = _layer_norm(DEEPNORM_ALPHA * x_ref[...] + mix, g1_ref[...], b1_ref[...])
        xb_ref[...] = x1.astype(_BF)
        acc_ref[...] = DEEPNORM_ALPHA * x1

    pair0 = (unit % units_per_row) * (UNIT_HEADS // 2)
    cache_refs = (q_ref, kn_ref, vn_ref, kbuf_ref, vbuf_ref, bias_ref, mult_ref, y_ref, ko_ref, vo_ref)
    stages = _cached_attn_stages((0,), pair0, cache_refs, None, cache_row=lax.rem(unit, CACHE_SLOTS), **statics)
    next(stages)
    h = jnp.dot(xb_ref[...], w1_ref[c], preferred_element_type=_F32)
    h = jnp.square(jnp.maximum(h, 0.0)).astype(_BF)
    next(stages)
    acc_ref[...] += jnp.dot(h, w2_ref[c], preferred_element_type=_F32)
    for _ in stages:
        pass

    @pl.when(c == n_chunks - 1)
    def _():
        o_ref[...] = _layer_norm(acc_ref[...], g2_ref[...], b2_ref[...])


def _even_tail_with_cache(x, ya, yb, tail_params, p3, q_col, k_col, v_col, pair_heads, branches, k_cache, v_cache,
                          tm, n_chunks):
    n, dm = x.shape
    bsz, t_new, _ = p3.shape
    _, n_hist, kvh, _ = k_cache.shape
    units_per_row = kvh // UNIT_HEADS
    unit_w = UNIT_HEADS * HEAD_DIM
    assert (n // tm) * n_chunks == bsz * units_per_row >= CACHE_SLOTS and kvh == N_HEADS
    wo, bo, g1, b1, w1, w2, g2, b2 = tail_params
    dh = w1.shape[1]
    w1c = jnp.transpose(w1.reshape(dm, n_chunks, dh // n_chunks), (1, 0, 2)).astype(_BF)
    w2c = w2.reshape(n_chunks, dh // n_chunks, dm)
    kt = jnp.transpose(k_cache, (0, 2, 3, 1))
    vt = jnp.transpose(v_cache, (0, 2, 3, 1))
    bias, mult = _sample_tables(pair_heads, branches, n_hist, t_new)

    row = lambda w: pl.BlockSpec((tm, w), lambda t, c: (t, 0))
    unit_of = lambda t, c: t * n_chunks + c
    new = lambda col: pl.BlockSpec(
        (1, t_new, unit_w),
        lambda t, c: (unit_of(t, c) // units_per_row, 0, col // unit_w + unit_of(t, c) % units_per_row))
    cache = pl.BlockSpec((1, UNIT_HEADS, HEAD_DIM, n_hist),
                         lambda t, c: (unit_of(t, c) // units_per_row, unit_of(t, c) % units_per_row, 0, 0))
    params = (wo, bo, g1, b1, w1c, w2c, g2, b2)
    out, y, ko, vo = pl.pallas_call(
        functools.partial(_tail_cache_kernel, units_per_row=units_per_row, n_hist=n_hist, t_new=t_new,
                          q_pairs_of_kv=tuple((p,) for p in range(UNIT_HEADS // 2)), pair_heads=pair_heads),
        grid=(n // tm, n_chunks),
        in_specs=[row(dm), row(MIX_W), row(MIX_W)] + [_resident_spec(p.shape) for p in params]
        + [new(q_col), new(k_col), new(v_col), pl.BlockSpec(memory_space=pl.ANY), pl.BlockSpec(memory_space=pl.ANY),
           _resident_spec(bias.shape), _resident_spec(mult.shape)],
        out_specs=[row(dm), new(0), cache, cache],
        out_shape=[jax.ShapeDtypeStruct((n, dm), _F32), jax.ShapeDtypeStruct((bsz, t_new, MIX_W), _BF),
                   jax.ShapeDtypeStruct(kt.shape, _F32), jax.ShapeDtypeStruct(vt.shape, _F32)],
        scratch_shapes=[pltpu.VMEM((tm, dm), _BF), pltpu.VMEM((tm, dm), _F32)]
        + [pltpu.VMEM((CACHE_SLOTS, UNIT_HEADS, HEAD_DIM, n_hist), _F32)] * 2
        + [pltpu.SemaphoreType.DMA((2, CACHE_SLOTS))],
        compiler_params=pltpu.CompilerParams(dimension_semantics=("arbitrary", "arbitrary"),
                                             vmem_limit_bytes=FUSED_VMEM_LIMIT),
        name="even_tail_with_cache",
    )(x, ya, yb, *params, p3, p3, p3, kt, vt, bias, mult)
    return out, y, jnp.transpose(ko, (0, 3, 1, 2)), jnp.transpose(vo, (0, 3, 1, 2))


def _sample_attn(p3, q_col, k_col, v_col, pair_heads, branches, k_cache, v_cache, sinks, bt):
    bsz, t_new, _ = p3.shape
    _, n_hist, kvh, _ = k_cache.shape
    kv_width = kvh * HEAD_DIM
    kt = jnp.transpose(k_cache, (0, 2, 3, 1))
    vt = jnp.transpose(v_cache, (0, 2, 3, 1))
    n_kv_pairs = kvh // 2
    q_pairs_of_kv = tuple(tuple(p for p in range(N_PAIRS) if p % n_kv_pairs == kvp) for kvp in range(n_kv_pairs))
    bias, mult = _sample_tables(pair_heads, branches, n_hist, t_new)
    new = lambda col, w: pl.BlockSpec((bt, t_new, w), lambda b: (b, 0, col // w))
    cache = pl.BlockSpec((bt, kvh, HEAD_DIM, n_hist), lambda b: (b, 0, 0, 0))
    in_specs = [new(q_col, MIX_W), new(k_col, kv_width), new(v_col, kv_width), cache, cache,
                _const_spec(bias.shape), _const_spec(mult.shape)]
    args = [p3, p3, p3, kt, vt, bias, mult]
    if sinks is not None:
        in_specs.append(pl.BlockSpec(memory_space=pltpu.SMEM))
        args.append(sinks)
    y, ko, vo = pl.pallas_call(
        functools.partial(_sample_attn_kernel, n_hist=n_hist, t_new=t_new, q_pairs_of_kv=q_pairs_of_kv,
                          pair_heads=pair_heads, has_sink=sinks is not None),
        grid=(bsz // bt,),
        in_specs=in_specs,
        out_specs=[pl.BlockSpec((bt, t_new, MIX_W), lambda b: (b, 0, 0)), cache, cache],
        out_shape=[jax.ShapeDtypeStruct((bsz, t_new, MIX_W), _BF),
                   jax.ShapeDtypeStruct(kt.shape, _F32), jax.ShapeDtypeStruct(vt.shape, _F32)],
        compiler_params=_cparams(1),
        name=f"sample_attn_{n_hist}",
    )(*args)
    return y, jnp.transpose(ko, (0, 3, 1, 2)), jnp.transpose(vo, (0, 3, 1, 2))


A_Q = N_HEADS * HEAD_DIM
A_KV = A_KV_HEADS * HEAD_DIM
E_Q, E_H, E_GB, E_GC, E_K, E_V = 0, 512, 1024, 1536, 2048, 2176
O_U, O_Q, O_K, O_V = 0, 512, 1024, 1536


def _prep_layer_weights(even_w_in, even_b_in, even_w_out, odd_w_in, odd_b_in, c_w_group):
    q_cols = np.concatenate([h * HEAD_DIM + np.arange(HEAD_DIM) for h in A_HEAD_ORDER])
    o1, o2, o3 = A_Q, A_Q + A_KV, A_Q + 2 * A_KV
    order = np.concatenate([q_cols, np.arange(o3, o3 + 3 * MIX_W), np.arange(o1, o3)])
    scale = np.ones((order.size,), np.float32)
    scale[:A_Q] = HEAD_DIM ** -0.5 * LOG2_E
    ew = (even_w_in[:, order] * scale).astype(_BF)
    eb = (even_b_in[order] * scale)[None, :]
    ewo = jnp.concatenate([even_w_out[q_cols], even_w_out[A_Q:]], axis=0).astype(_BF)
    oscale = np.ones((odd_w_in.shape[1],), np.float32)
    oscale[O_Q:O_K] = HEAD_DIM ** -0.5 * LOG2_E
    ow = (odd_w_in * oscale).astype(_BF)
    ob = (odd_b_in * oscale)[None, :]
    groups, gw, _ = c_w_group.shape
    wg = jnp.zeros((MIX_W, MIX_W), _F32)
    for g in range(groups):
        wg = wg.at[g * gw:(g + 1) * gw, g * gw:(g + 1) * gw].set(c_w_group[g])
    return ew, eb, ewo, ow, ob, wg.astype(_BF)


def _row(v):
    return v[None, :]


def _forward(xp, xs, caches, wts, tm, tm_proj, bt_attn, bt_shift, n_chunks):
    (even_w_in, even_b_in, a_sinks, b_conv_w, even_w_out, even_b_out, odd_w_in, odd_b_in, c_w_group, c_scale,
     odd_w_out, odd_b_out, mlp_w1, mlp_w2, ln1_g, ln1_b, ln2_g, ln2_b) = wts
    cache_a_k, cache_a_v, state_b_conv, state_c_pool, cache_d_k, cache_d_v = caches
    bsz, seq, dm = xp.shape
    bs, ts, _ = xs.shape
    n, ns = bsz * seq, bs * ts
    tms = min(tm, ns)
    ew, eb, ewo, ow, ob, wg = _prep_layer_weights(even_w_in[0], even_b_in[0], even_w_out[0], odd_w_in[0],
                                                  odd_b_in[0], c_w_group[0])
    owo = odd_w_out[0].astype(_BF)
    w1 = mlp_w1.astype(_BF)
    w2 = mlp_w2.astype(_BF)
    tails = [(wo, _row(bo), _row(ln1_g[i]), _row(ln1_b[i]), w1[i], w2[i], _row(ln2_g[i]), _row(ln2_b[i]))
             for i, (wo, bo) in enumerate(((ewo, even_b_out[0]), (owo, odd_b_out[0])))]
    a_kv_lane = (0,) * N_PAIRS
    d_kv_lane = tuple(p * LANES for p in range(N_PAIRS))

    xs = xs.reshape(ns, dm)
    pe3 = _proj(xs, ew, eb, _F32, tms).reshape(bs, ts, -1)
    ya, ak_s, av_s = _sample_attn(pe3, E_Q, E_K, E_V, A_PAIR_HEADS, ((A_WINDOW, 1),), cache_a_k[0], cache_a_v[0],
                                  a_sinks[0] * LOG2_E, bt_attn)
    c_hist = jnp.pad(state_b_conv[0], ((0, 0), (HALO - (CONV_WIDTH - 1), 0), (0, 0)))
    yb, ctail = _sample_conv(pe3, c_hist, b_conv_w[0], bt_shift)
    bc_s = ctail[:, -(CONV_WIDTH - 1):, :]
    xs = _even_tail(xs, ya.reshape(ns, MIX_W), yb.reshape(ns, MIX_W), tails[0], tms)
    po3s = _proj(xs, ow, ob, _F32, tms).reshape(bs, ts, -1)
    u_hist = jnp.pad(state_c_pool[0], ((0, 0), (HALO - (POOL_MAX - 1), 0), (0, 0)))
    pooled_s = _sample_pool(po3s, u_hist, bt_shift).reshape(ns, MIX_W)
    cp_s = jnp.concatenate([state_c_pool[0], po3s[:, :, O_U:O_Q]], axis=1)[:, -(POOL_MAX - 1):]

    x = xp.reshape(n, dm)
    qkv, yb, ctail = _proj_conv(xp, ew, eb, b_conv_w[0], tm_proj)
    k_col, v_col = MIX_W, MIX_W + A_KV
    ya = _band_attn(qkv[:, None], 0, k_col, v_col, LANES, a_kv_lane, A_PAIR_HEADS, a_sinks[0] * LOG2_E, False)
    n_keep = min(A_WINDOW, seq)
    a_k = qkv[:, seq - n_keep:, k_col:v_col].astype(_F32).reshape(bsz, n_keep, A_KV_HEADS, HEAD_DIM)
    a_v = qkv[:, seq - n_keep:, v_col:].astype(_F32).reshape(bsz, n_keep, A_KV_HEADS, HEAD_DIM)
    b_conv = ctail[:, -(CONV_WIDTH - 1):, :]
    fused_tail = tails[0][:4] + (mlp_w1[0],) + tails[0][5:]
    x, yd, dk_s, dv_s = _even_tail_with_cache(x, ya, yb.reshape(n, MIX_W), fused_tail, po3s, O_Q, O_K, O_V,
                                              D_PAIR_HEADS, D_BRANCHES, cache_d_k[0], cache_d_v[0], tm, n_chunks)
    xs = _sample_odd_tail(xs, pooled_s, wg, _row(c_scale[0]), yd.reshape(ns, MIX_W), tails[1], tms)

    dils = tuple(d for _, d in D_BRANCHES if d > 1)
    n_keep = min(D_BRANCHES[-1][0], seq)
    qkv, qkv4, qkv16, pooled, utail, k_keep, v_keep = _proj_dilated(x.reshape(bsz, seq, dm), ow, ob, tm, O_Q, dils,
                                                                    n_keep)
    k_col, v_col = O_K - O_Q, O_V - O_Q
    outs, stats = [], []
    for arr in (qkv[:, None], qkv4, qkv16):
        o, st = _band_attn(arr, 0, k_col, v_col, MIX_W, d_kv_lane, D_PAIR_HEADS, None, True)
        outs.append(o)
        stats.append(st)
    c_pool = utail[:, -(POOL_MAX - 1):, :]
    d_k = k_keep.reshape(bsz, n_keep, N_HEADS, HEAD_DIM)
    d_v = v_keep.reshape(bsz, n_keep, N_HEADS, HEAD_DIM)
    x = _odd_tail(x, pooled.reshape(n, MIX_W), wg, _row(c_scale[0]), outs, stats, tails[1], tm)
    return (x.reshape(bsz, seq, dm), xs.reshape(bs, ts, dm), a_k[None], a_v[None], b_conv[None], c_pool[None],
            d_k[None], d_v[None], ak_s[None], av_s[None], bc_s[None], cp_s[None], dk_s[None], dv_s[None])


def kernel(x_prompt, x_sample, cache_a_k, cache_a_v, state_b_conv, state_c_pool, cache_d_k, cache_d_v, even_w_in, even_b_in, a_sinks, b_conv_w, even_w_out, even_b_out, odd_w_in, odd_b_in, c_w_group, c_scale, odd_w_out, odd_b_out, mlp_w1, mlp_w2, ln1_g, ln1_b, ln2_g, ln2_b):
    wts = (even_w_in, even_b_in, a_sinks, b_conv_w, even_w_out, even_b_out, odd_w_in, odd_b_in, c_w_group, c_scale,
           odd_w_out, odd_b_out, mlp_w1, mlp_w2, ln1_g, ln1_b, ln2_g, ln2_b)
    caches = (cache_a_k, cache_a_v, state_b_conv, state_c_pool, cache_d_k, cache_d_v)
    return _forward(x_prompt, x_sample, caches, wts, tm=512, tm_proj=1024, bt_attn=8, bt_shift=32, n_chunks=4)
```
